```python
import math
import jax, jax.numpy as jnp
from jax import lax
import numpy as np

D_MODEL = 1024
BATCH = 2
SEQ = 16384
DEPTH = 2

CTX_LEN = 256
GRID_W = 64
HEAD_DIM = 64
ROPE_BASE = 10000.0
NEG_INF = -1e30

A_HEADS = 8
A_KV_HEADS = 2
A_WINDOW = 128
A_BLOCK = 128
B_HEADS = 8
NA_ROWS = 8
NA_COLS = 16
C_HEADS = 8
C_Q_RANK = 384
C_KV_RANK = 256
C_NOPE = 64
C_ROPE = 32
C_V = 64
D_HEADS = 4
D_QK = 64
D_V = 128
Q_BLOCK = 128
N_GROUPS = 4
EXP_PER_GROUP = 8
N_EXPERTS = N_GROUPS * EXP_PER_GROUP
TOP_K = 2
D_EXPERT = 512
MOE_BLOCK = 128

DN_ALPHA = (2 * DEPTH) ** 0.25
DN_BETA = (8 * DEPTH) ** -0.25

A_QW = A_HEADS * HEAD_DIM
A_KVW = A_KV_HEADS * HEAD_DIM
B_W = B_HEADS * HEAD_DIM
AB_SPLITS = (A_QW, A_QW + A_KVW, A_QW + 2 * A_KVW, A_QW + 2 * A_KVW + B_W, A_QW + 2 * A_KVW + 2 * B_W)
AB_IN = A_QW + 2 * A_KVW + 3 * B_W
AB_OUT = A_QW + B_W
D_QW = D_HEADS * 2 * D_QK
D_VW = D_HEADS * D_V
CD_SPLITS = (C_Q_RANK, C_Q_RANK + C_KV_RANK, C_Q_RANK + C_KV_RANK + C_ROPE, C_Q_RANK + C_KV_RANK + C_ROPE + D_QW, C_Q_RANK + C_KV_RANK + C_ROPE + 2 * D_QW)
CD_IN = C_Q_RANK + C_KV_RANK + C_ROPE + 2 * D_QW + D_VW
CD_OUT = C_HEADS * C_V + D_VW
N_EVEN = (DEPTH + 1) // 2
N_ODD = DEPTH // 2

kernel_name = 'hybrid_dit_window_natten_mla_diff_hmoe'


def layer_norm(x, g, b, eps=1e-5):
    xf = x.astype(jnp.float32)
    mu = jnp.mean(xf, -1, keepdims=True)
    var = jnp.mean(jnp.square(xf - mu), -1, keepdims=True)
    return ((xf - mu) * lax.rsqrt(var + eps) * g + b).astype(x.dtype)


def rms_norm(x, g, eps=1e-6):
    xf = x.astype(jnp.float32)
    return (xf * lax.rsqrt(jnp.mean(xf * xf, -1, keepdims=True) + eps) * g).astype(x.dtype)


def axial_rope_tables(n_tok, dim):
    t = jnp.arange(n_tok)
    pos_r = (t // GRID_W).astype(jnp.float32)
    pos_c = (t % GRID_W).astype(jnp.float32)
    quarter = dim // 4
    inv = ROPE_BASE ** (-jnp.arange(quarter, dtype=jnp.float32) / quarter)
    ang_r = pos_r[:, None] * inv
    ang_c = pos_c[:, None] * inv
    ang = jnp.concatenate([ang_r, ang_r, ang_c, ang_c], -1)
    return jnp.cos(ang), jnp.sin(ang)


def apply_rope(x, cos, sin):
    r1, r2, c1, c2 = jnp.split(x, 4, axis=-1)
    rot = jnp.concatenate([-r2, r1, -c2, c1], -1)
    y = x.astype(jnp.float32) * cos[:, None] + rot.astype(jnp.float32) * sin[:, None]
    return y.astype(x.dtype)


def joint_softmax(parts):
    m = parts[0].max(-1, keepdims=True)
    for p in parts[1:]:
        m = jnp.maximum(m, p.max(-1, keepdims=True))
    es = [jnp.exp(p - m) for p in parts]
    denom = sum(e.sum(-1, keepdims=True) for e in es)
    return [e / denom for e in es]


def sweep_query_blocks(fn, q):
    Bn, S = q.shape[0], q.shape[1]
    nblk = S // Q_BLOCK
    qb = jnp.moveaxis(q.reshape((Bn, nblk, Q_BLOCK) + q.shape[2:]), 1, 0)
    out = lax.map(fn, qb)
    return jnp.moveaxis(out, 0, 1).reshape((Bn, S) + out.shape[3:])


def window_gqa(q, k, v, q_c, k_c, v_c, sink, need_ctx):
    Bn, S = q.shape[0], q.shape[1]
    L = k_c.shape[1]
    G = A_HEADS // A_KV_HEADS
    nblk = S // A_BLOCK
    scale = HEAD_DIM ** -0.5
    pad = ((0, 0), (A_BLOCK, A_BLOCK), (0, 0), (0, 0))

    def band(t):
        tb = jnp.pad(t, pad).reshape(Bn, nblk + 2, A_BLOCK, A_KV_HEADS, HEAD_DIM)
        return jnp.concatenate([tb[:, :-2], tb[:, 1:-1], tb[:, 2:]], axis=2)

    kb, vb = band(k), band(v)
    qb = q.reshape(Bn, nblk, A_BLOCK, A_KV_HEADS, G, HEAD_DIM)
    s_loc = jnp.einsum('bnqhgd,bnkhd->bnhgqk', qb, kb).astype(jnp.float32) * scale
    qi = jnp.arange(A_BLOCK)
    kj = jnp.arange(3 * A_BLOCK)
    rel = kj[None, :] - A_BLOCK - qi[:, None]
    kpos = jnp.arange(nblk)[:, None] * A_BLOCK + kj[None, :] - A_BLOCK
    valid = (jnp.abs(rel)[None] <= A_WINDOW) & ((kpos >= 0) & (kpos < S))[:, None, :]
    s_loc = jnp.where(valid[None, :, None, None], s_loc, NEG_INF)
    s_ctx = jnp.einsum('bnqhgd,blhd->bnhgql', qb, k_c).astype(jnp.float32) * scale
    s_sink = jnp.broadcast_to(sink.astype(jnp.float32).reshape(A_KV_HEADS, G, 1, 1), s_ctx.shape[:-1] + (1,))
    _, p_ctx, p_loc = joint_softmax([s_sink, s_ctx, s_loc])
    o = (jnp.einsum('bnhgql,blhd->bnqhgd', p_ctx.astype(v.dtype), v_c)
         + jnp.einsum('bnhgqk,bnkhd->bnqhgd', p_loc.astype(v.dtype), vb))
    o = o.reshape(Bn, S, A_HEADS * HEAD_DIM)
    o_c = None
    if need_ctx:
        qc = q_c.reshape(Bn, L, A_KV_HEADS, G, HEAD_DIM)
        sc = jnp.einsum('blhgd,bmhd->bhglm', qc, k_c).astype(jnp.float32) * scale
        sc_sink = jnp.broadcast_to(sink.astype(jnp.float32).reshape(A_KV_HEADS, G, 1, 1), sc.shape[:-1] + (1,))
        _, pc = joint_softmax([sc_sink, sc])
        o_c = jnp.einsum('bhglm,bmhd->blhgd', pc.astype(v.dtype), v_c).reshape(Bn, L, A_HEADS * HEAD_DIM)
    return o, o_c


def neighbourhood_attn(q, k, v, q_c, k_c, v_c, rpb, need_ctx):
    Bn, S = q.shape[0], q.shape[1]
    L = k_c.shape[1]
    rows = S // GRID_W
    kr = min(NA_ROWS, rows)
    scale = HEAD_DIM ** -0.5
    qg = q.reshape(Bn, rows, GRID_W, B_HEADS, HEAD_DIM)
    r = jnp.arange(rows)
    row_idx = jnp.clip(r - kr // 2, 0, rows - kr)[:, None] + jnp.arange(kr)[None, :]
    kg = k.reshape(Bn, rows, GRID_W, B_HEADS, HEAD_DIM)[:, row_idx]
    vg = v.reshape(Bn, rows, GRID_W, B_HEADS, HEAD_DIM)[:, row_idx]
    s_loc = jnp.einsum('brqhd,brxkhd->brhqxk', qg, kg).astype(jnp.float32) * scale
    col = jnp.arange(GRID_W)
    col_start = jnp.clip(col - NA_COLS // 2, 0, GRID_W - NA_COLS)
    valid = (col[None, :] >= col_start[:, None]) & (col[None, :] < col_start[:, None] + NA_COLS)
    dr = row_idx - r[:, None] + (NA_ROWS - 1)
    dc = jnp.clip(col[None, :] - col[:, None], -(NA_COLS - 1), NA_COLS - 1) + (NA_COLS - 1)
    bias = rpb.astype(jnp.float32)[:, dr][:, :, :, dc]
    s_loc = jnp.where(valid[:, None, :], s_loc + bias.transpose(1, 0, 3, 2, 4), NEG_INF)
    s_loc = s_loc.reshape(Bn, rows, B_HEADS, GRID_W, kr * GRID_W)
    s_ctx = jnp.einsum('brqhd,blhd->brhql', qg, k_c).astype(jnp.float32) * scale
    p_ctx, p_loc = joint_softmax([s_ctx, s_loc])
    p_loc = p_loc.reshape(Bn, rows, B_HEADS, GRID_W, kr, GRID_W)
    o = (jnp.einsum('brhql,blhd->brqhd', p_ctx.astype(v.dtype), v_c)
         + jnp.einsum('brhqxk,brxkhd->brqhd', p_loc.astype(v.dtype), vg))
    o = o.reshape(Bn, S, B_HEADS * HEAD_DIM)
    o_c = None
    if need_ctx:
        sc = jnp.einsum('blhd,bmhd->bhlm', q_c, k_c).astype(jnp.float32) * scale
        pc = jax.nn.softmax(sc, axis=-1).astype(v.dtype)
        o_c = jnp.einsum('bhlm,bmhd->blhd', pc, v_c).reshape(Bn, L, B_HEADS * HEAD_DIM)
    return o, o_c


def mixer_ab(h, hc, w_in, sink, rpb, w_out, cos, sin, need_ctx):
    Bn, S = h.shape[0], h.shape[1]
    L = hc.shape[1]

    def project(t, n):
        aq, ak, av, bq, bk, bv = jnp.split(t @ w_in, AB_SPLITS, axis=-1)
        return (aq.reshape(Bn, n, A_HEADS, HEAD_DIM), ak.reshape(Bn, n, A_KV_HEADS, HEAD_DIM),
                av.reshape(Bn, n, A_KV_HEADS, HEAD_DIM), bq.reshape(Bn, n, B_HEADS, HEAD_DIM),
                bk.reshape(Bn, n, B_HEADS, HEAD_DIM), bv.reshape(Bn, n, B_HEADS, HEAD_DIM))

    aq, ak, av, bq, bk, bv = project(h, S)
    aqc, akc, avc, bqc, bkc, bvc = project(hc, L)
    aq = apply_rope(aq, cos, sin)
    ak = apply_rope(ak, cos, sin)
    oa, oac = window_gqa(aq, ak, av, aqc, akc, avc, sink, need_ctx)
    ob, obc = neighbourhood_attn(bq, bk, bv, bqc, bkc, bvc, rpb, need_ctx)
    y = jnp.concatenate([oa, ob], -1) @ w_out
    yc = jnp.concatenate([oac, obc], -1) @ w_out if need_ctx else None
    return y, yc


def mla_attend(q, k, v):
    s = jnp.einsum('bqhd,bkhd->bhqk', q, k).astype(jnp.float32) * (C_NOPE + C_ROPE) ** -0.5
    p = jax.nn.softmax(s, axis=-1).astype(v.dtype)
    return jnp.einsum('bhqk,bkhd->bqhd', p, v)


def diff_attend(q, k, v, lam):
    s = jnp.einsum('bqhid,bkhid->bhiqk', q, k).astype(jnp.float32) * D_QK ** -0.5
    p = jax.nn.softmax(s, axis=-1)
    pd = (p[:, :, 0] - lam * p[:, :, 1]).astype(v.dtype)
    return jnp.einsum('bhqk,bkhd->bqhd', pd, v)


def mixer_cd(h, hc, w_in, q_norm, w_uq, kv_norm, w_ukv, lam_p, subln, w_out,
             cos32, sin32, cos64, sin64, lam_init, need_ctx):
    Bn, S = h.shape[0], h.shape[1]
    L = hc.shape[1]

    def project(t, n, rope):
        cq, ckv, krope, dq, dk, dv = jnp.split(t @ w_in, CD_SPLITS, axis=-1)
        q = (rms_norm(cq, q_norm) @ w_uq).reshape(Bn, n, C_HEADS, C_NOPE + C_ROPE)
        kv = (rms_norm(ckv, kv_norm) @ w_ukv).reshape(Bn, n, C_HEADS, C_NOPE + C_V)
        q_nope, q_rope = jnp.split(q, [C_NOPE], axis=-1)
        k_nope, cv = jnp.split(kv, [C_NOPE], axis=-1)
        k_rope = krope.reshape(Bn, n, 1, C_ROPE)
        dq = dq.reshape(Bn, n, 2 * D_HEADS, D_QK)
        dk = dk.reshape(Bn, n, 2 * D_HEADS, D_QK)
        if rope:
            q_rope = apply_rope(q_rope, cos32, sin32)
            k_rope = apply_rope(k_rope, cos32, sin32)
            dq = apply_rope(dq, cos64, sin64)
            dk = apply_rope(dk, cos64, sin64)
        cq_full = jnp.concatenate([q_nope, q_rope], -1)
        ck_full = jnp.concatenate([k_nope, jnp.broadcast_to(k_rope, (Bn, n, C_HEADS, C_ROPE))], -1)
        return (cq_full, ck_full, cv, dq.reshape(Bn, n, D_HEADS, 2, D_QK),
                dk.reshape(Bn, n, D_HEADS, 2, D_QK), dv.reshape(Bn, n, D_HEADS, D_V))

    cq, ck, cv, dq, dk, dv = project(h, S, True)
    cqc, ckc, cvc, dqc, dkc, dvc = project(hc, L, False)
    lp = lam_p.astype(jnp.float32)
    lam = jnp.exp(jnp.sum(lp[0] * lp[1])) - jnp.exp(jnp.sum(lp[2] * lp[3])) + lam_init

    ck_all = jnp.concatenate([ckc, ck], 1)
    cv_all = jnp.concatenate([cvc, cv], 1)
    dk_all = jnp.concatenate([dkc, dk], 1)
    dv_all = jnp.concatenate([dvc, dv], 1)
    oc = sweep_query_blocks(lambda qb: mla_attend(qb, ck_all, cv_all), cq)
    od = sweep_query_blocks(lambda qb: diff_attend(qb, dk_all, dv_all, lam), dq)
    od = rms_norm(od, subln) * (1.0 - lam_init)
    y = jnp.concatenate([oc.reshape(Bn, S, C_HEADS * C_V), od.reshape(Bn, S, D_VW)], -1) @ w_out
    yc = None
    if need_ctx:
        occ = mla_attend(cqc, ckc, cvc).reshape(Bn, L, C_HEADS * C_V)
        odc = rms_norm(diff_attend(dqc, dkc, dvc, lam), subln) * (1.0 - lam_init)
        yc = jnp.concatenate([occ, odc.reshape(Bn, L, D_VW)], -1) @ w_out
    return y, yc


def hier_moe(h, w_group, b_group, w_er, b_er, w_gu, w_dn):
    T, D = h.shape
    g_prob = jax.nn.softmax((h @ w_group + b_group).astype(jnp.float32), axis=-1)
    g_val, g_idx = lax.top_k(g_prob, 1)
    g_val, g_idx = g_val[:, 0], g_idx[:, 0]
    e_logits = (h @ w_er + b_er).astype(jnp.float32).reshape(T, N_GROUPS, EXP_PER_GROUP)
    e_prob = jax.nn.softmax(e_logits[jnp.arange(T), g_idx], axis=-1)
    e_val, e_idx = lax.top_k(e_prob, TOP_K)
    wts = g_val[:, None] * e_val / e_val.sum(-1, keepdims=True)
    experts = g_idx[:, None] * EXP_PER_GROUP + e_idx

    A = T * TOP_K
    flat_e = experts.reshape(-1)
    flat_tok = jnp.broadcast_to(jnp.arange(T)[:, None], (T, TOP_K)).reshape(-1)
    flat_w = wts.reshape(-1)
    order = jnp.argsort(flat_e)
    se, stok, sw = flat_e[order], flat_tok[order], flat_w[order]
    counts = jnp.bincount(flat_e, length=N_EXPERTS)
    starts = jnp.cumsum(counts) - counts
    pcounts = ((counts + MOE_BLOCK - 1) // MOE_BLOCK) * MOE_BLOCK
    pends = jnp.cumsum(pcounts)
    pstarts = pends - pcounts
    dest = pstarts[se] + (jnp.arange(A) - starts[se])
    NB = -(-A // MOE_BLOCK) + N_EXPERTS
    R = NB * MOE_BLOCK
    row_tok = jnp.full((R,), T, jnp.int32).at[dest].set(stok)
    row_w = jnp.zeros((R,), h.dtype).at[dest].set(sw.astype(h.dtype))
    block_e = jnp.minimum(jnp.searchsorted(pends, jnp.arange(NB) * MOE_BLOCK, side='right'), N_EXPERTS - 1)
    h_pad = jnp.concatenate([h, jnp.zeros((1, D), h.dtype)], 0)
    xs = h_pad[row_tok].reshape(NB, MOE_BLOCK, D)

    def expert_block(args):
        xb, e = args
        gate, up = jnp.split(xb @ w_gu[e], 2, axis=-1)
        return (jax.nn.silu(gate) * up) @ w_dn[e]

    ys = lax.map(expert_block, (xs, block_e)).reshape(R, D)
    out = jnp.zeros((T + 1, D), h.dtype).at[row_tok].add(ys * row_w[:, None])
    return out[:T]


def setup_inputs(seed: int = 0) -> dict:
    key = jax.random.key(seed)
    ks = jax.random.split(key, 26)
    f32 = jnp.float32
    D = D_MODEL

    def nrm(k, shape, scale):
        return jax.random.normal(k, shape, f32) * scale

    return {
        'x': nrm(ks[0], (BATCH, SEQ, D), 1.0),
        'c': nrm(ks[1], (BATCH, D), 1.0),
        'ctx': nrm(ks[2], (BATCH, CTX_LEN, D), 1.0),
        'c_ctx': nrm(ks[3], (D,), 1.0),
        'w_ada': nrm(ks[4], (DEPTH, D, 6 * D), 0.5 * D ** -0.5),
        'b_ada': nrm(ks[5], (DEPTH, 6 * D), 0.02),
        'ln_g': 1.0 + nrm(ks[6], (DEPTH, 2, D), 0.02),
        'ln_b': nrm(ks[7], (DEPTH, 2, D), 0.02),
        'ab_w_in': nrm(ks[8], (N_EVEN, D, AB_IN), D ** -0.5),
        'a_sink': nrm(ks[9], (N_EVEN, A_HEADS), 0.5),
        'b_rpb': nrm(ks[10], (N_EVEN, B_HEADS, 2 * NA_ROWS - 1, 2 * NA_COLS - 1), 0.1),
        'ab_w_out': nrm(ks[11], (N_EVEN, AB_OUT, D), DN_BETA * AB_OUT ** -0.5),
        'cd_w_in': nrm(ks[12], (N_ODD, D, CD_IN), D ** -0.5),
        'c_q_norm': 1.0 + nrm(ks[13], (N_ODD, C_Q_RANK), 0.02),
        'c_w_uq': nrm(ks[14], (N_ODD, C_Q_RANK, C_HEADS * (C_NOPE + C_ROPE)), C_Q_RANK ** -0.5),
        'c_kv_norm': 1.0 + nrm(ks[15], (N_ODD, C_KV_RANK), 0.02),
        'c_w_ukv': nrm(ks[16], (N_ODD, C_KV_RANK, C_HEADS * (C_NOPE + C_V)), C_KV_RANK ** -0.5),
        'd_lambda': nrm(ks[17], (N_ODD, 4, D_QK), 0.1),
        'd_subln': 1.0 + nrm(ks[18], (N_ODD, D_V), 0.02),
        'cd_w_out': nrm(ks[19], (N_ODD, CD_OUT, D), DN_BETA * CD_OUT ** -0.5),
        'w_group': nrm(ks[20], (DEPTH, D, N_GROUPS), D ** -0.5),
        'b_group': nrm(ks[21], (DEPTH, N_GROUPS), 0.01),
        'w_exp_router': nrm(ks[22], (DEPTH, D, N_EXPERTS), D ** -0.5),
        'b_exp_router': nrm(ks[23], (DEPTH, N_EXPERTS), 0.01),
        'w_gate_up': nrm(ks[24], (DEPTH, N_EXPERTS, D, 2 * D_EXPERT), D ** -0.5),
        'w_down': nrm(ks[25], (DEPTH, N_EXPERTS, D_EXPERT, D), DN_BETA * D_EXPERT ** -0.5),
    }


def reference(x, c, ctx, c_ctx, w_ada, b_ada, ln_g, ln_b, ab_w_in, a_sink, b_rpb, ab_w_out,
              cd_w_in, c_q_norm, c_w_uq, c_kv_norm, c_w_ukv, d_lambda, d_subln, cd_w_out,
              w_group, b_group, w_exp_router, b_exp_router, w_gate_up, w_down):
    Bn, S, D = x.shape
    L = ctx.shape[1]
    cos64, sin64 = axial_rope_tables(S, HEAD_DIM)
    cos32, sin32 = axial_rope_tables(S, C_ROPE)
    for l in range(DEPTH):
        need_ctx = l < DEPTH - 1
        mod = jax.nn.silu(c) @ w_ada[l] + b_ada[l]
        mod_c = jax.nn.silu(c_ctx) @ w_ada[l] + b_ada[l]
        sh1, sc1, g1, sh2, sc2, g2 = [m[:, None, :] for m in jnp.split(mod, 6, axis=-1)]
        sh1c, sc1c, g1c, sh2c, sc2c, g2c = jnp.split(mod_c, 6, axis=-1)

        h = x * (1.0 + sc1) + sh1
        hc = ctx * (1.0 + sc1c) + sh1c
        if l % 2 == 0:
            e = l // 2
            y, yc = mixer_ab(h, hc, ab_w_in[e], a_sink[e], b_rpb[e], ab_w_out[e], cos64, sin64, need_ctx)
        else:
            o = l // 2
            lam_init = 0.8 - 0.6 * math.exp(-0.3 * l)
            y, yc = mixer_cd(h, hc, cd_w_in[o], c_q_norm[o], c_w_uq[o], c_kv_norm[o], c_w_ukv[o],
                             d_lambda[o], d_subln[o], cd_w_out[o], cos32, sin32, cos64, sin64,
                             lam_init, need_ctx)
        x = layer_norm(DN_ALPHA * x + g1 * y, ln_g[l, 0], ln_b[l, 0])
        if need_ctx:
            ctx = layer_norm(DN_ALPHA * ctx + g1c * yc, ln_g[l, 0], ln_b[l, 0])

        h = (x * (1.0 + sc2) + sh2).reshape(Bn * S, D)
        if need_ctx:
            hc = (ctx * (1.0 + sc2c) + sh2c).reshape(Bn * L, D)
            h = jnp.concatenate([h, hc], 0)
        y_all = hier_moe(h, w_group[l], b_group[l], w_exp_router[l], b_exp_router[l], w_gate_up[l], w_down[l])
        y = y_all[:Bn * S].reshape(Bn, S, D)
        x = layer_norm(DN_ALPHA * x + g2 * y, ln_g[l, 1], ln_b[l, 1])
        if need_ctx:
            yc = y_all[Bn * S:].reshape(Bn, L, D)
            ctx = layer_norm(DN_ALPHA * ctx + g2c * yc, ln_g[l, 1], ln_b[l, 1])
    return x
```

```python
import functools
import math

import jax
import jax.numpy as jnp
from jax import lax
from jax.experimental import pallas as pl
from jax.experimental.pallas import tpu as pltpu

F32 = jnp.float32
BF16 = jnp.bfloat16

GRID_W = 64
HEAD_DIM = 64
ROPE_BASE = 10000.0
NEG_INF = -1e30
LOG2E = 1.4426950408889634

A_HEADS = 8
A_KV_HEADS = 2
A_WINDOW = 128
A_BLOCK = 128
B_HEADS = 8
NA_ROWS = 8
NA_COLS = 16
C_HEADS = 8
C_Q_RANK = 384
C_KV_RANK = 256
C_NOPE = 64
C_ROPE = 32
C_V = 64
D_HEADS = 4
D_QK = 64
D_V = 128
N_GROUPS = 4
EXP_PER_GROUP = 8
N_EXPERTS = N_GROUPS * EXP_PER_GROUP
TOP_K = 2
MOE_BLOCK = 128
ROUTER_PAD = 128

LANES = 128
VMEM_LIMIT = 56 * 1024 * 1024


def _params(sem):
    return pltpu.CompilerParams(dimension_semantics=sem, vmem_limit_bytes=VMEM_LIMIT)


def _dot(a, b):
    return jnp.dot(a, b, preferred_element_type=F32)


def _dot_nt(a, b):
    return lax.dot_general(a, b, (((1,), (1,)), ((), ())), preferred_element_type=F32)


def _split_bf16(x):
    hi = x.astype(BF16)
    lo = (x - hi.astype(F32)).astype(BF16)
    return hi, lo


def _ada_kernel(c_ref, w_ref, b_ref, o_ref):
    c = c_ref[...]
    a = c / (1.0 + jnp.exp(-c))
    a_hi, a_lo = _split_bf16(a)
    w_hi, w_lo = _split_bf16(w_ref[0])
    acc = _dot(a_hi, w_hi) + _dot(a_lo, w_hi) + _dot(a_hi, w_lo)
    o_ref[0] = acc + b_ref[0]


def _ada(c_all, w_ada, b_ada):
    depth, d, n = w_ada.shape
    tn = 1536
    return pl.pallas_call(
        _ada_kernel,
        grid=(depth, n // tn),
        in_specs=[
            pl.BlockSpec((8, d), lambda l, j: (0, 0)),
            pl.BlockSpec((1, d, tn), lambda l, j: (l, 0, j)),
            pl.BlockSpec((1, 1, tn), lambda l, j: (l, 0, j)),
        ],
        out_specs=pl.BlockSpec((1, 8, tn), lambda l, j: (l, 0, j)),
        out_shape=jax.ShapeDtypeStruct((depth, 8, n), F32),
        compiler_params=_params(("arbitrary", "arbitrary")),
        name="ada",
    )(c_all, w_ada, b_ada.reshape(depth, 1, n))


def _rope_tile(t, cos, sin_signed, q):
    lane = lax.broadcasted_iota(jnp.int32, t.shape, 1)
    first = (lane & (2 * q - 1)) < q
    up = pltpu.roll(t, LANES - q, 1)
    dn = pltpu.roll(t, q, 1)
    return t * cos + jnp.where(first, up, dn) * sin_signed


def _rope_wide(t, cos, sin_signed, q):
    n = t.shape[1] // LANES
    return jnp.concatenate(
        [_rope_tile(t[:, i * LANES:(i + 1) * LANES], cos, sin_signed, q) for i in range(n)], axis=1)


def _modulate(x_ref, mod_ref, which):
    sh = mod_ref[0, 3 * which:3 * which + 1, :]
    sc = mod_ref[0, 3 * which + 1:3 * which + 2, :]
    return x_ref[...] * (1.0 + sc) + sh


def _proj_ab_kernel(x_ref, mod_ref, w_ref, cos_ref, sin_ref,
                    aq_ref, ak_ref, av_ref, bq_ref, bk_ref, bv_ref, *, rope):
    h = _modulate(x_ref, mod_ref, 0).astype(BF16)
    r = _dot(h, w_ref[...])
    aq = r[:, 0:512]
    ak = r[:, 512:640]
    if rope:
        cos = cos_ref[...]
        sin = sin_ref[...]
        aq = _rope_wide(aq, cos, sin, 16)
        ak = _rope_wide(ak, cos, sin, 16)
    scale = HEAD_DIM ** -0.5
    aq_ref[...] = (aq * scale).astype(BF16)
    ak_ref[...] = ak.astype(BF16)
    av_ref[...] = r[:, 640:768].astype(BF16)
    bq_ref[...] = (r[:, 768:1280] * scale).astype(BF16)
    bk_ref[...] = r[:, 1280:1792].astype(BF16)
    bv_ref[...] = r[:, 1792:2304].astype(BF16)


def _proj_ab(x2d, row0, nrows, mod, gid, w_bf, cos, sin, pos_blocks, rope, tm):
    d = x2d.shape[1]
    n_in = w_bf.shape[1]
    b0 = row0 // tm
    widths = (512, 128, 128, 512, 512, 512)
    return pl.pallas_call(
        functools.partial(_proj_ab_kernel, rope=rope),
        grid=(nrows // tm,),
        in_specs=[
            pl.BlockSpec((tm, d), lambda i: (b0 + i, 0)),
            pl.BlockSpec((1, 6, d), lambda i: (gid(i), 0, 0)),
            pl.BlockSpec((d, n_in), lambda i: (0, 0)),
            pl.BlockSpec((tm, LANES), lambda i: (i % pos_blocks, 0)),
            pl.BlockSpec((tm, LANES), lambda i: (i % pos_blocks, 0)),
        ],
        out_specs=[pl.BlockSpec((tm, w), lambda i: (i, 0)) for w in widths],
        out_shape=[jax.ShapeDtypeStruct((nrows, w), BF16) for w in widths],
        compiler_params=_params(("arbitrary",)),
        name="proj_ab",
    )(x2d, mod, w_bf, cos, sin)


def _softmax_pv(parts, sink_col):
    m = parts[0][0].max(-1, keepdims=True)
    for s, _ in parts[1:]:
        m = jnp.maximum(m, s.max(-1, keepdims=True))
    if sink_col is not None:
        m = jnp.maximum(m, sink_col)
    denom = None if sink_col is None else jnp.exp(sink_col - m)
    o = None
    for s, v in parts:
        e = jnp.exp(s - m)
        d = e.sum(-1, keepdims=True)
        denom = d if denom is None else denom + d
        pv = _dot(e.astype(BF16), v)
        o = pv if o is None else o + pv
    return o / denom


def _win_kernel(sink_ref, q_ref, kp_ref, kc_ref, kn_ref, vp_ref, vc_ref, vn_ref,
                kx_ref, vx_ref, o_ref, *, seq):
    n = pl.program_id(1)
    blk = A_BLOCK
    g_sz = A_HEADS // A_KV_HEADS
    qi = lax.broadcasted_iota(jnp.int32, (blk, 3 * blk), 0)
    kj = lax.broadcasted_iota(jnp.int32, (blk, 3 * blk), 1)
    rel = kj - blk - qi
    kpos = n * blk + kj - blk
    valid = (jnp.abs(rel) <= A_WINDOW) & (kpos >= 0) & (kpos < seq)
    valid = jnp.concatenate([valid] * g_sz, axis=0)
    q = q_ref[...]
    k_all = jnp.concatenate([kp_ref[...], kc_ref[...], kn_ref[...]], axis=0)
    v_all = jnp.concatenate([vp_ref[...], vc_ref[...], vn_ref[...]], axis=0)
    kx = kx_ref[...]
    vx = vx_ref[...]
    outs = []
    for g in range(A_KV_HEADS):
        lo, hi = g * HEAD_DIM, (g + 1) * HEAD_DIM
        heads = [g * g_sz + i for i in range(g_sz)]
        qs = jnp.concatenate([q[:, h * HEAD_DIM:(h + 1) * HEAD_DIM] for h in heads], axis=0)
        sink = jnp.concatenate(
            [jnp.full((blk, 1), sink_ref[h], F32) for h in heads], axis=0)
        s_loc = jnp.where(valid, _dot_nt(qs, k_all[:, lo:hi]), NEG_INF)
        s_ctx = _dot_nt(qs, kx[:, lo:hi])
        o = _softmax_pv([(s_ctx, vx[:, lo:hi]), (s_loc, v_all[:, lo:hi])], sink)
        outs.extend(o[i * blk:(i + 1) * blk] for i in range(g_sz))
    o_ref[...] = jnp.concatenate(outs, axis=1).astype(o_ref.dtype)


def _window_attn(sink, aq, ak, av, akx, avx, batch, seq, ctx_len):
    nblk = seq // A_BLOCK
    kvw = A_KV_HEADS * HEAD_DIM
    qw = A_HEADS * HEAD_DIM

    def nb(delta):
        return lambda b, n: (b * nblk + jnp.clip(n + delta, 0, nblk - 1), 0)

    kv_specs = [pl.BlockSpec((A_BLOCK, kvw), nb(d)) for d in (-1, 0, 1)]
    return pl.pallas_call(
        functools.partial(_win_kernel, seq=seq),
        grid=(batch, nblk),
        in_specs=[pl.BlockSpec(memory_space=pltpu.SMEM),
                  pl.BlockSpec((A_BLOCK, qw), lambda b, n: (b * nblk + n, 0))]
        + kv_specs + kv_specs
        + [pl.BlockSpec((ctx_len, kvw), lambda b, n: (b, 0)),
           pl.BlockSpec((ctx_len, kvw), lambda b, n: (b, 0))],
        out_specs=pl.BlockSpec((A_BLOCK, qw), lambda b, n: (b * nblk + n, 0)),
        out_shape=jax.ShapeDtypeStruct((batch * seq, qw), BF16),
        compiler_params=_params(("arbitrary", "arbitrary")),
        name="window_attn",
    )(sink, aq, ak, ak, ak, av, av, av, akx, avx)


NA_QROWS = 8
NA_Q = NA_QROWS * GRID_W
NA_KBLK = 4 * GRID_W
NA_K = 4 * NA_KBLK


def _na_bias_tables(rpb, rows):
    nj = rows // NA_QROWS
    kr = min(NA_ROWS, rows)
    col = jnp.arange(GRID_W)
    col_start = jnp.clip(col - NA_COLS // 2, 0, GRID_W - NA_COLS)
    cvalid = (col[None, :] >= col_start[:, None]) & (col[None, :] < col_start[:, None] + NA_COLS)
    dc = jnp.clip(col[None, :] - col[:, None], -(NA_COLS - 1), NA_COLS - 1) + (NA_COLS - 1)
    tabs = []
    for j in (0, min(1, nj - 1), nj - 1):
        r = j * NA_QROWS + jnp.arange(NA_QROWS)
        krow = j * NA_QROWS - NA_KBLK // GRID_W + jnp.arange(NA_K // GRID_W)
        start = jnp.clip(r - kr // 2, 0, rows - kr)
        rvalid = (krow[None, :] >= start[:, None]) & (krow[None, :] < start[:, None] + kr)
        dr = jnp.clip(krow[None, :] - r[:, None] + (NA_ROWS - 1), 0, 2 * NA_ROWS - 2)
        bias = rpb.astype(F32)[:, dr][:, :, :, dc]
        ok = rvalid[:, :, None, None] & cvalid[None, None, :, :]
        t = jnp.where(ok[None], bias, NEG_INF)
        t = t.transpose(0, 1, 3, 2, 4).reshape(rpb.shape[0], NA_Q, NA_K)
        tabs.append(t)
    return jnp.stack(tabs)


def _na_kernel(q_ref, k0, k1, k2, k3, v0, v1, v2, v3, kx_ref, vx_ref, t_ref, o_ref):
    q = q_ref[...]
    k_all = jnp.concatenate([k0[...], k1[...], k2[...], k3[...]], axis=0)
    v_all = jnp.concatenate([v0[...], v1[...], v2[...], v3[...]], axis=0)
    kx = kx_ref[...]
    vx = vx_ref[...]
    outs = []
    for hh in range(LANES // HEAD_DIM):
        lo, hi = hh * HEAD_DIM, (hh + 1) * HEAD_DIM
        qh = q[:, lo:hi]
        s_loc = _dot_nt(qh, k_all[:, lo:hi]) + t_ref[0, hh]
        s_ctx = _dot_nt(qh, kx[:, lo:hi])
        outs.append(_softmax_pv([(s_ctx, vx[:, lo:hi]), (s_loc, v_all[:, lo:hi])], None))
    o_ref[...] = jnp.concatenate(outs, axis=1).astype(o_ref.dtype)


def _na_attn(bq, bk, bv, bkx, bvx, tables, batch, seq, ctx_len):
    nj = seq // NA_Q
    nkb = seq // NA_KBLK
    hp = B_HEADS * HEAD_DIM // LANES
    per = LANES // HEAD_DIM

    def kb(t):
        return lambda p, b, j: (b * nkb + jnp.clip(2 * j - 1 + t, 0, nkb - 1), p)

    def variant(p, b, j):
        return (jnp.where(j == 0, 0, jnp.where(j == nj - 1, 2, 1)), p, 0, 0)

    kv_specs = [pl.BlockSpec((NA_KBLK, LANES), kb(t)) for t in range(4)]
    return pl.pallas_call(
        _na_kernel,
        grid=(hp, batch, nj),
        in_specs=[pl.BlockSpec((NA_Q, LANES), lambda p, b, j: (b * nj + j, p))]
        + kv_specs + kv_specs
        + [pl.BlockSpec((ctx_len, LANES), lambda p, b, j: (b, p)),
           pl.BlockSpec((ctx_len, LANES), lambda p, b, j: (b, p)),
           pl.BlockSpec((1, per, NA_Q, NA_K), variant)],
        out_specs=pl.BlockSpec((NA_Q, LANES), lambda p, b, j: (b * nj + j, p)),
        out_shape=jax.ShapeDtypeStruct((batch * seq, B_HEADS * HEAD_DIM), BF16),
        compiler_params=_params(("arbitrary", "arbitrary", "arbitrary")),
        name="na_attn",
    )(bq, bk, bk, bk, bk, bv, bv, bv, bv, bkx, bvx, tables)


def _ctx_ab_kernel(sink_ref, aq_ref, ak_ref, av_ref, bq_ref, bk_ref, bv_ref, oa_ref, ob_ref):
    ctx_len = aq_ref.shape[0]
    g_sz = A_HEADS // A_KV_HEADS
    aq = aq_ref[...]
    ak = ak_ref[...]
    av = av_ref[...]
    outs = []
    for g in range(A_KV_HEADS):
        lo, hi = g * HEAD_DIM, (g + 1) * HEAD_DIM
        heads = [g * g_sz + i for i in range(g_sz)]
        qs = jnp.concatenate([aq[:, h * HEAD_DIM:(h + 1) * HEAD_DIM] for h in heads], axis=0)
        sink = jnp.concatenate(
            [jnp.full((ctx_len, 1), sink_ref[h], F32) for h in heads], axis=0)
        o = _softmax_pv([(_dot_nt(qs, ak[:, lo:hi]), av[:, lo:hi])], sink)
        outs.extend(o[i * ctx_len:(i + 1) * ctx_len] for i in range(g_sz))
    oa_ref[...] = jnp.concatenate(outs, axis=1).astype(oa_ref.dtype)
    bq = bq_ref[...]
    bk = bk_ref[...]
    bv = bv_ref[...]
    outs = []
    for h in range(B_HEADS):
        lo, hi = h * HEAD_DIM, (h + 1) * HEAD_DIM
        outs.append(_softmax_pv([(_dot_nt(bq[:, lo:hi], bk[:, lo:hi]), bv[:, lo:hi])], None))
    ob_ref[...] = jnp.concatenate(outs, axis=1).astype(ob_ref.dtype)


def _ctx_ab_attn(sink, aq, ak, av, bq, bk, bv, batch, ctx_len):
    def spec(w):
        return pl.BlockSpec((ctx_len, w), lambda b: (b, 0))

    return pl.pallas_call(
        _ctx_ab_kernel,
        grid=(batch,),
        in_specs=[pl.BlockSpec(memory_space=pltpu.SMEM)]
        + [spec(a.shape[1]) for a in (aq, ak, av, bq, bk, bv)],
        out_specs=[spec(aq.shape[1]), spec(bq.shape[1])],
        out_shape=[jax.ShapeDtypeStruct(aq.shape, BF16), jax.ShapeDtypeStruct(bq.shape, BF16)],
        compiler_params=_params(("arbitrary",)),
        name="ctx_ab_attn",
    )(sink, aq, ak, av, bq, bk, bv)


def _layer_norm(t, g, b):
    mu = jnp.mean(t, -1, keepdims=True)
    c = t - mu
    var = jnp.mean(c * c, -1, keepdims=True)
    return c * lax.rsqrt(var + 1e-5) * g + b


def _post_attn_kernel(*refs, alpha, lat_tiles, with_ctx):
    n_src = 6 if with_ctx else 3
    srcs = refs[:n_src]
    (w_ref, mod_ref, lng_ref, lnb_ref, wr_hi_ref, wr_lo_ref, br_ref,
     x1_ref, h2_ref, lg_ref) = refs[n_src:]

    def run(o1_ref, o2_ref, x_ref):
        half = o1_ref.shape[1]
        y = _dot(o1_ref[...], w_ref[0:half, :]) + _dot(o2_ref[...], w_ref[half:, :])
        g1 = mod_ref[0, 2:3, :]
        x1 = _layer_norm(alpha * x_ref[...] + g1 * y, lng_ref[...], lnb_ref[...])
        x1_ref[...] = x1
        h2 = x1 * (1.0 + mod_ref[0, 4:5, :]) + mod_ref[0, 3:4, :]
        h_hi, h_lo = _split_bf16(h2)
        h2_ref[...] = h_hi
        wr_hi = wr_hi_ref[...]
        lg_ref[...] = (_dot(h_hi, wr_hi) + _dot(h_lo, wr_hi) + _dot(h_hi, wr_lo_ref[...])
                       + br_ref[...])

    if not with_ctx:
        run(*srcs)
    else:
        i = pl.program_id(0)
        pl.when(i < lat_tiles)(lambda: run(*srcs[:3]))
        pl.when(i >= lat_tiles)(lambda: run(*srcs[3:]))


def _post_attn(lat, ctx_src, w_bf, mod, gid, lng, lnb, wr_hi, wr_lo, br, n_lat, alpha, tm):
    half = lat[0].shape[1]
    d = lat[2].shape[1]
    lat_tiles = n_lat // tm
    n_rows = n_lat + (ctx_src[0].shape[0] if ctx_src is not None else 0)

    def lat_map(i):
        return (jnp.minimum(i, lat_tiles - 1), 0)

    def ctx_map(i):
        return (jnp.maximum(i - lat_tiles, 0), 0)

    def src_specs(index_map):
        return [pl.BlockSpec((tm, half), index_map), pl.BlockSpec((tm, half), index_map),
                pl.BlockSpec((tm, d), index_map)]

    in_specs = src_specs(lat_map)
    args = list(lat)
    if ctx_src is not None:
        in_specs += src_specs(ctx_map)
        args += list(ctx_src)

    def const(shape):
        return pl.BlockSpec(shape, lambda i: (0,) * len(shape))

    in_specs += [const((2 * half, d)),
                 pl.BlockSpec((1, 6, d), lambda i: (gid(i), 0, 0)),
                 const((1, d)), const((1, d)),
                 const((d, ROUTER_PAD)), const((d, ROUTER_PAD)), const((1, ROUTER_PAD))]
    args += [w_bf, mod, lng, lnb, wr_hi, wr_lo, br]
    widths = (d, d, ROUTER_PAD)
    dtypes = (F32, BF16, F32)
    return pl.pallas_call(
        functools.partial(_post_attn_kernel, alpha=alpha, lat_tiles=lat_tiles,
                          with_ctx=ctx_src is not None),
        grid=(n_rows // tm,),
        in_specs=in_specs,
        out_specs=[pl.BlockSpec((tm, w), lambda i: (i, 0)) for w in widths],
        out_shape=[jax.ShapeDtypeStruct((n_rows, w), t) for w, t in zip(widths, dtypes)],
        compiler_params=_params(("arbitrary",)),
        name="post_attn",
    )(*args)


def _moe_kernel(be_ref, xs_ref, wgu_ref, wdn_ref, ys_ref, wgu_bf, wdn_bf):
    i = pl.program_id(0)
    prev = be_ref[jnp.maximum(i - 1, 0)]
    changed = (i == 0) | (be_ref[i] != prev)

    @pl.when(changed)
    def _():
        wgu_bf[...] = wgu_ref[0].astype(BF16)
        wdn_bf[...] = wdn_ref[0].astype(BF16)

    de = wdn_bf.shape[0]
    gu = _dot(xs_ref[...], wgu_bf[...])
    gate = gu[:, :de]
    up = gu[:, de:]
    act = gate / (1.0 + jnp.exp(-gate)) * up
    ys_ref[...] = _dot(act.astype(BF16), wdn_bf[...])


def _moe_experts(block_e, xs, w_gu, w_dn):
    rows, d = xs.shape
    nb = rows // MOE_BLOCK
    de = w_dn.shape[1]
    grid_spec = pltpu.PrefetchScalarGridSpec(
        num_scalar_prefetch=1,
        grid=(nb,),
        in_specs=[
            pl.BlockSpec((MOE_BLOCK, d), lambda i, be: (i, 0)),
            pl.BlockSpec((1, d, 2 * de), lambda i, be: (be[i], 0, 0)),
            pl.BlockSpec((1, de, d), lambda i, be: (be[i], 0, 0)),
        ],
        out_specs=pl.BlockSpec((MOE_BLOCK, d), lambda i, be: (i, 0)),
        scratch_shapes=[pltpu.VMEM((d, 2 * de), BF16), pltpu.VMEM((de, d), BF16)],
    )
    return pl.pallas_call(
        _moe_kernel,
        grid_spec=grid_spec,
        out_shape=jax.ShapeDtypeStruct((rows, d), F32),
        compiler_params=_params(("arbitrary",)),
        name="moe_experts",
    )(block_e, xs, w_gu, w_dn)


def _route(logits, n_tok):
    g_prob = jax.nn.softmax(logits[:, :N_GROUPS], axis=-1)
    g_idx = jnp.argmax(g_prob, axis=-1)
    g_val = jnp.max(g_prob, axis=-1)
    e_logits = logits[:, N_GROUPS:N_GROUPS + N_EXPERTS].reshape(n_tok, N_GROUPS, EXP_PER_GROUP)
    e_sel = jnp.take_along_axis(e_logits, g_idx[:, None, None], axis=1)[:, 0]
    e_prob = jax.nn.softmax(e_sel, axis=-1)
    e_val, e_idx = lax.top_k(e_prob, TOP_K)
    wts = g_val[:, None] * e_val / e_val.sum(-1, keepdims=True)
    experts = (g_idx[:, None] * EXP_PER_GROUP + e_idx).astype(jnp.int32)

    n_asg = n_tok * TOP_K
    flat_e = experts.reshape(-1)
    order = jnp.argsort(flat_e).astype(jnp.int32)
    rank = jnp.argsort(order).astype(jnp.int32)
    counts = (flat_e[:, None] == jnp.arange(N_EXPERTS)[None, :]).sum(0).astype(jnp.int32)
    starts = jnp.cumsum(counts) - counts
    pcounts = ((counts + MOE_BLOCK - 1) // MOE_BLOCK) * MOE_BLOCK
    pends = jnp.cumsum(pcounts)
    pstarts = pends - pcounts
    dest = (pstarts[flat_e] + rank - starts[flat_e]).reshape(n_tok, TOP_K)
    nb = -(-n_asg // MOE_BLOCK) + N_EXPERTS
    block_e = jnp.minimum(
        jnp.searchsorted(pends, jnp.arange(nb) * MOE_BLOCK, side='right'), N_EXPERTS - 1).astype(jnp.int32)
    r = jnp.arange(nb * MOE_BLOCK)
    e_r = block_e[r // MOE_BLOCK]
    off = r - pstarts[e_r]
    valid = off < counts[e_r]
    src = order[jnp.clip(starts[e_r] + off, 0, n_asg - 1)] // TOP_K
    row_tok = jnp.where(valid, src, 0).astype(jnp.int32)
    return wts, dest, row_tok, block_e


def _post_moe_kernel(x_ref, ya_ref, yb_ref, w_ref, mod_ref, lng_ref, lnb_ref, o_ref, *, alpha):
    w = w_ref[...]
    y = w[:, 0:1] * ya_ref[...] + w[:, 1:2] * yb_ref[...]
    g2 = mod_ref[0, 5:6, :]
    o_ref[...] = _layer_norm(alpha * x_ref[...] + g2 * y, lng_ref[...], lnb_ref[...])


def _post_moe(x1, ya, yb, wts, mod, gid, lng, lnb, alpha, tm):
    rows, d = x1.shape
    row = pl.BlockSpec((tm, d), lambda i: (i, 0))
    vec = pl.BlockSpec((1, d), lambda i: (0, 0))
    return pl.pallas_call(
        functools.partial(_post_moe_kernel, alpha=alpha),
        grid=(rows // tm,),
        in_specs=[row, row, row,
                  pl.BlockSpec((tm, TOP_K), lambda i: (i, 0)),
                  pl.BlockSpec((1, 6, d), lambda i: (gid(i), 0, 0)),
                  vec, vec],
        out_specs=row,
        out_shape=jax.ShapeDtypeStruct((rows, d), F32),
        compiler_params=_params(("arbitrary",)),
        name="post_moe",
    )(x1, ya, yb, wts, mod, lng, lnb)


CD_COLS = dict(cq=(0, 384), ckv=(384, 640), dq=(640, 1152), dk=(1152, 1664), dv=(1664, 2176),
               kr=(2176, 2304))
C_HEAD_PAD = LANES


def _rms(t, g, eps=1e-6):
    return t * lax.rsqrt(jnp.mean(t * t, -1, keepdims=True) + eps) * g


def _proj_cd_kernel(x_ref, mod_ref, w_ref, qn_ref, wuq_ref, kvn_ref, wukv_ref,
                    cos64_ref, sin64_ref, cos32_ref, sin32_ref,
                    qc_ref, kc_ref, vc_ref, dq_ref, dk_ref, dv_ref, *, rope):
    h = _modulate(x_ref, mod_ref, 0).astype(BF16)
    r = _dot(h, w_ref[...])

    def cols(name):
        lo, hi = CD_COLS[name]
        return r[:, lo:hi]

    q = _dot(_rms(cols('cq'), qn_ref[...]).astype(BF16), wuq_ref[...])
    kv = _dot(_rms(cols('ckv'), kvn_ref[...]).astype(BF16), wukv_ref[...])
    kr = cols('kr')
    dq = cols('dq')
    dk = cols('dk')
    if rope:
        c32, s32 = cos32_ref[...], sin32_ref[...]
        c64, s64 = cos64_ref[...], sin64_ref[...]
        q = _rope_wide(q, c32, s32, C_ROPE // 4)
        kr = _rope_tile(kr, c32, s32, C_ROPE // 4)
        dq = _rope_wide(dq, c64, s64, D_QK // 4)
        dk = _rope_wide(dk, c64, s64, D_QK // 4)
    kw = C_HEADS * C_HEAD_PAD
    qc_ref[...] = (q * ((C_NOPE + C_ROPE) ** -0.5 * LOG2E)).astype(BF16)
    kc_ref[...] = (kv[:, :kw] + jnp.concatenate([kr] * C_HEADS, axis=1)).astype(BF16)
    vc_ref[...] = kv[:, kw:].astype(BF16)
    dq_ref[...] = (dq * (D_QK ** -0.5 * LOG2E)).astype(BF16)
    dk_ref[...] = dk.astype(BF16)
    dv_ref[...] = cols('dv').astype(BF16)


def _proj_cd(x2d, row0, nrows, mod, gid, w_bf, qn, wuq, kvn, wukv, tabs, pos_blocks, rope, tm):
    d = x2d.shape[1]
    b0 = row0 // tm
    widths = (C_HEADS * C_HEAD_PAD, C_HEADS * C_HEAD_PAD, C_HEADS * C_V,
              2 * D_HEADS * D_QK, 2 * D_HEADS * D_QK, D_HEADS * D_V)

    def full(a):
        return pl.BlockSpec(a.shape, lambda i: (0,) * a.ndim)

    tab = pl.BlockSpec((tm, LANES), lambda i: (i % pos_blocks, 0))
    return pl.pallas_call(
        functools.partial(_proj_cd_kernel, rope=rope),
        grid=(nrows // tm,),
        in_specs=[pl.BlockSpec((tm, d), lambda i: (b0 + i, 0)),
                  pl.BlockSpec((1, 6, d), lambda i: (gid(i), 0, 0)),
                  full(w_bf), full(qn), full(wuq), full(kvn), full(wukv), tab, tab, tab, tab],
        out_specs=[pl.BlockSpec((tm, w), lambda i: (i, 0)) for w in widths],
        out_shape=[jax.ShapeDtypeStruct((nrows, w), BF16) for w in widths],
        compiler_params=_params(("arbitrary",)),
        name="proj_cd",
    )(x2d, mod, w_bf, qn, wuq, kvn, wukv, *tabs)


def _online_step(q, k, v_t, m_ref, l_ref, acc_ref, idx):
    s = _dot_nt(k, q)
    m_prev = m_ref[idx]
    m_new = jnp.maximum(m_prev, s.max(0, keepdims=True))
    alpha = jnp.exp2(m_prev - m_new)
    p = jnp.exp2(s - m_new)
    l_ref[idx] = alpha * l_ref[idx] + p.sum(0, keepdims=True)
    acc_ref[idx] = alpha * acc_ref[idx] + _dot(v_t, p.astype(BF16))
    m_ref[idx] = m_new


def _mla_kernel(q_ref, k_ref, vt_ref, kx_ref, vxt_ref, o_ref, m_ref, l_ref, acc_ref):
    ki = pl.program_id(2)

    @pl.when(ki == 0)
    def _():
        m_ref[...] = jnp.full(m_ref.shape, NEG_INF, F32)
        l_ref[...] = jnp.zeros(l_ref.shape, F32)
        acc_ref[...] = jnp.zeros(acc_ref.shape, F32)
        for h in range(C_HEADS):
            hs = slice(h * C_HEAD_PAD, (h + 1) * C_HEAD_PAD)
            _online_step(q_ref[:, hs], kx_ref[:, hs], vxt_ref[0, h * C_V:(h + 1) * C_V, :],
                         m_ref, l_ref, acc_ref, h)

    for h in range(C_HEADS):
        hs = slice(h * C_HEAD_PAD, (h + 1) * C_HEAD_PAD)
        _online_step(q_ref[:, hs], k_ref[:, hs], vt_ref[0, h * C_V:(h + 1) * C_V, :],
                     m_ref, l_ref, acc_ref, h)

    @pl.when(ki == pl.num_programs(2) - 1)
    def _():
        for h in range(C_HEADS):
            o_ref[0, h * C_V:(h + 1) * C_V, :] = (acc_ref[h] / l_ref[h]).astype(o_ref.dtype)


def _mla_attn(qc, kc, vct, kxc, vxct, batch, seq, ctx_len, tq, tk):
    nq, nk = seq // tq, seq // tk
    qw = C_HEADS * C_HEAD_PAD
    vw = C_HEADS * C_V
    return pl.pallas_call(
        _mla_kernel,
        grid=(batch, nq, nk),
        in_specs=[
            pl.BlockSpec((tq, qw), lambda b, i, j: (b * nq + i, 0)),
            pl.BlockSpec((tk, qw), lambda b, i, j: (b * nk + j, 0)),
            pl.BlockSpec((1, vw, tk), lambda b, i, j: (b, 0, j)),
            pl.BlockSpec((ctx_len, qw), lambda b, i, j: (b, 0)),
            pl.BlockSpec((1, vw, ctx_len), lambda b, i, j: (b, 0, 0)),
        ],
        out_specs=pl.BlockSpec((1, vw, tq), lambda b, i, j: (b, 0, i)),
        out_shape=jax.ShapeDtypeStruct((batch, vw, seq), BF16),
        scratch_shapes=[pltpu.VMEM((C_HEADS, 1, tq), F32), pltpu.VMEM((C_HEADS, 1, tq), F32),
                        pltpu.VMEM((C_HEADS, C_V, tq), F32)],
        compiler_params=_params(("arbitrary", "arbitrary", "arbitrary")),
        name="mla_attn",
    )(qc, kc, vct, kxc, vxct)


def _diff_kernel(lam_ref, q_ref, k_ref, vt_ref, kx_ref, vxt_ref, subln_ref, o_ref,
                 m_ref, l_ref, acc_ref, *, out_scale):
    ki = pl.program_id(2)
    pair = 2 * D_QK

    def q_parts(h):
        qh = q_ref[:, h * pair:(h + 1) * pair]
        lane = lax.broadcasted_iota(jnp.int32, qh.shape, 1)
        zero = jnp.zeros_like(qh)
        return jnp.where(lane < D_QK, qh, zero), jnp.where(lane >= D_QK, qh, zero)

    def steps(kref, vtref):
        for h in range(D_HEADS):
            q1, q2 = q_parts(h)
            k = kref[:, h * pair:(h + 1) * pair]
            v_t = vtref[0, h * D_V:(h + 1) * D_V, :]
            _online_step(q1, k, v_t, m_ref, l_ref, acc_ref, 2 * h)
            _online_step(q2, k, v_t, m_ref, l_ref, acc_ref, 2 * h + 1)

    @pl.when(ki == 0)
    def _():
        m_ref[...] = jnp.full(m_ref.shape, NEG_INF, F32)
        l_ref[...] = jnp.zeros(l_ref.shape, F32)
        acc_ref[...] = jnp.zeros(acc_ref.shape, F32)
        steps(kx_ref, vxt_ref)

    steps(k_ref, vt_ref)

    @pl.when(ki == pl.num_programs(2) - 1)
    def _():
        lam = lam_ref[0]
        for h in range(D_HEADS):
            o = acc_ref[2 * h] / l_ref[2 * h] - lam * (acc_ref[2 * h + 1] / l_ref[2 * h + 1])
            o = o * lax.rsqrt(jnp.mean(o * o, 0, keepdims=True) + 1e-6) * subln_ref[...] * out_scale
            o_ref[0, h * D_V:(h + 1) * D_V, :] = o.astype(o_ref.dtype)


def _diff_attn(lam, dq, dk, dvt, dkx, dvxt, subln_col, out_scale, batch, seq, ctx_len, tq, tk):
    nq, nk = seq // tq, seq // tk
    qw = 2 * D_HEADS * D_QK
    vw = D_HEADS * D_V
    return pl.pallas_call(
        functools.partial(_diff_kernel, out_scale=out_scale),
        grid=(batch, nq, nk),
        in_specs=[
            pl.BlockSpec(memory_space=pltpu.SMEM),
            pl.BlockSpec((tq, qw), lambda b, i, j: (b * nq + i, 0)),
            pl.BlockSpec((tk, qw), lambda b, i, j: (b * nk + j, 0)),
            pl.BlockSpec((1, vw, tk), lambda b, i, j: (b, 0, j)),
            pl.BlockSpec((ctx_len, qw), lambda b, i, j: (b, 0)),
            pl.BlockSpec((1, vw, ctx_len), lambda b, i, j: (b, 0, 0)),
            pl.BlockSpec((D_V, 1), lambda b, i, j: (0, 0)),
        ],
        out_specs=pl.BlockSpec((1, vw, tq), lambda b, i, j: (b, 0, i)),
        out_shape=jax.ShapeDtypeStruct((batch, vw, seq), BF16),
        scratch_shapes=[pltpu.VMEM((2 * D_HEADS, 1, tq), F32), pltpu.VMEM((2 * D_HEADS, 1, tq), F32),
                        pltpu.VMEM((2 * D_HEADS, D_V, tq), F32)],
        compiler_params=_params(("arbitrary", "arbitrary", "arbitrary")),
        name="diff_attn",
    )(lam, dq, dk, dvt, dkx, dvxt, subln_col)


def _rope_tables(n_tok, dim, lane_lo):
    t = jnp.arange(n_tok)
    pos_r = (t // GRID_W).astype(F32)
    pos_c = (t % GRID_W).astype(F32)
    quarter = dim // 4
    inv = ROPE_BASE ** (-jnp.arange(quarter, dtype=F32) / quarter)
    ang_r = pos_r[:, None] * inv
    ang_c = pos_c[:, None] * inv
    ang = jnp.concatenate([ang_r, ang_r, ang_c, ang_c], -1)
    sign = jnp.tile(jnp.concatenate([-jnp.ones(quarter), jnp.ones(quarter)]), 2).astype(F32)
    cos, sin = jnp.cos(ang), jnp.sin(ang) * sign
    reps = (LANES - lane_lo) // dim
    cos = jnp.concatenate([jnp.ones((n_tok, lane_lo), F32)] + [cos] * reps, axis=1)
    sin = jnp.concatenate([jnp.zeros((n_tok, lane_lo), F32)] + [sin] * reps, axis=1)
    return cos, sin


def _cd_weights(w_in, w_uq, w_ukv):
    d = w_in.shape[0]
    s0 = C_Q_RANK
    s1 = s0 + C_KV_RANK
    s2 = s1 + C_ROPE
    s3 = s2 + 2 * D_HEADS * D_QK
    s4 = s3 + 2 * D_HEADS * D_QK
    kr = jnp.zeros((d, LANES), F32).at[:, C_NOPE:C_NOPE + C_ROPE].set(w_in[:, s1:s2])
    w_in_p = jnp.concatenate([w_in[:, :s1], w_in[:, s2:s3], w_in[:, s3:s4], w_in[:, s4:], kr], axis=1)
    dqk = C_NOPE + C_ROPE
    wq = w_uq.reshape(C_Q_RANK, C_HEADS, dqk)
    wq = jnp.pad(wq, ((0, 0), (0, 0), (0, C_HEAD_PAD - dqk))).reshape(C_Q_RANK, C_HEADS * C_HEAD_PAD)
    wkv = w_ukv.reshape(C_KV_RANK, C_HEADS, C_NOPE + C_V)
    wk = jnp.pad(wkv[:, :, :C_NOPE], ((0, 0), (0, 0), (0, C_HEAD_PAD - C_NOPE)))
    wk = wk.reshape(C_KV_RANK, C_HEADS * C_HEAD_PAD)
    wv = wkv[:, :, C_NOPE:].reshape(C_KV_RANK, C_HEADS * C_V)
    return w_in_p.astype(BF16), wq.astype(BF16), jnp.concatenate([wk, wv], axis=1).astype(BF16)


def _moe_layer(h2, logits, x1, w_gu, w_dn, mod, gid, lng, lnb, alpha, tm):
    n_tok = h2.shape[0]
    wts, dest, row_tok, block_e = _route(logits, n_tok)
    xs = jnp.take(h2, row_tok, axis=0)
    ys = _moe_experts(block_e, xs, w_gu, w_dn)
    ya = jnp.take(ys, dest[:, 0], axis=0)
    yb = jnp.take(ys, dest[:, 1], axis=0)
    return _post_moe(x1, ya, yb, wts.astype(F32), mod, gid, lng, lnb, alpha, tm)


def kernel(x, c, ctx, c_ctx, w_ada, b_ada, ln_g, ln_b, ab_w_in, a_sink, b_rpb, ab_w_out, cd_w_in, c_q_norm, c_w_uq, c_kv_norm, c_w_ukv, d_lambda, d_subln, cd_w_out, w_group, b_group, w_exp_router, b_exp_router, w_gate_up, w_down):
    batch, seq, d = x.shape
    ctx_len = ctx.shape[1]
    depth = w_ada.shape[0]
    alpha = (2 * depth) ** 0.25
    n_lat = batch * seq
    n_ctx = batch * ctx_len
    tm = 512
    tq = tk = 512
    assert depth == 2 and seq % NA_Q == 0 and seq % tq == 0 and ctx_len % 8 == 0 and n_ctx % tm == 0
    lat_tiles = seq // tm

    def gid_lat(i):
        return i // lat_tiles

    def gid_ctx(i):
        return batch

    def gid_all(i):
        return jnp.minimum(i // lat_tiles, batch)

    c_all = jnp.zeros((8, d), F32).at[:batch].set(c).at[batch].set(c_ctx)
    mod = _ada(c_all, w_ada, b_ada).reshape(depth, 8, 6, d)

    cos64, sin64 = _rope_tables(seq, HEAD_DIM, 0)
    cos32, sin32 = _rope_tables(seq, C_ROPE, C_NOPE)

    def router_weights(l):
        wr = jnp.zeros((d, ROUTER_PAD), F32)
        wr = wr.at[:, :N_GROUPS].set(w_group[l]).at[:, N_GROUPS:N_GROUPS + N_EXPERTS].set(w_exp_router[l])
        br = jnp.zeros((1, ROUTER_PAD), F32)
        br = br.at[0, :N_GROUPS].set(b_group[l]).at[0, N_GROUPS:N_GROUPS + N_EXPERTS].set(b_exp_router[l])
        hi = wr.astype(BF16)
        lo = (wr - hi.astype(F32)).astype(BF16)
        return hi, lo, br

    x2d = x.reshape(n_lat, d)
    ctx2d = ctx.reshape(n_ctx, d)

    l = 0
    w_in_bf = ab_w_in[0].astype(BF16)
    aq, ak, av, bq, bk, bv = _proj_ab(x2d, 0, n_lat, mod[l], gid_lat, w_in_bf, cos64, sin64,
                                      lat_tiles, True, tm)
    aqx, akx, avx, bqx, bkx, bvx = _proj_ab(ctx2d, 0, n_ctx, mod[l], gid_ctx, w_in_bf, cos64, sin64,
                                            1, False, tm)
    sink = a_sink[0].astype(F32)
    oa = _window_attn(sink, aq, ak, av, akx, avx, batch, seq, ctx_len)
    tables = _na_bias_tables(b_rpb[0], seq // GRID_W)
    ob = _na_attn(bq, bk, bv, bkx, bvx, tables, batch, seq, ctx_len)
    oax, obx = _ctx_ab_attn(sink, aqx, akx, avx, bqx, bkx, bvx, batch, ctx_len)

    n_all = n_lat + n_ctx
    w_out_bf = ab_w_out[0].astype(BF16)
    lng, lnb = ln_g[l, 0][None], ln_b[l, 0][None]
    wr_hi, wr_lo, br = router_weights(l)
    x1, h2, logits = _post_attn((oa, ob, x2d), (oax, obx, ctx2d), w_out_bf, mod[l], gid_all,
                                lng, lnb, wr_hi, wr_lo, br, n_lat, alpha, tm)
    xall = _moe_layer(h2, logits, x1, w_gate_up[l], w_down[l], mod[l], gid_all,
                      ln_g[l, 1][None], ln_b[l, 1][None], alpha, tm)

    l = 1
    lam_init = 0.8 - 0.6 * math.exp(-0.3 * l)
    lp = d_lambda[0].astype(F32)
    lam = (jnp.exp(jnp.sum(lp[0] * lp[1])) - jnp.exp(jnp.sum(lp[2] * lp[3])) + lam_init).reshape(1)
    w_in_p, wuq_p, wukv_p = _cd_weights(cd_w_in[0], c_w_uq[0], c_w_ukv[0])
    qn, kvn = c_q_norm[0][None].astype(F32), c_kv_norm[0][None].astype(F32)
    tabs = (cos64, sin64, cos32, sin32)
    qc, kc, vc, dq, dk, dv = _proj_cd(xall, 0, n_lat, mod[l], gid_lat, w_in_p, qn, wuq_p, kvn, wukv_p,
                                      tabs, lat_tiles, True, tm)
    _, kxc, vxc, _, dkx, dvx = _proj_cd(xall, n_lat, n_ctx, mod[l], gid_ctx, w_in_p, qn, wuq_p, kvn,
                                        wukv_p, tabs, 1, False, tm)

    def t_lat(a):
        return a.reshape(batch, seq, a.shape[1]).transpose(0, 2, 1)

    def t_ctx(a):
        return a.reshape(batch, ctx_len, a.shape[1]).transpose(0, 2, 1)

    oc_t = _mla_attn(qc, kc, t_lat(vc), kxc, t_ctx(vxc), batch, seq, ctx_len, tq, tk)
    od_t = _diff_attn(lam, dq, dk, t_lat(dv), dkx, t_ctx(dvx), d_subln[0].astype(F32).reshape(D_V, 1),
                      1.0 - lam_init, batch, seq, ctx_len, tq, tk)
    oc = oc_t.transpose(0, 2, 1).reshape(n_lat, -1)
    od = od_t.transpose(0, 2, 1).reshape(n_lat, -1)

    wr_hi, wr_lo, br = router_weights(l)
    x1, h2, logits = _post_attn((oc, od, xall), None, cd_w_out[0].astype(BF16), mod[l], gid_lat,
                                ln_g[l, 0][None], ln_b[l, 0][None], wr_hi, wr_lo, br,
                                n_lat, alpha, tm)
    out = _moe_layer(h2, logits, x1, w_gate_up[l], w_down[l], mod[l], gid_lat,
                     ln_g[l, 1][None], ln_b[l, 1][None], alpha, tm)
    return out.reshape(batch, seq, d)
```

```python
import functools
import math

import jax
import jax.numpy as jnp
from jax import lax
from jax.experimental import pallas as pl
from jax.experimental.pallas import tpu as pltpu

F32 = jnp.float32
BF16 = jnp.bfloat16

GRID_W = 64
HEAD_DIM = 64
ROPE_BASE = 10000.0
NEG_INF = -1e30
LOG2E = 1.4426950408889634

A_HEADS = 8
A_KV_HEADS = 2
A_WINDOW = 128
A_BLOCK = 128
B_HEADS = 8
NA_ROWS = 8
NA_COLS = 16
C_HEADS = 8
C_Q_RANK = 384
C_KV_RANK = 256
C_NOPE = 64
C_ROPE = 32
C_V = 64
D_HEADS = 4
D_QK = 64
D_V = 128
N_GROUPS = 4
EXP_PER_GROUP = 8
N_EXPERTS = N_GROUPS * EXP_PER_GROUP
TOP_K = 2
MOE_BLOCK = 128
ROUTER_PAD = 128

LANES = 128
VMEM_LIMIT = 56 * 1024 * 1024


def _params(sem):
    return pltpu.CompilerParams(dimension_semantics=sem, vmem_limit_bytes=VMEM_LIMIT)


def _dot(a, b):
    return jnp.dot(a, b, preferred_element_type=F32)


def _dot_nt(a, b):
    return lax.dot_general(a, b, (((1,), (1,)), ((), ())), preferred_element_type=F32)


def _split_bf16(x):
    hi = x.astype(BF16)
    lo = (x - hi.astype(F32)).astype(BF16)
    return hi, lo


def _ada_kernel(c_ref, w_ref, b_ref, o_ref):
    c = c_ref[...]
    a = c / (1.0 + jnp.exp(-c))
    a_hi, a_lo = _split_bf16(a)
    w_hi, w_lo = _split_bf16(w_ref[0])
    acc = _dot(a_hi, w_hi) + _dot(a_lo, w_hi) + _dot(a_hi, w_lo)
    o_ref[0] = acc + b_ref[0]


def _ada(c_all, w_ada, b_ada):
    depth, d, n = w_ada.shape
    tn = 1536
    return pl.pallas_call(
        _ada_kernel,
        grid=(depth, n // tn),
        in_specs=[
            pl.BlockSpec((8, d), lambda l, j: (0, 0)),
            pl.BlockSpec((1, d, tn), lambda l, j: (l, 0, j)),
            pl.BlockSpec((1, 1, tn), lambda l, j: (l, 0, j)),
        ],
        out_specs=pl.BlockSpec((1, 8, tn), lambda l, j: (l, 0, j)),
        out_shape=jax.ShapeDtypeStruct((depth, 8, n), F32),
        compiler_params=_params(("arbitrary", "arbitrary")),
        name="ada",
    )(c_all, w_ada, b_ada.reshape(depth, 1, n))


def _rope_tile(t, cos, sin_signed, q):
    lane = lax.broadcasted_iota(jnp.int32, t.shape, 1)
    first = (lane & (2 * q - 1)) < q
    up = pltpu.roll(t, LANES - q, 1)
    dn = pltpu.roll(t, q, 1)
    return t * cos + jnp.where(first, up, dn) * sin_signed


def _rope_wide(t, cos, sin_signed, q):
    n = t.shape[1] // LANES
    return jnp.concatenate(
        [_rope_tile(t[:, i * LANES:(i + 1) * LANES], cos, sin_signed, q) for i in range(n)], axis=1)


def _modulate(x_ref, mod_ref, which):
    sh = mod_ref[0, 3 * which:3 * which + 1, :]
    sc = mod_ref[0, 3 * which + 1:3 * which + 2, :]
    return x_ref[...] * (1.0 + sc) + sh


def _proj_ab_kernel(x_ref, mod_ref, w_ref, cos_ref, sin_ref,
                    aq_ref, ak_ref, av_ref, bq_ref, bk_ref, bv_ref, *, rope):
    h = _modulate(x_ref, mod_ref, 0).astype(BF16)
    r = _dot(h, w_ref[...])
    aq = r[:, 0:512]
    ak = r[:, 512:640]
    if rope:
        cos = cos_ref[...]
        sin = sin_ref[...]
        aq = _rope_wide(aq, cos, sin, 16)
        ak = _rope_wide(ak, cos, sin, 16)
    scale = HEAD_DIM ** -0.5
    aq_ref[...] = (aq * scale).astype(BF16)
    ak_ref[...] = ak.astype(BF16)
    av_ref[...] = r[:, 640:768].astype(BF16)
    bq_ref[...] = (r[:, 768:1280] * scale).astype(BF16)
    bk_ref[...] = r[:, 1280:1792].astype(BF16)
    bv_ref[...] = r[:, 1792:2304].astype(BF16)


def _proj_ab(x2d, row0, nrows, mod, gid, w_bf, cos, sin, pos_blocks, rope, tm):
    d = x2d.shape[1]
    n_in = w_bf.shape[1]
    b0 = row0 // tm
    widths = (512, 128, 128, 512, 512, 512)
    return pl.pallas_call(
        functools.partial(_proj_ab_kernel, rope=rope),
        grid=(nrows // tm,),
        in_specs=[
            pl.BlockSpec((tm, d), lambda i: (b0 + i, 0)),
            pl.BlockSpec((1, 6, d), lambda i: (gid(i), 0, 0)),
            pl.BlockSpec((d, n_in), lambda i: (0, 0)),
            pl.BlockSpec((tm, LANES), lambda i: (i % pos_blocks, 0)),
            pl.BlockSpec((tm, LANES), lambda i: (i % pos_blocks, 0)),
        ],
        out_specs=[pl.BlockSpec((tm, w), lambda i: (i, 0)) for w in widths],
        out_shape=[jax.ShapeDtypeStruct((nrows, w), BF16) for w in widths],
        compiler_params=_params(("arbitrary",)),
        name="proj_ab",
    )(x2d, mod, w_bf, cos, sin)


def _softmax_pv(parts, sink_col):
    m = parts[0][0].max(-1, keepdims=True)
    for s, _ in parts[1:]:
        m = jnp.maximum(m, s.max(-1, keepdims=True))
    if sink_col is not None:
        m = jnp.maximum(m, sink_col)
    denom = None if sink_col is None else jnp.exp(sink_col - m)
    o = None
    for s, v in parts:
        e = jnp.exp(s - m)
        d = e.sum(-1, keepdims=True)
        denom = d if denom is None else denom + d
        pv = _dot(e.astype(BF16), v)
        o = pv if o is None else o + pv
    return o / denom


def _win_kernel(sink_ref, q_ref, kp_ref, kc_ref, kn_ref, vp_ref, vc_ref, vn_ref,
                kx_ref, vx_ref, o_ref, *, seq):
    n = pl.program_id(1)
    blk = A_BLOCK
    g_sz = A_HEADS // A_KV_HEADS
    qi = lax.broadcasted_iota(jnp.int32, (blk, 3 * blk), 0)
    kj = lax.broadcasted_iota(jnp.int32, (blk, 3 * blk), 1)
    rel = kj - blk - qi
    kpos = n * blk + kj - blk
    valid = (jnp.abs(rel) <= A_WINDOW) & (kpos >= 0) & (kpos < seq)
    valid = jnp.concatenate([valid] * g_sz, axis=0)
    q = q_ref[...]
    k_all = jnp.concatenate([kp_ref[...], kc_ref[...], kn_ref[...]], axis=0)
    v_all = jnp.concatenate([vp_ref[...], vc_ref[...], vn_ref[...]], axis=0)
    kx = kx_ref[...]
    vx = vx_ref[...]
    outs = []
    for g in range(A_KV_HEADS):
        lo, hi = g * HEAD_DIM, (g + 1) * HEAD_DIM
        heads = [g * g_sz + i for i in range(g_sz)]
        qs = jnp.concatenate([q[:, h * HEAD_DIM:(h + 1) * HEAD_DIM] for h in heads], axis=0)
        sink = jnp.concatenate(
            [jnp.full((blk, 1), sink_ref[h], F32) for h in heads], axis=0)
        s_loc = jnp.where(valid, _dot_nt(qs, k_all[:, lo:hi]), NEG_INF)
        s_ctx = _dot_nt(qs, kx[:, lo:hi])
        o = _softmax_pv([(s_ctx, vx[:, lo:hi]), (s_loc, v_all[:, lo:hi])], sink)
        outs.extend(o[i * blk:(i + 1) * blk] for i in range(g_sz))
    o_ref[...] = jnp.concatenate(outs, axis=1).astype(o_ref.dtype)


def _window_attn(sink, aq, ak, av, akx, avx, batch, seq, ctx_len):
    nblk = seq // A_BLOCK
    kvw = A_KV_HEADS * HEAD_DIM
    qw = A_HEADS * HEAD_DIM

    def nb(delta):
        return lambda b, n: (b * nblk + jnp.clip(n + delta, 0, nblk - 1), 0)

    kv_specs = [pl.BlockSpec((A_BLOCK, kvw), nb(d)) for d in (-1, 0, 1)]
    return pl.pallas_call(
        functools.partial(_win_kernel, seq=seq),
        grid=(batch, nblk),
        in_specs=[pl.BlockSpec(memory_space=pltpu.SMEM),
                  pl.BlockSpec((A_BLOCK, qw), lambda b, n: (b * nblk + n, 0))]
        + kv_specs + kv_specs
        + [pl.BlockSpec((ctx_len, kvw), lambda b, n: (b, 0)),
           pl.BlockSpec((ctx_len, kvw), lambda b, n: (b, 0))],
        out_specs=pl.BlockSpec((A_BLOCK, qw), lambda b, n: (b * nblk + n, 0)),
        out_shape=jax.ShapeDtypeStruct((batch * seq, qw), BF16),
        compiler_params=_params(("arbitrary", "arbitrary")),
        name="window_attn",
    )(sink, aq, ak, ak, ak, av, av, av, akx, avx)


NA_QROWS = 8
NA_Q = NA_QROWS * GRID_W
NA_KBLK = 4 * GRID_W
NA_K = 4 * NA_KBLK


def _na_bias_tables(rpb, rows):
    nj = rows // NA_QROWS
    kr = min(NA_ROWS, rows)
    col = jnp.arange(GRID_W)
    col_start = jnp.clip(col - NA_COLS // 2, 0, GRID_W - NA_COLS)
    cvalid = (col[None, :] >= col_start[:, None]) & (col[None, :] < col_start[:, None] + NA_COLS)
    dc = jnp.clip(col[None, :] - col[:, None], -(NA_COLS - 1), NA_COLS - 1) + (NA_COLS - 1)
    tabs = []
    for j in (0, min(1, nj - 1), nj - 1):
        r = j * NA_QROWS + jnp.arange(NA_QROWS)
        krow = j * NA_QROWS - NA_KBLK // GRID_W + jnp.arange(NA_K // GRID_W)
        start = jnp.clip(r - kr // 2, 0, rows - kr)
        rvalid = (krow[None, :] >= start[:, None]) & (krow[None, :] < start[:, None] + kr)
        dr = jnp.clip(krow[None, :] - r[:, None] + (NA_ROWS - 1), 0, 2 * NA_ROWS - 2)
        bias = rpb.astype(F32)[:, dr][:, :, :, dc]
        ok = rvalid[:, :, None, None] & cvalid[None, None, :, :]
        t = jnp.where(ok[None], bias, NEG_INF)
        t = t.transpose(0, 1, 3, 2, 4).reshape(rpb.shape[0], NA_Q, NA_K)
        tabs.append(t)
    return jnp.stack(tabs)


def _na_kernel(q_ref, k0, k1, k2, k3, v0, v1, v2, v3, kx_ref, vx_ref, t_ref, o_ref):
    q = q_ref[...]
    k_all = jnp.concatenate([k0[...], k1[...], k2[...], k3[...]], axis=0)
    v_all = jnp.concatenate([v0[...], v1[...], v2[...], v3[...]], axis=0)
    kx = kx_ref[...]
    vx = vx_ref[...]
    outs = []
    for hh in range(LANES // HEAD_DIM):
        lo, hi = hh * HEAD_DIM, (hh + 1) * HEAD_DIM
        qh = q[:, lo:hi]
        s_loc = _dot_nt(qh, k_all[:, lo:hi]) + t_ref[0, hh]
        s_ctx = _dot_nt(qh, kx[:, lo:hi])
        outs.append(_softmax_pv([(s_ctx, vx[:, lo:hi]), (s_loc, v_all[:, lo:hi])], None))
    o_ref[...] = jnp.concatenate(outs, axis=1).astype(o_ref.dtype)


def _na_attn(bq, bk, bv, bkx, bvx, tables, batch, seq, ctx_len):
    nj = seq // NA_Q
    nkb = seq // NA_KBLK
    hp = B_HEADS * HEAD_DIM // LANES
    per = LANES // HEAD_DIM

    def kb(t):
        return lambda p, b, j: (b * nkb + jnp.clip(2 * j - 1 + t, 0, nkb - 1), p)

    def variant(p, b, j):
        return (jnp.where(j == 0, 0, jnp.where(j == nj - 1, 2, 1)), p, 0, 0)

    kv_specs = [pl.BlockSpec((NA_KBLK, LANES), kb(t)) for t in range(4)]
    return pl.pallas_call(
        _na_kernel,
        grid=(hp, batch, nj),
        in_specs=[pl.BlockSpec((NA_Q, LANES), lambda p, b, j: (b * nj + j, p))]
        + kv_specs + kv_specs
        + [pl.BlockSpec((ctx_len, LANES), lambda p, b, j: (b, p)),
           pl.BlockSpec((ctx_len, LANES), lambda p, b, j: (b, p)),
           pl.BlockSpec((1, per, NA_Q, NA_K), variant)],
        out_specs=pl.BlockSpec((NA_Q, LANES), lambda p, b, j: (b * nj + j, p)),
        out_shape=jax.ShapeDtypeStruct((batch * seq, B_HEADS * HEAD_DIM), BF16),
        compiler_params=_params(("arbitrary", "arbitrary", "arbitrary")),
        name="na_attn",
    )(bq, bk, bk, bk, bk, bv, bv, bv, bv, bkx, bvx, tables)


def _ctx_ab_kernel(sink_ref, aq_ref, ak_ref, av_ref, bq_ref, bk_ref, bv_ref, oa_ref, ob_ref):
    ctx_len = aq_ref.shape[0]
    g_sz = A_HEADS // A_KV_HEADS
    aq = aq_ref[...]
    ak = ak_ref[...]
    av = av_ref[...]
    outs = []
    for g in range(A_KV_HEADS):
        lo, hi = g * HEAD_DIM, (g + 1) * HEAD_DIM
        heads = [g * g_sz + i for i in range(g_sz)]
        qs = jnp.concatenate([aq[:, h * HEAD_DIM:(h + 1) * HEAD_DIM] for h in heads], axis=0)
        sink = jnp.concatenate(
            [jnp.full((ctx_len, 1), sink_ref[h], F32) for h in heads], axis=0)
        o = _softmax_pv([(_dot_nt(qs, ak[:, lo:hi]), av[:, lo:hi])], sink)
        outs.extend(o[i * ctx_len:(i + 1) * ctx_len] for i in range(g_sz))
    oa_ref[...] = jnp.concatenate(outs, axis=1).astype(oa_ref.dtype)
    bq = bq_ref[...]
    bk = bk_ref[...]
    bv = bv_ref[...]
    outs = []
    for h in range(B_HEADS):
        lo, hi = h * HEAD_DIM, (h + 1) * HEAD_DIM
        outs.append(_softmax_pv([(_dot_nt(bq[:, lo:hi], bk[:, lo:hi]), bv[:, lo:hi])], None))
    ob_ref[...] = jnp.concatenate(outs, axis=1).astype(ob_ref.dtype)


def _ctx_ab_attn(sink, aq, ak, av, bq, bk, bv, batch, ctx_len):
    def spec(w):
        return pl.BlockSpec((ctx_len, w), lambda b: (b, 0))

    return pl.pallas_call(
        _ctx_ab_kernel,
        grid=(batch,),
        in_specs=[pl.BlockSpec(memory_space=pltpu.SMEM)]
        + [spec(a.shape[1]) for a in (aq, ak, av, bq, bk, bv)],
        out_specs=[spec(aq.shape[1]), spec(bq.shape[1])],
        out_shape=[jax.ShapeDtypeStruct(aq.shape, BF16), jax.ShapeDtypeStruct(bq.shape, BF16)],
        compiler_params=_params(("arbitrary",)),
        name="ctx_ab_attn",
    )(sink, aq, ak, av, bq, bk, bv)


def _layer_norm(t, g, b):
    mu = jnp.mean(t, -1, keepdims=True)
    c = t - mu
    var = jnp.mean(c * c, -1, keepdims=True)
    return c * lax.rsqrt(var + 1e-5) * g + b


def _post_attn_kernel(*refs, alpha, lat_tiles, with_ctx):
    n_src = 6 if with_ctx else 3
    srcs = refs[:n_src]
    (w_ref, mod_ref, lng_ref, lnb_ref, wr_hi_ref, wr_lo_ref, br_ref,
     x1_ref, h2_ref, lg_ref) = refs[n_src:]

    def run(o1_ref, o2_ref, x_ref):
        half = o1_ref.shape[1]
        y = _dot(o1_ref[...], w_ref[0:half, :]) + _dot(o2_ref[...], w_ref[half:, :])
        g1 = mod_ref[0, 2:3, :]
        x1 = _layer_norm(alpha * x_ref[...] + g1 * y, lng_ref[...], lnb_ref[...])
        x1_ref[...] = x1
        h2 = x1 * (1.0 + mod_ref[0, 4:5, :]) + mod_ref[0, 3:4, :]
        h_hi, h_lo = _split_bf16(h2)
        h2_ref[...] = h_hi
        wr_hi = wr_hi_ref[...]
        lg_ref[...] = (_dot(h_hi, wr_hi) + _dot(h_lo, wr_hi) + _dot(h_hi, wr_lo_ref[...])
                       + br_ref[...])

    if not with_ctx:
        run(*srcs)
    else:
        i = pl.program_id(0)
        pl.when(i < lat_tiles)(lambda: run(*srcs[:3]))
        pl.when(i >= lat_tiles)(lambda: run(*srcs[3:]))


def _post_attn(lat, ctx_src, w_bf, mod, gid, lng, lnb, wr_hi, wr_lo, br, n_lat, alpha, tm):
    half = lat[0].shape[1]
    d = lat[2].shape[1]
    lat_tiles = n_lat // tm
    n_rows = n_lat + (ctx_src[0].shape[0] if ctx_src is not None else 0)

    def lat_map(i):
        return (jnp.minimum(i, lat_tiles - 1), 0)

    def ctx_map(i):
        return (jnp.maximum(i - lat_tiles, 0), 0)

    def src_specs(index_map):
        return [pl.BlockSpec((tm, half), index_map), pl.BlockSpec((tm, half), index_map),
                pl.BlockSpec((tm, d), index_map)]

    in_specs = src_specs(lat_map)
    args = list(lat)
    if ctx_src is not None:
        in_specs += src_specs(ctx_map)
        args += list(ctx_src)

    def const(shape):
        return pl.BlockSpec(shape, lambda i: (0,) * len(shape))

    in_specs += [const((2 * half, d)),
                 pl.BlockSpec((1, 6, d), lambda i: (gid(i), 0, 0)),
                 const((1, d)), const((1, d)),
                 const((d, ROUTER_PAD)), const((d, ROUTER_PAD)), const((1, ROUTER_PAD))]
    args += [w_bf, mod, lng, lnb, wr_hi, wr_lo, br]
    widths = (d, d, ROUTER_PAD)
    dtypes = (F32, BF16, F32)
    return pl.pallas_call(
        functools.partial(_post_attn_kernel, alpha=alpha, lat_tiles=lat_tiles,
                          with_ctx=ctx_src is not None),
        grid=(n_rows // tm,),
        in_specs=in_specs,
        out_specs=[pl.BlockSpec((tm, w), lambda i: (i, 0)) for w in widths],
        out_shape=[jax.ShapeDtypeStruct((n_rows, w), t) for w, t in zip(widths, dtypes)],
        compiler_params=_params(("arbitrary",)),
        name="post_attn",
    )(*args)


def _moe_kernel(be_ref, xs_ref, wgu_ref, wdn_ref, ys_ref, wgu_bf, wdn_bf):
    i = pl.program_id(0)
    prev = be_ref[jnp.maximum(i - 1, 0)]
    changed = (i == 0) | (be_ref[i] != prev)

    @pl.when(changed)
    def _():
        wgu_bf[...] = wgu_ref[0].astype(BF16)
        wdn_bf[...] = wdn_ref[0].astype(BF16)

    de = wdn_bf.shape[0]
    gu = _dot(xs_ref[...], wgu_bf[...])
    gate = gu[:, :de]
    up = gu[:, de:]
    act = gate / (1.0 + jnp.exp(-gate)) * up
    ys_ref[...] = _dot(act.astype(BF16), wdn_bf[...])


def _moe_experts(block_e, xs, w_gu, w_dn):
    rows, d = xs.shape
    nb = rows // MOE_BLOCK
    de = w_dn.shape[1]
    grid_spec = pltpu.PrefetchScalarGridSpec(
        num_scalar_prefetch=1,
        grid=(nb,),
        in_specs=[
            pl.BlockSpec((MOE_BLOCK, d), lambda i, be: (i, 0)),
            pl.BlockSpec((1, d, 2 * de), lambda i, be: (be[i], 0, 0)),
            pl.BlockSpec((1, de, d), lambda i, be: (be[i], 0, 0)),
        ],
        out_specs=pl.BlockSpec((MOE_BLOCK, d), lambda i, be: (i, 0)),
        scratch_shapes=[pltpu.VMEM((d, 2 * de), BF16), pltpu.VMEM((de, d), BF16)],
    )
    return pl.pallas_call(
        _moe_kernel,
        grid_spec=grid_spec,
        out_shape=jax.ShapeDtypeStruct((rows, d), F32),
        compiler_params=_params(("arbitrary",)),
        name="moe_experts",
    )(block_e, xs, w_gu, w_dn)


def _route(logits, n_tok):
    g_prob = jax.nn.softmax(logits[:, :N_GROUPS], axis=-1)
    g_idx = jnp.argmax(g_prob, axis=-1)
    g_val = jnp.max(g_prob, axis=-1)
    e_logits = logits[:, N_GROUPS:N_GROUPS + N_EXPERTS].reshape(n_tok, N_GROUPS, EXP_PER_GROUP)
    e_sel = jnp.take_along_axis(e_logits, g_idx[:, None, None], axis=1)[:, 0]
    e_prob = jax.nn.softmax(e_sel, axis=-1)
    e_val, e_idx = lax.top_k(e_prob, TOP_K)
    wts = g_val[:, None] * e_val / e_val.sum(-1, keepdims=True)
    experts = (g_idx[:, None] * EXP_PER_GROUP + e_idx).astype(jnp.int32)

    n_asg = n_tok * TOP_K
    flat_e = experts.reshape(-1)
    order = jnp.argsort(flat_e).astype(jnp.int32)
    rank = jnp.argsort(order).astype(jnp.int32)
    counts = (flat_e[:, None] == jnp.arange(N_EXPERTS)[None, :]).sum(0).astype(jnp.int32)
    starts = jnp.cumsum(counts) - counts
    pcounts = ((counts + MOE_BLOCK - 1) // MOE_BLOCK) * MOE_BLOCK
    pends = jnp.cumsum(pcounts)
    pstarts = pends - pcounts
    dest = (pstarts[flat_e] + rank - starts[flat_e]).reshape(n_tok, TOP_K)
    nb = -(-n_asg // MOE_BLOCK) + N_EXPERTS
    block_e = jnp.minimum(
        jnp.searchsorted(pends, jnp.arange(nb) * MOE_BLOCK, side='right'), N_EXPERTS - 1).astype(jnp.int32)
    r = jnp.arange(nb * MOE_BLOCK)
    e_r = block_e[r // MOE_BLOCK]
    off = r - pstarts[e_r]
    valid = off < counts[e_r]
    src = order[jnp.clip(starts[e_r] + off, 0, n_asg - 1)] // TOP_K
    row_tok = jnp.where(valid, src, 0).astype(jnp.int32)
    return wts, dest, row_tok, block_e


def _post_moe_kernel(x_ref, ya_ref, yb_ref, w_ref, mod_ref, lng_ref, lnb_ref, o_ref, *, alpha):
    w = w_ref[...]
    y = w[:, 0:1] * ya_ref[...] + w[:, 1:2] * yb_ref[...]
    g2 = mod_ref[0, 5:6, :]
    o_ref[...] = _layer_norm(alpha * x_ref[...] + g2 * y, lng_ref[...], lnb_ref[...])


def _post_moe(x1, ya, yb, wts, mod, gid, lng, lnb, alpha, tm):
    rows, d = x1.shape
    row = pl.BlockSpec((tm, d), lambda i: (i, 0))
    vec = pl.BlockSpec((1, d), lambda i: (0, 0))
    return pl.pallas_call(
        functools.partial(_post_moe_kernel, alpha=alpha),
        grid=(rows // tm,),
        in_specs=[row, row, row,
                  pl.BlockSpec((tm, TOP_K), lambda i: (i, 0)),
                  pl.BlockSpec((1, 6, d), lambda i: (gid(i), 0, 0)),
                  vec, vec],
        out_specs=row,
        out_shape=jax.ShapeDtypeStruct((rows, d), F32),
        compiler_params=_params(("arbitrary",)),
        name="post_moe",
    )(x1, ya, yb, wts, mod, lng, lnb)


CD_COLS = dict(cq=(0, 384), ckv=(384, 640), dq=(640, 1152), dk=(1152, 1664), dv=(1664, 2176),
               kr=(2176, 2304))
C_HEAD_PAD = LANES


def _rms(t, g, eps=1e-6):
    return t * lax.rsqrt(jnp.mean(t * t, -1, keepdims=True) + eps) * g


def _proj_cd_kernel(x_ref, mod_ref, w_ref, qn_ref, wuq_ref, kvn_ref, wukv_ref,
                    cos64_ref, sin64_ref, cos32_ref, sin32_ref,
                    qc_ref, kc_ref, vc_ref, dq_ref, dk_ref, dv_ref):
    h = _modulate(x_ref, mod_ref, 0).astype(BF16)
    r = _dot(h, w_ref[...])

    def cols(name):
        lo, hi = CD_COLS[name]
        return r[:, lo:hi]

    q = _dot(_rms(cols('cq'), qn_ref[...]).astype(BF16), wuq_ref[...])
    kv = _dot(_rms(cols('ckv'), kvn_ref[...]).astype(BF16), wukv_ref[...])
    c32, s32 = cos32_ref[...], sin32_ref[...]
    c64, s64 = cos64_ref[...], sin64_ref[...]
    q = _rope_wide(q, c32, s32, C_ROPE // 4)
    kr = _rope_tile(cols('kr'), c32, s32, C_ROPE // 4)
    dq = _rope_wide(cols('dq'), c64, s64, D_QK // 4)
    dk = _rope_wide(cols('dk'), c64, s64, D_QK // 4)
    kw = C_HEADS * C_HEAD_PAD
    qc_ref[0] = (q * ((C_NOPE + C_ROPE) ** -0.5 * LOG2E)).astype(BF16)
    kc_ref[0] = (kv[:, :kw] + jnp.concatenate([kr] * C_HEADS, axis=1)).astype(BF16)
    vc_ref[0] = kv[:, kw:].astype(BF16)
    dq_ref[0] = (dq * (D_QK ** -0.5 * LOG2E)).astype(BF16)
    dk_ref[0] = dk.astype(BF16)
    dv_ref[0] = cols('dv').astype(BF16)


def _proj_cd(xall, mod, w_bf, qn, wuq, kvn, wukv, tabs, batch, seq, ctx_len, tm):
    d = xall.shape[1]
    lat_t, ctx_t = seq // tm, ctx_len // tm
    n_keys = seq + ctx_len
    widths = (C_HEADS * C_HEAD_PAD, C_HEADS * C_HEAD_PAD, C_HEADS * C_V,
              2 * D_HEADS * D_QK, 2 * D_HEADS * D_QK, D_HEADS * D_V)

    def src(b, t):
        return (jnp.where(t < lat_t, b * lat_t + t, batch * lat_t + b * ctx_t + t - lat_t), 0)

    def full(a):
        return pl.BlockSpec(a.shape, lambda b, t: (0,) * a.ndim)

    tab = pl.BlockSpec((tm, LANES), lambda b, t: (t, 0))
    return pl.pallas_call(
        _proj_cd_kernel,
        grid=(batch, lat_t + ctx_t),
        in_specs=[pl.BlockSpec((tm, d), src),
                  pl.BlockSpec((1, 6, d), lambda b, t: (jnp.where(t < lat_t, b, batch), 0, 0)),
                  full(w_bf), full(qn), full(wuq), full(kvn), full(wukv), tab, tab, tab, tab],
        out_specs=[pl.BlockSpec((1, tm, w), lambda b, t: (b, t, 0)) for w in widths],
        out_shape=[jax.ShapeDtypeStruct((batch, n_keys, w), BF16) for w in widths],
        compiler_params=_params(("arbitrary", "arbitrary")),
        name="proj_cd",
    )(xall, mod, w_bf, qn, wuq, kvn, wukv, *tabs)


DENSE_SHIFT = 80.0
DENSE_L_MIN = 2.0 ** -60
DENSE_L_MAX = 2.0 ** 120
ONES_ROWS = 16


def _sq_norm_row(x):
    ones = jnp.ones((8, x.shape[1]), BF16)
    return _dot_nt(ones, (x * x).astype(BF16))[0:1] * 1.02


def _key_norm_max(kchunk, n_chunks, masks):
    def body(j, mx):
        kf = kchunk(j).astype(F32)
        ksq = kf * kf
        out = []
        for msk, cur in zip(masks, mx):
            part = ksq if msk is None else jnp.where(msk, ksq, 0.0)
            rn = jnp.sum(part, axis=1, keepdims=True)
            out.append(jnp.maximum(cur, jnp.max(rn, axis=0, keepdims=True)))
        return tuple(out)

    mx = lax.fori_loop(0, n_chunks, body, tuple(jnp.zeros((1, 1), F32) for _ in masks))
    return [jnp.sqrt(v) * 1.01 for v in mx]


def _safe_online(q, kchunk, vt_ref, n_chunks, m_scr, l_scr, acc_scr):
    m_scr[...] = jnp.full(m_scr.shape, NEG_INF, F32)
    l_scr[...] = jnp.zeros(l_scr.shape, F32)
    acc_scr[...] = jnp.zeros(acc_scr.shape, F32)

    def body(j, carry):
        s = _dot_nt(kchunk(j), q)
        m_prev = m_scr[...]
        m_new = jnp.maximum(m_prev, s.max(0, keepdims=True))
        alpha = jnp.exp2(m_prev - m_new)
        p = jnp.exp2(s - m_new)
        l_scr[...] = alpha * l_scr[...] + p.sum(0, keepdims=True)
        acc_scr[...] = alpha * acc_scr[...] + _dot(vt_ref[0, 0, j], p.astype(BF16))
        m_scr[...] = m_new
        return carry

    lax.fori_loop(0, n_chunks, body, 0)


def _denominators_ok(*ls):
    ok = None
    for l in ls:
        cur = (l > DENSE_L_MIN) & (l < DENSE_L_MAX)
        ok = cur if ok is None else ok & cur
    return jnp.max(jnp.where(ok, 0.0, 1.0)) == 0.0


def _mla_kernel(q_ref, k_ref, vt_ref, o_ref, kmax_scr, acc_scr, m_scr, l_scr,
                *, n_chunks, tk, unroll, depth):
    def kchunk(j):
        return k_ref[0, pl.ds(pl.multiple_of(j * tk, tk), tk), :]

    @pl.when(pl.program_id(2) == 0)
    def _():
        (kmax,) = _key_norm_max(kchunk, n_chunks, [None])
        kmax_scr[...] = jnp.broadcast_to(kmax, kmax_scr.shape)

    q = q_ref[0]
    m_row = jnp.sqrt(_sq_norm_row(q.astype(F32))) * kmax_scr[0:1, 0:1] - DENSE_SHIFT
    acc_scr[...] = jnp.zeros(acc_scr.shape, F32)

    def body(it, carry):
        j0 = it * unroll
        pend = [_dot_nt(kchunk(j0 + u), q) for u in range(min(depth, unroll))]
        acc = acc_scr[...]
        for u in range(unroll):
            s = pend.pop(0)
            if u + depth < unroll:
                pend.append(_dot_nt(kchunk(j0 + u + depth), q))
            acc = acc + _dot(vt_ref[0, 0, j0 + u], jnp.exp2(s - m_row).astype(BF16))
        acc_scr[...] = acc
        return carry

    lax.fori_loop(0, n_chunks // unroll, body, 0)
    l = acc_scr[C_V:C_V + 1, :]
    o_ref[0] = (acc_scr[0:C_V, :] / l).astype(o_ref.dtype)

    @pl.when(jnp.logical_not(_denominators_ok(l)))
    def _():
        _safe_online(q, kchunk, vt_ref, n_chunks, m_scr, l_scr, acc_scr)
        o_ref[0] = (acc_scr[0:C_V, :] / l_scr[...]).astype(o_ref.dtype)


def _mla_attn(qc, kc, vt, batch, seq, tq, tk, unroll, depth):
    n_keys = kc.shape[1]
    nq, n_chunks = seq // tq, n_keys // tk
    rows = C_V + ONES_ROWS
    return pl.pallas_call(
        functools.partial(_mla_kernel, n_chunks=n_chunks, tk=tk, unroll=unroll, depth=depth),
        grid=(batch, C_HEADS, nq),
        in_specs=[
            pl.BlockSpec((1, tq, C_HEAD_PAD), lambda b, h, i: (b, i, h)),
            pl.BlockSpec((1, n_keys, C_HEAD_PAD), lambda b, h, i: (b, 0, h)),
            pl.BlockSpec((1, 1, n_chunks, rows, tk), lambda b, h, i: (b, h, 0, 0, 0)),
        ],
        out_specs=pl.BlockSpec((1, C_V, tq), lambda b, h, i: (b, h, i)),
        out_shape=jax.ShapeDtypeStruct((batch, C_HEADS * C_V, seq), BF16),
        scratch_shapes=[pltpu.VMEM((8, LANES), F32), pltpu.VMEM((rows, tq), F32),
                        pltpu.VMEM((1, tq), F32), pltpu.VMEM((1, tq), F32)],
        compiler_params=_params(("arbitrary", "arbitrary", "arbitrary")),
        name="mla_attn",
    )(qc, kc, vt)


def _diff_kernel(lam_ref, q_ref, k_ref, vt_ref, subln_ref, o_ref, kmax_scr, acc1_scr, acc2_scr,
                 m_scr, l_scr, *, n_chunks, tk, unroll, depth, out_scale):
    def kchunk(j):
        return k_ref[0, pl.ds(pl.multiple_of(j * tk, tk), tk), :]

    @pl.when(pl.program_id(2) == 0)
    def _():
        klane = lax.broadcasted_iota(jnp.int32, (tk, 2 * D_QK), 1)
        k1, k2 = _key_norm_max(kchunk, n_chunks, [klane < D_QK, klane >= D_QK])
        kmax_scr[0:1, :] = jnp.broadcast_to(k1, (1, LANES))
        kmax_scr[1:2, :] = jnp.broadcast_to(k2, (1, LANES))

    q = q_ref[0]
    qlane = lax.broadcasted_iota(jnp.int32, q.shape, 1)
    zero = jnp.zeros_like(q)
    q1 = jnp.where(qlane < D_QK, q, zero)
    q2 = jnp.where(qlane >= D_QK, q, zero)
    m1 = jnp.sqrt(_sq_norm_row(q1.astype(F32))) * kmax_scr[0:1, 0:1] - DENSE_SHIFT
    m2 = jnp.sqrt(_sq_norm_row(q2.astype(F32))) * kmax_scr[1:2, 0:1] - DENSE_SHIFT
    acc1_scr[...] = jnp.zeros(acc1_scr.shape, F32)
    acc2_scr[...] = jnp.zeros(acc2_scr.shape, F32)

    def scores(j):
        k = kchunk(j)
        return _dot_nt(k, q1), _dot_nt(k, q2)

    def body(it, carry):
        j0 = it * unroll
        pend = [scores(j0 + u) for u in range(min(depth, unroll))]
        a1 = acc1_scr[...]
        a2 = acc2_scr[...]
        for u in range(unroll):
            s1, s2 = pend.pop(0)
            if u + depth < unroll:
                pend.append(scores(j0 + u + depth))
            vt = vt_ref[0, 0, j0 + u]
            a1 = a1 + _dot(vt, jnp.exp2(s1 - m1).astype(BF16))
            a2 = a2 + _dot(vt, jnp.exp2(s2 - m2).astype(BF16))
        acc1_scr[...] = a1
        acc2_scr[...] = a2
        return carry

    lax.fori_loop(0, n_chunks // unroll, body, 0)

    def finish(o1, o2):
        o = o1 - lam_ref[0] * o2
        o = o * lax.rsqrt(jnp.mean(o * o, 0, keepdims=True) + 1e-6) * subln_ref[...] * out_scale
        o_ref[0] = o.astype(o_ref.dtype)

    l1 = acc1_scr[D_V:D_V + 1, :]
    l2 = acc2_scr[D_V:D_V + 1, :]
    finish(acc1_scr[0:D_V, :] / l1, acc2_scr[0:D_V, :] / l2)

    @pl.when(jnp.logical_not(_denominators_ok(l1, l2)))
    def _():
        _safe_online(q1, kchunk, vt_ref, n_chunks, m_scr, l_scr, acc1_scr)
        o1 = acc1_scr[0:D_V, :] / l_scr[...]
        _safe_online(q2, kchunk, vt_ref, n_chunks, m_scr, l_scr, acc2_scr)
        finish(o1, acc2_scr[0:D_V, :] / l_scr[...])


def _diff_attn(lam, dq, dk, vt, subln_col, out_scale, batch, seq, tq, tk, unroll, depth):
    n_keys = dk.shape[1]
    nq, n_chunks = seq // tq, n_keys // tk
    rows = D_V + ONES_ROWS
    pair = 2 * D_QK
    return pl.pallas_call(
        functools.partial(_diff_kernel, n_chunks=n_chunks, tk=tk, unroll=unroll, depth=depth,
                          out_scale=out_scale),
        grid=(batch, D_HEADS, nq),
        in_specs=[
            pl.BlockSpec(memory_space=pltpu.SMEM),
            pl.BlockSpec((1, tq, pair), lambda b, h, i: (b, i, h)),
            pl.BlockSpec((1, n_keys, pair), lambda b, h, i: (b, 0, h)),
            pl.BlockSpec((1, 1, n_chunks, rows, tk), lambda b, h, i: (b, h, 0, 0, 0)),
            pl.BlockSpec((D_V, 1), lambda b, h, i: (0, 0)),
        ],
        out_specs=pl.BlockSpec((1, D_V, tq), lambda b, h, i: (b, h, i)),
        out_shape=jax.ShapeDtypeStruct((batch, D_HEADS * D_V, seq), BF16),
        scratch_shapes=[pltpu.VMEM((8, LANES), F32), pltpu.VMEM((rows, tq), F32),
                        pltpu.VMEM((rows, tq), F32), pltpu.VMEM((1, tq), F32), pltpu.VMEM((1, tq), F32)],
        compiler_params=_params(("arbitrary", "arbitrary", "arbitrary")),
        name="diff_attn",
    )(lam, dq, dk, vt, subln_col)


def _vt_chunks(v, heads, dv, tk):
    batch, n_keys, _ = v.shape
    n_chunks = n_keys // tk
    vt = v.reshape(batch, n_chunks, tk, heads, dv).transpose(0, 3, 1, 4, 2)
    extra = jnp.zeros((batch, heads, n_chunks, ONES_ROWS, tk), v.dtype).at[:, :, :, 0, :].set(1.0)
    return jnp.concatenate([vt, extra], axis=3)


def _rope_tables(n_tok, dim, lane_lo):
    t = jnp.arange(n_tok)
    pos_r = (t // GRID_W).astype(F32)
    pos_c = (t % GRID_W).astype(F32)
    quarter = dim // 4
    inv = ROPE_BASE ** (-jnp.arange(quarter, dtype=F32) / quarter)
    ang_r = pos_r[:, None] * inv
    ang_c = pos_c[:, None] * inv
    ang = jnp.concatenate([ang_r, ang_r, ang_c, ang_c], -1)
    sign = jnp.tile(jnp.concatenate([-jnp.ones(quarter), jnp.ones(quarter)]), 2).astype(F32)
    cos, sin = jnp.cos(ang), jnp.sin(ang) * sign
    reps = (LANES - lane_lo) // dim
    cos = jnp.concatenate([jnp.ones((n_tok, lane_lo), F32)] + [cos] * reps, axis=1)
    sin = jnp.concatenate([jnp.zeros((n_tok, lane_lo), F32)] + [sin] * reps, axis=1)
    return cos, sin


def _cd_weights(w_in, w_uq, w_ukv):
    d = w_in.shape[0]
    s0 = C_Q_RANK
    s1 = s0 + C_KV_RANK
    s2 = s1 + C_ROPE
    s3 = s2 + 2 * D_HEADS * D_QK
    s4 = s3 + 2 * D_HEADS * D_QK
    kr = jnp.zeros((d, LANES), F32).at[:, C_NOPE:C_NOPE + C_ROPE].set(w_in[:, s1:s2])
    w_in_p = jnp.concatenate([w_in[:, :s1], w_in[:, s2:s3], w_in[:, s3:s4], w_in[:, s4:], kr], axis=1)
    dqk = C_NOPE + C_ROPE
    wq = w_uq.reshape(C_Q_RANK, C_HEADS, dqk)
    wq = jnp.pad(wq, ((0, 0), (0, 0), (0, C_HEAD_PAD - dqk))).reshape(C_Q_RANK, C_HEADS * C_HEAD_PAD)
    wkv = w_ukv.reshape(C_KV_RANK, C_HEADS, C_NOPE + C_V)
    wk = jnp.pad(wkv[:, :, :C_NOPE], ((0, 0), (0, 0), (0, C_HEAD_PAD - C_NOPE)))
    wk = wk.reshape(C_KV_RANK, C_HEADS * C_HEAD_PAD)
    wv = wkv[:, :, C_NOPE:].reshape(C_KV_RANK, C_HEADS * C_V)
    return w_in_p.astype(BF16), wq.astype(BF16), jnp.concatenate([wk, wv], axis=1).astype(BF16)


def _moe_layer(h2, logits, x1, w_gu, w_dn, mod, gid, lng, lnb, alpha, tm):
    n_tok = h2.shape[0]
    wts, dest, row_tok, block_e = _route(logits, n_tok)
    xs = jnp.take(h2, row_tok, axis=0)
    ys = _moe_experts(block_e, xs, w_gu, w_dn)
    ya = jnp.take(ys, dest[:, 0], axis=0)
    yb = jnp.take(ys, dest[:, 1], axis=0)
    return _post_moe(x1, ya, yb, wts.astype(F32), mod, gid, lng, lnb, alpha, tm)


def kernel(x, c, ctx, c_ctx, w_ada, b_ada, ln_g, ln_b, ab_w_in, a_sink, b_rpb, ab_w_out, cd_w_in, c_q_norm, c_w_uq, c_kv_norm, c_w_ukv, d_lambda, d_subln, cd_w_out, w_group, b_group, w_exp_router, b_exp_router, w_gate_up, w_down):
    batch, seq, d = x.shape
    ctx_len = ctx.shape[1]
    depth = w_ada.shape[0]
    alpha = (2 * depth) ** 0.25
    n_lat = batch * seq
    n_ctx = batch * ctx_len
    tm = 512
    tq, tk = 512, 256
    assert depth == 2 and seq % NA_Q == 0 and seq % tq == 0 and n_ctx % tm == 0
    assert seq % tk == 0 and ctx_len % tk == 0
    n_chunks = (seq + ctx_len) // tk

    def unroll_for(cap):
        return max(u for u in range(1, cap + 1) if n_chunks % u == 0)
    lat_tiles = seq // tm

    def gid_lat(i):
        return i // lat_tiles

    def gid_ctx(i):
        return batch

    def gid_all(i):
        return jnp.minimum(i // lat_tiles, batch)

    c_all = jnp.zeros((8, d), F32).at[:batch].set(c).at[batch].set(c_ctx)
    mod = _ada(c_all, w_ada, b_ada).reshape(depth, 8, 6, d)

    cos64, sin64 = _rope_tables(seq, HEAD_DIM, 0)
    cos32, sin32 = _rope_tables(seq, C_ROPE, C_NOPE)

    def router_weights(l):
        wr = jnp.zeros((d, ROUTER_PAD), F32)
        wr = wr.at[:, :N_GROUPS].set(w_group[l]).at[:, N_GROUPS:N_GROUPS + N_EXPERTS].set(w_exp_router[l])
        br = jnp.zeros((1, ROUTER_PAD), F32)
        br = br.at[0, :N_GROUPS].set(b_group[l]).at[0, N_GROUPS:N_GROUPS + N_EXPERTS].set(b_exp_router[l])
        hi = wr.astype(BF16)
        lo = (wr - hi.astype(F32)).astype(BF16)
        return hi, lo, br

    x2d = x.reshape(n_lat, d)
    ctx2d = ctx.reshape(n_ctx, d)

    l = 0
    w_in_bf = ab_w_in[0].astype(BF16)
    aq, ak, av, bq, bk, bv = _proj_ab(x2d, 0, n_lat, mod[l], gid_lat, w_in_bf, cos64, sin64,
                                      lat_tiles, True, tm)
    aqx, akx, avx, bqx, bkx, bvx = _proj_ab(ctx2d, 0, n_ctx, mod[l], gid_ctx, w_in_bf, cos64, sin64,
                                            1, False, tm)
    sink = a_sink[0].astype(F32)
    oa = _window_attn(sink, aq, ak, av, akx, avx, batch, seq, ctx_len)
    tables = _na_bias_tables(b_rpb[0], seq // GRID_W)
    ob = _na_attn(bq, bk, bv, bkx, bvx, tables, batch, seq, ctx_len)
    oax, obx = _ctx_ab_attn(sink, aqx, akx, avx, bqx, bkx, bvx, batch, ctx_len)

    n_all = n_lat + n_ctx
    w_out_bf = ab_w_out[0].astype(BF16)
    lng, lnb = ln_g[l, 0][None], ln_b[l, 0][None]
    wr_hi, wr_lo, br = router_weights(l)
    x1, h2, logits = _post_attn((oa, ob, x2d), (oax, obx, ctx2d), w_out_bf, mod[l], gid_all,
                                lng, lnb, wr_hi, wr_lo, br, n_lat, alpha, tm)
    xall = _moe_layer(h2, logits, x1, w_gate_up[l], w_down[l], mod[l], gid_all,
                      ln_g[l, 1][None], ln_b[l, 1][None], alpha, tm)

    l = 1
    lam_init = 0.8 - 0.6 * math.exp(-0.3 * l)
    lp = d_lambda[0].astype(F32)
    lam = (jnp.exp(jnp.sum(lp[0] * lp[1])) - jnp.exp(jnp.sum(lp[2] * lp[3])) + lam_init).reshape(1)
    w_in_p, wuq_p, wukv_p = _cd_weights(cd_w_in[0], c_w_uq[0], c_w_ukv[0])
    qn, kvn = c_q_norm[0][None].astype(F32), c_kv_norm[0][None].astype(F32)
    def with_ctx_identity(cos, sin):
        return (jnp.concatenate([cos, jnp.ones((ctx_len, LANES), F32)]),
                jnp.concatenate([sin, jnp.zeros((ctx_len, LANES), F32)]))

    tabs = with_ctx_identity(cos64, sin64) + with_ctx_identity(cos32, sin32)
    qc, kc, vc, dq, dk, dv = _proj_cd(xall, mod[l], w_in_p, qn, wuq_p, kvn, wukv_p, tabs,
                                      batch, seq, ctx_len, tk)
    oc_t = _mla_attn(qc, kc, _vt_chunks(vc, C_HEADS, C_V, tk), batch, seq, tq, tk, unroll_for(13), 2)
    od_t = _diff_attn(lam, dq, dk, _vt_chunks(dv, D_HEADS, D_V, tk),
                      d_subln[0].astype(F32).reshape(D_V, 1), 1.0 - lam_init, batch, seq, tq, tk,
                      unroll_for(5), 1)
    oc = oc_t.transpose(0, 2, 1).reshape(n_lat, -1)
    od = od_t.transpose(0, 2, 1).reshape(n_lat, -1)

    wr_hi, wr_lo, br = router_weights(l)
    x1, h2, logits = _post_attn((oc, od, xall), None, cd_w_out[0].astype(BF16), mod[l], gid_lat,
                                ln_g[l, 0][None], ln_b[l, 0][None], wr_hi, wr_lo, br,
                                n_lat, alpha, tm)
    out = _moe_layer(h2, logits, x1, w_gate_up[l], w_down[l], mod[l], gid_lat,
                     ln_g[l, 1][None], ln_b[l, 1][None], alpha, tm)
    return out.reshape(batch, seq, d)
```

```python
import functools
import math

import jax
import jax.numpy as jnp
from jax import lax
from jax.experimental import pallas as pl
from jax.experimental.pallas import tpu as pltpu

F32 = jnp.float32
BF16 = jnp.bfloat16

GRID_W = 64
HEAD_DIM = 64
ROPE_BASE = 10000.0
NEG_INF = -1e30
LOG2E = 1.4426950408889634

A_HEADS = 8
A_KV_HEADS = 2
A_WINDOW = 128
A_BLOCK = 128
B_HEADS = 8
NA_ROWS = 8
NA_COLS = 16
C_HEADS = 8
C_Q_RANK = 384
C_KV_RANK = 256
C_NOPE = 64
C_ROPE = 32
C_V = 64
D_HEADS = 4
D_QK = 64
D_V = 128
N_GROUPS = 4
EXP_PER_GROUP = 8
N_EXPERTS = N_GROUPS * EXP_PER_GROUP
TOP_K = 2
MOE_BLOCK = 256
ROUTER_PAD = 128

LANES = 128
VMEM_LIMIT = 56 * 1024 * 1024


def _params(sem):
    return pltpu.CompilerParams(dimension_semantics=sem, vmem_limit_bytes=VMEM_LIMIT)


def _dot(a, b):
    return jnp.dot(a, b, preferred_element_type=F32)


def _dot_nt(a, b):
    return lax.dot_general(a, b, (((1,), (1,)), ((), ())), preferred_element_type=F32)


def _split_bf16(x):
    hi = x.astype(BF16)
    lo = (x - hi.astype(F32)).astype(BF16)
    return hi, lo


def _ada_kernel(c_ref, w_ref, b_ref, o_ref):
    c = c_ref[...]
    a = c / (1.0 + jnp.exp(-c))
    a_hi, a_lo = _split_bf16(a)
    w_hi, w_lo = _split_bf16(w_ref[0])
    acc = _dot(a_hi, w_hi) + _dot(a_lo, w_hi) + _dot(a_hi, w_lo)
    o_ref[0] = acc + b_ref[0]


def _ada(c_all, w_ada, b_ada):
    depth, d, n = w_ada.shape
    tn = 1536
    return pl.pallas_call(
        _ada_kernel,
        grid=(depth, n // tn),
        in_specs=[
            pl.BlockSpec((8, d), lambda l, j: (0, 0)),
            pl.BlockSpec((1, d, tn), lambda l, j: (l, 0, j)),
            pl.BlockSpec((1, 1, tn), lambda l, j: (l, 0, j)),
        ],
        out_specs=pl.BlockSpec((1, 8, tn), lambda l, j: (l, 0, j)),
        out_shape=jax.ShapeDtypeStruct((depth, 8, n), F32),
        compiler_params=_params(("arbitrary", "arbitrary")),
        name="ada",
    )(c_all, w_ada, b_ada.reshape(depth, 1, n))


def _rope_tile(t, cos, sin_signed, q):
    lane = lax.broadcasted_iota(jnp.int32, t.shape, 1)
    first = (lane & (2 * q - 1)) < q
    up = pltpu.roll(t, LANES - q, 1)
    dn = pltpu.roll(t, q, 1)
    return t * cos + jnp.where(first, up, dn) * sin_signed


def _rope_wide(t, cos, sin_signed, q):
    n = t.shape[1] // LANES
    return jnp.concatenate(
        [_rope_tile(t[:, i * LANES:(i + 1) * LANES], cos, sin_signed, q) for i in range(n)], axis=1)


def _modulate(x_ref, mod_ref, which):
    sh = mod_ref[0, 3 * which:3 * which + 1, :]
    sc = mod_ref[0, 3 * which + 1:3 * which + 2, :]
    return x_ref[...] * (1.0 + sc) + sh


def _proj_ab_kernel(x_ref, mod_ref, w_ref, cos_ref, sin_ref,
                    aq_ref, ak_ref, av_ref, bq_ref, bk_ref, bv_ref, *, rope):
    h = _modulate(x_ref, mod_ref, 0).astype(BF16)
    r = _dot(h, w_ref[...])
    aq = r[:, 0:512]
    ak = r[:, 512:640]
    if rope:
        cos = cos_ref[...]
        sin = sin_ref[...]
        aq = _rope_wide(aq, cos, sin, 16)
        ak = _rope_wide(ak, cos, sin, 16)
    scale = HEAD_DIM ** -0.5
    aq_ref[...] = (aq * scale).astype(BF16)
    ak_ref[...] = ak.astype(BF16)
    av_ref[...] = r[:, 640:768].astype(BF16)
    bq_ref[...] = (r[:, 768:1280] * scale).astype(BF16)
    bk_ref[...] = r[:, 1280:1792].astype(BF16)
    bv_ref[...] = r[:, 1792:2304].astype(BF16)


def _proj_ab(x2d, row0, nrows, mod, gid, w_bf, cos, sin, pos_blocks, rope, tm):
    d = x2d.shape[1]
    n_in = w_bf.shape[1]
    b0 = row0 // tm
    widths = (512, 128, 128, 512, 512, 512)
    return pl.pallas_call(
        functools.partial(_proj_ab_kernel, rope=rope),
        grid=(nrows // tm,),
        in_specs=[
            pl.BlockSpec((tm, d), lambda i: (b0 + i, 0)),
            pl.BlockSpec((1, 6, d), lambda i: (gid(i), 0, 0)),
            pl.BlockSpec((d, n_in), lambda i: (0, 0)),
            pl.BlockSpec((tm, LANES), lambda i: (i % pos_blocks, 0)),
            pl.BlockSpec((tm, LANES), lambda i: (i % pos_blocks, 0)),
        ],
        out_specs=[pl.BlockSpec((tm, w), lambda i: (i, 0)) for w in widths],
        out_shape=[jax.ShapeDtypeStruct((nrows, w), BF16) for w in widths],
        compiler_params=_params(("arbitrary",)),
        name="proj_ab",
    )(x2d, mod, w_bf, cos, sin)


def _softmax_pv(parts, sink_col):
    m = parts[0][0].max(-1, keepdims=True)
    for s, _ in parts[1:]:
        m = jnp.maximum(m, s.max(-1, keepdims=True))
    if sink_col is not None:
        m = jnp.maximum(m, sink_col)
    denom = None if sink_col is None else jnp.exp(sink_col - m)
    o = None
    for s, v in parts:
        e = jnp.exp(s - m)
        d = e.sum(-1, keepdims=True)
        denom = d if denom is None else denom + d
        pv = _dot(e.astype(BF16), v)
        o = pv if o is None else o + pv
    return o / denom


def _win_kernel(sink_ref, q_ref, kp_ref, kc_ref, kn_ref, vp_ref, vc_ref, vn_ref,
                kx_ref, vx_ref, o_ref, *, seq):
    n = pl.program_id(1)
    blk = A_BLOCK
    g_sz = A_HEADS // A_KV_HEADS
    qi = lax.broadcasted_iota(jnp.int32, (blk, 3 * blk), 0)
    kj = lax.broadcasted_iota(jnp.int32, (blk, 3 * blk), 1)
    rel = kj - blk - qi
    kpos = n * blk + kj - blk
    valid = (jnp.abs(rel) <= A_WINDOW) & (kpos >= 0) & (kpos < seq)
    valid = jnp.concatenate([valid] * g_sz, axis=0)
    q = q_ref[...]
    k_all = jnp.concatenate([kp_ref[...], kc_ref[...], kn_ref[...]], axis=0)
    v_all = jnp.concatenate([vp_ref[...], vc_ref[...], vn_ref[...]], axis=0)
    kx = kx_ref[...]
    vx = vx_ref[...]
    outs = []
    for g in range(A_KV_HEADS):
        lo, hi = g * HEAD_DIM, (g + 1) * HEAD_DIM
        heads = [g * g_sz + i for i in range(g_sz)]
        qs = jnp.concatenate([q[:, h * HEAD_DIM:(h + 1) * HEAD_DIM] for h in heads], axis=0)
        sink = jnp.concatenate(
            [jnp.full((blk, 1), sink_ref[h], F32) for h in heads], axis=0)
        s_loc = jnp.where(valid, _dot_nt(qs, k_all[:, lo:hi]), NEG_INF)
        s_ctx = _dot_nt(qs, kx[:, lo:hi])
        o = _softmax_pv([(s_ctx, vx[:, lo:hi]), (s_loc, v_all[:, lo:hi])], sink)
        outs.extend(o[i * blk:(i + 1) * blk] for i in range(g_sz))
    o_ref[...] = jnp.concatenate(outs, axis=1).astype(o_ref.dtype)


def _window_attn(sink, aq, ak, av, akx, avx, batch, seq, ctx_len):
    nblk = seq // A_BLOCK
    kvw = A_KV_HEADS * HEAD_DIM
    qw = A_HEADS * HEAD_DIM

    def nb(delta):
        return lambda b, n: (b * nblk + jnp.clip(n + delta, 0, nblk - 1), 0)

    kv_specs = [pl.BlockSpec((A_BLOCK, kvw), nb(d)) for d in (-1, 0, 1)]
    return pl.pallas_call(
        functools.partial(_win_kernel, seq=seq),
        grid=(batch, nblk),
        in_specs=[pl.BlockSpec(memory_space=pltpu.SMEM),
                  pl.BlockSpec((A_BLOCK, qw), lambda b, n: (b * nblk + n, 0))]
        + kv_specs + kv_specs
        + [pl.BlockSpec((ctx_len, kvw), lambda b, n: (b, 0)),
           pl.BlockSpec((ctx_len, kvw), lambda b, n: (b, 0))],
        out_specs=pl.BlockSpec((A_BLOCK, qw), lambda b, n: (b * nblk + n, 0)),
        out_shape=jax.ShapeDtypeStruct((batch * seq, qw), BF16),
        compiler_params=_params(("arbitrary", "arbitrary")),
        name="window_attn",
    )(sink, aq, ak, ak, ak, av, av, av, akx, avx)


NA_QROWS = 8
NA_Q = NA_QROWS * GRID_W
NA_KBLK = 4 * GRID_W
NA_K = 4 * NA_KBLK


def _na_bias_tables(rpb, rows):
    nj = rows // NA_QROWS
    kr = min(NA_ROWS, rows)
    col = jnp.arange(GRID_W)
    col_start = jnp.clip(col - NA_COLS // 2, 0, GRID_W - NA_COLS)
    cvalid = (col[None, :] >= col_start[:, None]) & (col[None, :] < col_start[:, None] + NA_COLS)
    dc = jnp.clip(col[None, :] - col[:, None], -(NA_COLS - 1), NA_COLS - 1) + (NA_COLS - 1)
    tabs = []
    for j in (0, min(1, nj - 1), nj - 1):
        r = j * NA_QROWS + jnp.arange(NA_QROWS)
        krow = j * NA_QROWS - NA_KBLK // GRID_W + jnp.arange(NA_K // GRID_W)
        start = jnp.clip(r - kr // 2, 0, rows - kr)
        rvalid = (krow[None, :] >= start[:, None]) & (krow[None, :] < start[:, None] + kr)
        dr = jnp.clip(krow[None, :] - r[:, None] + (NA_ROWS - 1), 0, 2 * NA_ROWS - 2)
        bias = rpb.astype(F32)[:, dr][:, :, :, dc]
        ok = rvalid[:, :, None, None] & cvalid[None, None, :, :]
        t = jnp.where(ok[None], bias, NEG_INF)
        t = t.transpose(0, 1, 3, 2, 4).reshape(rpb.shape[0], NA_Q, NA_K)
        tabs.append(t)
    return jnp.stack(tabs)


def _na_kernel(q_ref, k0, k1, k2, k3, v0, v1, v2, v3, kx_ref, vx_ref, t_ref, o_ref):
    q = q_ref[...]
    k_all = jnp.concatenate([k0[...], k1[...], k2[...], k3[...]], axis=0)
    v_all = jnp.concatenate([v0[...], v1[...], v2[...], v3[...]], axis=0)
    kx = kx_ref[...]
    vx = vx_ref[...]
    outs = []
    for hh in range(LANES // HEAD_DIM):
        lo, hi = hh * HEAD_DIM, (hh + 1) * HEAD_DIM
        qh = q[:, lo:hi]
        s_loc = _dot_nt(qh, k_all[:, lo:hi]) + t_ref[0, hh]
        s_ctx = _dot_nt(qh, kx[:, lo:hi])
        outs.append(_softmax_pv([(s_ctx, vx[:, lo:hi]), (s_loc, v_all[:, lo:hi])], None))
    o_ref[...] = jnp.concatenate(outs, axis=1).astype(o_ref.dtype)


def _na_attn(bq, bk, bv, bkx, bvx, tables, batch, seq, ctx_len):
    nj = seq // NA_Q
    nkb = seq // NA_KBLK
    hp = B_HEADS * HEAD_DIM // LANES
    per = LANES // HEAD_DIM

    def kb(t):
        return lambda p, b, j: (b * nkb + jnp.clip(2 * j - 1 + t, 0, nkb - 1), p)

    def variant(p, b, j):
        return (jnp.where(j == 0, 0, jnp.where(j == nj - 1, 2, 1)), p, 0, 0)

    kv_specs = [pl.BlockSpec((NA_KBLK, LANES), kb(t)) for t in range(4)]
    return pl.pallas_call(
        _na_kernel,
        grid=(hp, batch, nj),
        in_specs=[pl.BlockSpec((NA_Q, LANES), lambda p, b, j: (b * nj + j, p))]
        + kv_specs + kv_specs
        + [pl.BlockSpec((ctx_len, LANES), lambda p, b, j: (b, p)),
           pl.BlockSpec((ctx_len, LANES), lambda p, b, j: (b, p)),
           pl.BlockSpec((1, per, NA_Q, NA_K), variant)],
        out_specs=pl.BlockSpec((NA_Q, LANES), lambda p, b, j: (b * nj + j, p)),
        out_shape=jax.ShapeDtypeStruct((batch * seq, B_HEADS * HEAD_DIM), BF16),
        compiler_params=_params(("arbitrary", "arbitrary", "arbitrary")),
        name="na_attn",
    )(bq, bk, bk, bk, bk, bv, bv, bv, bv, bkx, bvx, tables)


def _ctx_ab_kernel(sink_ref, aq_ref, ak_ref, av_ref, bq_ref, bk_ref, bv_ref, oa_ref, ob_ref):
    ctx_len = aq_ref.shape[0]
    g_sz = A_HEADS // A_KV_HEADS
    aq = aq_ref[...]
    ak = ak_ref[...]
    av = av_ref[...]
    outs = []
    for g in range(A_KV_HEADS):
        lo, hi = g * HEAD_DIM, (g + 1) * HEAD_DIM
        heads = [g * g_sz + i for i in range(g_sz)]
        qs = jnp.concatenate([aq[:, h * HEAD_DIM:(h + 1) * HEAD_DIM] for h in heads], axis=0)
        sink = jnp.concatenate(
            [jnp.full((ctx_len, 1), sink_ref[h], F32) for h in heads], axis=0)
        o = _softmax_pv([(_dot_nt(qs, ak[:, lo:hi]), av[:, lo:hi])], sink)
        outs.extend(o[i * ctx_len:(i + 1) * ctx_len] for i in range(g_sz))
    oa_ref[...] = jnp.concatenate(outs, axis=1).astype(oa_ref.dtype)
    bq = bq_ref[...]
    bk = bk_ref[...]
    bv = bv_ref[...]
    outs = []
    for h in range(B_HEADS):
        lo, hi = h * HEAD_DIM, (h + 1) * HEAD_DIM
        outs.append(_softmax_pv([(_dot_nt(bq[:, lo:hi], bk[:, lo:hi]), bv[:, lo:hi])], None))
    ob_ref[...] = jnp.concatenate(outs, axis=1).astype(ob_ref.dtype)


def _ctx_ab_attn(sink, aq, ak, av, bq, bk, bv, batch, ctx_len):
    def spec(w):
        return pl.BlockSpec((ctx_len, w), lambda b: (b, 0))

    return pl.pallas_call(
        _ctx_ab_kernel,
        grid=(batch,),
        in_specs=[pl.BlockSpec(memory_space=pltpu.SMEM)]
        + [spec(a.shape[1]) for a in (aq, ak, av, bq, bk, bv)],
        out_specs=[spec(aq.shape[1]), spec(bq.shape[1])],
        out_shape=[jax.ShapeDtypeStruct(aq.shape, BF16), jax.ShapeDtypeStruct(bq.shape, BF16)],
        compiler_params=_params(("arbitrary",)),
        name="ctx_ab_attn",
    )(sink, aq, ak, av, bq, bk, bv)


def _layer_norm(t, g, b):
    mu = jnp.mean(t, -1, keepdims=True)
    c = t - mu
    var = jnp.mean(c * c, -1, keepdims=True)
    return c * lax.rsqrt(var + 1e-5) * g + b


def _post_attn_kernel(*refs, alpha, lat_tiles, with_ctx):
    n_src = 6 if with_ctx else 3
    srcs = refs[:n_src]
    (w_ref, mod_ref, lng_ref, lnb_ref, wr_hi_ref, wr_lo_ref, br_ref,
     x1_ref, h2_ref, lg_ref) = refs[n_src:]

    def run(o1_ref, o2_ref, x_ref):
        half = o1_ref.shape[1]
        y = _dot(o1_ref[...], w_ref[0:half, :]) + _dot(o2_ref[...], w_ref[half:, :])
        g1 = mod_ref[0, 2:3, :]
        x1 = _layer_norm(alpha * x_ref[...] + g1 * y, lng_ref[...], lnb_ref[...])
        x1_ref[...] = x1
        h2 = x1 * (1.0 + mod_ref[0, 4:5, :]) + mod_ref[0, 3:4, :]
        h_hi, h_lo = _split_bf16(h2)
        h2_ref[...] = h_hi
        wr_hi = wr_hi_ref[...]
        lg_ref[...] = (_dot(h_hi, wr_hi) + _dot(h_lo, wr_hi) + _dot(h_hi, wr_lo_ref[...])
                       + br_ref[...])

    if not with_ctx:
        run(*srcs)
    else:
        i = pl.program_id(0)
        pl.when(i < lat_tiles)(lambda: run(*srcs[:3]))
        pl.when(i >= lat_tiles)(lambda: run(*srcs[3:]))


def _post_attn(lat, ctx_src, w_bf, mod, gid, lng, lnb, wr_hi, wr_lo, br, n_lat, alpha, tm):
    half = lat[0].shape[1]
    d = lat[2].shape[1]
    lat_tiles = n_lat // tm
    n_rows = n_lat + (ctx_src[0].shape[0] if ctx_src is not None else 0)

    def lat_map(i):
        return (jnp.minimum(i, lat_tiles - 1), 0)

    def ctx_map(i):
        return (jnp.maximum(i - lat_tiles, 0), 0)

    def src_specs(index_map):
        return [pl.BlockSpec((tm, half), index_map), pl.BlockSpec((tm, half), index_map),
                pl.BlockSpec((tm, d), index_map)]

    in_specs = src_specs(lat_map)
    args = list(lat)
    if ctx_src is not None:
        in_specs += src_specs(ctx_map)
        args += list(ctx_src)

    def const(shape):
        return pl.BlockSpec(shape, lambda i: (0,) * len(shape))

    in_specs += [const((2 * half, d)),
                 pl.BlockSpec((1, 6, d), lambda i: (gid(i), 0, 0)),
                 const((1, d)), const((1, d)),
                 const((d, ROUTER_PAD)), const((d, ROUTER_PAD)), const((1, ROUTER_PAD))]
    args += [w_bf, mod, lng, lnb, wr_hi, wr_lo, br]
    widths = (d, d, ROUTER_PAD)
    dtypes = (F32, BF16, F32)
    return pl.pallas_call(
        functools.partial(_post_attn_kernel, alpha=alpha, lat_tiles=lat_tiles,
                          with_ctx=ctx_src is not None),
        grid=(n_rows // tm,),
        in_specs=in_specs,
        out_specs=[pl.BlockSpec((tm, w), lambda i: (i, 0)) for w in widths],
        out_shape=[jax.ShapeDtypeStruct((n_rows, w), t) for w, t in zip(widths, dtypes)],
        compiler_params=_params(("arbitrary",)),
        name="post_attn",
    )(*args)


def _moe_kernel(be_ref, xs_ref, wgu_ref, wdn_ref, ys_ref, wgu_bf, wdn_bf):
    i = pl.program_id(0)
    prev = be_ref[jnp.maximum(i - 1, 0)]
    changed = (i == 0) | (be_ref[i] != prev)

    @pl.when(changed)
    def _():
        wgu_bf[...] = wgu_ref[0].astype(BF16)
        wdn_bf[...] = wdn_ref[0].astype(BF16)

    de = wdn_bf.shape[0]
    gu = _dot(xs_ref[...], wgu_bf[...])
    gate = gu[:, :de]
    up = gu[:, de:]
    act = gate / (1.0 + jnp.exp(-gate)) * up
    ys_ref[...] = _dot(act.astype(BF16), wdn_bf[...]).astype(ys_ref.dtype)


def _moe_experts(block_e, xs, w_gu, w_dn):
    rows, d = xs.shape
    nb = rows // MOE_BLOCK
    de = w_dn.shape[1]
    grid_spec = pltpu.PrefetchScalarGridSpec(
        num_scalar_prefetch=1,
        grid=(nb,),
        in_specs=[
            pl.BlockSpec((MOE_BLOCK, d), lambda i, be: (i, 0)),
            pl.BlockSpec((1, d, 2 * de), lambda i, be: (be[i], 0, 0)),
            pl.BlockSpec((1, de, d), lambda i, be: (be[i], 0, 0)),
        ],
        out_specs=pl.BlockSpec((MOE_BLOCK, d), lambda i, be: (i, 0)),
        scratch_shapes=[pltpu.VMEM((d, 2 * de), BF16), pltpu.VMEM((de, d), BF16)],
    )
    return pl.pallas_call(
        _moe_kernel,
        grid_spec=grid_spec,
        out_shape=jax.ShapeDtypeStruct((rows, d), BF16),
        compiler_params=_params(("arbitrary",)),
        name="moe_experts",
    )(block_e, xs, w_gu, w_dn)


ROUTE_TILE = 512


def _route_kernel(lg_ref, rec_ref, cnt_ref, carry):
    i = pl.program_id(0)

    @pl.when(i == 0)
    def _():
        carry[...] = jnp.zeros(carry.shape, F32)

    lg = lg_ref[...]
    tm = lg.shape[0]
    lane = lax.broadcasted_iota(jnp.int32, lg.shape, 1)

    def first_lane(mask):
        return jnp.min(jnp.where(mask, lane, ROUTER_PAD), axis=1, keepdims=True)

    is_g = lane < N_GROUPS
    g_log = jnp.where(is_g, lg, NEG_INF)
    g_max = jnp.max(g_log, axis=1, keepdims=True)
    g_sum = jnp.sum(jnp.where(is_g, jnp.exp(lg - g_max), 0.0), axis=1, keepdims=True)
    g_val = 1.0 / g_sum
    g_idx = first_lane(g_log == g_max)
    lo = N_GROUPS + EXP_PER_GROUP * g_idx
    sel = (lane >= lo) & (lane < lo + EXP_PER_GROUP)
    e_log = jnp.where(sel, lg, NEG_INF)
    e_max = jnp.max(e_log, axis=1, keepdims=True)
    e_exp = jnp.where(sel, jnp.exp(lg - e_max), 0.0)
    e_prob = e_exp / jnp.sum(e_exp, axis=1, keepdims=True)
    p1 = jnp.where(sel, e_prob, -1.0)
    v1 = jnp.max(p1, axis=1, keepdims=True)
    i1 = first_lane(p1 == v1)
    p2 = jnp.where(lane == i1, -1.0, p1)
    v2 = jnp.max(p2, axis=1, keepdims=True)
    i2 = first_lane(p2 == v2)
    norm = g_val / (v1 + v2)

    hot1 = lane == i1
    hot2 = lane == i2
    hot = jnp.where(hot1 | hot2, 1.0, 0.0)
    rows = lax.broadcasted_iota(jnp.int32, (tm, tm), 0)
    cols = lax.broadcasted_iota(jnp.int32, (tm, tm), 1)
    before = jnp.where(cols < rows, 1.0, 0.0).astype(BF16)
    prefix = _dot(before, hot.astype(BF16)) + carry[0:1, :]
    r1 = jnp.sum(jnp.where(hot1, prefix, 0.0), axis=1, keepdims=True)
    r2 = jnp.sum(jnp.where(hot2, prefix, 0.0), axis=1, keepdims=True)
    carry[0:1, :] = carry[0:1, :] + jnp.sum(hot, axis=0, keepdims=True)

    fields = [(i1 - N_GROUPS).astype(F32), (i2 - N_GROUPS).astype(F32), r1, r2, v1 * norm, v2 * norm]
    rec = jnp.zeros(lg.shape, F32)
    for k, f in enumerate(fields):
        rec = jnp.where(lane == k, f, rec)
    rec_ref[...] = rec

    @pl.when(i == pl.num_programs(0) - 1)
    def _():
        cnt_ref[...] = carry[...]


def _route(logits, n_tok):
    rec, cnt = pl.pallas_call(
        _route_kernel,
        grid=(n_tok // ROUTE_TILE,),
        in_specs=[pl.BlockSpec((ROUTE_TILE, ROUTER_PAD), lambda i: (i, 0))],
        out_specs=[pl.BlockSpec((ROUTE_TILE, ROUTER_PAD), lambda i: (i, 0)),
                   pl.BlockSpec((8, ROUTER_PAD), lambda i: (0, 0))],
        out_shape=[jax.ShapeDtypeStruct((n_tok, ROUTER_PAD), F32),
                   jax.ShapeDtypeStruct((8, ROUTER_PAD), F32)],
        scratch_shapes=[pltpu.VMEM((8, ROUTER_PAD), F32)],
        compiler_params=_params(("arbitrary",)),
        name="route",
    )(logits)
    experts = rec[:, 0:2].astype(jnp.int32)
    rank = rec[:, 2:4].astype(jnp.int32)
    wts = rec[:, 4:6]
    counts = cnt[0, N_GROUPS:N_GROUPS + N_EXPERTS].astype(jnp.int32)

    n_asg = n_tok * TOP_K
    pcounts = ((counts + MOE_BLOCK - 1) // MOE_BLOCK) * MOE_BLOCK
    pends = jnp.cumsum(pcounts)
    pstarts = pends - pcounts
    dest = pstarts[experts] + rank
    nb = -(-n_asg // MOE_BLOCK) + N_EXPERTS
    block_e = jnp.minimum(
        jnp.searchsorted(pends, jnp.arange(nb) * MOE_BLOCK, side='right'), N_EXPERTS - 1).astype(jnp.int32)
    tok = jnp.broadcast_to(jnp.arange(n_tok, dtype=jnp.int32)[:, None], (n_tok, TOP_K))
    row_tok = jnp.zeros((nb * MOE_BLOCK,), jnp.int32).at[dest.reshape(-1)].set(
        tok.reshape(-1), unique_indices=True)
    return wts, dest, row_tok, block_e


def _post_moe_kernel(x_ref, ya_ref, yb_ref, w_ref, mod_ref, lng_ref, lnb_ref, o_ref, *, alpha):
    w = w_ref[...]
    y = w[:, 0:1] * ya_ref[...] + w[:, 1:2] * yb_ref[...]
    g2 = mod_ref[0, 5:6, :]
    o_ref[...] = _layer_norm(alpha * x_ref[...] + g2 * y, lng_ref[...], lnb_ref[...])


def _post_moe(x1, ya, yb, wts, mod, gid, lng, lnb, alpha, tm):
    rows, d = x1.shape
    row = pl.BlockSpec((tm, d), lambda i: (i, 0))
    vec = pl.BlockSpec((1, d), lambda i: (0, 0))
    return pl.pallas_call(
        functools.partial(_post_moe_kernel, alpha=alpha),
        grid=(rows // tm,),
        in_specs=[row, row, row,
                  pl.BlockSpec((tm, TOP_K), lambda i: (i, 0)),
                  pl.BlockSpec((1, 6, d), lambda i: (gid(i), 0, 0)),
                  vec, vec],
        out_specs=row,
        out_shape=jax.ShapeDtypeStruct((rows, d), F32),
        compiler_params=_params(("arbitrary",)),
        name="post_moe",
    )(x1, ya, yb, wts, mod, lng, lnb)


CD_COLS = dict(cq=(0, 384), ckv=(384, 640), dq=(640, 1152), dk=(1152, 1664), dv=(1664, 2176),
               kr=(2176, 2304))
C_HEAD_PAD = LANES


def _rms(t, g, eps=1e-6):
    return t * lax.rsqrt(jnp.mean(t * t, -1, keepdims=True) + eps) * g


def _proj_cd_kernel(x_ref, mod_ref, w_ref, qn_ref, wuq_ref, kvn_ref, wukv_ref,
                    cos64_ref, sin64_ref, cos32_ref, sin32_ref,
                    qc_ref, kc_ref, vc_ref, dq_ref, dk_ref, dv_ref):
    h = _modulate(x_ref, mod_ref, 0).astype(BF16)
    r = _dot(h, w_ref[...])

    def cols(name):
        lo, hi = CD_COLS[name]
        return r[:, lo:hi]

    q = _dot(_rms(cols('cq'), qn_ref[...]).astype(BF16), wuq_ref[...])
    kv = _dot(_rms(cols('ckv'), kvn_ref[...]).astype(BF16), wukv_ref[...])
    c32, s32 = cos32_ref[...], sin32_ref[...]
    c64, s64 = cos64_ref[...], sin64_ref[...]
    q = _rope_wide(q, c32, s32, C_ROPE // 4)
    kr = _rope_tile(cols('kr'), c32, s32, C_ROPE // 4)
    dq = _rope_wide(cols('dq'), c64, s64, D_QK // 4)
    dk = _rope_wide(cols('dk'), c64, s64, D_QK // 4)
    kw = C_HEADS * C_HEAD_PAD
    qc_ref[0] = (q * ((C_NOPE + C_ROPE) ** -0.5 * LOG2E)).astype(BF16)
    kc_ref[0] = (kv[:, :kw] + jnp.concatenate([kr] * C_HEADS, axis=1)).astype(BF16)
    vc_ref[0] = kv[:, kw:].astype(BF16)
    dq_ref[0] = (dq * (D_QK ** -0.5 * LOG2E)).astype(BF16)
    dk_ref[0] = dk.astype(BF16)
    dv_ref[0] = cols('dv').astype(BF16)


def _proj_cd(xall, mod, w_bf, qn, wuq, kvn, wukv, tabs, batch, seq, ctx_len, tm):
    d = xall.shape[1]
    lat_t, ctx_t = seq // tm, ctx_len // tm
    n_keys = seq + ctx_len
    widths = (C_HEADS * C_HEAD_PAD, C_HEADS * C_HEAD_PAD, C_HEADS * C_V,
              2 * D_HEADS * D_QK, 2 * D_HEADS * D_QK, D_HEADS * D_V)

    def src(b, t):
        return (jnp.where(t < lat_t, b * lat_t + t, batch * lat_t + b * ctx_t + t - lat_t), 0)

    def full(a):
        return pl.BlockSpec(a.shape, lambda b, t: (0,) * a.ndim)

    tab = pl.BlockSpec((tm, LANES), lambda b, t: (t, 0))
    return pl.pallas_call(
        _proj_cd_kernel,
        grid=(batch, lat_t + ctx_t),
        in_specs=[pl.BlockSpec((tm, d), src),
                  pl.BlockSpec((1, 6, d), lambda b, t: (jnp.where(t < lat_t, b, batch), 0, 0)),
                  full(w_bf), full(qn), full(wuq), full(kvn), full(wukv), tab, tab, tab, tab],
        out_specs=[pl.BlockSpec((1, tm, w), lambda b, t: (b, t, 0)) for w in widths],
        out_shape=[jax.ShapeDtypeStruct((batch, n_keys, w), BF16) for w in widths],
        compiler_params=_params(("arbitrary", "arbitrary")),
        name="proj_cd",
    )(xall, mod, w_bf, qn, wuq, kvn, wukv, *tabs)


DENSE_SHIFT = 80.0
DENSE_L_MIN = 2.0 ** -60
DENSE_L_MAX = 2.0 ** 120
DENSE_TQ = 1024
DENSE_TK_MAX = 1280
DENSE_UNROLL_MAX = 13
MXU_DEPTH = 256
PROJ_CD_TILE = 256
ONES_ROWS = 16


def _sq_norm_row(x):
    ones = jnp.ones((8, x.shape[1]), BF16)
    return _dot_nt(ones, (x * x).astype(BF16))[0:1] * 1.02


def _key_norm_max(kchunk, n_chunks, masks):
    def body(j, mx):
        kf = kchunk(j).astype(F32)
        ksq = kf * kf
        out = []
        for msk, cur in zip(masks, mx):
            part = ksq if msk is None else jnp.where(msk, ksq, 0.0)
            rn = jnp.sum(part, axis=1, keepdims=True)
            out.append(jnp.maximum(cur, jnp.max(rn, axis=0, keepdims=True)))
        return tuple(out)

    mx = lax.fori_loop(0, n_chunks, body, tuple(jnp.zeros((1, 1), F32) for _ in masks))
    return [jnp.sqrt(v) * 1.01 for v in mx]


def _safe_online(q, kchunk, vt_ref, n_chunks, m_scr, l_scr, acc_scr):
    m_scr[...] = jnp.full(m_scr.shape, NEG_INF, F32)
    l_scr[...] = jnp.zeros(l_scr.shape, F32)
    acc_scr[...] = jnp.zeros(acc_scr.shape, F32)

    def body(j, carry):
        s = _dot_nt(kchunk(j), q)
        m_prev = m_scr[...]
        m_new = jnp.maximum(m_prev, s.max(0, keepdims=True))
        alpha = jnp.exp2(m_prev - m_new)
        p = jnp.exp2(s - m_new)
        l_scr[...] = alpha * l_scr[...] + p.sum(0, keepdims=True)
        acc_scr[...] = alpha * acc_scr[...] + _dot(vt_ref[0, 0, j], p.astype(BF16))
        m_scr[...] = m_new
        return carry

    lax.fori_loop(0, n_chunks, body, 0)


def _denominators_ok(*ls):
    ok = None
    for l in ls:
        cur = (l > DENSE_L_MIN) & (l < DENSE_L_MAX)
        ok = cur if ok is None else ok & cur
    return jnp.max(jnp.where(ok, 0.0, 1.0)) == 0.0


def _mla_kernel(q_ref, k_ref, vt_ref, o_ref, kmax_scr, acc_scr, m_scr, l_scr,
                *, n_chunks, tk, unroll, depth):
    def kchunk(j):
        return k_ref[0, pl.ds(pl.multiple_of(j * tk, tk), tk), :]

    @pl.when(pl.program_id(2) == 0)
    def _():
        (kmax,) = _key_norm_max(kchunk, n_chunks, [None])
        kmax_scr[...] = jnp.broadcast_to(kmax, kmax_scr.shape)

    q = q_ref[0]
    m_row = jnp.sqrt(_sq_norm_row(q.astype(F32))) * kmax_scr[0:1, 0:1] - DENSE_SHIFT
    acc_scr[...] = jnp.zeros(acc_scr.shape, F32)

    def body(it, carry):
        j0 = it * unroll
        pend = [_dot_nt(kchunk(j0 + u), q) for u in range(min(depth, unroll))]
        acc = acc_scr[...]
        for u in range(unroll):
            s = pend.pop(0)
            if u + depth < unroll:
                pend.append(_dot_nt(kchunk(j0 + u + depth), q))
            acc = acc + _dot(vt_ref[0, 0, j0 + u], jnp.exp2(s - m_row).astype(BF16))
        acc_scr[...] = acc
        return carry

    lax.fori_loop(0, n_chunks // unroll, body, 0)
    l = acc_scr[C_V:C_V + 1, :]
    o_ref[0] = (acc_scr[0:C_V, :] / l).astype(o_ref.dtype)

    @pl.when(jnp.logical_not(_denominators_ok(l)))
    def _():
        _safe_online(q, kchunk, vt_ref, n_chunks, m_scr, l_scr, acc_scr)
        o_ref[0] = (acc_scr[0:C_V, :] / l_scr[...]).astype(o_ref.dtype)


def _mla_attn(qc, kc, vt, batch, seq, tq, tk, unroll, depth):
    n_keys = kc.shape[1]
    nq, n_chunks = seq // tq, n_keys // tk
    rows = C_V + ONES_ROWS
    return pl.pallas_call(
        functools.partial(_mla_kernel, n_chunks=n_chunks, tk=tk, unroll=unroll, depth=depth),
        grid=(batch, C_HEADS, nq),
        in_specs=[
            pl.BlockSpec((1, tq, C_HEAD_PAD), lambda b, h, i: (b, i, h)),
            pl.BlockSpec((1, n_keys, C_HEAD_PAD), lambda b, h, i: (b, 0, h)),
            pl.BlockSpec((1, 1, n_chunks, rows, tk), lambda b, h, i: (b, h, 0, 0, 0)),
        ],
        out_specs=pl.BlockSpec((1, C_V, tq), lambda b, h, i: (b, h, i)),
        out_shape=jax.ShapeDtypeStruct((batch, C_HEADS * C_V, seq), BF16),
        scratch_shapes=[pltpu.VMEM((8, LANES), F32), pltpu.VMEM((rows, tq), F32),
                        pltpu.VMEM((1, tq), F32), pltpu.VMEM((1, tq), F32)],
        compiler_params=_params(("arbitrary", "arbitrary", "arbitrary")),
        name="mla_attn",
    )(qc, kc, vt)


def _diff_kernel(lam_ref, q_ref, k_ref, vt_ref, subln_ref, o_ref, kmax_scr, acc1_scr, acc2_scr,
                 m_scr, l_scr, *, n_chunks, tk, unroll, depth, out_scale):
    def kchunk(j):
        return k_ref[0, pl.ds(pl.multiple_of(j * tk, tk), tk), :]

    @pl.when(pl.program_id(2) == 0)
    def _():
        klane = lax.broadcasted_iota(jnp.int32, (tk, 2 * D_QK), 1)
        k1, k2 = _key_norm_max(kchunk, n_chunks, [klane < D_QK, klane >= D_QK])
        kmax_scr[0:1, :] = jnp.broadcast_to(k1, (1, LANES))
        kmax_scr[1:2, :] = jnp.broadcast_to(k2, (1, LANES))

    q = q_ref[0]
    qlane = lax.broadcasted_iota(jnp.int32, q.shape, 1)
    zero = jnp.zeros_like(q)
    q1 = jnp.where(qlane < D_QK, q, zero)
    q2 = jnp.where(qlane >= D_QK, q, zero)
    m1 = jnp.sqrt(_sq_norm_row(q1.astype(F32))) * kmax_scr[0:1, 0:1] - DENSE_SHIFT
    m2 = jnp.sqrt(_sq_norm_row(q2.astype(F32))) * kmax_scr[1:2, 0:1] - DENSE_SHIFT
    acc1_scr[...] = jnp.zeros(acc1_scr.shape, F32)
    acc2_scr[...] = jnp.zeros(acc2_scr.shape, F32)

    def scores(j):
        k = kchunk(j)
        return _dot_nt(k, q1), _dot_nt(k, q2)

    def body(it, carry):
        j0 = it * unroll
        pend = [scores(j0 + u) for u in range(min(depth, unroll))]
        a1 = acc1_scr[...]
        a2 = acc2_scr[...]
        for u in range(unroll):
            s1, s2 = pend.pop(0)
            if u + depth < unroll:
                pend.append(scores(j0 + u + depth))
            vt = vt_ref[0, 0, j0 + u]
            a1 = a1 + _dot(vt, jnp.exp2(s1 - m1).astype(BF16))
            a2 = a2 + _dot(vt, jnp.exp2(s2 - m2).astype(BF16))
        acc1_scr[...] = a1
        acc2_scr[...] = a2
        return carry

    lax.fori_loop(0, n_chunks // unroll, body, 0)

    def finish(o1, o2):
        o = o1 - lam_ref[0] * o2
        o = o * lax.rsqrt(jnp.mean(o * o, 0, keepdims=True) + 1e-6) * subln_ref[...] * out_scale
        o_ref[0] = o.astype(o_ref.dtype)

    l1 = acc1_scr[D_V:D_V + 1, :]
    l2 = acc2_scr[D_V:D_V + 1, :]
    finish(acc1_scr[0:D_V, :] / l1, acc2_scr[0:D_V, :] / l2)

    @pl.when(jnp.logical_not(_denominators_ok(l1, l2)))
    def _():
        _safe_online(q1, kchunk, vt_ref, n_chunks, m_scr, l_scr, acc1_scr)
        o1 = acc1_scr[0:D_V, :] / l_scr[...]
        _safe_online(q2, kchunk, vt_ref, n_chunks, m_scr, l_scr, acc2_scr)
        finish(o1, acc2_scr[0:D_V, :] / l_scr[...])


def _diff_attn(lam, dq, dk, vt, subln_col, out_scale, batch, seq, tq, tk, unroll, depth):
    n_keys = dk.shape[1]
    nq, n_chunks = seq // tq, n_keys // tk
    rows = D_V + ONES_ROWS
    pair = 2 * D_QK
    return pl.pallas_call(
        functools.partial(_diff_kernel, n_chunks=n_chunks, tk=tk, unroll=unroll, depth=depth,
                          out_scale=out_scale),
        grid=(batch, D_HEADS, nq),
        in_specs=[
            pl.BlockSpec(memory_space=pltpu.SMEM),
            pl.BlockSpec((1, tq, pair), lambda b, h, i: (b, i, h)),
            pl.BlockSpec((1, n_keys, pair), lambda b, h, i: (b, 0, h)),
            pl.BlockSpec((1, 1, n_chunks, rows, tk), lambda b, h, i: (b, h, 0, 0, 0)),
            pl.BlockSpec((D_V, 1), lambda b, h, i: (0, 0)),
        ],
        out_specs=pl.BlockSpec((1, D_V, tq), lambda b, h, i: (b, h, i)),
        out_shape=jax.ShapeDtypeStruct((batch, D_HEADS * D_V, seq), BF16),
        scratch_shapes=[pltpu.VMEM((8, LANES), F32), pltpu.VMEM((rows, tq), F32),
                        pltpu.VMEM((rows, tq), F32), pltpu.VMEM((1, tq), F32), pltpu.VMEM((1, tq), F32)],
        compiler_params=_params(("arbitrary", "arbitrary", "arbitrary")),
        name="diff_attn",
    )(lam, dq, dk, vt, subln_col)


def _vt_chunks(v, heads, dv, tk):
    batch, n_keys, _ = v.shape
    n_chunks = n_keys // tk
    vt = v.reshape(batch, n_chunks, tk, heads, dv).transpose(0, 3, 1, 4, 2)
    extra = jnp.zeros((batch, heads, n_chunks, ONES_ROWS, tk), v.dtype).at[:, :, :, 0, :].set(1.0)
    return jnp.concatenate([vt, extra], axis=3)


def _rope_tables(n_tok, dim, lane_lo):
    t = jnp.arange(n_tok)
    pos_r = (t // GRID_W).astype(F32)
    pos_c = (t % GRID_W).astype(F32)
    quarter = dim // 4
    inv = ROPE_BASE ** (-jnp.arange(quarter, dtype=F32) / quarter)
    ang_r = pos_r[:, None] * inv
    ang_c = pos_c[:, None] * inv
    ang = jnp.concatenate([ang_r, ang_r, ang_c, ang_c], -1)
    sign = jnp.tile(jnp.concatenate([-jnp.ones(quarter), jnp.ones(quarter)]), 2).astype(F32)
    cos, sin = jnp.cos(ang), jnp.sin(ang) * sign
    reps = (LANES - lane_lo) // dim
    cos = jnp.concatenate([jnp.ones((n_tok, lane_lo), F32)] + [cos] * reps, axis=1)
    sin = jnp.concatenate([jnp.zeros((n_tok, lane_lo), F32)] + [sin] * reps, axis=1)
    return cos, sin


def _cd_weights(w_in, w_uq, w_ukv):
    d = w_in.shape[0]
    s0 = C_Q_RANK
    s1 = s0 + C_KV_RANK
    s2 = s1 + C_ROPE
    s3 = s2 + 2 * D_HEADS * D_QK
    s4 = s3 + 2 * D_HEADS * D_QK
    kr = jnp.zeros((d, LANES), F32).at[:, C_NOPE:C_NOPE + C_ROPE].set(w_in[:, s1:s2])
    w_in_p = jnp.concatenate([w_in[:, :s1], w_in[:, s2:s3], w_in[:, s3:s4], w_in[:, s4:], kr], axis=1)
    dqk = C_NOPE + C_ROPE
    wq = w_uq.reshape(C_Q_RANK, C_HEADS, dqk)
    wq = jnp.pad(wq, ((0, 0), (0, 0), (0, C_HEAD_PAD - dqk))).reshape(C_Q_RANK, C_HEADS * C_HEAD_PAD)
    wkv = w_ukv.reshape(C_KV_RANK, C_HEADS, C_NOPE + C_V)
    wk = jnp.pad(wkv[:, :, :C_NOPE], ((0, 0), (0, 0), (0, C_HEAD_PAD - C_NOPE)))
    wk = wk.reshape(C_KV_RANK, C_HEADS * C_HEAD_PAD)
    wv = wkv[:, :, C_NOPE:].reshape(C_KV_RANK, C_HEADS * C_V)
    return w_in_p.astype(BF16), wq.astype(BF16), jnp.concatenate([wk, wv], axis=1).astype(BF16)


def _moe_layer(h2, logits, x1, w_gu, w_dn, mod, gid, lng, lnb, alpha, tm):
    n_tok = h2.shape[0]
    wts, dest, row_tok, block_e = _route(logits, n_tok)
    xs = jnp.take(h2, row_tok, axis=0)
    ys = _moe_experts(block_e, xs, w_gu, w_dn)
    ya = jnp.take(ys, dest[:, 0], axis=0)
    yb = jnp.take(ys, dest[:, 1], axis=0)
    return _post_moe(x1, ya, yb, wts.astype(F32), mod, gid, lng, lnb, alpha, tm)


def kernel(x, c, ctx, c_ctx, w_ada, b_ada, ln_g, ln_b, ab_w_in, a_sink, b_rpb, ab_w_out, cd_w_in, c_q_norm, c_w_uq, c_kv_norm, c_w_ukv, d_lambda, d_subln, cd_w_out, w_group, b_group, w_exp_router, b_exp_router, w_gate_up, w_down):
    batch, seq, d = x.shape
    ctx_len = ctx.shape[1]
    depth = w_ada.shape[0]
    alpha = (2 * depth) ** 0.25
    n_lat = batch * seq
    n_ctx = batch * ctx_len
    tm = 512
    tq = DENSE_TQ
    assert depth == 2 and seq % NA_Q == 0 and seq % tq == 0 and n_ctx % tm == 0
    assert seq % PROJ_CD_TILE == 0 and ctx_len % PROJ_CD_TILE == 0
    key_tiles = (seq + ctx_len) // MXU_DEPTH
    tk = MXU_DEPTH * max(u for u in range(1, DENSE_TK_MAX // MXU_DEPTH + 1) if key_tiles % u == 0)
    n_chunks = (seq + ctx_len) // tk

    def unroll_for(cap):
        return max(u for u in range(1, cap + 1) if n_chunks % u == 0)
    lat_tiles = seq // tm

    def gid_lat(i):
        return i // lat_tiles

    def gid_ctx(i):
        return batch

    def gid_all(i):
        return jnp.minimum(i // lat_tiles, batch)

    c_all = jnp.zeros((8, d), F32).at[:batch].set(c).at[batch].set(c_ctx)
    mod = _ada(c_all, w_ada, b_ada).reshape(depth, 8, 6, d)

    cos64, sin64 = _rope_tables(seq, HEAD_DIM, 0)
    cos32, sin32 = _rope_tables(seq, C_ROPE, C_NOPE)

    def router_weights(l):
        wr = jnp.zeros((d, ROUTER_PAD), F32)
        wr = wr.at[:, :N_GROUPS].set(w_group[l]).at[:, N_GROUPS:N_GROUPS + N_EXPERTS].set(w_exp_router[l])
        br = jnp.zeros((1, ROUTER_PAD), F32)
        br = br.at[0, :N_GROUPS].set(b_group[l]).at[0, N_GROUPS:N_GROUPS + N_EXPERTS].set(b_exp_router[l])
        hi = wr.astype(BF16)
        lo = (wr - hi.astype(F32)).astype(BF16)
        return hi, lo, br

    x2d = x.reshape(n_lat, d)
    ctx2d = ctx.reshape(n_ctx, d)

    l = 0
    w_in_bf = ab_w_in[0].astype(BF16)
    aq, ak, av, bq, bk, bv = _proj_ab(x2d, 0, n_lat, mod[l], gid_lat, w_in_bf, cos64, sin64,
                                      lat_tiles, True, tm)
    aqx, akx, avx, bqx, bkx, bvx = _proj_ab(ctx2d, 0, n_ctx, mod[l], gid_ctx, w_in_bf, cos64, sin64,
                                            1, False, tm)
    sink = a_sink[0].astype(F32)
    oa = _window_attn(sink, aq, ak, av, akx, avx, batch, seq, ctx_len)
    tables = _na_bias_tables(b_rpb[0], seq // GRID_W)
    ob = _na_attn(bq, bk, bv, bkx, bvx, tables, batch, seq, ctx_len)
    oax, obx = _ctx_ab_attn(sink, aqx, akx, avx, bqx, bkx, bvx, batch, ctx_len)

    n_all = n_lat + n_ctx
    w_out_bf = ab_w_out[0].astype(BF16)
    lng, lnb = ln_g[l, 0][None], ln_b[l, 0][None]
    wr_hi, wr_lo, br = router_weights(l)
    x1, h2, logits = _post_attn((oa, ob, x2d), (oax, obx, ctx2d), w_out_bf, mod[l], gid_all,
                                lng, lnb, wr_hi, wr_lo, br, n_lat, alpha, tm)
    xall = _moe_layer(h2, logits, x1, w_gate_up[l], w_down[l], mod[l], gid_all,
                      ln_g[l, 1][None], ln_b[l, 1][None], alpha, tm)

    l = 1
    lam_init = 0.8 - 0.6 * math.exp(-0.3 * l)
    lp = d_lambda[0].astype(F32)
    lam = (jnp.exp(jnp.sum(lp[0] * lp[1])) - jnp.exp(jnp.sum(lp[2] * lp[3])) + lam_init).reshape(1)
    w_in_p, wuq_p, wukv_p = _cd_weights(cd_w_in[0], c_w_uq[0], c_w_ukv[0])
    qn, kvn = c_q_norm[0][None].astype(F32), c_kv_norm[0][None].astype(F32)
    def with_ctx_identity(cos, sin):
        return (jnp.concatenate([cos, jnp.ones((ctx_len, LANES), F32)]),
                jnp.concatenate([sin, jnp.zeros((ctx_len, LANES), F32)]))

    tabs = with_ctx_identity(cos64, sin64) + with_ctx_identity(cos32, sin32)
    qc, kc, vc, dq, dk, dv = _proj_cd(xall, mod[l], w_in_p, qn, wuq_p, kvn, wukv_p, tabs,
                                      batch, seq, ctx_len, PROJ_CD_TILE)
    unroll = unroll_for(DENSE_UNROLL_MAX)
    oc_t = _mla_attn(qc, kc, _vt_chunks(vc, C_HEADS, C_V, tk), batch, seq, tq, tk, unroll, 1)
    od_t = _diff_attn(lam, dq, dk, _vt_chunks(dv, D_HEADS, D_V, tk),
                      d_subln[0].astype(F32).reshape(D_V, 1), 1.0 - lam_init, batch, seq, tq, tk,
                      unroll, 1)
    oc = oc_t.transpose(0, 2, 1).reshape(n_lat, -1)
    od = od_t.transpose(0, 2, 1).reshape(n_lat, -1)

    wr_hi, wr_lo, br = router_weights(l)
    x1, h2, logits = _post_attn((oc, od, xall), None, cd_w_out[0].astype(BF16), mod[l], gid_lat,
                                ln_g[l, 0][None], ln_b[l, 0][None], wr_hi, wr_lo, br,
                                n_lat, alpha, tm)
    out = _moe_layer(h2, logits, x1, w_gate_up[l], w_down[l], mod[l], gid_lat,
                     ln_g[l, 1][None], ln_b[l, 1][None], alpha, tm)
    return out.reshape(batch, seq, d)
```

```python
import functools
import math

import jax
import jax.numpy as jnp
from jax import lax
from jax.experimental import pallas as pl
from jax.experimental.pallas import tpu as pltpu

F32 = jnp.float32
BF16 = jnp.bfloat16

GRID_W = 64
HEAD_DIM = 64
ROPE_BASE = 10000.0
NEG_INF = -1e30
LOG2E = 1.4426950408889634

A_HEADS = 8
A_KV_HEADS = 2
A_WINDOW = 128
A_BLOCK = 128
B_HEADS = 8
NA_ROWS = 8
NA_COLS = 16
C_HEADS = 8
C_Q_RANK = 384
C_KV_RANK = 256
C_NOPE = 64
C_ROPE = 32
C_V = 64
D_HEADS = 4
D_QK = 64
D_V = 128
N_GROUPS = 4
EXP_PER_GROUP = 8
N_EXPERTS = N_GROUPS * EXP_PER_GROUP
TOP_K = 2
MOE_BLOCK = 256
ROUTER_PAD = 128

LANES = 128
VMEM_LIMIT = 56 * 1024 * 1024


def _params(sem):
    return pltpu.CompilerParams(dimension_semantics=sem, vmem_limit_bytes=VMEM_LIMIT)


def _dot(a, b):
    return jnp.dot(a, b, preferred_element_type=F32)


def _dot_nt(a, b):
    return lax.dot_general(a, b, (((1,), (1,)), ((), ())), preferred_element_type=F32)


def _split_bf16(x):
    hi = x.astype(BF16)
    lo = (x - hi.astype(F32)).astype(BF16)
    return hi, lo


def _ada_kernel(c_ref, w_ref, b_ref, o_ref):
    c = c_ref[...]
    a = c / (1.0 + jnp.exp(-c))
    a_hi, a_lo = _split_bf16(a)
    w_hi, w_lo = _split_bf16(w_ref[0])
    acc = _dot(a_hi, w_hi) + _dot(a_lo, w_hi) + _dot(a_hi, w_lo)
    o_ref[0] = acc + b_ref[0]


def _ada(c_all, w_ada, b_ada):
    depth, d, n = w_ada.shape
    tn = 1536
    return pl.pallas_call(
        _ada_kernel,
        grid=(depth, n // tn),
        in_specs=[
            pl.BlockSpec((8, d), lambda l, j: (0, 0)),
            pl.BlockSpec((1, d, tn), lambda l, j: (l, 0, j)),
            pl.BlockSpec((1, 1, tn), lambda l, j: (l, 0, j)),
        ],
        out_specs=pl.BlockSpec((1, 8, tn), lambda l, j: (l, 0, j)),
        out_shape=jax.ShapeDtypeStruct((depth, 8, n), F32),
        compiler_params=_params(("arbitrary", "arbitrary")),
        name="ada",
    )(c_all, w_ada, b_ada.reshape(depth, 1, n))


def _rope_tile(t, cos, sin_signed, q):
    lane = lax.broadcasted_iota(jnp.int32, t.shape, 1)
    first = (lane & (2 * q - 1)) < q
    up = pltpu.roll(t, LANES - q, 1)
    dn = pltpu.roll(t, q, 1)
    return t * cos + jnp.where(first, up, dn) * sin_signed


def _rope_wide(t, cos, sin_signed, q):
    n = t.shape[1] // LANES
    return jnp.concatenate(
        [_rope_tile(t[:, i * LANES:(i + 1) * LANES], cos, sin_signed, q) for i in range(n)], axis=1)


def _modulate(x_ref, mod_ref, which):
    sh = mod_ref[0, 3 * which:3 * which + 1, :]
    sc = mod_ref[0, 3 * which + 1:3 * which + 2, :]
    return x_ref[...] * (1.0 + sc) + sh


def _proj_ab_kernel(x_ref, mod_ref, w_ref, cos_ref, sin_ref,
                    aq_ref, ak_ref, av_ref, bq_ref, bk_ref, bv_ref, *, rope):
    h = _modulate(x_ref, mod_ref, 0).astype(BF16)
    r = _dot(h, w_ref[...])
    aq = r[:, 0:512]
    ak = r[:, 512:640]
    if rope:
        cos = cos_ref[...]
        sin = sin_ref[...]
        aq = _rope_wide(aq, cos, sin, 16)
        ak = _rope_wide(ak, cos, sin, 16)
    scale = HEAD_DIM ** -0.5
    aq_ref[...] = (aq * scale).astype(BF16)
    ak_ref[...] = ak.astype(BF16)
    av_ref[...] = r[:, 640:768].astype(BF16)
    bq_ref[...] = (r[:, 768:1280] * scale).astype(BF16)
    bk_ref[...] = r[:, 1280:1792].astype(BF16)
    bv_ref[...] = r[:, 1792:2304].astype(BF16)


def _proj_ab(x2d, row0, nrows, mod, gid, w_bf, cos, sin, pos_blocks, rope, tm):
    d = x2d.shape[1]
    n_in = w_bf.shape[1]
    b0 = row0 // tm
    widths = (512, 128, 128, 512, 512, 512)
    return pl.pallas_call(
        functools.partial(_proj_ab_kernel, rope=rope),
        grid=(nrows // tm,),
        in_specs=[
            pl.BlockSpec((tm, d), lambda i: (b0 + i, 0)),
            pl.BlockSpec((1, 6, d), lambda i: (gid(i), 0, 0)),
            pl.BlockSpec((d, n_in), lambda i: (0, 0)),
            pl.BlockSpec((tm, LANES), lambda i: (i % pos_blocks, 0)),
            pl.BlockSpec((tm, LANES), lambda i: (i % pos_blocks, 0)),
        ],
        out_specs=[pl.BlockSpec((tm, w), lambda i: (i, 0)) for w in widths],
        out_shape=[jax.ShapeDtypeStruct((nrows, w), BF16) for w in widths],
        compiler_params=_params(("arbitrary",)),
        name="proj_ab",
    )(x2d, mod, w_bf, cos, sin)


def _softmax_pv(parts, sink_col):
    m = parts[0][0].max(-1, keepdims=True)
    for s, _ in parts[1:]:
        m = jnp.maximum(m, s.max(-1, keepdims=True))
    if sink_col is not None:
        m = jnp.maximum(m, sink_col)
    denom = None if sink_col is None else jnp.exp(sink_col - m)
    o = None
    for s, v in parts:
        e = jnp.exp(s - m)
        d = e.sum(-1, keepdims=True)
        denom = d if denom is None else denom + d
        pv = _dot(e.astype(BF16), v)
        o = pv if o is None else o + pv
    return o / denom


def _win_kernel(sink_ref, q_ref, kp_ref, kc_ref, kn_ref, vp_ref, vc_ref, vn_ref,
                kx_ref, vx_ref, o_ref, *, seq):
    step = pl.program_id(1)
    blk = A_BLOCK
    g_sz = A_HEADS // A_KV_HEADS
    qi = lax.broadcasted_iota(jnp.int32, (blk, 3 * blk), 0)
    kj = lax.broadcasted_iota(jnp.int32, (blk, 3 * blk), 1)
    in_window = jnp.abs(kj - blk - qi) <= A_WINDOW
    k_ext = jnp.concatenate([kp_ref[...], kc_ref[...], kn_ref[...]], axis=0)
    v_ext = jnp.concatenate([vp_ref[...], vc_ref[...], vn_ref[...]], axis=0)
    kx = kx_ref[...]
    vx = vx_ref[...]
    for t in range(WIN_GROUP):
        kpos = (step * WIN_GROUP + t) * blk + kj - blk
        valid = in_window & (kpos >= 0) & (kpos < seq)
        valid = jnp.concatenate([valid] * g_sz, axis=0)
        q = q_ref[t * blk:(t + 1) * blk, :]
        k_all = k_ext[t * blk:(t + 3) * blk]
        v_all = v_ext[t * blk:(t + 3) * blk]
        outs = []
        for g in range(A_KV_HEADS):
            lo, hi = g * HEAD_DIM, (g + 1) * HEAD_DIM
            heads = [g * g_sz + i for i in range(g_sz)]
            qs = jnp.concatenate([q[:, h * HEAD_DIM:(h + 1) * HEAD_DIM] for h in heads], axis=0)
            sink = jnp.concatenate(
                [jnp.full((blk, 1), sink_ref[h], F32) for h in heads], axis=0)
            s_loc = jnp.where(valid, _dot_nt(qs, k_all[:, lo:hi]), NEG_INF)
            s_ctx = _dot_nt(qs, kx[:, lo:hi])
            o = _softmax_pv([(s_ctx, vx[:, lo:hi]), (s_loc, v_all[:, lo:hi])], sink)
            outs.extend(o[i * blk:(i + 1) * blk] for i in range(g_sz))
        o_ref[t * blk:(t + 1) * blk, :] = jnp.concatenate(outs, axis=1).astype(o_ref.dtype)


WIN_GROUP = 4


def _window_attn(sink, aq, ak, av, akx, avx, batch, seq, ctx_len):
    nblk = seq // A_BLOCK
    nstep = nblk // WIN_GROUP
    kvw = A_KV_HEADS * HEAD_DIM
    qw = A_HEADS * HEAD_DIM
    rows = WIN_GROUP * A_BLOCK

    def edge(delta):
        return lambda b, n: (b * nblk + jnp.clip(n * WIN_GROUP + delta, 0, nblk - 1), 0)

    def own(b, n):
        return (b * nstep + n, 0)

    kv_specs = [pl.BlockSpec((A_BLOCK, kvw), edge(-1)), pl.BlockSpec((rows, kvw), own),
                pl.BlockSpec((A_BLOCK, kvw), edge(WIN_GROUP))]
    return pl.pallas_call(
        functools.partial(_win_kernel, seq=seq),
        grid=(batch, nstep),
        in_specs=[pl.BlockSpec(memory_space=pltpu.SMEM), pl.BlockSpec((rows, qw), own)]
        + kv_specs + kv_specs
        + [pl.BlockSpec((ctx_len, kvw), lambda b, n: (b, 0)),
           pl.BlockSpec((ctx_len, kvw), lambda b, n: (b, 0))],
        out_specs=pl.BlockSpec((rows, qw), own),
        out_shape=jax.ShapeDtypeStruct((batch * seq, qw), BF16),
        compiler_params=_params(("arbitrary", "arbitrary")),
        name="window_attn",
    )(sink, aq, ak, ak, ak, av, av, av, akx, avx)


NA_QROWS = 8
NA_Q = NA_QROWS * GRID_W
NA_KBLK = 4 * GRID_W
NA_K = 4 * NA_KBLK


NA_DR = 2 * NA_ROWS - 1


def _na_bias_tiles(rpb, rows):
    nj = rows // NA_QROWS
    kr = min(NA_ROWS, rows)
    col = jnp.arange(GRID_W)
    col_start = jnp.clip(col - NA_COLS // 2, 0, GRID_W - NA_COLS)
    cvalid = (col[None, :] >= col_start[:, None]) & (col[None, :] < col_start[:, None] + NA_COLS)
    dc = jnp.clip(col[None, :] - col[:, None], -(NA_COLS - 1), NA_COLS - 1) + (NA_COLS - 1)
    tiles = jnp.where(cvalid[None, None], rpb.astype(F32)[:, :, dc], NEG_INF)
    tiles = jnp.concatenate([tiles, jnp.full_like(tiles[:, :1], NEG_INF)], axis=1)
    idx = []
    for j in (0, min(1, nj - 1), nj - 1):
        r = j * NA_QROWS + jnp.arange(NA_QROWS)
        krow = j * NA_QROWS - NA_KBLK // GRID_W + jnp.arange(NA_K // GRID_W)
        start = jnp.clip(r - kr // 2, 0, rows - kr)
        rvalid = (krow[None, :] >= start[:, None]) & (krow[None, :] < start[:, None] + kr)
        dr = krow[None, :] - r[:, None] + (NA_ROWS - 1)
        idx.append(jnp.where(rvalid, dr, NA_DR))
    return tiles, jnp.stack(idx).reshape(-1).astype(jnp.int32)


def _na_kernel(idx_ref, q_ref, k0, k1, k2, k3, v0, v1, v2, v3, kx_ref, vx_ref, t_ref, o_ref, bias_scr,
               *, nj):
    j = pl.program_id(2)
    nkr = NA_K // GRID_W

    @pl.when((j == 0) | (j == 1) | (j == nj - 1))
    def _():
        variant = jnp.where(j == 0, 0, jnp.where(j == nj - 1, 2, 1))
        for hh in range(LANES // HEAD_DIM):
            for qr in range(NA_QROWS):
                for kr in range(nkr):
                    d = idx_ref[(variant * NA_QROWS + qr) * nkr + kr]
                    bias_scr[hh, qr * GRID_W:(qr + 1) * GRID_W, kr * GRID_W:(kr + 1) * GRID_W] = (
                        t_ref[hh, d])

    q = q_ref[...]
    k_all = jnp.concatenate([k0[...], k1[...], k2[...], k3[...]], axis=0)
    v_all = jnp.concatenate([v0[...], v1[...], v2[...], v3[...]], axis=0)
    kx = kx_ref[...]
    vx = vx_ref[...]
    outs = []
    for hh in range(LANES // HEAD_DIM):
        lo, hi = hh * HEAD_DIM, (hh + 1) * HEAD_DIM
        qh = q[:, lo:hi]
        s_loc = _dot_nt(qh, k_all[:, lo:hi]) + bias_scr[hh]
        s_ctx = _dot_nt(qh, kx[:, lo:hi])
        outs.append(_softmax_pv([(s_ctx, vx[:, lo:hi]), (s_loc, v_all[:, lo:hi])], None))
    o_ref[...] = jnp.concatenate(outs, axis=1).astype(o_ref.dtype)


def _na_attn(bq, bk, bv, bkx, bvx, tiles, tile_idx, batch, seq, ctx_len):
    nj = seq // NA_Q
    nkb = seq // NA_KBLK
    hp = B_HEADS * HEAD_DIM // LANES
    per = LANES // HEAD_DIM

    def kb(t):
        return lambda p, b, j, idx: (b * nkb + jnp.clip(2 * j - 1 + t, 0, nkb - 1), p)

    kv_specs = [pl.BlockSpec((NA_KBLK, LANES), kb(t)) for t in range(4)]
    grid_spec = pltpu.PrefetchScalarGridSpec(
        num_scalar_prefetch=1,
        grid=(hp, batch, nj),
        in_specs=[pl.BlockSpec((NA_Q, LANES), lambda p, b, j, idx: (b * nj + j, p))]
        + kv_specs + kv_specs
        + [pl.BlockSpec((ctx_len, LANES), lambda p, b, j, idx: (b, p)),
           pl.BlockSpec((ctx_len, LANES), lambda p, b, j, idx: (b, p)),
           pl.BlockSpec((per, NA_DR + 1, GRID_W, GRID_W), lambda p, b, j, idx: (p, 0, 0, 0))],
        out_specs=pl.BlockSpec((NA_Q, LANES), lambda p, b, j, idx: (b * nj + j, p)),
        scratch_shapes=[pltpu.VMEM((per, NA_Q, NA_K), F32)],
    )
    return pl.pallas_call(
        functools.partial(_na_kernel, nj=nj),
        grid_spec=grid_spec,
        out_shape=jax.ShapeDtypeStruct((batch * seq, B_HEADS * HEAD_DIM), BF16),
        compiler_params=_params(("arbitrary", "arbitrary", "arbitrary")),
        name="na_attn",
    )(tile_idx, bq, bk, bk, bk, bk, bv, bv, bv, bv, bkx, bvx, tiles)


def _ctx_ab_kernel(sink_ref, aq_ref, ak_ref, av_ref, bq_ref, bk_ref, bv_ref, oa_ref, ob_ref):
    ctx_len = aq_ref.shape[0]
    g_sz = A_HEADS // A_KV_HEADS
    aq = aq_ref[...]
    ak = ak_ref[...]
    av = av_ref[...]
    outs = []
    for g in range(A_KV_HEADS):
        lo, hi = g * HEAD_DIM, (g + 1) * HEAD_DIM
        heads = [g * g_sz + i for i in range(g_sz)]
        qs = jnp.concatenate([aq[:, h * HEAD_DIM:(h + 1) * HEAD_DIM] for h in heads], axis=0)
        sink = jnp.concatenate(
            [jnp.full((ctx_len, 1), sink_ref[h], F32) for h in heads], axis=0)
        o = _softmax_pv([(_dot_nt(qs, ak[:, lo:hi]), av[:, lo:hi])], sink)
        outs.extend(o[i * ctx_len:(i + 1) * ctx_len] for i in range(g_sz))
    oa_ref[...] = jnp.concatenate(outs, axis=1).astype(oa_ref.dtype)
    bq = bq_ref[...]
    bk = bk_ref[...]
    bv = bv_ref[...]
    outs = []
    for h in range(B_HEADS):
        lo, hi = h * HEAD_DIM, (h + 1) * HEAD_DIM
        outs.append(_softmax_pv([(_dot_nt(bq[:, lo:hi], bk[:, lo:hi]), bv[:, lo:hi])], None))
    ob_ref[...] = jnp.concatenate(outs, axis=1).astype(ob_ref.dtype)


def _ctx_ab_attn(sink, aq, ak, av, bq, bk, bv, batch, ctx_len):
    def spec(w):
        return pl.BlockSpec((ctx_len, w), lambda b: (b, 0))

    return pl.pallas_call(
        _ctx_ab_kernel,
        grid=(batch,),
        in_specs=[pl.BlockSpec(memory_space=pltpu.SMEM)]
        + [spec(a.shape[1]) for a in (aq, ak, av, bq, bk, bv)],
        out_specs=[spec(aq.shape[1]), spec(bq.shape[1])],
        out_shape=[jax.ShapeDtypeStruct(aq.shape, BF16), jax.ShapeDtypeStruct(bq.shape, BF16)],
        compiler_params=_params(("arbitrary",)),
        name="ctx_ab_attn",
    )(sink, aq, ak, av, bq, bk, bv)


def _layer_norm(t, g, b):
    mu = jnp.mean(t, -1, keepdims=True)
    c = t - mu
    var = jnp.mean(c * c, -1, keepdims=True)
    return c * lax.rsqrt(var + 1e-5) * g + b


def _post_attn_kernel(*refs, alpha, lat_tiles, with_ctx):
    n_src = 6 if with_ctx else 3
    srcs = refs[:n_src]
    (w_ref, mod_ref, lng_ref, lnb_ref, wr_hi_ref, wr_lo_ref, br_ref,
     x1_ref, h2_ref, lg_ref) = refs[n_src:]

    def run(o1_ref, o2_ref, x_ref):
        half = o1_ref.shape[1]
        y = _dot(o1_ref[...], w_ref[0:half, :]) + _dot(o2_ref[...], w_ref[half:, :])
        g1 = mod_ref[0, 2:3, :]
        x1 = _layer_norm(alpha * x_ref[...] + g1 * y, lng_ref[...], lnb_ref[...])
        x1_ref[...] = x1
        h2 = x1 * (1.0 + mod_ref[0, 4:5, :]) + mod_ref[0, 3:4, :]
        h_hi, h_lo = _split_bf16(h2)
        h2_ref[...] = h_hi
        wr_hi = wr_hi_ref[...]
        lg_ref[...] = (_dot(h_hi, wr_hi) + _dot(h_lo, wr_hi) + _dot(h_hi, wr_lo_ref[...])
                       + br_ref[...])

    if not with_ctx:
        run(*srcs)
    else:
        i = pl.program_id(0)
        pl.when(i < lat_tiles)(lambda: run(*srcs[:3]))
        pl.when(i >= lat_tiles)(lambda: run(*srcs[3:]))


def _post_attn(lat, ctx_src, w_bf, mod, gid, lng, lnb, wr_hi, wr_lo, br, n_lat, alpha, tm):
    half = lat[0].shape[1]
    d = lat[2].shape[1]
    lat_tiles = n_lat // tm
    n_rows = n_lat + (ctx_src[0].shape[0] if ctx_src is not None else 0)

    def lat_map(i):
        return (jnp.minimum(i, lat_tiles - 1), 0)

    def ctx_map(i):
        return (jnp.maximum(i - lat_tiles, 0), 0)

    def src_specs(index_map):
        return [pl.BlockSpec((tm, half), index_map), pl.BlockSpec((tm, half), index_map),
                pl.BlockSpec((tm, d), index_map)]

    in_specs = src_specs(lat_map)
    args = list(lat)
    if ctx_src is not None:
        in_specs += src_specs(ctx_map)
        args += list(ctx_src)

    def const(shape):
        return pl.BlockSpec(shape, lambda i: (0,) * len(shape))

    in_specs += [const((2 * half, d)),
                 pl.BlockSpec((1, 6, d), lambda i: (gid(i), 0, 0)),
                 const((1, d)), const((1, d)),
                 const((d, ROUTER_PAD)), const((d, ROUTER_PAD)), const((1, ROUTER_PAD))]
    args += [w_bf, mod, lng, lnb, wr_hi, wr_lo, br]
    widths = (d, d, ROUTER_PAD)
    dtypes = (F32, BF16, F32)
    return pl.pallas_call(
        functools.partial(_post_attn_kernel, alpha=alpha, lat_tiles=lat_tiles,
                          with_ctx=ctx_src is not None),
        grid=(n_rows // tm,),
        in_specs=in_specs,
        out_specs=[pl.BlockSpec((tm, w), lambda i: (i, 0)) for w in widths],
        out_shape=[jax.ShapeDtypeStruct((n_rows, w), t) for w, t in zip(widths, dtypes)],
        compiler_params=_params(("arbitrary",)),
        name="post_attn",
    )(*args)


def _moe_kernel(be_ref, xs_ref, wgu_ref, wdn_ref, ys_ref, wgu_bf, wdn_bf):
    i = pl.program_id(0)
    prev = be_ref[jnp.maximum(i - 1, 0)]
    changed = (i == 0) | (be_ref[i] != prev)

    @pl.when(changed)
    def _():
        wgu_bf[...] = wgu_ref[0].astype(BF16)
        wdn_bf[...] = wdn_ref[0].astype(BF16)

    de = wdn_bf.shape[0]
    gu = _dot(xs_ref[...], wgu_bf[...])
    gate = gu[:, :de]
    up = gu[:, de:]
    act = gate / (1.0 + jnp.exp(-gate)) * up
    ys_ref[...] = _dot(act.astype(BF16), wdn_bf[...]).astype(ys_ref.dtype)


def _moe_experts(block_e, xs, w_gu, w_dn):
    rows, d = xs.shape
    nb = rows // MOE_BLOCK
    de = w_dn.shape[1]
    grid_spec = pltpu.PrefetchScalarGridSpec(
        num_scalar_prefetch=1,
        grid=(nb,),
        in_specs=[
            pl.BlockSpec((MOE_BLOCK, d), lambda i, be: (i, 0)),
            pl.BlockSpec((1, d, 2 * de), lambda i, be: (be[i], 0, 0)),
            pl.BlockSpec((1, de, d), lambda i, be: (be[i], 0, 0)),
        ],
        out_specs=pl.BlockSpec((MOE_BLOCK, d), lambda i, be: (i, 0)),
        scratch_shapes=[pltpu.VMEM((d, 2 * de), BF16), pltpu.VMEM((de, d), BF16)],
    )
    return pl.pallas_call(
        _moe_kernel,
        grid_spec=grid_spec,
        out_shape=jax.ShapeDtypeStruct((rows, d), BF16),
        compiler_params=_params(("arbitrary",)),
        name="moe_experts",
    )(block_e, xs, w_gu, w_dn)


ROUTE_TILE = 512


def _route_kernel(lg_ref, rec_ref, cnt_ref, carry):
    i = pl.program_id(0)

    @pl.when(i == 0)
    def _():
        carry[...] = jnp.zeros(carry.shape, F32)

    lg = lg_ref[...]
    tm = lg.shape[0]
    lane = lax.broadcasted_iota(jnp.int32, lg.shape, 1)

    def first_lane(mask):
        return jnp.min(jnp.where(mask, lane, ROUTER_PAD), axis=1, keepdims=True)

    is_g = lane < N_GROUPS
    g_log = jnp.where(is_g, lg, NEG_INF)
    g_max = jnp.max(g_log, axis=1, keepdims=True)
    g_sum = jnp.sum(jnp.where(is_g, jnp.exp(lg - g_max), 0.0), axis=1, keepdims=True)
    g_val = 1.0 / g_sum
    g_idx = first_lane(g_log == g_max)
    lo = N_GROUPS + EXP_PER_GROUP * g_idx
    sel = (lane >= lo) & (lane < lo + EXP_PER_GROUP)
    e_log = jnp.where(sel, lg, NEG_INF)
    e_max = jnp.max(e_log, axis=1, keepdims=True)
    e_exp = jnp.where(sel, jnp.exp(lg - e_max), 0.0)
    e_prob = e_exp / jnp.sum(e_exp, axis=1, keepdims=True)
    p1 = jnp.where(sel, e_prob, -1.0)
    v1 = jnp.max(p1, axis=1, keepdims=True)
    i1 = first_lane(p1 == v1)
    p2 = jnp.where(lane == i1, -1.0, p1)
    v2 = jnp.max(p2, axis=1, keepdims=True)
    i2 = first_lane(p2 == v2)
    norm = g_val / (v1 + v2)

    hot1 = lane == i1
    hot2 = lane == i2
    hot = jnp.where(hot1 | hot2, 1.0, 0.0)
    rows = lax.broadcasted_iota(jnp.int32, (tm, tm), 0)
    cols = lax.broadcasted_iota(jnp.int32, (tm, tm), 1)
    before = jnp.where(cols < rows, 1.0, 0.0).astype(BF16)
    prefix = _dot(before, hot.astype(BF16)) + carry[0:1, :]
    r1 = jnp.sum(jnp.where(hot1, prefix, 0.0), axis=1, keepdims=True)
    r2 = jnp.sum(jnp.where(hot2, prefix, 0.0), axis=1, keepdims=True)
    carry[0:1, :] = carry[0:1, :] + jnp.sum(hot, axis=0, keepdims=True)

    fields = [(i1 - N_GROUPS).astype(F32), (i2 - N_GROUPS).astype(F32), r1, r2, v1 * norm, v2 * norm]
    rec = jnp.zeros(lg.shape, F32)
    for k, f in enumerate(fields):
        rec = jnp.where(lane == k, f, rec)
    rec_ref[...] = rec

    @pl.when(i == pl.num_programs(0) - 1)
    def _():
        cnt_ref[...] = carry[...]


def _route(logits, n_tok):
    rec, cnt = pl.pallas_call(
        _route_kernel,
        grid=(n_tok // ROUTE_TILE,),
        in_specs=[pl.BlockSpec((ROUTE_TILE, ROUTER_PAD), lambda i: (i, 0))],
        out_specs=[pl.BlockSpec((ROUTE_TILE, ROUTER_PAD), lambda i: (i, 0)),
                   pl.BlockSpec((8, ROUTER_PAD), lambda i: (0, 0))],
        out_shape=[jax.ShapeDtypeStruct((n_tok, ROUTER_PAD), F32),
                   jax.ShapeDtypeStruct((8, ROUTER_PAD), F32)],
        scratch_shapes=[pltpu.VMEM((8, ROUTER_PAD), F32)],
        compiler_params=_params(("arbitrary",)),
        name="route",
    )(logits)
    experts = rec[:, 0:2].astype(jnp.int32)
    rank = rec[:, 2:4].astype(jnp.int32)
    wts = rec[:, 4:6]
    counts = cnt[0, N_GROUPS:N_GROUPS + N_EXPERTS].astype(jnp.int32)

    n_asg = n_tok * TOP_K
    pcounts = ((counts + MOE_BLOCK - 1) // MOE_BLOCK) * MOE_BLOCK
    pends = jnp.cumsum(pcounts)
    pstarts = pends - pcounts
    dest = pstarts[experts] + rank
    nb = -(-n_asg // MOE_BLOCK) + N_EXPERTS
    block_start = jnp.arange(nb, dtype=jnp.int32) * MOE_BLOCK
    block_e = jnp.minimum((pends[None, :] <= block_start[:, None]).sum(1), N_EXPERTS - 1).astype(jnp.int32)
    tok =jnp.broadcast_to(jnp.arange(n_tok, dtype=jnp.int32)[:, None], (n_tok, TOP_K))
    row_tok = jnp.zeros((nb * MOE_BLOCK,), jnp.int32).at[dest.reshape(-1)].set(
        tok.reshape(-1), unique_indices=True)
    return wts, dest, row_tok, block_e


def _post_moe_kernel(x_ref, ya_ref, yb_ref, w_ref, mod_ref, lng_ref, lnb_ref, o_ref, *, alpha):
    w = w_ref[...]
    y = w[:, 0:1] * ya_ref[...] + w[:, 1:2] * yb_ref[...]
    g2 = mod_ref[0, 5:6, :]
    o_ref[...] = _layer_norm(alpha * x_ref[...] + g2 * y, lng_ref[...], lnb_ref[...])


def _post_moe(x1, ya, yb, wts, mod, gid, lng, lnb, alpha, tm):
    rows, d = x1.shape
    row = pl.BlockSpec((tm, d), lambda i: (i, 0))
    vec = pl.BlockSpec((1, d), lambda i: (0, 0))
    return pl.pallas_call(
        functools.partial(_post_moe_kernel, alpha=alpha),
        grid=(rows // tm,),
        in_specs=[row, row, row,
                  pl.BlockSpec((tm, TOP_K), lambda i: (i, 0)),
                  pl.BlockSpec((1, 6, d), lambda i: (gid(i), 0, 0)),
                  vec, vec],
        out_specs=row,
        out_shape=jax.ShapeDtypeStruct((rows, d), F32),
        compiler_params=_params(("arbitrary",)),
        name="post_moe",
    )(x1, ya, yb, wts, mod, lng, lnb)


CD_COLS = dict(cq=(0, 384), ckv=(384, 640), dq=(640, 1152), dk=(1152, 1664), dv=(1664, 2176),
               kr=(2176, 2304))
C_HEAD_PAD = LANES


def _rms(t, g, eps=1e-6):
    return t * lax.rsqrt(jnp.mean(t * t, -1, keepdims=True) + eps) * g


def _proj_cd_kernel(x_ref, mod_ref, w_ref, qn_ref, wuq_ref, kvn_ref, wukv_ref,
                    cos64_ref, sin64_ref, cos32_ref, sin32_ref,
                    qc_ref, kc_ref, vc_ref, dq_ref, dk_ref, dv_ref):
    h = _modulate(x_ref, mod_ref, 0).astype(BF16)
    r = _dot(h, w_ref[...])

    def cols(name):
        lo, hi = CD_COLS[name]
        return r[:, lo:hi]

    q = _dot(_rms(cols('cq'), qn_ref[...]).astype(BF16), wuq_ref[...])
    kv = _dot(_rms(cols('ckv'), kvn_ref[...]).astype(BF16), wukv_ref[...])
    c32, s32 = cos32_ref[...], sin32_ref[...]
    c64, s64 = cos64_ref[...], sin64_ref[...]
    q = _rope_wide(q, c32, s32, C_ROPE // 4)
    kr = _rope_tile(cols('kr'), c32, s32, C_ROPE // 4)
    dq = _rope_wide(cols('dq'), c64, s64, D_QK // 4)
    dk = _rope_wide(cols('dk'), c64, s64, D_QK // 4)
    kw = C_HEADS * C_HEAD_PAD
    qc_ref[0] = (q * ((C_NOPE + C_ROPE) ** -0.5 * LOG2E)).astype(BF16)
    kc_ref[0] = (kv[:, :kw] + jnp.concatenate([kr] * C_HEADS, axis=1)).astype(BF16)
    vc_ref[0] = kv[:, kw:].astype(BF16)
    dq_ref[0] = (dq * (D_QK ** -0.5 * LOG2E)).astype(BF16)
    dk_ref[0] = dk.astype(BF16)
    dv_ref[0] = cols('dv').astype(BF16)


def _proj_cd(xall, mod, w_bf, qn, wuq, kvn, wukv, tabs, batch, seq, ctx_len, tm):
    d = xall.shape[1]
    lat_t, ctx_t = seq // tm, ctx_len // tm
    n_keys = seq + ctx_len
    widths = (C_HEADS * C_HEAD_PAD, C_HEADS * C_HEAD_PAD, C_HEADS * C_V,
              2 * D_HEADS * D_QK, 2 * D_HEADS * D_QK, D_HEADS * D_V)

    def src(b, t):
        return (jnp.where(t < lat_t, b * lat_t + t, batch * lat_t + b * ctx_t + t - lat_t), 0)

    def full(a):
        return pl.BlockSpec(a.shape, lambda b, t: (0,) * a.ndim)

    tab = pl.BlockSpec((tm, LANES), lambda b, t: (t, 0))
    return pl.pallas_call(
        _proj_cd_kernel,
        grid=(batch, lat_t + ctx_t),
        in_specs=[pl.BlockSpec((tm, d), src),
                  pl.BlockSpec((1, 6, d), lambda b, t: (jnp.where(t < lat_t, b, batch), 0, 0)),
                  full(w_bf), full(qn), full(wuq), full(kvn), full(wukv), tab, tab, tab, tab],
        out_specs=[pl.BlockSpec((1, tm, w), lambda b, t: (b, t, 0)) for w in widths],
        out_shape=[jax.ShapeDtypeStruct((batch, n_keys, w), BF16) for w in widths],
        compiler_params=_params(("arbitrary", "arbitrary")),
        name="proj_cd",
    )(xall, mod, w_bf, qn, wuq, kvn, wukv, *tabs)


DENSE_SHIFT = 80.0
DENSE_L_MIN = 2.0 ** -60
DENSE_L_MAX = 2.0 ** 120
DENSE_TQ = 1024
DENSE_TK_MAX = 1280
DENSE_UNROLL_MAX = 13
MXU_DEPTH = 256
PROJ_CD_TILE = 256
ONES_ROWS = 16


def _sq_norm_row(x):
    ones = jnp.ones((8, x.shape[1]), BF16)
    return _dot_nt(ones, (x * x).astype(BF16))[0:1] * 1.02


def _key_norm_max(kchunk, n_chunks, masks):
    def body(j, mx):
        kf = kchunk(j).astype(F32)
        ksq = kf * kf
        out = []
        for msk, cur in zip(masks, mx):
            part = ksq if msk is None else jnp.where(msk, ksq, 0.0)
            rn = jnp.sum(part, axis=1, keepdims=True)
            out.append(jnp.maximum(cur, jnp.max(rn, axis=0, keepdims=True)))
        return tuple(out)

    mx = lax.fori_loop(0, n_chunks, body, tuple(jnp.zeros((1, 1), F32) for _ in masks))
    return [jnp.sqrt(v) * 1.01 for v in mx]


def _safe_online(q, kchunk, vt_ref, n_chunks, m_scr, l_scr, acc_scr):
    m_scr[...] = jnp.full(m_scr.shape, NEG_INF, F32)
    l_scr[...] = jnp.zeros(l_scr.shape, F32)
    acc_scr[...] = jnp.zeros(acc_scr.shape, F32)

    def body(j, carry):
        s = _dot_nt(kchunk(j), q)
        m_prev = m_scr[...]
        m_new = jnp.maximum(m_prev, s.max(0, keepdims=True))
        alpha = jnp.exp2(m_prev - m_new)
        p = jnp.exp2(s - m_new)
        l_scr[...] = alpha * l_scr[...] + p.sum(0, keepdims=True)
        acc_scr[...] = alpha * acc_scr[...] + _dot(vt_ref[0, 0, j], p.astype(BF16))
        m_scr[...] = m_new
        return carry

    lax.fori_loop(0, n_chunks, body, 0)


def _denominators_ok(*ls):
    ok = None
    for l in ls:
        cur = (l > DENSE_L_MIN) & (l < DENSE_L_MAX)
        ok = cur if ok is None else ok & cur
    return jnp.max(jnp.where(ok, 0.0, 1.0)) == 0.0


def _mla_kernel(q_ref, k_ref, vt_ref, o_ref, kmax_scr, acc_scr, m_scr, l_scr,
                *, n_chunks, tk, unroll, depth):
    def kchunk(j):
        return k_ref[0, pl.ds(pl.multiple_of(j * tk, tk), tk), :]

    @pl.when(pl.program_id(2) == 0)
    def _():
        (kmax,) = _key_norm_max(kchunk, n_chunks, [None])
        kmax_scr[...] = jnp.broadcast_to(kmax, kmax_scr.shape)

    q = q_ref[0]
    m_row = jnp.sqrt(_sq_norm_row(q.astype(F32))) * kmax_scr[0:1, 0:1] - DENSE_SHIFT
    acc_scr[...] = jnp.zeros(acc_scr.shape, F32)

    def body(it, carry):
        j0 = it * unroll
        pend = [_dot_nt(kchunk(j0 + u), q) for u in range(min(depth, unroll))]
        acc = acc_scr[...]
        for u in range(unroll):
            s = pend.pop(0)
            if u + depth < unroll:
                pend.append(_dot_nt(kchunk(j0 + u + depth), q))
            acc = acc + _dot(vt_ref[0, 0, j0 + u], jnp.exp2(s - m_row).astype(BF16))
        acc_scr[...] = acc
        return carry

    lax.fori_loop(0, n_chunks // unroll, body, 0)
    l = acc_scr[C_V:C_V + 1, :]
    o_ref[0] = (acc_scr[0:C_V, :] / l).astype(o_ref.dtype)

    @pl.when(jnp.logical_not(_denominators_ok(l)))
    def _():
        _safe_online(q, kchunk, vt_ref, n_chunks, m_scr, l_scr, acc_scr)
        o_ref[0] = (acc_scr[0:C_V, :] / l_scr[...]).astype(o_ref.dtype)


def _mla_attn(qc, kc, vt, batch, seq, tq, tk, unroll, depth):
    n_keys = kc.shape[1]
    nq, n_chunks = seq // tq, n_keys // tk
    rows = C_V + ONES_ROWS
    return pl.pallas_call(
        functools.partial(_mla_kernel, n_chunks=n_chunks, tk=tk, unroll=unroll, depth=depth),
        grid=(batch, C_HEADS, nq),
        in_specs=[
            pl.BlockSpec((1, tq, C_HEAD_PAD), lambda b, h, i: (b, i, h)),
            pl.BlockSpec((1, n_keys, C_HEAD_PAD), lambda b, h, i: (b, 0, h)),
            pl.BlockSpec((1, 1, n_chunks, rows, tk), lambda b, h, i: (b, h, 0, 0, 0)),
        ],
        out_specs=pl.BlockSpec((1, C_V, tq), lambda b, h, i: (b, h, i)),
        out_shape=jax.ShapeDtypeStruct((batch, C_HEADS * C_V, seq), BF16),
        scratch_shapes=[pltpu.VMEM((8, LANES), F32), pltpu.VMEM((rows, tq), F32),
                        pltpu.VMEM((1, tq), F32), pltpu.VMEM((1, tq), F32)],
        compiler_params=_params(("arbitrary", "arbitrary", "arbitrary")),
        name="mla_attn",
    )(qc, kc, vt)


def _diff_kernel(lam_ref, q_ref, k_ref, vt_ref, subln_ref, o_ref, kmax_scr, acc1_scr, acc2_scr,
                 m_scr, l_scr, *, n_chunks, tk, unroll, depth, out_scale):
    def kchunk(j):
        return k_ref[0, pl.ds(pl.multiple_of(j * tk, tk), tk), :]

    @pl.when(pl.program_id(2) == 0)
    def _():
        klane = lax.broadcasted_iota(jnp.int32, (tk, 2 * D_QK), 1)
        k1, k2 = _key_norm_max(kchunk, n_chunks, [klane < D_QK, klane >= D_QK])
        kmax_scr[0:1, :] = jnp.broadcast_to(k1, (1, LANES))
        kmax_scr[1:2, :] = jnp.broadcast_to(k2, (1, LANES))

    q = q_ref[0]
    qlane = lax.broadcasted_iota(jnp.int32, q.shape, 1)
    zero = jnp.zeros_like(q)
    q1 = jnp.where(qlane < D_QK, q, zero)
    q2 = jnp.where(qlane >= D_QK, q, zero)
    m1 = jnp.sqrt(_sq_norm_row(q1.astype(F32))) * kmax_scr[0:1, 0:1] - DENSE_SHIFT
    m2 = jnp.sqrt(_sq_norm_row(q2.astype(F32))) * kmax_scr[1:2, 0:1] - DENSE_SHIFT
    acc1_scr[...] = jnp.zeros(acc1_scr.shape, F32)
    acc2_scr[...] = jnp.zeros(acc2_scr.shape, F32)

    def scores(j):
        k = kchunk(j)
        return _dot_nt(k, q1), _dot_nt(k, q2)

    def body(it, carry):
        j0 = it * unroll
        pend = [scores(j0 + u) for u in range(min(depth, unroll))]
        a1 = acc1_scr[...]
        a2 = acc2_scr[...]
        for u in range(unroll):
            s1, s2 = pend.pop(0)
            if u + depth < unroll:
                pend.append(scores(j0 + u + depth))
            vt = vt_ref[0, 0, j0 + u]
            a1 = a1 + _dot(vt, jnp.exp2(s1 - m1).astype(BF16))
            a2 = a2 + _dot(vt, jnp.exp2(s2 - m2).astype(BF16))
        acc1_scr[...] = a1
        acc2_scr[...] = a2
        return carry

    lax.fori_loop(0, n_chunks // unroll, body, 0)

    def finish(o1, o2):
        o = o1 - lam_ref[0] * o2
        o = o * lax.rsqrt(jnp.mean(o * o, 0, keepdims=True) + 1e-6) * subln_ref[...] * out_scale
        o_ref[0] = o.astype(o_ref.dtype)

    l1 = acc1_scr[D_V:D_V + 1, :]
    l2 = acc2_scr[D_V:D_V + 1, :]
    finish(acc1_scr[0:D_V, :] / l1, acc2_scr[0:D_V, :] / l2)

    @pl.when(jnp.logical_not(_denominators_ok(l1, l2)))
    def _():
        _safe_online(q1, kchunk, vt_ref, n_chunks, m_scr, l_scr, acc1_scr)
        o1 = acc1_scr[0:D_V, :] / l_scr[...]
        _safe_online(q2, kchunk, vt_ref, n_chunks, m_scr, l_scr, acc2_scr)
        finish(o1, acc2_scr[0:D_V, :] / l_scr[...])


def _diff_attn(lam, dq, dk, vt, subln_col, out_scale, batch, seq, tq, tk, unroll, depth):
    n_keys = dk.shape[1]
    nq, n_chunks = seq // tq, n_keys // tk
    rows = D_V + ONES_ROWS
    pair = 2 * D_QK
    return pl.pallas_call(
        functools.partial(_diff_kernel, n_chunks=n_chunks, tk=tk, unroll=unroll, depth=depth,
                          out_scale=out_scale),
        grid=(batch, D_HEADS, nq),
        in_specs=[
            pl.BlockSpec(memory_space=pltpu.SMEM),
            pl.BlockSpec((1, tq, pair), lambda b, h, i: (b, i, h)),
            pl.BlockSpec((1, n_keys, pair), lambda b, h, i: (b, 0, h)),
            pl.BlockSpec((1, 1, n_chunks, rows, tk), lambda b, h, i: (b, h, 0, 0, 0)),
            pl.BlockSpec((D_V, 1), lambda b, h, i: (0, 0)),
        ],
        out_specs=pl.BlockSpec((1, D_V, tq), lambda b, h, i: (b, h, i)),
        out_shape=jax.ShapeDtypeStruct((batch, D_HEADS * D_V, seq), BF16),
        scratch_shapes=[pltpu.VMEM((8, LANES), F32), pltpu.VMEM((rows, tq), F32),
                        pltpu.VMEM((rows, tq), F32), pltpu.VMEM((1, tq), F32), pltpu.VMEM((1, tq), F32)],
        compiler_params=_params(("arbitrary", "arbitrary", "arbitrary")),
        name="diff_attn",
    )(lam, dq, dk, vt, subln_col)


def _vt_chunks(v, heads, dv, tk):
    batch, n_keys, _ = v.shape
    n_chunks = n_keys // tk
    vt = v.reshape(batch, n_chunks, tk, heads, dv).transpose(0, 3, 1, 4, 2)
    extra = jnp.zeros((batch, heads, n_chunks, ONES_ROWS, tk), v.dtype).at[:, :, :, 0, :].set(1.0)
    return jnp.concatenate([vt, extra], axis=3)


def _rope_tables(n_tok, dim, lane_lo):
    t = jnp.arange(n_tok)
    pos_r = (t // GRID_W).astype(F32)
    pos_c = (t % GRID_W).astype(F32)
    quarter = dim // 4
    inv = ROPE_BASE ** (-jnp.arange(quarter, dtype=F32) / quarter)
    ang_r = pos_r[:, None] * inv
    ang_c = pos_c[:, None] * inv
    ang = jnp.concatenate([ang_r, ang_r, ang_c, ang_c], -1)
    sign = jnp.tile(jnp.concatenate([-jnp.ones(quarter), jnp.ones(quarter)]), 2).astype(F32)
    cos, sin = jnp.cos(ang), jnp.sin(ang) * sign
    reps = (LANES - lane_lo) // dim
    cos = jnp.concatenate([jnp.ones((n_tok, lane_lo), F32)] + [cos] * reps, axis=1)
    sin = jnp.concatenate([jnp.zeros((n_tok, lane_lo), F32)] + [sin] * reps, axis=1)
    return cos, sin


def _cd_weights(w_in, w_uq, w_ukv):
    d = w_in.shape[0]
    s0 = C_Q_RANK
    s1 = s0 + C_KV_RANK
    s2 = s1 + C_ROPE
    s3 = s2 + 2 * D_HEADS * D_QK
    s4 = s3 + 2 * D_HEADS * D_QK
    kr = jnp.zeros((d, LANES), F32).at[:, C_NOPE:C_NOPE + C_ROPE].set(w_in[:, s1:s2])
    w_in_p = jnp.concatenate([w_in[:, :s1], w_in[:, s2:s3], w_in[:, s3:s4], w_in[:, s4:], kr], axis=1)
    dqk = C_NOPE + C_ROPE
    wq = w_uq.reshape(C_Q_RANK, C_HEADS, dqk)
    wq = jnp.pad(wq, ((0, 0), (0, 0), (0, C_HEAD_PAD - dqk))).reshape(C_Q_RANK, C_HEADS * C_HEAD_PAD)
    wkv = w_ukv.reshape(C_KV_RANK, C_HEADS, C_NOPE + C_V)
    wk = jnp.pad(wkv[:, :, :C_NOPE], ((0, 0), (0, 0), (0, C_HEAD_PAD - C_NOPE)))
    wk = wk.reshape(C_KV_RANK, C_HEADS * C_HEAD_PAD)
    wv = wkv[:, :, C_NOPE:].reshape(C_KV_RANK, C_HEADS * C_V)
    return w_in_p.astype(BF16), wq.astype(BF16), jnp.concatenate([wk, wv], axis=1).astype(BF16)


def _moe_layer(h2, logits, x1, w_gu, w_dn, mod, gid, lng, lnb, alpha, tm):
    n_tok = h2.shape[0]
    wts, dest, row_tok, block_e = _route(logits, n_tok)
    xs = jnp.take(h2, row_tok, axis=0)
    ys = _moe_experts(block_e, xs, w_gu, w_dn)
    ya = jnp.take(ys, dest[:, 0], axis=0)
    yb = jnp.take(ys, dest[:, 1], axis=0)
    return _post_moe(x1, ya, yb, wts.astype(F32), mod, gid, lng, lnb, alpha, tm)


def kernel(x, c, ctx, c_ctx, w_ada, b_ada, ln_g, ln_b, ab_w_in, a_sink, b_rpb, ab_w_out, cd_w_in, c_q_norm, c_w_uq, c_kv_norm, c_w_ukv, d_lambda, d_subln, cd_w_out, w_group, b_group, w_exp_router, b_exp_router, w_gate_up, w_down):
    batch, seq, d = x.shape
    ctx_len = ctx.shape[1]
    depth = w_ada.shape[0]
    alpha = (2 * depth) ** 0.25
    n_lat = batch * seq
    n_ctx = batch * ctx_len
    tm = 512
    tq = DENSE_TQ
    assert depth == 2 and seq % NA_Q == 0 and seq % tq == 0 and n_ctx % tm == 0
    assert seq % PROJ_CD_TILE == 0 and ctx_len % PROJ_CD_TILE == 0
    assert seq % (WIN_GROUP * A_BLOCK) == 0 and seq // NA_Q >= 2
    key_tiles = (seq + ctx_len) // MXU_DEPTH
    tk = MXU_DEPTH * max(u for u in range(1, DENSE_TK_MAX // MXU_DEPTH + 1) if key_tiles % u == 0)
    n_chunks = (seq + ctx_len) // tk

    def unroll_for(cap):
        return max(u for u in range(1, cap + 1) if n_chunks % u == 0)
    lat_tiles = seq // tm

    def gid_lat(i):
        return i // lat_tiles

    def gid_ctx(i):
        return batch

    def gid_all(i):
        return jnp.minimum(i // lat_tiles, batch)

    c_all = jnp.zeros((8, d), F32).at[:batch].set(c).at[batch].set(c_ctx)
    mod = _ada(c_all, w_ada, b_ada).reshape(depth, 8, 6, d)

    cos64, sin64 = _rope_tables(seq, HEAD_DIM, 0)
    cos32, sin32 = _rope_tables(seq, C_ROPE, C_NOPE)

    def router_weights(l):
        wr = jnp.zeros((d, ROUTER_PAD), F32)
        wr = wr.at[:, :N_GROUPS].set(w_group[l]).at[:, N_GROUPS:N_GROUPS + N_EXPERTS].set(w_exp_router[l])
        br = jnp.zeros((1, ROUTER_PAD), F32)
        br = br.at[0, :N_GROUPS].set(b_group[l]).at[0, N_GROUPS:N_GROUPS + N_EXPERTS].set(b_exp_router[l])
        hi = wr.astype(BF16)
        lo = (wr - hi.astype(F32)).astype(BF16)
        return hi, lo, br

    x2d = x.reshape(n_lat, d)
    ctx2d = ctx.reshape(n_ctx, d)

    l = 0
    w_in_bf = ab_w_in[0].astype(BF16)
    aq, ak, av, bq, bk, bv = _proj_ab(x2d, 0, n_lat, mod[l], gid_lat, w_in_bf, cos64, sin64,
                                      lat_tiles, True, tm)
    aqx, akx, avx, bqx, bkx, bvx = _proj_ab(ctx2d, 0, n_ctx, mod[l], gid_ctx, w_in_bf, cos64, sin64,
                                            1, False, tm)
    sink = a_sink[0].astype(F32)
    oa = _window_attn(sink, aq, ak, av, akx, avx, batch, seq, ctx_len)
    tiles, tile_idx = _na_bias_tiles(b_rpb[0], seq // GRID_W)
    ob = _na_attn(bq, bk, bv, bkx, bvx, tiles, tile_idx, batch, seq, ctx_len)
    oax, obx = _ctx_ab_attn(sink, aqx, akx, avx, bqx, bkx, bvx, batch, ctx_len)

    n_all = n_lat + n_ctx
    w_out_bf = ab_w_out[0].astype(BF16)
    lng, lnb = ln_g[l, 0][None], ln_b[l, 0][None]
    wr_hi, wr_lo, br = router_weights(l)
    x1, h2, logits = _post_attn((oa, ob, x2d), (oax, obx, ctx2d), w_out_bf, mod[l], gid_all,
                                lng, lnb, wr_hi, wr_lo, br, n_lat, alpha, tm)
    xall = _moe_layer(h2, logits, x1, w_gate_up[l], w_down[l], mod[l], gid_all,
                      ln_g[l, 1][None], ln_b[l, 1][None], alpha, tm)

    l = 1
    lam_init = 0.8 - 0.6 * math.exp(-0.3 * l)
    lp = d_lambda[0].astype(F32)
    lam = (jnp.exp(jnp.sum(lp[0] * lp[1])) - jnp.exp(jnp.sum(lp[2] * lp[3])) + lam_init).reshape(1)
    w_in_p, wuq_p, wukv_p = _cd_weights(cd_w_in[0], c_w_uq[0], c_w_ukv[0])
    qn, kvn = c_q_norm[0][None].astype(F32), c_kv_norm[0][None].astype(F32)
    def with_ctx_identity(cos, sin):
        return (jnp.concatenate([cos, jnp.ones((ctx_len, LANES), F32)]),
                jnp.concatenate([sin, jnp.zeros((ctx_len, LANES), F32)]))

    tabs = with_ctx_identity(cos64, sin64) + with_ctx_identity(cos32, sin32)
    qc, kc, vc, dq, dk, dv = _proj_cd(xall, mod[l], w_in_p, qn, wuq_p, kvn, wukv_p, tabs,
                                      batch, seq, ctx_len, PROJ_CD_TILE)
    unroll = unroll_for(DENSE_UNROLL_MAX)
    oc_t = _mla_attn(qc, kc, _vt_chunks(vc, C_HEADS, C_V, tk), batch, seq, tq, tk, unroll, 1)
    od_t = _diff_attn(lam, dq, dk, _vt_chunks(dv, D_HEADS, D_V, tk),
                      d_subln[0].astype(F32).reshape(D_V, 1), 1.0 - lam_init, batch, seq, tq, tk,
                      unroll, 1)
    oc = oc_t.transpose(0, 2, 1).reshape(n_lat, -1)
    od = od_t.transpose(0, 2, 1).reshape(n_lat, -1)

    wr_hi, wr_lo, br = router_weights(l)
    x1, h2, logits = _post_attn((oc, od, xall), None, cd_w_out[0].astype(BF16), mod[l], gid_lat,
                                ln_g[l, 0][None], ln_b[l, 0][None], wr_hi, wr_lo, br,
                                n_lat, alpha, tm)
    out = _moe_layer(h2, logits, x1, w_gate_up[l], w_down[l], mod[l], gid_lat,
                     ln_g[l, 1][None], ln_b[l, 1][None], alpha, tm)
    return out.reshape(batch, seq, d)
```

```python
import functools
import math

import jax
import jax.numpy as jnp
from jax import lax
from jax.experimental import pallas as pl
from jax.experimental.pallas import tpu as pltpu

F32 = jnp.float32
BF16 = jnp.bfloat16

GRID_W = 64
HEAD_DIM = 64
ROPE_BASE = 10000.0
NEG_INF = -1e30
LOG2E = 1.4426950408889634

A_HEADS = 8
A_KV_HEADS = 2
A_WINDOW = 128
A_BLOCK = 128
B_HEADS = 8
NA_ROWS = 8
NA_COLS = 16
C_HEADS = 8
C_Q_RANK = 384
C_KV_RANK = 256
C_NOPE = 64
C_ROPE = 32
C_V = 64
D_HEADS = 4
D_QK = 64
D_V = 128
N_GROUPS = 4
EXP_PER_GROUP = 8
N_EXPERTS = N_GROUPS * EXP_PER_GROUP
TOP_K = 2
MOE_BLOCK = 512
MOE_SUB = 256
ROUTER_PAD = 128

LANES = 128
VMEM_LIMIT = 56 * 1024 * 1024


def _params(sem):
    return pltpu.CompilerParams(dimension_semantics=sem, vmem_limit_bytes=VMEM_LIMIT)


def _dot(a, b):
    return jnp.dot(a, b, preferred_element_type=F32)


def _dot_nt(a, b):
    return lax.dot_general(a, b, (((1,), (1,)), ((), ())), preferred_element_type=F32)


def _split_bf16(x):
    hi = x.astype(BF16)
    lo = (x - hi.astype(F32)).astype(BF16)
    return hi, lo


HI_HALF = 0xFFFF0000


def _pack_halves(x):
    w = x.shape[1] // 2
    lo = lax.bitcast_convert_type(x[:, :w].astype(BF16).astype(F32), jnp.uint32)
    hi = lax.bitcast_convert_type(x[:, w:].astype(BF16).astype(F32), jnp.uint32)
    return (lo >> 16) | (hi & jnp.uint32(HI_HALF))


def _unpack_halves(u):
    lo = lax.bitcast_convert_type(u << 16, F32)
    hi = lax.bitcast_convert_type(u & jnp.uint32(HI_HALF), F32)
    return jnp.concatenate([lo, hi], axis=1)


def _ada_kernel(c_ref, w_ref, b_ref, o_ref):
    c = c_ref[...]
    a = c / (1.0 + jnp.exp(-c))
    a_hi, a_lo = _split_bf16(a)
    w_hi, w_lo = _split_bf16(w_ref[0])
    acc = _dot(a_hi, w_hi) + _dot(a_lo, w_hi) + _dot(a_hi, w_lo)
    o_ref[0] = acc + b_ref[0]


def _ada(c_all, w_ada, b_ada):
    depth, d, n = w_ada.shape
    tn = 1536
    return pl.pallas_call(
        _ada_kernel,
        grid=(depth, n // tn),
        in_specs=[
            pl.BlockSpec((8, d), lambda l, j: (0, 0)),
            pl.BlockSpec((1, d, tn), lambda l, j: (l, 0, j)),
            pl.BlockSpec((1, 1, tn), lambda l, j: (l, 0, j)),
        ],
        out_specs=pl.BlockSpec((1, 8, tn), lambda l, j: (l, 0, j)),
        out_shape=jax.ShapeDtypeStruct((depth, 8, n), F32),
        compiler_params=_params(("arbitrary", "arbitrary")),
        name="ada",
    )(c_all, w_ada, b_ada.reshape(depth, 1, n))


def _rope_tile(t, cos, sin_signed, q):
    lane = lax.broadcasted_iota(jnp.int32, t.shape, 1)
    first = (lane & (2 * q - 1)) < q
    up = pltpu.roll(t, LANES - q, 1)
    dn = pltpu.roll(t, q, 1)
    return t * cos + jnp.where(first, up, dn) * sin_signed


def _rope_wide(t, cos, sin_signed, q):
    n = t.shape[1] // LANES
    return jnp.concatenate(
        [_rope_tile(t[:, i * LANES:(i + 1) * LANES], cos, sin_signed, q) for i in range(n)], axis=1)


def _modulate(x_ref, mod_ref, which):
    sh = mod_ref[0, 3 * which:3 * which + 1, :]
    sc = mod_ref[0, 3 * which + 1:3 * which + 2, :]
    return x_ref[...] * (1.0 + sc) + sh


def _proj_ab_kernel(x_ref, mod_ref, w_ref, cos_ref, sin_ref,
                    aq_ref, ak_ref, av_ref, bq_ref, bk_ref, bv_ref, *, rope):
    h = _modulate(x_ref, mod_ref, 0).astype(BF16)
    r = _dot(h, w_ref[...])
    aq = r[:, 0:512]
    ak = r[:, 512:640]
    if rope:
        cos = cos_ref[...]
        sin = sin_ref[...]
        aq = _rope_wide(aq, cos, sin, 16)
        ak = _rope_wide(ak, cos, sin, 16)
    scale = HEAD_DIM ** -0.5
    aq_ref[...] = (aq * scale).astype(BF16)
    ak_ref[...] = ak.astype(BF16)
    av_ref[...] = r[:, 640:768].astype(BF16)
    bq_ref[...] = (r[:, 768:1280] * scale).astype(BF16)
    bk_ref[...] = r[:, 1280:1792].astype(BF16)
    bv_ref[...] = r[:, 1792:2304].astype(BF16)


def _proj_ab(x2d, row0, nrows, mod, gid, w_bf, cos, sin, pos_blocks, rope, tm):
    d = x2d.shape[1]
    n_in = w_bf.shape[1]
    b0 = row0 // tm
    widths = (512, 128, 128, 512, 512, 512)
    return pl.pallas_call(
        functools.partial(_proj_ab_kernel, rope=rope),
        grid=(nrows // tm,),
        in_specs=[
            pl.BlockSpec((tm, d), lambda i: (b0 + i, 0)),
            pl.BlockSpec((1, 6, d), lambda i: (gid(i), 0, 0)),
            pl.BlockSpec((d, n_in), lambda i: (0, 0)),
            pl.BlockSpec((tm, LANES), lambda i: (i % pos_blocks, 0)),
            pl.BlockSpec((tm, LANES), lambda i: (i % pos_blocks, 0)),
        ],
        out_specs=[pl.BlockSpec((tm, w), lambda i: (i, 0)) for w in widths],
        out_shape=[jax.ShapeDtypeStruct((nrows, w), BF16) for w in widths],
        compiler_params=_params(("arbitrary",)),
        name="proj_ab",
    )(x2d, mod, w_bf, cos, sin)


def _softmax_pv(parts, sink_col):
    m = parts[0][0].max(-1, keepdims=True)
    for s, _ in parts[1:]:
        m = jnp.maximum(m, s.max(-1, keepdims=True))
    if sink_col is not None:
        m = jnp.maximum(m, sink_col)
    denom = None if sink_col is None else jnp.exp(sink_col - m)
    o = None
    for s, v in parts:
        e = jnp.exp(s - m)
        d = e.sum(-1, keepdims=True)
        denom = d if denom is None else denom + d
        pv = _dot(e.astype(BF16), v)
        o = pv if o is None else o + pv
    return o / denom


def _win_kernel(sink_ref, q_ref, kp_ref, kc_ref, kn_ref, vp_ref, vc_ref, vn_ref,
                kx_ref, vx_ref, o_ref, *, seq):
    step = pl.program_id(1)
    blk = A_BLOCK
    g_sz = A_HEADS // A_KV_HEADS
    qi = lax.broadcasted_iota(jnp.int32, (blk, 3 * blk), 0)
    kj = lax.broadcasted_iota(jnp.int32, (blk, 3 * blk), 1)
    in_window = jnp.abs(kj - blk - qi) <= A_WINDOW
    k_ext = jnp.concatenate([kp_ref[...], kc_ref[...], kn_ref[...]], axis=0)
    v_ext = jnp.concatenate([vp_ref[...], vc_ref[...], vn_ref[...]], axis=0)
    kx = kx_ref[...]
    vx = vx_ref[...]
    for t in range(WIN_GROUP):
        kpos = (step * WIN_GROUP + t) * blk + kj - blk
        valid = in_window & (kpos >= 0) & (kpos < seq)
        valid = jnp.concatenate([valid] * g_sz, axis=0)
        q = q_ref[t * blk:(t + 1) * blk, :]
        k_all = k_ext[t * blk:(t + 3) * blk]
        v_all = v_ext[t * blk:(t + 3) * blk]
        outs = []
        for g in range(A_KV_HEADS):
            lo, hi = g * HEAD_DIM, (g + 1) * HEAD_DIM
            heads = [g * g_sz + i for i in range(g_sz)]
            qs = jnp.concatenate([q[:, h * HEAD_DIM:(h + 1) * HEAD_DIM] for h in heads], axis=0)
            sink = jnp.concatenate(
                [jnp.full((blk, 1), sink_ref[h], F32) for h in heads], axis=0)
            s_loc = jnp.where(valid, _dot_nt(qs, k_all[:, lo:hi]), NEG_INF)
            s_ctx = _dot_nt(qs, kx[:, lo:hi])
            o = _softmax_pv([(s_ctx, vx[:, lo:hi]), (s_loc, v_all[:, lo:hi])], sink)
            outs.extend(o[i * blk:(i + 1) * blk] for i in range(g_sz))
        o_ref[t * blk:(t + 1) * blk, :] = jnp.concatenate(outs, axis=1).astype(o_ref.dtype)


WIN_GROUP = 4


def _window_attn(sink, aq, ak, av, akx, avx, batch, seq, ctx_len):
    nblk = seq // A_BLOCK
    nstep = nblk // WIN_GROUP
    kvw = A_KV_HEADS * HEAD_DIM
    qw = A_HEADS * HEAD_DIM
    rows = WIN_GROUP * A_BLOCK

    def edge(delta):
        return lambda b, n: (b * nblk + jnp.clip(n * WIN_GROUP + delta, 0, nblk - 1), 0)

    def own(b, n):
        return (b * nstep + n, 0)

    kv_specs = [pl.BlockSpec((A_BLOCK, kvw), edge(-1)), pl.BlockSpec((rows, kvw), own),
                pl.BlockSpec((A_BLOCK, kvw), edge(WIN_GROUP))]
    return pl.pallas_call(
        functools.partial(_win_kernel, seq=seq),
        grid=(batch, nstep),
        in_specs=[pl.BlockSpec(memory_space=pltpu.SMEM), pl.BlockSpec((rows, qw), own)]
        + kv_specs + kv_specs
        + [pl.BlockSpec((ctx_len, kvw), lambda b, n: (b, 0)),
           pl.BlockSpec((ctx_len, kvw), lambda b, n: (b, 0))],
        out_specs=pl.BlockSpec((rows, qw), own),
        out_shape=jax.ShapeDtypeStruct((batch * seq, qw), BF16),
        compiler_params=_params(("arbitrary", "arbitrary")),
        name="window_attn",
    )(sink, aq, ak, ak, ak, av, av, av, akx, avx)


NA_QROWS = 8
NA_Q = NA_QROWS * GRID_W
NA_KBLK = 4 * GRID_W
NA_K = 4 * NA_KBLK


NA_DR = 2 * NA_ROWS - 1


def _na_bias_tiles(rpb, rows):
    nj = rows // NA_QROWS
    kr = min(NA_ROWS, rows)
    col = jnp.arange(GRID_W)
    col_start = jnp.clip(col - NA_COLS // 2, 0, GRID_W - NA_COLS)
    cvalid = (col[None, :] >= col_start[:, None]) & (col[None, :] < col_start[:, None] + NA_COLS)
    dc = jnp.clip(col[None, :] - col[:, None], -(NA_COLS - 1), NA_COLS - 1) + (NA_COLS - 1)
    tiles = jnp.where(cvalid[None, None], rpb.astype(F32)[:, :, dc], NEG_INF)
    tiles = jnp.concatenate([tiles, jnp.full_like(tiles[:, :1], NEG_INF)], axis=1)
    idx = []
    for j in (0, min(1, nj - 1), nj - 1):
        r = j * NA_QROWS + jnp.arange(NA_QROWS)
        krow = j * NA_QROWS - NA_KBLK // GRID_W + jnp.arange(NA_K // GRID_W)
        start = jnp.clip(r - kr // 2, 0, rows - kr)
        rvalid = (krow[None, :] >= start[:, None]) & (krow[None, :] < start[:, None] + kr)
        dr = krow[None, :] - r[:, None] + (NA_ROWS - 1)
        idx.append(jnp.where(rvalid, dr, NA_DR))
    return tiles, jnp.stack(idx).reshape(-1).astype(jnp.int32)


def _na_kernel(idx_ref, q_ref, k0, k1, k2, k3, v0, v1, v2, v3, kx_ref, vx_ref, t_ref, o_ref, bias_scr,
               *, nj):
    j = pl.program_id(2)
    nkr = NA_K // GRID_W

    @pl.when((j == 0) | (j == 1) | (j == nj - 1))
    def _():
        variant = jnp.where(j == 0, 0, jnp.where(j == nj - 1, 2, 1))
        for hh in range(LANES // HEAD_DIM):
            for qr in range(NA_QROWS):
                for kr in range(nkr):
                    d = idx_ref[(variant * NA_QROWS + qr) * nkr + kr]
                    bias_scr[hh, qr * GRID_W:(qr + 1) * GRID_W, kr * GRID_W:(kr + 1) * GRID_W] = (
                        t_ref[hh, d])

    q = q_ref[...]
    k_all = jnp.concatenate([k0[...], k1[...], k2[...], k3[...]], axis=0)
    v_all = jnp.concatenate([v0[...], v1[...], v2[...], v3[...]], axis=0)
    kx = kx_ref[...]
    vx = vx_ref[...]
    outs = []
    for hh in range(LANES // HEAD_DIM):
        lo, hi = hh * HEAD_DIM, (hh + 1) * HEAD_DIM
        qh = q[:, lo:hi]
        s_loc = _dot_nt(qh, k_all[:, lo:hi]) + bias_scr[hh]
        s_ctx = _dot_nt(qh, kx[:, lo:hi])
        outs.append(_softmax_pv([(s_ctx, vx[:, lo:hi]), (s_loc, v_all[:, lo:hi])], None))
    o_ref[...] = jnp.concatenate(outs, axis=1).astype(o_ref.dtype)


def _na_attn(bq, bk, bv, bkx, bvx, tiles, tile_idx, batch, seq, ctx_len):
    nj = seq // NA_Q
    nkb = seq // NA_KBLK
    hp = B_HEADS * HEAD_DIM // LANES
    per = LANES // HEAD_DIM

    def kb(t):
        return lambda p, b, j, idx: (b * nkb + jnp.clip(2 * j - 1 + t, 0, nkb - 1), p)

    kv_specs = [pl.BlockSpec((NA_KBLK, LANES), kb(t)) for t in range(4)]
    grid_spec = pltpu.PrefetchScalarGridSpec(
        num_scalar_prefetch=1,
        grid=(hp, batch, nj),
        in_specs=[pl.BlockSpec((NA_Q, LANES), lambda p, b, j, idx: (b * nj + j, p))]
        + kv_specs + kv_specs
        + [pl.BlockSpec((ctx_len, LANES), lambda p, b, j, idx: (b, p)),
           pl.BlockSpec((ctx_len, LANES), lambda p, b, j, idx: (b, p)),
           pl.BlockSpec((per, NA_DR + 1, GRID_W, GRID_W), lambda p, b, j, idx: (p, 0, 0, 0))],
        out_specs=pl.BlockSpec((NA_Q, LANES), lambda p, b, j, idx: (b * nj + j, p)),
        scratch_shapes=[pltpu.VMEM((per, NA_Q, NA_K), F32)],
    )
    return pl.pallas_call(
        functools.partial(_na_kernel, nj=nj),
        grid_spec=grid_spec,
        out_shape=jax.ShapeDtypeStruct((batch * seq, B_HEADS * HEAD_DIM), BF16),
        compiler_params=_params(("arbitrary", "arbitrary", "arbitrary")),
        name="na_attn",
    )(tile_idx, bq, bk, bk, bk, bk, bv, bv, bv, bv, bkx, bvx, tiles)


def _ctx_ab_kernel(sink_ref, aq_ref, ak_ref, av_ref, bq_ref, bk_ref, bv_ref, oa_ref, ob_ref):
    ctx_len = aq_ref.shape[0]
    g_sz = A_HEADS // A_KV_HEADS
    aq = aq_ref[...]
    ak = ak_ref[...]
    av = av_ref[...]
    outs = []
    for g in range(A_KV_HEADS):
        lo, hi = g * HEAD_DIM, (g + 1) * HEAD_DIM
        heads = [g * g_sz + i for i in range(g_sz)]
        qs = jnp.concatenate([aq[:, h * HEAD_DIM:(h + 1) * HEAD_DIM] for h in heads], axis=0)
        sink = jnp.concatenate(
            [jnp.full((ctx_len, 1), sink_ref[h], F32) for h in heads], axis=0)
        o = _softmax_pv([(_dot_nt(qs, ak[:, lo:hi]), av[:, lo:hi])], sink)
        outs.extend(o[i * ctx_len:(i + 1) * ctx_len] for i in range(g_sz))
    oa_ref[...] = jnp.concatenate(outs, axis=1).astype(oa_ref.dtype)
    bq = bq_ref[...]
    bk = bk_ref[...]
    bv = bv_ref[...]
    outs = []
    for h in range(B_HEADS):
        lo, hi = h * HEAD_DIM, (h + 1) * HEAD_DIM
        outs.append(_softmax_pv([(_dot_nt(bq[:, lo:hi], bk[:, lo:hi]), bv[:, lo:hi])], None))
    ob_ref[...] = jnp.concatenate(outs, axis=1).astype(ob_ref.dtype)


def _ctx_ab_attn(sink, aq, ak, av, bq, bk, bv, batch, ctx_len):
    def spec(w):
        return pl.BlockSpec((ctx_len, w), lambda b: (b, 0))

    return pl.pallas_call(
        _ctx_ab_kernel,
        grid=(batch,),
        in_specs=[pl.BlockSpec(memory_space=pltpu.SMEM)]
        + [spec(a.shape[1]) for a in (aq, ak, av, bq, bk, bv)],
        out_specs=[spec(aq.shape[1]), spec(bq.shape[1])],
        out_shape=[jax.ShapeDtypeStruct(aq.shape, BF16), jax.ShapeDtypeStruct(bq.shape, BF16)],
        compiler_params=_params(("arbitrary",)),
        name="ctx_ab_attn",
    )(sink, aq, ak, av, bq, bk, bv)


def _layer_norm(t, g, b):
    mu = jnp.mean(t, -1, keepdims=True)
    c = t - mu
    var = jnp.mean(c * c, -1, keepdims=True)
    return c * lax.rsqrt(var + 1e-5) * g + b


def _post_attn_kernel(*refs, alpha, lat_tiles, with_ctx, lat_transposed):
    n_src = 6 if with_ctx else 3
    srcs = refs[:n_src]
    (w_ref, mod_ref, lng_ref, lnb_ref, wr_hi_ref, wr_lo_ref, br_ref,
     x1_ref, h2_ref, lg_ref) = refs[n_src:]

    def rows_of(o_ref):
        if o_ref.ndim == 2:
            return o_ref[...]
        return o_ref[0].astype(F32).T.astype(BF16)

    def run(o1_ref, o2_ref, x_ref):
        half = w_ref.shape[0] // 2
        y = _dot(rows_of(o1_ref), w_ref[0:half, :]) + _dot(rows_of(o2_ref), w_ref[half:, :])
        g1 = mod_ref[0, 2:3, :]
        x1 = _layer_norm(alpha * x_ref[...] + g1 * y, lng_ref[...], lnb_ref[...])
        x1_ref[...] = x1
        h2 = x1 * (1.0 + mod_ref[0, 4:5, :]) + mod_ref[0, 3:4, :]
        h_hi, h_lo = _split_bf16(h2)
        h2_ref[...] = _pack_halves(h2)
        wr_hi = wr_hi_ref[...]
        lg_ref[...] = (_dot(h_hi, wr_hi) + _dot(h_lo, wr_hi) + _dot(h_hi, wr_lo_ref[...])
                       + br_ref[...])

    if not with_ctx:
        run(*srcs)
    else:
        i = pl.program_id(0)
        pl.when(i < lat_tiles)(lambda: run(*srcs[:3]))
        pl.when(i >= lat_tiles)(lambda: run(*srcs[3:]))


def _post_attn(lat, ctx_src, w_bf, mod, gid, lng, lnb, wr_hi, wr_lo, br, n_lat, alpha, tm):
    half = w_bf.shape[0] // 2
    d = lat[2].shape[1]
    lat_tiles = n_lat // tm
    n_rows = n_lat + (ctx_src[0].shape[0] if ctx_src is not None else 0)
    lat_transposed = lat[0].ndim == 3

    def lat_map(i):
        return (jnp.minimum(i, lat_tiles - 1), 0)

    def ctx_map(i):
        return (jnp.maximum(i - lat_tiles, 0), 0)

    def src_specs(index_map):
        return [pl.BlockSpec((tm, half), index_map), pl.BlockSpec((tm, half), index_map),
                pl.BlockSpec((tm, d), index_map)]

    in_specs = src_specs(lat_map)
    if lat_transposed:
        per_batch = lat[0].shape[2] // tm
        t_spec = pl.BlockSpec((1, half, tm), lambda i: (i // per_batch, 0, i % per_batch))
        in_specs[0] = in_specs[1] = t_spec
    args = list(lat)
    if ctx_src is not None:
        in_specs += src_specs(ctx_map)
        args += list(ctx_src)

    def const(shape):
        return pl.BlockSpec(shape, lambda i: (0,) * len(shape))

    in_specs += [const((2 * half, d)),
                 pl.BlockSpec((1, 6, d), lambda i: (gid(i), 0, 0)),
                 const((1, d)), const((1, d)),
                 const((d, ROUTER_PAD)), const((d, ROUTER_PAD)), const((1, ROUTER_PAD))]
    args += [w_bf, mod, lng, lnb, wr_hi, wr_lo, br]
    widths = (d, d // 2, ROUTER_PAD)
    dtypes = (F32, jnp.uint32, F32)
    return pl.pallas_call(
        functools.partial(_post_attn_kernel, alpha=alpha, lat_tiles=lat_tiles,
                          with_ctx=ctx_src is not None, lat_transposed=lat_transposed),
        grid=(n_rows // tm,),
        in_specs=in_specs,
        out_specs=[pl.BlockSpec((tm, w), lambda i: (i, 0)) for w in widths],
        out_shape=[jax.ShapeDtypeStruct((n_rows, w), t) for w, t in zip(widths, dtypes)],
        compiler_params=_params(("arbitrary",)),
        name="post_attn",
    )(*args)


def _moe_kernel(be_ref, xs_ref, wgu_ref, wdn_ref, ys_ref, wgu_bf, wdn_bf):
    i = pl.program_id(0)
    prev = be_ref[jnp.maximum(i - 1, 0)]
    changed = (i == 0) | (be_ref[i] != prev)

    @pl.when(changed)
    def _():
        wgu_bf[...] = wgu_ref[0].astype(BF16)
        wdn_bf[...] = wdn_ref[0].astype(BF16)

    de = wdn_bf.shape[0]
    subs = [slice(r, r + MOE_SUB) for r in range(0, MOE_BLOCK, MOE_SUB)]
    gus = [_dot(_unpack_halves(xs_ref[sl, :]).astype(BF16), wgu_bf[...]) for sl in subs]
    for sl, gu in zip(subs, gus):
        gate = gu[:, :de]
        up = gu[:, de:]
        act = gate / (1.0 + jnp.exp(-gate)) * up
        ys_ref[sl, :] = _pack_halves(_dot(act.astype(BF16), wdn_bf[...]))


def _moe_experts(block_e, xs, w_gu, w_dn):
    rows, dw = xs.shape
    d = 2 * dw
    nb = rows // MOE_BLOCK
    de = w_dn.shape[1]
    grid_spec = pltpu.PrefetchScalarGridSpec(
        num_scalar_prefetch=1,
        grid=(nb,),
        in_specs=[
            pl.BlockSpec((MOE_BLOCK, dw), lambda i, be: (i, 0)),
            pl.BlockSpec((1, d, 2 * de), lambda i, be: (be[i], 0, 0)),
            pl.BlockSpec((1, de, d), lambda i, be: (be[i], 0, 0)),
        ],
        out_specs=pl.BlockSpec((MOE_BLOCK, dw), lambda i, be: (i, 0)),
        scratch_shapes=[pltpu.VMEM((d, 2 * de), BF16), pltpu.VMEM((de, d), BF16)],
    )
    return pl.pallas_call(
        _moe_kernel,
        grid_spec=grid_spec,
        out_shape=jax.ShapeDtypeStruct((rows, dw), jnp.uint32),
        compiler_params=_params(("arbitrary",)),
        name="moe_experts",
    )(block_e, xs, w_gu, w_dn)


ROUTE_TILE = 512


def _route_kernel(lg_ref, rec_ref, cnt_ref, carry):
    i = pl.program_id(0)

    @pl.when(i == 0)
    def _():
        carry[...] = jnp.zeros(carry.shape, F32)

    lg = lg_ref[...]
    tm = lg.shape[0]
    lane = lax.broadcasted_iota(jnp.int32, lg.shape, 1)

    def first_lane(mask):
        return jnp.min(jnp.where(mask, lane, ROUTER_PAD), axis=1, keepdims=True)

    is_g = lane < N_GROUPS
    g_log = jnp.where(is_g, lg, NEG_INF)
    g_max = jnp.max(g_log, axis=1, keepdims=True)
    g_sum = jnp.sum(jnp.where(is_g, jnp.exp(lg - g_max), 0.0), axis=1, keepdims=True)
    g_val = 1.0 / g_sum
    g_idx = first_lane(g_log == g_max)
    lo = N_GROUPS + EXP_PER_GROUP * g_idx
    sel = (lane >= lo) & (lane < lo + EXP_PER_GROUP)
    e_log = jnp.where(sel, lg, NEG_INF)
    e_max = jnp.max(e_log, axis=1, keepdims=True)
    e_exp = jnp.where(sel, jnp.exp(lg - e_max), 0.0)
    e_prob = e_exp / jnp.sum(e_exp, axis=1, keepdims=True)
    p1 = jnp.where(sel, e_prob, -1.0)
    v1 = jnp.max(p1, axis=1, keepdims=True)
    i1 = first_lane(p1 == v1)
    p2 = jnp.where(lane == i1, -1.0, p1)
    v2 = jnp.max(p2, axis=1, keepdims=True)
    i2 = first_lane(p2 == v2)
    norm = g_val / (v1 + v2)

    hot1 = lane == i1
    hot2 = lane == i2
    hot = jnp.where(hot1 | hot2, 1.0, 0.0)
    rows = lax.broadcasted_iota(jnp.int32, (tm, tm), 0)
    cols = lax.broadcasted_iota(jnp.int32, (tm, tm), 1)
    before = jnp.where(cols < rows, 1.0, 0.0).astype(BF16)
    prefix = _dot(before, hot.astype(BF16)) + carry[0:1, :]
    r1 = jnp.sum(jnp.where(hot1, prefix, 0.0), axis=1, keepdims=True)
    r2 = jnp.sum(jnp.where(hot2, prefix, 0.0), axis=1, keepdims=True)
    carry[0:1, :] = carry[0:1, :] + jnp.sum(hot, axis=0, keepdims=True)

    fields = [(i1 - N_GROUPS).astype(F32), (i2 - N_GROUPS).astype(F32), r1, r2, v1 * norm, v2 * norm]
    rec = jnp.zeros(lg.shape, F32)
    for k, f in enumerate(fields):
        rec = jnp.where(lane == k, f, rec)
    rec_ref[...] = rec

    @pl.when(i == pl.num_programs(0) - 1)
    def _():
        cnt_ref[...] = carry[...]


def _route(logits, n_tok):
    rec, cnt = pl.pallas_call(
        _route_kernel,
        grid=(n_tok // ROUTE_TILE,),
        in_specs=[pl.BlockSpec((ROUTE_TILE, ROUTER_PAD), lambda i: (i, 0))],
        out_specs=[pl.BlockSpec((ROUTE_TILE, ROUTER_PAD), lambda i: (i, 0)),
                   pl.BlockSpec((8, ROUTER_PAD), lambda i: (0, 0))],
        out_shape=[jax.ShapeDtypeStruct((n_tok, ROUTER_PAD), F32),
                   jax.ShapeDtypeStruct((8, ROUTER_PAD), F32)],
        scratch_shapes=[pltpu.VMEM((8, ROUTER_PAD), F32)],
        compiler_params=_params(("arbitrary",)),
        name="route",
    )(logits)
    experts = rec[:, 0:2].astype(jnp.int32)
    rank = rec[:, 2:4].astype(jnp.int32)
    wts = rec[:, 4:6]
    counts = cnt[0, N_GROUPS:N_GROUPS + N_EXPERTS].astype(jnp.int32)

    n_asg = n_tok * TOP_K
    pcounts = ((counts + MOE_BLOCK - 1) // MOE_BLOCK) * MOE_BLOCK
    pends = jnp.cumsum(pcounts)
    pstarts = pends - pcounts
    dest = pstarts[experts] + rank
    nb = -(-n_asg // MOE_BLOCK) + N_EXPERTS
    block_start = jnp.arange(nb, dtype=jnp.int32) * MOE_BLOCK
    block_e = jnp.minimum((pends[None, :] <= block_start[:, None]).sum(1), N_EXPERTS - 1).astype(jnp.int32)
    tok =jnp.broadcast_to(jnp.arange(n_tok, dtype=jnp.int32)[:, None], (n_tok, TOP_K))
    row_tok = jnp.zeros((nb * MOE_BLOCK,), jnp.int32).at[dest.reshape(-1)].set(
        tok.reshape(-1), unique_indices=True)
    return wts, dest, row_tok, block_e


def _post_moe_kernel(x_ref, ya_ref, yb_ref, w_ref, mod_ref, lng_ref, lnb_ref, o_ref, *, alpha):
    w = w_ref[...]
    y = w[:, 0:1] * _unpack_halves(ya_ref[...]) + w[:, 1:2] * _unpack_halves(yb_ref[...])
    g2 = mod_ref[0, 5:6, :]
    o_ref[...] = _layer_norm(alpha * x_ref[...] + g2 * y, lng_ref[...], lnb_ref[...])


def _post_moe(x1, ya, yb, wts, mod, gid, lng, lnb, alpha, tm):
    rows, d = x1.shape
    row = pl.BlockSpec((tm, d), lambda i: (i, 0))
    packed = pl.BlockSpec((tm, d // 2), lambda i: (i, 0))
    vec = pl.BlockSpec((1, d), lambda i: (0, 0))
    return pl.pallas_call(
        functools.partial(_post_moe_kernel, alpha=alpha),
        grid=(rows // tm,),
        in_specs=[row, packed, packed,
                  pl.BlockSpec((tm, TOP_K), lambda i: (i, 0)),
                  pl.BlockSpec((1, 6, d), lambda i: (gid(i), 0, 0)),
                  vec, vec],
        out_specs=row,
        out_shape=jax.ShapeDtypeStruct((rows, d), F32),
        compiler_params=_params(("arbitrary",)),
        name="post_moe",
    )(x1, ya, yb, wts, mod, lng, lnb)


CD_COLS = dict(cq=(0, 384), ckv=(384, 640), dq=(640, 1152), dk=(1152, 1664), dv=(1664, 2176),
               kr=(2176, 2304))
C_HEAD_PAD = LANES


def _rms(t, g, eps=1e-6):
    return t * lax.rsqrt(jnp.mean(t * t, -1, keepdims=True) + eps) * g


def _proj_cd_kernel(x_ref, mod_ref, w_ref, qn_ref, wuq_ref, kvn_ref, wukv_ref,
                    cos64_ref, sin64_ref, cos32_ref, sin32_ref,
                    qc_ref, kc_ref, vct_ref, dq_ref, dk_ref, dvt_ref):
    h = _modulate(x_ref, mod_ref, 0).astype(BF16)
    r = _dot(h, w_ref[...])

    def cols(name):
        lo, hi = CD_COLS[name]
        return r[:, lo:hi]

    q = _dot(_rms(cols('cq'), qn_ref[...]).astype(BF16), wuq_ref[...])
    kv = _dot(_rms(cols('ckv'), kvn_ref[...]).astype(BF16), wukv_ref[...])
    c32, s32 = cos32_ref[...], sin32_ref[...]
    c64, s64 = cos64_ref[...], sin64_ref[...]
    q = _rope_wide(q, c32, s32, C_ROPE // 4)
    kr = _rope_tile(cols('kr'), c32, s32, C_ROPE // 4)
    dq = _rope_wide(cols('dq'), c64, s64, D_QK // 4)
    dk = _rope_wide(cols('dk'), c64, s64, D_QK // 4)
    kw = C_HEADS * C_HEAD_PAD
    qc_ref[0] = (q * ((C_NOPE + C_ROPE) ** -0.5 * LOG2E)).astype(BF16)
    kc_ref[0] = (kv[:, :kw] + jnp.concatenate([kr] * C_HEADS, axis=1)).astype(BF16)
    dq_ref[0] = (dq * (D_QK ** -0.5 * LOG2E)).astype(BF16)
    dk_ref[0] = dk.astype(BF16)
    _store_vt(vct_ref, kv[:, kw:], C_HEADS, C_V)
    _store_vt(dvt_ref, cols('dv'), D_HEADS, D_V)


def _store_vt(vt_ref, v, heads, dv):
    tm = v.shape[0]
    v_t = v.T.astype(BF16)
    row = lax.broadcasted_iota(jnp.int32, (ONES_ROWS, tm), 0)
    extra = jnp.where(row == 0, 1.0, 0.0).astype(BF16)
    for h in range(heads):
        vt_ref[0, h, 0, 0:dv, :] = v_t[h * dv:(h + 1) * dv]
        vt_ref[0, h, 0, dv:dv + ONES_ROWS, :] = extra


def _proj_cd(xall, mod, w_bf, qn, wuq, kvn, wukv, tabs, batch, seq, ctx_len, tm, tk):
    d = xall.shape[1]
    lat_t, ctx_t = seq // tm, ctx_len // tm
    n_keys = seq + ctx_len
    per_chunk = tk // tm
    n_chunks = n_keys // tk
    widths = (C_HEADS * C_HEAD_PAD, C_HEADS * C_HEAD_PAD, 2 * D_HEADS * D_QK, 2 * D_HEADS * D_QK)

    def src(b, t):
        return (jnp.where(t < lat_t, b * lat_t + t, batch * lat_t + b * ctx_t + t - lat_t), 0)

    def full(a):
        return pl.BlockSpec(a.shape, lambda b, t: (0,) * a.ndim)

    def row_spec(w):
        return pl.BlockSpec((1, tm, w), lambda b, t: (b, t, 0))

    def row_shape(w):
        return jax.ShapeDtypeStruct((batch, n_keys, w), BF16)

    def vt_spec(heads, dv):
        return pl.BlockSpec((1, heads, 1, dv + ONES_ROWS, tm),
                            lambda b, t: (b, 0, t // per_chunk, 0, t % per_chunk))

    def vt_shape(heads, dv):
        return jax.ShapeDtypeStruct((batch, heads, n_chunks, dv + ONES_ROWS, tk), BF16)

    tab = pl.BlockSpec((tm, LANES), lambda b, t: (t, 0))
    return pl.pallas_call(
        _proj_cd_kernel,
        grid=(batch, lat_t + ctx_t),
        in_specs=[pl.BlockSpec((tm, d), src),
                  pl.BlockSpec((1, 6, d), lambda b, t: (jnp.where(t < lat_t, b, batch), 0, 0)),
                  full(w_bf), full(qn), full(wuq), full(kvn), full(wukv), tab, tab, tab, tab],
        out_specs=[row_spec(widths[0]), row_spec(widths[1]), vt_spec(C_HEADS, C_V),
                   row_spec(widths[2]), row_spec(widths[3]), vt_spec(D_HEADS, D_V)],
        out_shape=[row_shape(widths[0]), row_shape(widths[1]), vt_shape(C_HEADS, C_V),
                   row_shape(widths[2]), row_shape(widths[3]), vt_shape(D_HEADS, D_V)],
        compiler_params=_params(("arbitrary", "arbitrary")),
        name="proj_cd",
    )(xall, mod, w_bf, qn, wuq, kvn, wukv, *tabs)


DENSE_SHIFT = 80.0
DENSE_L_MIN = 2.0 ** -60
DENSE_L_MAX = 2.0 ** 120
DENSE_TQ = 1024
DENSE_TK_MAX = 1280
DENSE_UNROLL_MAX = 13
MXU_DEPTH = 256
PROJ_CD_TILE = 256
ONES_ROWS = 16


def _sq_norm_row(x):
    ones = jnp.ones((8, x.shape[1]), BF16)
    return _dot_nt(ones, (x * x).astype(BF16))[0:1] * 1.02


def _key_norm_max(kchunk, n_chunks, masks):
    def body(j, mx):
        kf = kchunk(j).astype(F32)
        ksq = kf * kf
        out = []
        for msk, cur in zip(masks, mx):
            part = ksq if msk is None else jnp.where(msk, ksq, 0.0)
            rn = jnp.sum(part, axis=1, keepdims=True)
            out.append(jnp.maximum(cur, jnp.max(rn, axis=0, keepdims=True)))
        return tuple(out)

    mx = lax.fori_loop(0, n_chunks, body, tuple(jnp.zeros((1, 1), F32) for _ in masks))
    return [jnp.sqrt(v) * 1.01 for v in mx]


def _safe_online(q, kchunk, vt_ref, n_chunks, m_scr, l_scr, acc_scr):
    m_scr[...] = jnp.full(m_scr.shape, NEG_INF, F32)
    l_scr[...] = jnp.zeros(l_scr.shape, F32)
    acc_scr[...] = jnp.zeros(acc_scr.shape, F32)

    def body(j, carry):
        s = _dot_nt(kchunk(j), q)
        m_prev = m_scr[...]
        m_new = jnp.maximum(m_prev, s.max(0, keepdims=True))
        alpha = jnp.exp2(m_prev - m_new)
        p = jnp.exp2(s - m_new)
        l_scr[...] = alpha * l_scr[...] + p.sum(0, keepdims=True)
        acc_scr[...] = alpha * acc_scr[...] + _dot(vt_ref[0, 0, j], p.astype(BF16))
        m_scr[...] = m_new
        return carry

    lax.fori_loop(0, n_chunks, body, 0)


def _denominators_ok(*ls):
    ok = None
    for l in ls:
        cur = (l > DENSE_L_MIN) & (l < DENSE_L_MAX)
        ok = cur if ok is None else ok & cur
    return jnp.max(jnp.where(ok, 0.0, 1.0)) == 0.0


def _mla_kernel(q_ref, k_ref, vt_ref, o_ref, kmax_scr, acc_scr, m_scr, l_scr,
                *, n_chunks, tk, unroll, depth):
    def kchunk(j):
        return k_ref[0, pl.ds(pl.multiple_of(j * tk, tk), tk), :]

    @pl.when(pl.program_id(2) == 0)
    def _():
        (kmax,) = _key_norm_max(kchunk, n_chunks, [None])
        kmax_scr[...] = jnp.broadcast_to(kmax, kmax_scr.shape)

    q = q_ref[0]
    m_row = jnp.sqrt(_sq_norm_row(q.astype(F32))) * kmax_scr[0:1, 0:1] - DENSE_SHIFT
    acc_scr[...] = jnp.zeros(acc_scr.shape, F32)

    def body(it, carry):
        j0 = it * unroll
        pend = [_dot_nt(kchunk(j0 + u), q) for u in range(min(depth, unroll))]
        acc = acc_scr[...]
        for u in range(unroll):
            s = pend.pop(0)
            if u + depth < unroll:
                pend.append(_dot_nt(kchunk(j0 + u + depth), q))
            acc = acc + _dot(vt_ref[0, 0, j0 + u], jnp.exp2(s - m_row).astype(BF16))
        acc_scr[...] = acc
        return carry

    lax.fori_loop(0, n_chunks // unroll, body, 0)
    l = acc_scr[C_V:C_V + 1, :]
    o_ref[0] = (acc_scr[0:C_V, :] / l).astype(o_ref.dtype)

    @pl.when(jnp.logical_not(_denominators_ok(l)))
    def _():
        _safe_online(q, kchunk, vt_ref, n_chunks, m_scr, l_scr, acc_scr)
        o_ref[0] = (acc_scr[0:C_V, :] / l_scr[...]).astype(o_ref.dtype)


def _mla_attn(qc, kc, vt, batch, seq, tq, tk, unroll, depth):
    n_keys = kc.shape[1]
    nq, n_chunks = seq // tq, n_keys // tk
    rows = C_V + ONES_ROWS
    return pl.pallas_call(
        functools.partial(_mla_kernel, n_chunks=n_chunks, tk=tk, unroll=unroll, depth=depth),
        grid=(batch, C_HEADS, nq),
        in_specs=[
            pl.BlockSpec((1, tq, C_HEAD_PAD), lambda b, h, i: (b, i, h)),
            pl.BlockSpec((1, n_keys, C_HEAD_PAD), lambda b, h, i: (b, 0, h)),
            pl.BlockSpec((1, 1, n_chunks, rows, tk), lambda b, h, i: (b, h, 0, 0, 0)),
        ],
        out_specs=pl.BlockSpec((1, C_V, tq), lambda b, h, i: (b, h, i)),
        out_shape=jax.ShapeDtypeStruct((batch, C_HEADS * C_V, seq), BF16),
        scratch_shapes=[pltpu.VMEM((8, LANES), F32), pltpu.VMEM((rows, tq), F32),
                        pltpu.VMEM((1, tq), F32), pltpu.VMEM((1, tq), F32)],
        compiler_params=_params(("arbitrary", "arbitrary", "arbitrary")),
        name="mla_attn",
    )(qc, kc, vt)


def _diff_kernel(lam_ref, q_ref, k_ref, vt_ref, subln_ref, o_ref, kmax_scr, acc1_scr, acc2_scr,
                 m_scr, l_scr, *, n_chunks, tk, unroll, depth, out_scale):
    def kchunk(j):
        return k_ref[0, pl.ds(pl.multiple_of(j * tk, tk), tk), :]

    @pl.when(pl.program_id(2) == 0)
    def _():
        klane = lax.broadcasted_iota(jnp.int32, (tk, 2 * D_QK), 1)
        k1, k2 = _key_norm_max(kchunk, n_chunks, [klane < D_QK, klane >= D_QK])
        kmax_scr[0:1, :] = jnp.broadcast_to(k1, (1, LANES))
        kmax_scr[1:2, :] = jnp.broadcast_to(k2, (1, LANES))

    q = q_ref[0]
    qlane = lax.broadcasted_iota(jnp.int32, q.shape, 1)
    zero = jnp.zeros_like(q)
    q1 = jnp.where(qlane < D_QK, q, zero)
    q2 = jnp.where(qlane >= D_QK, q, zero)
    m1 = jnp.sqrt(_sq_norm_row(q1.astype(F32))) * kmax_scr[0:1, 0:1] - DENSE_SHIFT
    m2 = jnp.sqrt(_sq_norm_row(q2.astype(F32))) * kmax_scr[1:2, 0:1] - DENSE_SHIFT
    acc1_scr[...] = jnp.zeros(acc1_scr.shape, F32)
    acc2_scr[...] = jnp.zeros(acc2_scr.shape, F32)

    def scores(j):
        k = kchunk(j)
        return _dot_nt(k, q1), _dot_nt(k, q2)

    def body(it, carry):
        j0 = it * unroll
        pend = [scores(j0 + u) for u in range(min(depth, unroll))]
        a1 = acc1_scr[...]
        a2 = acc2_scr[...]
        for u in range(unroll):
            s1, s2 = pend.pop(0)
            if u + depth < unroll:
                pend.append(scores(j0 + u + depth))
            vt = vt_ref[0, 0, j0 + u]
            a1 = a1 + _dot(vt, jnp.exp2(s1 - m1).astype(BF16))
            a2 = a2 + _dot(vt, jnp.exp2(s2 - m2).astype(BF16))
        acc1_scr[...] = a1
        acc2_scr[...] = a2
        return carry

    lax.fori_loop(0, n_chunks // unroll, body, 0)

    def finish(o1, o2):
        o = o1 - lam_ref[0] * o2
        o = o * lax.rsqrt(jnp.mean(o * o, 0, keepdims=True) + 1e-6) * subln_ref[...] * out_scale
        o_ref[0] = o.astype(o_ref.dtype)

    l1 = acc1_scr[D_V:D_V + 1, :]
    l2 = acc2_scr[D_V:D_V + 1, :]
    finish(acc1_scr[0:D_V, :] / l1, acc2_scr[0:D_V, :] / l2)

    @pl.when(jnp.logical_not(_denominators_ok(l1, l2)))
    def _():
        _safe_online(q1, kchunk, vt_ref, n_chunks, m_scr, l_scr, acc1_scr)
        o1 = acc1_scr[0:D_V, :] / l_scr[...]
        _safe_online(q2, kchunk, vt_ref, n_chunks, m_scr, l_scr, acc2_scr)
        finish(o1, acc2_scr[0:D_V, :] / l_scr[...])


def _diff_attn(lam, dq, dk, vt, subln_col, out_scale, batch, seq, tq, tk, unroll, depth):
    n_keys = dk.shape[1]
    nq, n_chunks = seq // tq, n_keys // tk
    rows = D_V + ONES_ROWS
    pair = 2 * D_QK
    return pl.pallas_call(
        functools.partial(_diff_kernel, n_chunks=n_chunks, tk=tk, unroll=unroll, depth=depth,
                          out_scale=out_scale),
        grid=(batch, D_HEADS, nq),
        in_specs=[
            pl.BlockSpec(memory_space=pltpu.SMEM),
            pl.BlockSpec((1, tq, pair), lambda b, h, i: (b, i, h)),
            pl.BlockSpec((1, n_keys, pair), lambda b, h, i: (b, 0, h)),
            pl.BlockSpec((1, 1, n_chunks, rows, tk), lambda b, h, i: (b, h, 0, 0, 0)),
            pl.BlockSpec((D_V, 1), lambda b, h, i: (0, 0)),
        ],
        out_specs=pl.BlockSpec((1, D_V, tq), lambda b, h, i: (b, h, i)),
        out_shape=jax.ShapeDtypeStruct((batch, D_HEADS * D_V, seq), BF16),
        scratch_shapes=[pltpu.VMEM((8, LANES), F32), pltpu.VMEM((rows, tq), F32),
                        pltpu.VMEM((rows, tq), F32), pltpu.VMEM((1, tq), F32), pltpu.VMEM((1, tq), F32)],
        compiler_params=_params(("arbitrary", "arbitrary", "arbitrary")),
        name="diff_attn",
    )(lam, dq, dk, vt, subln_col)


def _rope_tables(n_tok, dim, lane_lo):
    t = jnp.arange(n_tok)
    pos_r = (t // GRID_W).astype(F32)
    pos_c = (t % GRID_W).astype(F32)
    quarter = dim // 4
    inv = ROPE_BASE ** (-jnp.arange(quarter, dtype=F32) / quarter)
    ang_r = pos_r[:, None] * inv
    ang_c = pos_c[:, None] * inv
    ang = jnp.concatenate([ang_r, ang_r, ang_c, ang_c], -1)
    sign = jnp.tile(jnp.concatenate([-jnp.ones(quarter), jnp.ones(quarter)]), 2).astype(F32)
    cos, sin = jnp.cos(ang), jnp.sin(ang) * sign
    reps = (LANES - lane_lo) // dim
    cos = jnp.concatenate([jnp.ones((n_tok, lane_lo), F32)] + [cos] * reps, axis=1)
    sin = jnp.concatenate([jnp.zeros((n_tok, lane_lo), F32)] + [sin] * reps, axis=1)
    return cos, sin


def _cd_weights(w_in, w_uq, w_ukv):
    d = w_in.shape[0]
    s0 = C_Q_RANK
    s1 = s0 + C_KV_RANK
    s2 = s1 + C_ROPE
    s3 = s2 + 2 * D_HEADS * D_QK
    s4 = s3 + 2 * D_HEADS * D_QK
    kr = jnp.zeros((d, LANES), F32).at[:, C_NOPE:C_NOPE + C_ROPE].set(w_in[:, s1:s2])
    w_in_p = jnp.concatenate([w_in[:, :s1], w_in[:, s2:s3], w_in[:, s3:s4], w_in[:, s4:], kr], axis=1)
    dqk = C_NOPE + C_ROPE
    wq = w_uq.reshape(C_Q_RANK, C_HEADS, dqk)
    wq = jnp.pad(wq, ((0, 0), (0, 0), (0, C_HEAD_PAD - dqk))).reshape(C_Q_RANK, C_HEADS * C_HEAD_PAD)
    wkv = w_ukv.reshape(C_KV_RANK, C_HEADS, C_NOPE + C_V)
    wk = jnp.pad(wkv[:, :, :C_NOPE], ((0, 0), (0, 0), (0, C_HEAD_PAD - C_NOPE)))
    wk = wk.reshape(C_KV_RANK, C_HEADS * C_HEAD_PAD)
    wv = wkv[:, :, C_NOPE:].reshape(C_KV_RANK, C_HEADS * C_V)
    return w_in_p.astype(BF16), wq.astype(BF16), jnp.concatenate([wk, wv], axis=1).astype(BF16)


def _moe_layer(h2, logits, x1, w_gu, w_dn, mod, gid, lng, lnb, alpha, tm):
    n_tok = h2.shape[0]
    wts, dest, row_tok, block_e = _route(logits, n_tok)
    xs = jnp.take(h2, row_tok, axis=0)
    ys = _moe_experts(block_e, xs, w_gu, w_dn)
    ya = jnp.take(ys, dest[:, 0], axis=0)
    yb = jnp.take(ys, dest[:, 1], axis=0)
    return _post_moe(x1, ya, yb, wts.astype(F32), mod, gid, lng, lnb, alpha, tm)


def kernel(x, c, ctx, c_ctx, w_ada, b_ada, ln_g, ln_b, ab_w_in, a_sink, b_rpb, ab_w_out, cd_w_in, c_q_norm, c_w_uq, c_kv_norm, c_w_ukv, d_lambda, d_subln, cd_w_out, w_group, b_group, w_exp_router, b_exp_router, w_gate_up, w_down):
    batch, seq, d = x.shape
    ctx_len = ctx.shape[1]
    depth = w_ada.shape[0]
    alpha = (2 * depth) ** 0.25
    n_lat = batch * seq
    n_ctx = batch * ctx_len
    tm = 512
    tq = DENSE_TQ
    assert depth == 2 and seq % NA_Q == 0 and seq % tq == 0 and n_ctx % tm == 0
    assert seq % PROJ_CD_TILE == 0 and ctx_len % PROJ_CD_TILE == 0
    assert seq % (WIN_GROUP * A_BLOCK) == 0 and seq // NA_Q >= 2
    key_tiles = (seq + ctx_len) // MXU_DEPTH
    tk = MXU_DEPTH * max(u for u in range(1, DENSE_TK_MAX // MXU_DEPTH + 1) if key_tiles % u == 0)
    n_chunks = (seq + ctx_len) // tk

    def unroll_for(cap):
        return max(u for u in range(1, cap + 1) if n_chunks % u == 0)
    lat_tiles = seq // tm

    def gid_lat(i):
        return i // lat_tiles

    def gid_ctx(i):
        return batch

    def gid_all(i):
        return jnp.minimum(i // lat_tiles, batch)

    c_all = jnp.zeros((8, d), F32).at[:batch].set(c).at[batch].set(c_ctx)
    mod = _ada(c_all, w_ada, b_ada).reshape(depth, 8, 6, d)

    cos64, sin64 = _rope_tables(seq, HEAD_DIM, 0)
    cos32, sin32 = _rope_tables(seq, C_ROPE, C_NOPE)

    def router_weights(l):
        wr = jnp.zeros((d, ROUTER_PAD), F32)
        wr = wr.at[:, :N_GROUPS].set(w_group[l]).at[:, N_GROUPS:N_GROUPS + N_EXPERTS].set(w_exp_router[l])
        br = jnp.zeros((1, ROUTER_PAD), F32)
        br = br.at[0, :N_GROUPS].set(b_group[l]).at[0, N_GROUPS:N_GROUPS + N_EXPERTS].set(b_exp_router[l])
        hi = wr.astype(BF16)
        lo = (wr - hi.astype(F32)).astype(BF16)
        return hi, lo, br

    x2d = x.reshape(n_lat, d)
    ctx2d = ctx.reshape(n_ctx, d)

    l = 0
    w_in_bf = ab_w_in[0].astype(BF16)
    aq, ak, av, bq, bk, bv = _proj_ab(x2d, 0, n_lat, mod[l], gid_lat, w_in_bf, cos64, sin64,
                                      lat_tiles, True, tm)
    aqx, akx, avx, bqx, bkx, bvx = _proj_ab(ctx2d, 0, n_ctx, mod[l], gid_ctx, w_in_bf, cos64, sin64,
                                            1, False, tm)
    sink = a_sink[0].astype(F32)
    oa = _window_attn(sink, aq, ak, av, akx, avx, batch, seq, ctx_len)
    tiles, tile_idx = _na_bias_tiles(b_rpb[0], seq // GRID_W)
    ob = _na_attn(bq, bk, bv, bkx, bvx, tiles, tile_idx, batch, seq, ctx_len)
    oax, obx = _ctx_ab_attn(sink, aqx, akx, avx, bqx, bkx, bvx, batch, ctx_len)

    n_all = n_lat + n_ctx
    w_out_bf = ab_w_out[0].astype(BF16)
    lng, lnb = ln_g[l, 0][None], ln_b[l, 0][None]
    wr_hi, wr_lo, br = router_weights(l)
    x1, h2, logits = _post_attn((oa, ob, x2d), (oax, obx, ctx2d), w_out_bf, mod[l], gid_all,
                                lng, lnb, wr_hi, wr_lo, br, n_lat, alpha, tm)
    xall = _moe_layer(h2, logits, x1, w_gate_up[l], w_down[l], mod[l], gid_all,
                      ln_g[l, 1][None], ln_b[l, 1][None], alpha, tm)

    l = 1
    lam_init = 0.8 - 0.6 * math.exp(-0.3 * l)
    lp = d_lambda[0].astype(F32)
    lam = (jnp.exp(jnp.sum(lp[0] * lp[1])) - jnp.exp(jnp.sum(lp[2] * lp[3])) + lam_init).reshape(1)
    w_in_p, wuq_p, wukv_p = _cd_weights(cd_w_in[0], c_w_uq[0], c_w_ukv[0])
    qn, kvn = c_q_norm[0][None].astype(F32), c_kv_norm[0][None].astype(F32)
    def with_ctx_identity(cos, sin):
        return (jnp.concatenate([cos, jnp.ones((ctx_len, LANES), F32)]),
                jnp.concatenate([sin, jnp.zeros((ctx_len, LANES), F32)]))

    tabs = with_ctx_identity(cos64, sin64) + with_ctx_identity(cos32, sin32)
    qc, kc, vct, dq, dk, dvt = _proj_cd(xall, mod[l], w_in_p, qn, wuq_p, kvn, wukv_p, tabs,
                                        batch, seq, ctx_len, PROJ_CD_TILE, tk)
    unroll = unroll_for(DENSE_UNROLL_MAX)
    oc_t = _mla_attn(qc, kc, vct, batch, seq, tq, tk, unroll, 1)
    od_t = _diff_attn(lam, dq, dk, dvt, d_subln[0].astype(F32).reshape(D_V, 1), 1.0 - lam_init,
                      batch, seq, tq, tk, unroll, 1)
    wr_hi, wr_lo, br = router_weights(l)
    x1, h2, logits = _post_attn((oc_t, od_t, xall), None, cd_w_out[0].astype(BF16), mod[l], gid_lat,
                                ln_g[l, 0][None], ln_b[l, 0][None], wr_hi, wr_lo, br,
                                n_lat, alpha, tm)
    out = _moe_layer(h2, logits, x1, w_gate_up[l], w_down[l], mod[l], gid_lat,
                     ln_g[l, 1][None], ln_b[l, 1][None], alpha, tm)
    return out.reshape(batch, seq, d)
```

```python
import functools
import math

import jax
import jax.numpy as jnp
from jax import lax
from jax.experimental import pallas as pl
from jax.experimental.pallas import tpu as pltpu

F32 = jnp.float32
BF16 = jnp.bfloat16

GRID_W = 64
HEAD_DIM = 64
ROPE_BASE = 10000.0
NEG_INF = -1e30
LOG2E = 1.4426950408889634

A_HEADS = 8
A_KV_HEADS = 2
A_WINDOW = 128
A_BLOCK = 128
B_HEADS = 8
NA_ROWS = 8
NA_COLS = 16
C_HEADS = 8
C_Q_RANK = 384
C_KV_RANK = 256
C_NOPE = 64
C_ROPE = 32
C_V = 64
D_HEADS = 4
D_QK = 64
D_V = 128
N_GROUPS = 4
EXP_PER_GROUP = 8
N_EXPERTS = N_GROUPS * EXP_PER_GROUP
TOP_K = 2
MOE_BLOCK = 512
MOE_SUB = 256
ROUTER_PAD = 128

LANES = 128
VMEM_LIMIT = 56 * 1024 * 1024


def _params(sem):
    return pltpu.CompilerParams(dimension_semantics=sem, vmem_limit_bytes=VMEM_LIMIT)


def _dot(a, b):
    return jnp.dot(a, b, preferred_element_type=F32)


def _dot_nt(a, b):
    return lax.dot_general(a, b, (((1,), (1,)), ((), ())), preferred_element_type=F32)


def _split_bf16(x):
    hi = x.astype(BF16)
    lo = (x - hi.astype(F32)).astype(BF16)
    return hi, lo


HI_HALF = 0xFFFF0000


def _pack_halves(x):
    w = x.shape[1] // 2
    lo = lax.bitcast_convert_type(x[:, :w].astype(BF16).astype(F32), jnp.uint32)
    hi = lax.bitcast_convert_type(x[:, w:].astype(BF16).astype(F32), jnp.uint32)
    return (lo >> 16) | (hi & jnp.uint32(HI_HALF))


def _unpack_halves(u):
    lo = lax.bitcast_convert_type(u << 16, F32)
    hi = lax.bitcast_convert_type(u & jnp.uint32(HI_HALF), F32)
    return jnp.concatenate([lo, hi], axis=1)


def _ada_kernel(c_ref, w_ref, b_ref, o_ref):
    c = c_ref[...]
    a = c / (1.0 + jnp.exp(-c))
    a_hi, a_lo = _split_bf16(a)
    w_hi, w_lo = _split_bf16(w_ref[0])
    acc = _dot(a_hi, w_hi) + _dot(a_lo, w_hi) + _dot(a_hi, w_lo)
    o_ref[0] = acc + b_ref[0]


def _ada(c_all, w_ada, b_ada):
    depth, d, n = w_ada.shape
    tn = 1536
    return pl.pallas_call(
        _ada_kernel,
        grid=(depth, n // tn),
        in_specs=[
            pl.BlockSpec((8, d), lambda l, j: (0, 0)),
            pl.BlockSpec((1, d, tn), lambda l, j: (l, 0, j)),
            pl.BlockSpec((1, 1, tn), lambda l, j: (l, 0, j)),
        ],
        out_specs=pl.BlockSpec((1, 8, tn), lambda l, j: (l, 0, j)),
        out_shape=jax.ShapeDtypeStruct((depth, 8, n), F32),
        compiler_params=_params(("arbitrary", "arbitrary")),
        name="ada",
    )(c_all, w_ada, b_ada.reshape(depth, 1, n))


def _rope_tile(t, cos, sin_signed, q):
    lane = lax.broadcasted_iota(jnp.int32, t.shape, 1)
    first = (lane & (2 * q - 1)) < q
    up = pltpu.roll(t, LANES - q, 1)
    dn = pltpu.roll(t, q, 1)
    return t * cos + jnp.where(first, up, dn) * sin_signed


def _rope_wide(t, cos, sin_signed, q):
    n = t.shape[1] // LANES
    return jnp.concatenate(
        [_rope_tile(t[:, i * LANES:(i + 1) * LANES], cos, sin_signed, q) for i in range(n)], axis=1)


def _modulate(x_ref, mod_ref, which):
    sh = mod_ref[0, 3 * which:3 * which + 1, :]
    sc = mod_ref[0, 3 * which + 1:3 * which + 2, :]
    return x_ref[...] * (1.0 + sc) + sh


def _proj_ab_kernel(x_ref, mod_ref, w_ref, cos_ref, sin_ref,
                    aq_ref, ak_ref, av_ref, bq_ref, bk_ref, bv_ref, *, rope):
    h = _modulate(x_ref, mod_ref, 0).astype(BF16)
    r = _dot(h, w_ref[...])
    aq = r[:, 0:512]
    ak = r[:, 512:640]
    if rope:
        cos = cos_ref[...]
        sin = sin_ref[...]
        aq = _rope_wide(aq, cos, sin, 16)
        ak = _rope_wide(ak, cos, sin, 16)
    scale = HEAD_DIM ** -0.5
    aq_ref[...] = (aq * scale).astype(BF16)
    ak_ref[...] = ak.astype(BF16)
    av_ref[...] = r[:, 640:768].astype(BF16)
    bq_ref[...] = (r[:, 768:1280] * scale).astype(BF16)
    bk_ref[...] = r[:, 1280:1792].astype(BF16)
    bv_ref[...] = r[:, 1792:2304].astype(BF16)


def _proj_ab(x2d, row0, nrows, mod, gid, w_bf, cos, sin, pos_blocks, rope, tm):
    d = x2d.shape[1]
    n_in = w_bf.shape[1]
    b0 = row0 // tm
    widths = (512, 128, 128, 512, 512, 512)
    return pl.pallas_call(
        functools.partial(_proj_ab_kernel, rope=rope),
        grid=(nrows // tm,),
        in_specs=[
            pl.BlockSpec((tm, d), lambda i: (b0 + i, 0)),
            pl.BlockSpec((1, 6, d), lambda i: (gid(i), 0, 0)),
            pl.BlockSpec((d, n_in), lambda i: (0, 0)),
            pl.BlockSpec((tm, LANES), lambda i: (i % pos_blocks, 0)),
            pl.BlockSpec((tm, LANES), lambda i: (i % pos_blocks, 0)),
        ],
        out_specs=[pl.BlockSpec((tm, w), lambda i: (i, 0)) for w in widths],
        out_shape=[jax.ShapeDtypeStruct((nrows, w), BF16) for w in widths],
        compiler_params=_params(("arbitrary",)),
        name="proj_ab",
    )(x2d, mod, w_bf, cos, sin)


def _softmax_pv(parts, sink_col):
    m = parts[0][0].max(-1, keepdims=True)
    for s, _ in parts[1:]:
        m = jnp.maximum(m, s.max(-1, keepdims=True))
    if sink_col is not None:
        m = jnp.maximum(m, sink_col)
    denom = None if sink_col is None else jnp.exp(sink_col - m)
    o = None
    for s, v in parts:
        e = jnp.exp(s - m)
        d = e.sum(-1, keepdims=True)
        denom = d if denom is None else denom + d
        pv = _dot(e.astype(BF16), v)
        o = pv if o is None else o + pv
    return o / denom


def _win_kernel(sink_ref, q_ref, kp_ref, kc_ref, kn_ref, vp_ref, vc_ref, vn_ref,
                kx_ref, vx_ref, o_ref, *, seq):
    step = pl.program_id(1)
    blk = A_BLOCK
    g_sz = A_HEADS // A_KV_HEADS
    qi = lax.broadcasted_iota(jnp.int32, (blk, 3 * blk), 0)
    kj = lax.broadcasted_iota(jnp.int32, (blk, 3 * blk), 1)
    in_window = jnp.abs(kj - blk - qi) <= A_WINDOW
    k_ext = jnp.concatenate([kp_ref[...], kc_ref[...], kn_ref[...]], axis=0)
    v_ext = jnp.concatenate([vp_ref[...], vc_ref[...], vn_ref[...]], axis=0)
    kx = kx_ref[...]
    vx = vx_ref[...]
    for t in range(WIN_GROUP):
        kpos = (step * WIN_GROUP + t) * blk + kj - blk
        valid = in_window & (kpos >= 0) & (kpos < seq)
        valid = jnp.concatenate([valid] * g_sz, axis=0)
        q = q_ref[t * blk:(t + 1) * blk, :]
        k_all = k_ext[t * blk:(t + 3) * blk]
        v_all = v_ext[t * blk:(t + 3) * blk]
        outs = []
        for g in range(A_KV_HEADS):
            lo, hi = g * HEAD_DIM, (g + 1) * HEAD_DIM
            heads = [g * g_sz + i for i in range(g_sz)]
            qs = jnp.concatenate([q[:, h * HEAD_DIM:(h + 1) * HEAD_DIM] for h in heads], axis=0)
            sink = jnp.concatenate(
                [jnp.full((blk, 1), sink_ref[h], F32) for h in heads], axis=0)
            s_loc = jnp.where(valid, _dot_nt(qs, k_all[:, lo:hi]), NEG_INF)
            s_ctx = _dot_nt(qs, kx[:, lo:hi])
            o = _softmax_pv([(s_ctx, vx[:, lo:hi]), (s_loc, v_all[:, lo:hi])], sink)
            outs.extend(o[i * blk:(i + 1) * blk] for i in range(g_sz))
        o_ref[t * blk:(t + 1) * blk, :] = jnp.concatenate(outs, axis=1).astype(o_ref.dtype)


WIN_GROUP = 4


def _window_attn(sink, aq, ak, av, akx, avx, batch, seq, ctx_len):
    nblk = seq // A_BLOCK
    nstep = nblk // WIN_GROUP
    kvw = A_KV_HEADS * HEAD_DIM
    qw = A_HEADS * HEAD_DIM
    rows = WIN_GROUP * A_BLOCK

    def edge(delta):
        return lambda b, n: (b * nblk + jnp.clip(n * WIN_GROUP + delta, 0, nblk - 1), 0)

    def own(b, n):
        return (b * nstep + n, 0)

    kv_specs = [pl.BlockSpec((A_BLOCK, kvw), edge(-1)), pl.BlockSpec((rows, kvw), own),
                pl.BlockSpec((A_BLOCK, kvw), edge(WIN_GROUP))]
    return pl.pallas_call(
        functools.partial(_win_kernel, seq=seq),
        grid=(batch, nstep),
        in_specs=[pl.BlockSpec(memory_space=pltpu.SMEM), pl.BlockSpec((rows, qw), own)]
        + kv_specs + kv_specs
        + [pl.BlockSpec((ctx_len, kvw), lambda b, n: (b, 0)),
           pl.BlockSpec((ctx_len, kvw), lambda b, n: (b, 0))],
        out_specs=pl.BlockSpec((rows, qw), own),
        out_shape=jax.ShapeDtypeStruct((batch * seq, qw), BF16),
        compiler_params=_params(("arbitrary", "arbitrary")),
        name="window_attn",
    )(sink, aq, ak, ak, ak, av, av, av, akx, avx)


NA_QROWS = 8
NA_Q = NA_QROWS * GRID_W
NA_KBLK = 4 * GRID_W
NA_K = 4 * NA_KBLK


NA_DR = 2 * NA_ROWS - 1


def _na_bias_tiles(rpb, rows):
    nj = rows // NA_QROWS
    kr = min(NA_ROWS, rows)
    col = jnp.arange(GRID_W)
    col_start = jnp.clip(col - NA_COLS // 2, 0, GRID_W - NA_COLS)
    cvalid = (col[None, :] >= col_start[:, None]) & (col[None, :] < col_start[:, None] + NA_COLS)
    dc = jnp.clip(col[None, :] - col[:, None], -(NA_COLS - 1), NA_COLS - 1) + (NA_COLS - 1)
    tiles = jnp.where(cvalid[None, None], rpb.astype(F32)[:, :, dc], NEG_INF)
    tiles = jnp.concatenate([tiles, jnp.full_like(tiles[:, :1], NEG_INF)], axis=1)
    idx = []
    for j in (0, min(1, nj - 1), nj - 1):
        r = j * NA_QROWS + jnp.arange(NA_QROWS)
        krow = j * NA_QROWS - NA_KBLK // GRID_W + jnp.arange(NA_K // GRID_W)
        start = jnp.clip(r - kr // 2, 0, rows - kr)
        rvalid = (krow[None, :] >= start[:, None]) & (krow[None, :] < start[:, None] + kr)
        dr = krow[None, :] - r[:, None] + (NA_ROWS - 1)
        idx.append(jnp.where(rvalid, dr, NA_DR))
    return tiles, jnp.stack(idx).reshape(-1).astype(jnp.int32)


def _na_kernel(idx_ref, q_ref, k0, k1, k2, k3, v0, v1, v2, v3, kx_ref, vx_ref, t_ref, o_ref, bias_scr,
               *, nj):
    j = pl.program_id(2)
    nkr = NA_K // GRID_W

    @pl.when((j == 0) | (j == 1) | (j == nj - 1))
    def _():
        variant = jnp.where(j == 0, 0, jnp.where(j == nj - 1, 2, 1))
        for hh in range(LANES // HEAD_DIM):
            for qr in range(NA_QROWS):
                for kr in range(nkr):
                    d = idx_ref[(variant * NA_QROWS + qr) * nkr + kr]
                    bias_scr[hh, qr * GRID_W:(qr + 1) * GRID_W, kr * GRID_W:(kr + 1) * GRID_W] = (
                        t_ref[hh, d])

    q = q_ref[...]
    k_all = jnp.concatenate([k0[...], k1[...], k2[...], k3[...]], axis=0)
    v_all = jnp.concatenate([v0[...], v1[...], v2[...], v3[...]], axis=0)
    kx = kx_ref[...]
    vx = vx_ref[...]
    outs = []
    for hh in range(LANES // HEAD_DIM):
        lo, hi = hh * HEAD_DIM, (hh + 1) * HEAD_DIM
        qh = q[:, lo:hi]
        s_loc = _dot_nt(qh, k_all[:, lo:hi]) + bias_scr[hh]
        s_ctx = _dot_nt(qh, kx[:, lo:hi])
        outs.append(_softmax_pv([(s_ctx, vx[:, lo:hi]), (s_loc, v_all[:, lo:hi])], None))
    o_ref[...] = jnp.concatenate(outs, axis=1).astype(o_ref.dtype)


def _na_attn(bq, bk, bv, bkx, bvx, tiles, tile_idx, batch, seq, ctx_len):
    nj = seq // NA_Q
    nkb = seq // NA_KBLK
    hp = B_HEADS * HEAD_DIM // LANES
    per = LANES // HEAD_DIM

    def kb(t):
        return lambda p, b, j, idx: (b * nkb + jnp.clip(2 * j - 1 + t, 0, nkb - 1), p)

    kv_specs = [pl.BlockSpec((NA_KBLK, LANES), kb(t)) for t in range(4)]
    grid_spec = pltpu.PrefetchScalarGridSpec(
        num_scalar_prefetch=1,
        grid=(hp, batch, nj),
        in_specs=[pl.BlockSpec((NA_Q, LANES), lambda p, b, j, idx: (b * nj + j, p))]
        + kv_specs + kv_specs
        + [pl.BlockSpec((ctx_len, LANES), lambda p, b, j, idx: (b, p)),
           pl.BlockSpec((ctx_len, LANES), lambda p, b, j, idx: (b, p)),
           pl.BlockSpec((per, NA_DR + 1, GRID_W, GRID_W), lambda p, b, j, idx: (p, 0, 0, 0))],
        out_specs=pl.BlockSpec((NA_Q, LANES), lambda p, b, j, idx: (b * nj + j, p)),
        scratch_shapes=[pltpu.VMEM((per, NA_Q, NA_K), F32)],
    )
    return pl.pallas_call(
        functools.partial(_na_kernel, nj=nj),
        grid_spec=grid_spec,
        out_shape=jax.ShapeDtypeStruct((batch * seq, B_HEADS * HEAD_DIM), BF16),
        compiler_params=_params(("arbitrary", "arbitrary", "arbitrary")),
        name="na_attn",
    )(tile_idx, bq, bk, bk, bk, bk, bv, bv, bv, bv, bkx, bvx, tiles)


def _ctx_ab_kernel(sink_ref, aq_ref, ak_ref, av_ref, bq_ref, bk_ref, bv_ref, oa_ref, ob_ref):
    ctx_len = aq_ref.shape[0]
    g_sz = A_HEADS // A_KV_HEADS
    aq = aq_ref[...]
    ak = ak_ref[...]
    av = av_ref[...]
    outs = []
    for g in range(A_KV_HEADS):
        lo, hi = g * HEAD_DIM, (g + 1) * HEAD_DIM
        heads = [g * g_sz + i for i in range(g_sz)]
        qs = jnp.concatenate([aq[:, h * HEAD_DIM:(h + 1) * HEAD_DIM] for h in heads], axis=0)
        sink = jnp.concatenate(
            [jnp.full((ctx_len, 1), sink_ref[h], F32) for h in heads], axis=0)
        o = _softmax_pv([(_dot_nt(qs, ak[:, lo:hi]), av[:, lo:hi])], sink)
        outs.extend(o[i * ctx_len:(i + 1) * ctx_len] for i in range(g_sz))
    oa_ref[...] = jnp.concatenate(outs, axis=1).astype(oa_ref.dtype)
    bq = bq_ref[...]
    bk = bk_ref[...]
    bv = bv_ref[...]
    outs = []
    for h in range(B_HEADS):
        lo, hi = h * HEAD_DIM, (h + 1) * HEAD_DIM
        outs.append(_softmax_pv([(_dot_nt(bq[:, lo:hi], bk[:, lo:hi]), bv[:, lo:hi])], None))
    ob_ref[...] = jnp.concatenate(outs, axis=1).astype(ob_ref.dtype)


def _ctx_ab_attn(sink, aq, ak, av, bq, bk, bv, batch, ctx_len):
    def spec(w):
        return pl.BlockSpec((ctx_len, w), lambda b: (b, 0))

    return pl.pallas_call(
        _ctx_ab_kernel,
        grid=(batch,),
        in_specs=[pl.BlockSpec(memory_space=pltpu.SMEM)]
        + [spec(a.shape[1]) for a in (aq, ak, av, bq, bk, bv)],
        out_specs=[spec(aq.shape[1]), spec(bq.shape[1])],
        out_shape=[jax.ShapeDtypeStruct(aq.shape, BF16), jax.ShapeDtypeStruct(bq.shape, BF16)],
        compiler_params=_params(("arbitrary",)),
        name="ctx_ab_attn",
    )(sink, aq, ak, av, bq, bk, bv)


def _layer_norm(t, g, b):
    mu = jnp.mean(t, -1, keepdims=True)
    c = t - mu
    var = jnp.mean(c * c, -1, keepdims=True)
    return c * lax.rsqrt(var + 1e-5) * g + b


def _post_attn_kernel(*refs, alpha, lat_tiles, with_ctx, lat_transposed):
    n_src = 6 if with_ctx else 3
    srcs = refs[:n_src]
    (w_ref, mod_ref, lng_ref, lnb_ref, wr_hi_ref, wr_lo_ref, br_ref,
     x1_ref, h2_ref, lg_ref) = refs[n_src:]

    def rows_of(o_ref):
        if o_ref.ndim == 2:
            return o_ref[...]
        return o_ref[0].astype(F32).T.astype(BF16)

    def run(o1_ref, o2_ref, x_ref):
        half = w_ref.shape[0] // 2
        y = _dot(rows_of(o1_ref), w_ref[0:half, :]) + _dot(rows_of(o2_ref), w_ref[half:, :])
        g1 = mod_ref[0, 2:3, :]
        x1 = _layer_norm(alpha * x_ref[...] + g1 * y, lng_ref[...], lnb_ref[...])
        x1_ref[...] = x1
        h2 = x1 * (1.0 + mod_ref[0, 4:5, :]) + mod_ref[0, 3:4, :]
        h_hi, h_lo = _split_bf16(h2)
        h2_ref[...] = _pack_halves(h2)
        wr_hi = wr_hi_ref[...]
        lg_ref[...] = (_dot(h_hi, wr_hi) + _dot(h_lo, wr_hi) + _dot(h_hi, wr_lo_ref[...])
                       + br_ref[...])

    if not with_ctx:
        run(*srcs)
    else:
        i = pl.program_id(0)
        pl.when(i < lat_tiles)(lambda: run(*srcs[:3]))
        pl.when(i >= lat_tiles)(lambda: run(*srcs[3:]))


def _post_attn(lat, ctx_src, w_bf, mod, gid, lng, lnb, wr_hi, wr_lo, br, n_lat, alpha, tm):
    half = w_bf.shape[0] // 2
    d = lat[2].shape[1]
    lat_tiles = n_lat // tm
    n_rows = n_lat + (ctx_src[0].shape[0] if ctx_src is not None else 0)
    lat_transposed = lat[0].ndim == 3

    def lat_map(i):
        return (jnp.minimum(i, lat_tiles - 1), 0)

    def ctx_map(i):
        return (jnp.maximum(i - lat_tiles, 0), 0)

    def src_specs(index_map):
        return [pl.BlockSpec((tm, half), index_map), pl.BlockSpec((tm, half), index_map),
                pl.BlockSpec((tm, d), index_map)]

    in_specs = src_specs(lat_map)
    if lat_transposed:
        per_batch = lat[0].shape[2] // tm
        t_spec = pl.BlockSpec((1, half, tm), lambda i: (i // per_batch, 0, i % per_batch))
        in_specs[0] = in_specs[1] = t_spec
    args = list(lat)
    if ctx_src is not None:
        in_specs += src_specs(ctx_map)
        args += list(ctx_src)

    def const(shape):
        return pl.BlockSpec(shape, lambda i: (0,) * len(shape))

    in_specs += [const((2 * half, d)),
                 pl.BlockSpec((1, 6, d), lambda i: (gid(i), 0, 0)),
                 const((1, d)), const((1, d)),
                 const((d, ROUTER_PAD)), const((d, ROUTER_PAD)), const((1, ROUTER_PAD))]
    args += [w_bf, mod, lng, lnb, wr_hi, wr_lo, br]
    widths = (d, d // 2, ROUTER_PAD)
    dtypes = (F32, jnp.uint32, F32)
    return pl.pallas_call(
        functools.partial(_post_attn_kernel, alpha=alpha, lat_tiles=lat_tiles,
                          with_ctx=ctx_src is not None, lat_transposed=lat_transposed),
        grid=(n_rows // tm,),
        in_specs=in_specs,
        out_specs=[pl.BlockSpec((tm, w), lambda i: (i, 0)) for w in widths],
        out_shape=[jax.ShapeDtypeStruct((n_rows, w), t) for w, t in zip(widths, dtypes)],
        compiler_params=_params(("arbitrary",)),
        name="post_attn",
    )(*args)


def _moe_kernel(be_ref, xs_ref, wgu_ref, wdn_ref, ys_ref, wgu_bf, wdn_bf):
    i = pl.program_id(0)
    prev = be_ref[jnp.maximum(i - 1, 0)]
    changed = (i == 0) | (be_ref[i] != prev)

    @pl.when(changed)
    def _():
        wgu_bf[...] = wgu_ref[0].astype(BF16)
        wdn_bf[...] = wdn_ref[0].astype(BF16)

    de = wdn_bf.shape[0]
    subs = [slice(r, r + MOE_SUB) for r in range(0, MOE_BLOCK, MOE_SUB)]
    gus = [_dot(_unpack_halves(xs_ref[sl, :]).astype(BF16), wgu_bf[...]) for sl in subs]
    for sl, gu in zip(subs, gus):
        gate = gu[:, :de]
        up = gu[:, de:]
        act = gate / (1.0 + jnp.exp(-gate)) * up
        ys_ref[sl, :] = _pack_halves(_dot(act.astype(BF16), wdn_bf[...]))


def _moe_experts(block_e, xs, w_gu, w_dn):
    rows, dw = xs.shape
    d = 2 * dw
    nb = rows // MOE_BLOCK
    de = w_dn.shape[1]
    grid_spec = pltpu.PrefetchScalarGridSpec(
        num_scalar_prefetch=1,
        grid=(nb,),
        in_specs=[
            pl.BlockSpec((MOE_BLOCK, dw), lambda i, be: (i, 0)),
            pl.BlockSpec((1, d, 2 * de), lambda i, be: (be[i], 0, 0)),
            pl.BlockSpec((1, de, d), lambda i, be: (be[i], 0, 0)),
        ],
        out_specs=pl.BlockSpec((MOE_BLOCK, dw), lambda i, be: (i, 0)),
        scratch_shapes=[pltpu.VMEM((d, 2 * de), BF16), pltpu.VMEM((de, d), BF16)],
    )
    return pl.pallas_call(
        _moe_kernel,
        grid_spec=grid_spec,
        out_shape=jax.ShapeDtypeStruct((rows, dw), jnp.uint32),
        compiler_params=_params(("arbitrary",)),
        name="moe_experts",
    )(block_e, xs, w_gu, w_dn)


ROUTE_TILE = 512


def _route_kernel(lg_ref, rec_ref, cnt_ref, carry):
    i = pl.program_id(0)

    @pl.when(i == 0)
    def _():
        carry[...] = jnp.zeros(carry.shape, F32)

    lg = lg_ref[...]
    tm = lg.shape[0]
    lane = lax.broadcasted_iota(jnp.int32, lg.shape, 1)

    def first_lane(mask):
        return jnp.min(jnp.where(mask, lane, ROUTER_PAD), axis=1, keepdims=True)

    is_g = lane < N_GROUPS
    g_log = jnp.where(is_g, lg, NEG_INF)
    g_max = jnp.max(g_log, axis=1, keepdims=True)
    g_sum = jnp.sum(jnp.where(is_g, jnp.exp(lg - g_max), 0.0), axis=1, keepdims=True)
    g_val = 1.0 / g_sum
    g_idx = first_lane(g_log == g_max)
    lo = N_GROUPS + EXP_PER_GROUP * g_idx
    sel = (lane >= lo) & (lane < lo + EXP_PER_GROUP)
    e_log = jnp.where(sel, lg, NEG_INF)
    e_max = jnp.max(e_log, axis=1, keepdims=True)
    e_exp = jnp.where(sel, jnp.exp(lg - e_max), 0.0)
    e_prob = e_exp / jnp.sum(e_exp, axis=1, keepdims=True)
    p1 = jnp.where(sel, e_prob, -1.0)
    v1 = jnp.max(p1, axis=1, keepdims=True)
    i1 = first_lane(p1 == v1)
    p2 = jnp.where(lane == i1, -1.0, p1)
    v2 = jnp.max(p2, axis=1, keepdims=True)
    i2 = first_lane(p2 == v2)
    norm = g_val / (v1 + v2)

    hot1 = lane == i1
    hot2 = lane == i2
    hot = jnp.where(hot1 | hot2, 1.0, 0.0)
    rows = lax.broadcasted_iota(jnp.int32, (tm, tm), 0)
    cols = lax.broadcasted_iota(jnp.int32, (tm, tm), 1)
    before = jnp.where(cols < rows, 1.0, 0.0).astype(BF16)
    prefix = _dot(before, hot.astype(BF16)) + carry[0:1, :]
    r1 = jnp.sum(jnp.where(hot1, prefix, 0.0), axis=1, keepdims=True)
    r2 = jnp.sum(jnp.where(hot2, prefix, 0.0), axis=1, keepdims=True)
    carry[0:1, :] = carry[0:1, :] + jnp.sum(hot, axis=0, keepdims=True)

    fields = [(i1 - N_GROUPS).astype(F32), (i2 - N_GROUPS).astype(F32), r1, r2, v1 * norm, v2 * norm]
    rec = jnp.zeros(lg.shape, F32)
    for k, f in enumerate(fields):
        rec = jnp.where(lane == k, f, rec)
    rec_ref[...] = rec

    @pl.when(i == pl.num_programs(0) - 1)
    def _():
        cnt_ref[...] = carry[...]


def _route(logits, n_tok):
    rec, cnt = pl.pallas_call(
        _route_kernel,
        grid=(n_tok // ROUTE_TILE,),
        in_specs=[pl.BlockSpec((ROUTE_TILE, ROUTER_PAD), lambda i: (i, 0))],
        out_specs=[pl.BlockSpec((ROUTE_TILE, ROUTER_PAD), lambda i: (i, 0)),
                   pl.BlockSpec((8, ROUTER_PAD), lambda i: (0, 0))],
        out_shape=[jax.ShapeDtypeStruct((n_tok, ROUTER_PAD), F32),
                   jax.ShapeDtypeStruct((8, ROUTER_PAD), F32)],
        scratch_shapes=[pltpu.VMEM((8, ROUTER_PAD), F32)],
        compiler_params=_params(("arbitrary",)),
        name="route",
    )(logits)
    experts = rec[:, 0:2].astype(jnp.int32)
    rank = rec[:, 2:4].astype(jnp.int32)
    wts = rec[:, 4:6]
    counts = cnt[0, N_GROUPS:N_GROUPS + N_EXPERTS].astype(jnp.int32)

    n_asg = n_tok * TOP_K
    pcounts = ((counts + MOE_BLOCK - 1) // MOE_BLOCK) * MOE_BLOCK
    pends = jnp.cumsum(pcounts)
    pstarts = pends - pcounts
    dest = pstarts[experts] + rank
    nb = -(-n_asg // MOE_BLOCK) + N_EXPERTS
    block_start = jnp.arange(nb, dtype=jnp.int32) * MOE_BLOCK
    block_e = jnp.minimum((pends[None, :] <= block_start[:, None]).sum(1), N_EXPERTS - 1).astype(jnp.int32)
    tok = jnp.broadcast_to(jnp.arange(n_tok, dtype=jnp.int32)[:, None], (n_tok, TOP_K))
    filler = jnp.arange(nb * MOE_BLOCK, dtype=jnp.int32) % n_tok
    row_tok = filler.at[dest.reshape(-1)].set(tok.reshape(-1), unique_indices=True)
    return wts, dest, row_tok, block_e


def _post_moe_kernel(x_ref, ya_ref, yb_ref, w_ref, mod_ref, lng_ref, lnb_ref, o_ref, *, alpha):
    w = w_ref[...]
    y = w[:, 0:1] * _unpack_halves(ya_ref[...]) + w[:, 1:2] * _unpack_halves(yb_ref[...])
    g2 = mod_ref[0, 5:6, :]
    o_ref[...] = _layer_norm(alpha * x_ref[...] + g2 * y, lng_ref[...], lnb_ref[...])


def _post_moe(x1, ya, yb, wts, mod, gid, lng, lnb, alpha, tm):
    rows, d = x1.shape
    row = pl.BlockSpec((tm, d), lambda i: (i, 0))
    packed = pl.BlockSpec((tm, d // 2), lambda i: (i, 0))
    vec = pl.BlockSpec((1, d), lambda i: (0, 0))
    return pl.pallas_call(
        functools.partial(_post_moe_kernel, alpha=alpha),
        grid=(rows // tm,),
        in_specs=[row, packed, packed,
                  pl.BlockSpec((tm, TOP_K), lambda i: (i, 0)),
                  pl.BlockSpec((1, 6, d), lambda i: (gid(i), 0, 0)),
                  vec, vec],
        out_specs=row,
        out_shape=jax.ShapeDtypeStruct((rows, d), F32),
        compiler_params=_params(("arbitrary",)),
        name="post_moe",
    )(x1, ya, yb, wts, mod, lng, lnb)


CD_COLS = dict(cq=(0, 384), ckv=(384, 640), dq=(640, 1152), dk=(1152, 1664), dv=(1664, 2176),
               kr=(2176, 2304))
C_HEAD_PAD = LANES


def _rms(t, g, eps=1e-6):
    return t * lax.rsqrt(jnp.mean(t * t, -1, keepdims=True) + eps) * g


def _proj_cd_kernel(x_ref, mod_ref, w_ref, qn_ref, wuq_ref, kvn_ref, wukv_ref,
                    cos64_ref, sin64_ref, cos32_ref, sin32_ref,
                    qc_ref, kc_ref, vct_ref, dq_ref, dk_ref, dvt_ref):
    h = _modulate(x_ref, mod_ref, 0).astype(BF16)
    r = _dot(h, w_ref[...])

    def cols(name):
        lo, hi = CD_COLS[name]
        return r[:, lo:hi]

    q = _dot(_rms(cols('cq'), qn_ref[...]).astype(BF16), wuq_ref[...])
    kv = _dot(_rms(cols('ckv'), kvn_ref[...]).astype(BF16), wukv_ref[...])
    c32, s32 = cos32_ref[...], sin32_ref[...]
    c64, s64 = cos64_ref[...], sin64_ref[...]
    q = _rope_wide(q, c32, s32, C_ROPE // 4)
    kr = _rope_tile(cols('kr'), c32, s32, C_ROPE // 4)
    dq = _rope_wide(cols('dq'), c64, s64, D_QK // 4)
    dk = _rope_wide(cols('dk'), c64, s64, D_QK // 4)
    kw = C_HEADS * C_HEAD_PAD
    qc_ref[0] = (q * ((C_NOPE + C_ROPE) ** -0.5 * LOG2E)).astype(BF16)
    kc_ref[0] = (kv[:, :kw] + jnp.concatenate([kr] * C_HEADS, axis=1)).astype(BF16)
    dq_ref[0] = (dq * (D_QK ** -0.5 * LOG2E)).astype(BF16)
    dk_ref[0] = dk.astype(BF16)
    _store_vt(vct_ref, kv[:, kw:], C_HEADS, C_V)
    _store_vt(dvt_ref, cols('dv'), D_HEADS, D_V)


def _store_vt(vt_ref, v, heads, dv):
    tm = v.shape[0]
    v_t = v.T.astype(BF16)
    row = lax.broadcasted_iota(jnp.int32, (ONES_ROWS, tm), 0)
    extra = jnp.where(row == 0, 1.0, 0.0).astype(BF16)
    for h in range(heads):
        vt_ref[0, h, 0, 0:dv, :] = v_t[h * dv:(h + 1) * dv]
        vt_ref[0, h, 0, dv:dv + ONES_ROWS, :] = extra


def _proj_cd(xall, mod, w_bf, qn, wuq, kvn, wukv, tabs, batch, seq, ctx_len, tm, tk):
    d = xall.shape[1]
    lat_t, ctx_t = seq // tm, ctx_len // tm
    n_keys = seq + ctx_len
    per_chunk = tk // tm
    n_chunks = n_keys // tk
    widths = (C_HEADS * C_HEAD_PAD, C_HEADS * C_HEAD_PAD, 2 * D_HEADS * D_QK, 2 * D_HEADS * D_QK)

    def src(b, t):
        return (jnp.where(t < lat_t, b * lat_t + t, batch * lat_t + b * ctx_t + t - lat_t), 0)

    def full(a):
        return pl.BlockSpec(a.shape, lambda b, t: (0,) * a.ndim)

    def row_spec(w):
        return pl.BlockSpec((1, tm, w), lambda b, t: (b, t, 0))

    def row_shape(w):
        return jax.ShapeDtypeStruct((batch, n_keys, w), BF16)

    def vt_spec(heads, dv):
        return pl.BlockSpec((1, heads, 1, dv + ONES_ROWS, tm),
                            lambda b, t: (b, 0, t // per_chunk, 0, t % per_chunk))

    def vt_shape(heads, dv):
        return jax.ShapeDtypeStruct((batch, heads, n_chunks, dv + ONES_ROWS, tk), BF16)

    tab = pl.BlockSpec((tm, LANES), lambda b, t: (t, 0))
    return pl.pallas_call(
        _proj_cd_kernel,
        grid=(batch, lat_t + ctx_t),
        in_specs=[pl.BlockSpec((tm, d), src),
                  pl.BlockSpec((1, 6, d), lambda b, t: (jnp.where(t < lat_t, b, batch), 0, 0)),
                  full(w_bf), full(qn), full(wuq), full(kvn), full(wukv), tab, tab, tab, tab],
        out_specs=[row_spec(widths[0]), row_spec(widths[1]), vt_spec(C_HEADS, C_V),
                   row_spec(widths[2]), row_spec(widths[3]), vt_spec(D_HEADS, D_V)],
        out_shape=[row_shape(widths[0]), row_shape(widths[1]), vt_shape(C_HEADS, C_V),
                   row_shape(widths[2]), row_shape(widths[3]), vt_shape(D_HEADS, D_V)],
        compiler_params=_params(("arbitrary", "arbitrary")),
        name="proj_cd",
    )(xall, mod, w_bf, qn, wuq, kvn, wukv, *tabs)


DENSE_SHIFT = 80.0
DENSE_L_MIN = 2.0 ** -60
DENSE_L_MAX = 2.0 ** 120
DENSE_TQ = 1024
DENSE_TK_MAX = 1280
DENSE_UNROLL_MAX = 13
MXU_DEPTH = 256
PROJ_CD_TILE = 256
ONES_ROWS = 16


def _sq_norm_row(x):
    ones = jnp.ones((8, x.shape[1]), BF16)
    return _dot_nt(ones, (x * x).astype(BF16))[0:1] * 1.02


def _key_norm_max(kchunk, n_chunks, masks):
    def body(j, mx):
        kf = kchunk(j).astype(F32)
        ksq = kf * kf
        out = []
        for msk, cur in zip(masks, mx):
            part = ksq if msk is None else jnp.where(msk, ksq, 0.0)
            rn = jnp.sum(part, axis=1, keepdims=True)
            out.append(jnp.maximum(cur, jnp.max(rn, axis=0, keepdims=True)))
        return tuple(out)

    mx = lax.fori_loop(0, n_chunks, body, tuple(jnp.zeros((1, 1), F32) for _ in masks))
    return [jnp.sqrt(v) * 1.01 for v in mx]


def _safe_online(q, kchunk, vt_ref, n_chunks, m_scr, l_scr, acc_scr):
    m_scr[...] = jnp.full(m_scr.shape, NEG_INF, F32)
    l_scr[...] = jnp.zeros(l_scr.shape, F32)
    acc_scr[...] = jnp.zeros(acc_scr.shape, F32)

    def body(j, carry):
        s = _dot_nt(kchunk(j), q)
        m_prev = m_scr[...]
        m_new = jnp.maximum(m_prev, s.max(0, keepdims=True))
        alpha = jnp.exp2(m_prev - m_new)
        p = jnp.exp2(s - m_new)
        l_scr[...] = alpha * l_scr[...] + p.sum(0, keepdims=True)
        acc_scr[...] = alpha * acc_scr[...] + _dot(vt_ref[0, 0, j], p.astype(BF16))
        m_scr[...] = m_new
        return carry

    lax.fori_loop(0, n_chunks, body, 0)


def _denominators_ok(*ls):
    ok = None
    for l in ls:
        cur = (l > DENSE_L_MIN) & (l < DENSE_L_MAX)
        ok = cur if ok is None else ok & cur
    return jnp.max(jnp.where(ok, 0.0, 1.0)) == 0.0


def _mla_kernel(q_ref, k_ref, vt_ref, o_ref, kmax_scr, acc_scr, m_scr, l_scr,
                *, n_chunks, tk, unroll, depth):
    def kchunk(j):
        return k_ref[0, pl.ds(pl.multiple_of(j * tk, tk), tk), :]

    @pl.when(pl.program_id(2) == 0)
    def _():
        (kmax,) = _key_norm_max(kchunk, n_chunks, [None])
        kmax_scr[...] = jnp.broadcast_to(kmax, kmax_scr.shape)

    q = q_ref[0]
    m_row = jnp.sqrt(_sq_norm_row(q.astype(F32))) * kmax_scr[0:1, 0:1] - DENSE_SHIFT
    acc_scr[...] = jnp.zeros(acc_scr.shape, F32)

    def body(it, carry):
        j0 = it * unroll
        pend = [_dot_nt(kchunk(j0 + u), q) for u in range(min(depth, unroll))]
        acc = acc_scr[...]
        for u in range(unroll):
            s = pend.pop(0)
            if u + depth < unroll:
                pend.append(_dot_nt(kchunk(j0 + u + depth), q))
            acc = acc + _dot(vt_ref[0, 0, j0 + u], jnp.exp2(s - m_row).astype(BF16))
        acc_scr[...] = acc
        return carry

    lax.fori_loop(0, n_chunks // unroll, body, 0)
    l = acc_scr[C_V:C_V + 1, :]
    o_ref[0] = (acc_scr[0:C_V, :] / l).astype(o_ref.dtype)

    @pl.when(jnp.logical_not(_denominators_ok(l)))
    def _():
        _safe_online(q, kchunk, vt_ref, n_chunks, m_scr, l_scr, acc_scr)
        o_ref[0] = (acc_scr[0:C_V, :] / l_scr[...]).astype(o_ref.dtype)


def _mla_attn(qc, kc, vt, batch, seq, tq, tk, unroll, depth):
    n_keys = kc.shape[1]
    nq, n_chunks = seq // tq, n_keys // tk
    rows = C_V + ONES_ROWS
    return pl.pallas_call(
        functools.partial(_mla_kernel, n_chunks=n_chunks, tk=tk, unroll=unroll, depth=depth),
        grid=(batch, C_HEADS, nq),
        in_specs=[
            pl.BlockSpec((1, tq, C_HEAD_PAD), lambda b, h, i: (b, i, h)),
            pl.BlockSpec((1, n_keys, C_HEAD_PAD), lambda b, h, i: (b, 0, h)),
            pl.BlockSpec((1, 1, n_chunks, rows, tk), lambda b, h, i: (b, h, 0, 0, 0)),
        ],
        out_specs=pl.BlockSpec((1, C_V, tq), lambda b, h, i: (b, h, i)),
        out_shape=jax.ShapeDtypeStruct((batch, C_HEADS * C_V, seq), BF16),
        scratch_shapes=[pltpu.VMEM((8, LANES), F32), pltpu.VMEM((rows, tq), F32),
                        pltpu.VMEM((1, tq), F32), pltpu.VMEM((1, tq), F32)],
        compiler_params=_params(("arbitrary", "arbitrary", "arbitrary")),
        name="mla_attn",
    )(qc, kc, vt)


def _diff_kernel(lam_ref, q_ref, k_ref, vt_ref, subln_ref, o_ref, kmax_scr, acc1_scr, acc2_scr,
                 m_scr, l_scr, *, n_chunks, tk, unroll, depth, out_scale):
    def kchunk(j):
        return k_ref[0, pl.ds(pl.multiple_of(j * tk, tk), tk), :]

    @pl.when(pl.program_id(2) == 0)
    def _():
        klane = lax.broadcasted_iota(jnp.int32, (tk, 2 * D_QK), 1)
        k1, k2 = _key_norm_max(kchunk, n_chunks, [klane < D_QK, klane >= D_QK])
        kmax_scr[0:1, :] = jnp.broadcast_to(k1, (1, LANES))
        kmax_scr[1:2, :] = jnp.broadcast_to(k2, (1, LANES))

    q = q_ref[0]
    qlane = lax.broadcasted_iota(jnp.int32, q.shape, 1)
    zero = jnp.zeros_like(q)
    q1 = jnp.where(qlane < D_QK, q, zero)
    q2 = jnp.where(qlane >= D_QK, q, zero)
    m1 = jnp.sqrt(_sq_norm_row(q1.astype(F32))) * kmax_scr[0:1, 0:1] - DENSE_SHIFT
    m2 = jnp.sqrt(_sq_norm_row(q2.astype(F32))) * kmax_scr[1:2, 0:1] - DENSE_SHIFT
    acc1_scr[...] = jnp.zeros(acc1_scr.shape, F32)
    acc2_scr[...] = jnp.zeros(acc2_scr.shape, F32)

    def scores(j):
        k = kchunk(j)
        return _dot_nt(k, q1), _dot_nt(k, q2)

    def body(it, carry):
        j0 = it * unroll
        pend = [scores(j0 + u) for u in range(min(depth, unroll))]
        a1 = acc1_scr[...]
        a2 = acc2_scr[...]
        for u in range(unroll):
            s1, s2 = pend.pop(0)
            if u + depth < unroll:
                pend.append(scores(j0 + u + depth))
            vt = vt_ref[0, 0, j0 + u]
            a1 = a1 + _dot(vt, jnp.exp2(s1 - m1).astype(BF16))
            a2 = a2 + _dot(vt, jnp.exp2(s2 - m2).astype(BF16))
        acc1_scr[...] = a1
        acc2_scr[...] = a2
        return carry

    lax.fori_loop(0, n_chunks // unroll, body, 0)

    def finish(o1, o2):
        o = o1 - lam_ref[0] * o2
        o = o * lax.rsqrt(jnp.mean(o * o, 0, keepdims=True) + 1e-6) * subln_ref[...] * out_scale
        o_ref[0] = o.astype(o_ref.dtype)

    l1 = acc1_scr[D_V:D_V + 1, :]
    l2 = acc2_scr[D_V:D_V + 1, :]
    finish(acc1_scr[0:D_V, :] / l1, acc2_scr[0:D_V, :] / l2)

    @pl.when(jnp.logical_not(_denominators_ok(l1, l2)))
    def _():
        _safe_online(q1, kchunk, vt_ref, n_chunks, m_scr, l_scr, acc1_scr)
        o1 = acc1_scr[0:D_V, :] / l_scr[...]
        _safe_online(q2, kchunk, vt_ref, n_chunks, m_scr, l_scr, acc2_scr)
        finish(o1, acc2_scr[0:D_V, :] / l_scr[...])


def _diff_attn(lam, dq, dk, vt, subln_col, out_scale, batch, seq, tq, tk, unroll, depth):
    n_keys = dk.shape[1]
    nq, n_chunks = seq // tq, n_keys // tk
    rows = D_V + ONES_ROWS
    pair = 2 * D_QK
    return pl.pallas_call(
        functools.partial(_diff_kernel, n_chunks=n_chunks, tk=tk, unroll=unroll, depth=depth,
                          out_scale=out_scale),
        grid=(batch, D_HEADS, nq),
        in_specs=[
            pl.BlockSpec(memory_space=pltpu.SMEM),
            pl.BlockSpec((1, tq, pair), lambda b, h, i: (b, i, h)),
            pl.BlockSpec((1, n_keys, pair), lambda b, h, i: (b, 0, h)),
            pl.BlockSpec((1, 1, n_chunks, rows, tk), lambda b, h, i: (b, h, 0, 0, 0)),
            pl.BlockSpec((D_V, 1), lambda b, h, i: (0, 0)),
        ],
        out_specs=pl.BlockSpec((1, D_V, tq), lambda b, h, i: (b, h, i)),
        out_shape=jax.ShapeDtypeStruct((batch, D_HEADS * D_V, seq), BF16),
        scratch_shapes=[pltpu.VMEM((8, LANES), F32), pltpu.VMEM((rows, tq), F32),
                        pltpu.VMEM((rows, tq), F32), pltpu.VMEM((1, tq), F32), pltpu.VMEM((1, tq), F32)],
        compiler_params=_params(("arbitrary", "arbitrary", "arbitrary")),
        name="diff_attn",
    )(lam, dq, dk, vt, subln_col)


def _rope_tables(n_tok, dim, lane_lo):
    t = jnp.arange(n_tok)
    pos_r = (t // GRID_W).astype(F32)
    pos_c = (t % GRID_W).astype(F32)
    quarter = dim // 4
    inv = ROPE_BASE ** (-jnp.arange(quarter, dtype=F32) / quarter)
    ang_r = pos_r[:, None] * inv
    ang_c = pos_c[:, None] * inv
    ang = jnp.concatenate([ang_r, ang_r, ang_c, ang_c], -1)
    sign = jnp.tile(jnp.concatenate([-jnp.ones(quarter), jnp.ones(quarter)]), 2).astype(F32)
    cos, sin = jnp.cos(ang), jnp.sin(ang) * sign
    reps = (LANES - lane_lo) // dim
    cos = jnp.concatenate([jnp.ones((n_tok, lane_lo), F32)] + [cos] * reps, axis=1)
    sin = jnp.concatenate([jnp.zeros((n_tok, lane_lo), F32)] + [sin] * reps, axis=1)
    return cos, sin


def _cd_weights(w_in, w_uq, w_ukv):
    d = w_in.shape[0]
    s0 = C_Q_RANK
    s1 = s0 + C_KV_RANK
    s2 = s1 + C_ROPE
    s3 = s2 + 2 * D_HEADS * D_QK
    s4 = s3 + 2 * D_HEADS * D_QK
    kr = jnp.zeros((d, LANES), F32).at[:, C_NOPE:C_NOPE + C_ROPE].set(w_in[:, s1:s2])
    w_in_p = jnp.concatenate([w_in[:, :s1], w_in[:, s2:s3], w_in[:, s3:s4], w_in[:, s4:], kr], axis=1)
    dqk = C_NOPE + C_ROPE
    wq = w_uq.reshape(C_Q_RANK, C_HEADS, dqk)
    wq = jnp.pad(wq, ((0, 0), (0, 0), (0, C_HEAD_PAD - dqk))).reshape(C_Q_RANK, C_HEADS * C_HEAD_PAD)
    wkv = w_ukv.reshape(C_KV_RANK, C_HEADS, C_NOPE + C_V)
    wk = jnp.pad(wkv[:, :, :C_NOPE], ((0, 0), (0, 0), (0, C_HEAD_PAD - C_NOPE)))
    wk = wk.reshape(C_KV_RANK, C_HEADS * C_HEAD_PAD)
    wv = wkv[:, :, C_NOPE:].reshape(C_KV_RANK, C_HEADS * C_V)
    return w_in_p.astype(BF16), wq.astype(BF16), jnp.concatenate([wk, wv], axis=1).astype(BF16)


def _moe_layer(h2, logits, x1, w_gu, w_dn, mod, gid, lng, lnb, alpha, tm):
    n_tok = h2.shape[0]
    wts, dest, row_tok, block_e = _route(logits, n_tok)
    xs = jnp.take(h2, row_tok, axis=0, mode='clip')
    ys = _moe_experts(block_e, xs, w_gu, w_dn)
    ya = jnp.take(ys, dest[:, 0], axis=0, mode='clip')
    yb = jnp.take(ys, dest[:, 1], axis=0, mode='clip')
    return _post_moe(x1, ya, yb, wts.astype(F32), mod, gid, lng, lnb, alpha, tm)


def kernel(x, c, ctx, c_ctx, w_ada, b_ada, ln_g, ln_b, ab_w_in, a_sink, b_rpb, ab_w_out, cd_w_in, c_q_norm, c_w_uq, c_kv_norm, c_w_ukv, d_lambda, d_subln, cd_w_out, w_group, b_group, w_exp_router, b_exp_router, w_gate_up, w_down):
    batch, seq, d = x.shape
    ctx_len = ctx.shape[1]
    depth = w_ada.shape[0]
    alpha = (2 * depth) ** 0.25
    n_lat = batch * seq
    n_ctx = batch * ctx_len
    tm = 512
    tq = DENSE_TQ
    assert depth == 2 and seq % NA_Q == 0 and seq % tq == 0 and n_ctx % tm == 0
    assert seq % PROJ_CD_TILE == 0 and ctx_len % PROJ_CD_TILE == 0
    assert seq % (WIN_GROUP * A_BLOCK) == 0 and seq // NA_Q >= 2
    key_tiles = (seq + ctx_len) // MXU_DEPTH
    tk = MXU_DEPTH * max(u for u in range(1, DENSE_TK_MAX // MXU_DEPTH + 1) if key_tiles % u == 0)
    n_chunks = (seq + ctx_len) // tk

    def unroll_for(cap):
        return max(u for u in range(1, cap + 1) if n_chunks % u == 0)
    lat_tiles = seq // tm

    def gid_lat(i):
        return i // lat_tiles

    def gid_ctx(i):
        return batch

    def gid_all(i):
        return jnp.minimum(i // lat_tiles, batch)

    c_all = jnp.zeros((8, d), F32).at[:batch].set(c).at[batch].set(c_ctx)
    mod = _ada(c_all, w_ada, b_ada).reshape(depth, 8, 6, d)

    cos64, sin64 = _rope_tables(seq, HEAD_DIM, 0)
    cos32, sin32 = _rope_tables(seq, C_ROPE, C_NOPE)

    def router_weights(l):
        wr = jnp.zeros((d, ROUTER_PAD), F32)
        wr = wr.at[:, :N_GROUPS].set(w_group[l]).at[:, N_GROUPS:N_GROUPS + N_EXPERTS].set(w_exp_router[l])
        br = jnp.zeros((1, ROUTER_PAD), F32)
        br = br.at[0, :N_GROUPS].set(b_group[l]).at[0, N_GROUPS:N_GROUPS + N_EXPERTS].set(b_exp_router[l])
        hi = wr.astype(BF16)
        lo = (wr - hi.astype(F32)).astype(BF16)
        return hi, lo, br

    x2d = x.reshape(n_lat, d)
    ctx2d = ctx.reshape(n_ctx, d)

    l = 0
    w_in_bf = ab_w_in[0].astype(BF16)
    aq, ak, av, bq, bk, bv = _proj_ab(x2d, 0, n_lat, mod[l], gid_lat, w_in_bf, cos64, sin64,
                                      lat_tiles, True, tm)
    aqx, akx, avx, bqx, bkx, bvx = _proj_ab(ctx2d, 0, n_ctx, mod[l], gid_ctx, w_in_bf, cos64, sin64,
                                            1, False, tm)
    sink = a_sink[0].astype(F32)
    oa = _window_attn(sink, aq, ak, av, akx, avx, batch, seq, ctx_len)
    tiles, tile_idx = _na_bias_tiles(b_rpb[0], seq // GRID_W)
    ob = _na_attn(bq, bk, bv, bkx, bvx, tiles, tile_idx, batch, seq, ctx_len)
    oax, obx = _ctx_ab_attn(sink, aqx, akx, avx, bqx, bkx, bvx, batch, ctx_len)

    n_all = n_lat + n_ctx
    w_out_bf = ab_w_out[0].astype(BF16)
    lng, lnb = ln_g[l, 0][None], ln_b[l, 0][None]
    wr_hi, wr_lo, br = router_weights(l)
    x1, h2, logits = _post_attn((oa, ob, x2d), (oax, obx, ctx2d), w_out_bf, mod[l], gid_all,
                                lng, lnb, wr_hi, wr_lo, br, n_lat, alpha, tm)
    xall = _moe_layer(h2, logits, x1, w_gate_up[l], w_down[l], mod[l], gid_all,
                      ln_g[l, 1][None], ln_b[l, 1][None], alpha, tm)

    l = 1
    lam_init = 0.8 - 0.6 * math.exp(-0.3 * l)
    lp = d_lambda[0].astype(F32)
    lam = (jnp.exp(jnp.sum(lp[0] * lp[1])) - jnp.exp(jnp.sum(lp[2] * lp[3])) + lam_init).reshape(1)
    w_in_p, wuq_p, wukv_p = _cd_weights(cd_w_in[0], c_w_uq[0], c_w_ukv[0])
    qn, kvn = c_q_norm[0][None].astype(F32), c_kv_norm[0][None].astype(F32)
    def with_ctx_identity(cos, sin):
        return (jnp.concatenate([cos, jnp.ones((ctx_len, LANES), F32)]),
                jnp.concatenate([sin, jnp.zeros((ctx_len, LANES), F32)]))

    tabs = with_ctx_identity(cos64, sin64) + with_ctx_identity(cos32, sin32)
    qc, kc, vct, dq, dk, dvt = _proj_cd(xall, mod[l], w_in_p, qn, wuq_p, kvn, wukv_p, tabs,
                                        batch, seq, ctx_len, PROJ_CD_TILE, tk)
    unroll = unroll_for(DENSE_UNROLL_MAX)
    oc_t = _mla_attn(qc, kc, vct, batch, seq, tq, tk, unroll, 1)
    od_t = _diff_attn(lam, dq, dk, dvt, d_subln[0].astype(F32).reshape(D_V, 1), 1.0 - lam_init,
                      batch, seq, tq, tk, unroll, 1)
    wr_hi, wr_lo, br = router_weights(l)
    x1, h2, logits = _post_attn((oc_t, od_t, xall), None, cd_w_out[0].astype(BF16), mod[l], gid_lat,
                                ln_g[l, 0][None], ln_b[l, 0][None], wr_hi, wr_lo, br,
                                n_lat, alpha, tm)
    out = _moe_layer(h2, logits, x1, w_gate_up[l], w_down[l], mod[l], gid_lat,
                     ln_g[l, 1][None], ln_b[l, 1][None], alpha, tm)
    return out.reshape(batch, seq, d)
```

```python
import functools
import math

import jax
import jax.numpy as jnp
from jax import lax
from jax.experimental import pallas as pl
from jax.experimental.pallas import tpu as pltpu

F32 = jnp.float32
BF16 = jnp.bfloat16

GRID_W = 64
HEAD_DIM = 64
ROPE_BASE = 10000.0
NEG_INF = -1e30
LOG2E = 1.4426950408889634

A_HEADS = 8
A_KV_HEADS = 2
A_WINDOW = 128
A_BLOCK = 128
B_HEADS = 8
NA_ROWS = 8
NA_COLS = 16
C_HEADS = 8
C_Q_RANK = 384
C_KV_RANK = 256
C_NOPE = 64
C_ROPE = 32
C_V = 64
D_HEADS = 4
D_QK = 64
D_V = 128
N_GROUPS = 4
EXP_PER_GROUP = 8
N_EXPERTS = N_GROUPS * EXP_PER_GROUP
TOP_K = 2
MOE_BLOCK = 512
MOE_SUB = 256
ROUTER_PAD = 128

LANES = 128
VMEM_LIMIT = 56 * 1024 * 1024


def _params(sem):
    return pltpu.CompilerParams(dimension_semantics=sem, vmem_limit_bytes=VMEM_LIMIT)


def _dot(a, b):
    return jnp.dot(a, b, preferred_element_type=F32)


def _dot_nt(a, b):
    return lax.dot_general(a, b, (((1,), (1,)), ((), ())), preferred_element_type=F32)


def _split_bf16(x):
    hi = x.astype(BF16)
    lo = (x - hi.astype(F32)).astype(BF16)
    return hi, lo


HI_HALF = 0xFFFF0000


def _pack_halves(x):
    w = x.shape[1] // 2
    lo = lax.bitcast_convert_type(x[:, :w].astype(BF16).astype(F32), jnp.uint32)
    hi = lax.bitcast_convert_type(x[:, w:].astype(BF16).astype(F32), jnp.uint32)
    return (lo >> 16) | (hi & jnp.uint32(HI_HALF))


def _unpack_halves(u):
    lo = lax.bitcast_convert_type(u << 16, F32)
    hi = lax.bitcast_convert_type(u & jnp.uint32(HI_HALF), F32)
    return jnp.concatenate([lo, hi], axis=1)


def _ada_kernel(c_ref, w_ref, b_ref, o_ref):
    c = c_ref[...]
    a = c / (1.0 + jnp.exp(-c))
    a_hi, a_lo = _split_bf16(a)
    w_hi, w_lo = _split_bf16(w_ref[0])
    acc = _dot(a_hi, w_hi) + _dot(a_lo, w_hi) + _dot(a_hi, w_lo)
    o_ref[0] = acc + b_ref[0]


def _ada(c_all, w_ada, b_ada):
    depth, d, n = w_ada.shape
    tn = 1536
    return pl.pallas_call(
        _ada_kernel,
        grid=(depth, n // tn),
        in_specs=[
            pl.BlockSpec((8, d), lambda l, j: (0, 0)),
            pl.BlockSpec((1, d, tn), lambda l, j: (l, 0, j)),
            pl.BlockSpec((1, 1, tn), lambda l, j: (l, 0, j)),
        ],
        out_specs=pl.BlockSpec((1, 8, tn), lambda l, j: (l, 0, j)),
        out_shape=jax.ShapeDtypeStruct((depth, 8, n), F32),
        compiler_params=_params(("arbitrary", "arbitrary")),
        name="ada",
    )(c_all, w_ada, b_ada.reshape(depth, 1, n))


def _rope_tile(t, cos, sin_signed, q):
    lane = lax.broadcasted_iota(jnp.int32, t.shape, 1)
    first = (lane & (2 * q - 1)) < q
    up = pltpu.roll(t, LANES - q, 1)
    dn = pltpu.roll(t, q, 1)
    return t * cos + jnp.where(first, up, dn) * sin_signed


def _rope_wide(t, cos, sin_signed, q):
    n = t.shape[1] // LANES
    return jnp.concatenate(
        [_rope_tile(t[:, i * LANES:(i + 1) * LANES], cos, sin_signed, q) for i in range(n)], axis=1)


def _modulate(x_ref, mod_ref, which):
    sh = mod_ref[0, 3 * which:3 * which + 1, :]
    sc = mod_ref[0, 3 * which + 1:3 * which + 2, :]
    return x_ref[...] * (1.0 + sc) + sh


def _proj_ab_kernel(x_ref, mod_ref, w_ref, cos_ref, sin_ref,
                    aq_ref, ak_ref, av_ref, bq_ref, bk_ref, bv_ref, *, rope):
    h = _modulate(x_ref, mod_ref, 0).astype(BF16)
    r = _dot(h, w_ref[...])
    aq = r[:, 0:512]
    ak = r[:, 512:640]
    if rope:
        cos = cos_ref[...]
        sin = sin_ref[...]
        aq = _rope_wide(aq, cos, sin, 16)
        ak = _rope_wide(ak, cos, sin, 16)
    scale = HEAD_DIM ** -0.5
    aq_ref[...] = (aq * scale).astype(BF16)
    ak_ref[...] = ak.astype(BF16)
    av_ref[...] = r[:, 640:768].astype(BF16)
    bq_ref[...] = (r[:, 768:1280] * scale).astype(BF16)
    bk_ref[...] = r[:, 1280:1792].astype(BF16)
    bv_ref[...] = r[:, 1792:2304].astype(BF16)


def _proj_ab(x2d, row0, nrows, mod, gid, w_bf, cos, sin, pos_blocks, rope, tm):
    d = x2d.shape[1]
    n_in = w_bf.shape[1]
    b0 = row0 // tm
    widths = (512, 128, 128, 512, 512, 512)
    return pl.pallas_call(
        functools.partial(_proj_ab_kernel, rope=rope),
        grid=(nrows // tm,),
        in_specs=[
            pl.BlockSpec((tm, d), lambda i: (b0 + i, 0)),
            pl.BlockSpec((1, 6, d), lambda i: (gid(i), 0, 0)),
            pl.BlockSpec((d, n_in), lambda i: (0, 0)),
            pl.BlockSpec((tm, LANES), lambda i: (i % pos_blocks, 0)),
            pl.BlockSpec((tm, LANES), lambda i: (i % pos_blocks, 0)),
        ],
        out_specs=[pl.BlockSpec((tm, w), lambda i: (i, 0)) for w in widths],
        out_shape=[jax.ShapeDtypeStruct((nrows, w), BF16) for w in widths],
        compiler_params=_params(("arbitrary",)),
        name="proj_ab",
    )(x2d, mod, w_bf, cos, sin)


def _softmax_pv(parts, sink_col):
    m = parts[0][0].max(-1, keepdims=True)
    for s, _ in parts[1:]:
        m = jnp.maximum(m, s.max(-1, keepdims=True))
    if sink_col is not None:
        m = jnp.maximum(m, sink_col)
    denom = None if sink_col is None else jnp.exp(sink_col - m)
    o = None
    for s, v in parts:
        e = jnp.exp(s - m)
        d = e.sum(-1, keepdims=True)
        denom = d if denom is None else denom + d
        pv = _dot(e.astype(BF16), v)
        o = pv if o is None else o + pv
    return o / denom


def _win_kernel(sink_ref, q_ref, kp_ref, kc_ref, kn_ref, vp_ref, vc_ref, vn_ref,
                kx_ref, vx_ref, o_ref, *, seq):
    step = pl.program_id(1)
    blk = A_BLOCK
    g_sz = A_HEADS // A_KV_HEADS
    qi = lax.broadcasted_iota(jnp.int32, (blk, 3 * blk), 0)
    kj = lax.broadcasted_iota(jnp.int32, (blk, 3 * blk), 1)
    in_window = jnp.abs(kj - blk - qi) <= A_WINDOW
    k_ext = jnp.concatenate([kp_ref[...], kc_ref[...], kn_ref[...]], axis=0)
    v_ext = jnp.concatenate([vp_ref[...], vc_ref[...], vn_ref[...]], axis=0)
    kx = kx_ref[...]
    vx = vx_ref[...]

    def scores(t, g):
        lo, hi = g * HEAD_DIM, (g + 1) * HEAD_DIM
        kpos = (step * WIN_GROUP + t) * blk + kj - blk
        valid = in_window & (kpos >= 0) & (kpos < seq)
        valid = jnp.concatenate([valid] * g_sz, axis=0)
        q = q_ref[t * blk:(t + 1) * blk, :]
        heads = [g * g_sz + i for i in range(g_sz)]
        qs = jnp.concatenate([q[:, h * HEAD_DIM:(h + 1) * HEAD_DIM] for h in heads], axis=0)
        s_loc = jnp.where(valid, _dot_nt(qs, k_ext[t * blk:(t + 3) * blk, lo:hi]), NEG_INF)
        return s_loc, _dot_nt(qs, kx[:, lo:hi])

    units = [(t, g) for t in range(WIN_GROUP) for g in range(A_KV_HEADS)]
    pending = scores(*units[0])
    outs = []
    for n, (t, g) in enumerate(units):
        s_loc, s_ctx = pending
        if n + 1 < len(units):
            pending = scores(*units[n + 1])
        lo, hi = g * HEAD_DIM, (g + 1) * HEAD_DIM
        sink = jnp.concatenate(
            [jnp.full((blk, 1), sink_ref[g * g_sz + i], F32) for i in range(g_sz)], axis=0)
        o = _softmax_pv([(s_ctx, vx[:, lo:hi]), (s_loc, v_ext[t * blk:(t + 3) * blk, lo:hi])], sink)
        outs.extend(o[i * blk:(i + 1) * blk] for i in range(g_sz))
        if g == A_KV_HEADS - 1:
            o_ref[t * blk:(t + 1) * blk, :] = jnp.concatenate(outs, axis=1).astype(o_ref.dtype)
            outs = []


WIN_GROUP = 4


def _window_attn(sink, aq, ak, av, akx, avx, batch, seq, ctx_len):
    nblk = seq // A_BLOCK
    nstep = nblk // WIN_GROUP
    kvw = A_KV_HEADS * HEAD_DIM
    qw = A_HEADS * HEAD_DIM
    rows = WIN_GROUP * A_BLOCK

    def edge(delta):
        return lambda b, n: (b * nblk + jnp.clip(n * WIN_GROUP + delta, 0, nblk - 1), 0)

    def own(b, n):
        return (b * nstep + n, 0)

    kv_specs = [pl.BlockSpec((A_BLOCK, kvw), edge(-1)), pl.BlockSpec((rows, kvw), own),
                pl.BlockSpec((A_BLOCK, kvw), edge(WIN_GROUP))]
    return pl.pallas_call(
        functools.partial(_win_kernel, seq=seq),
        grid=(batch, nstep),
        in_specs=[pl.BlockSpec(memory_space=pltpu.SMEM), pl.BlockSpec((rows, qw), own)]
        + kv_specs + kv_specs
        + [pl.BlockSpec((ctx_len, kvw), lambda b, n: (b, 0)),
           pl.BlockSpec((ctx_len, kvw), lambda b, n: (b, 0))],
        out_specs=pl.BlockSpec((rows, qw), own),
        out_shape=jax.ShapeDtypeStruct((batch * seq, qw), BF16),
        compiler_params=_params(("arbitrary", "arbitrary")),
        name="window_attn",
    )(sink, aq, ak, ak, ak, av, av, av, akx, avx)


NA_QROWS = 8
NA_Q = NA_QROWS * GRID_W
NA_KBLK = 4 * GRID_W
NA_K = 4 * NA_KBLK


NA_DR = 2 * NA_ROWS - 1


def _na_bias_tiles(rpb, rows):
    nj = rows // NA_QROWS
    kr = min(NA_ROWS, rows)
    col = jnp.arange(GRID_W)
    col_start = jnp.clip(col - NA_COLS // 2, 0, GRID_W - NA_COLS)
    cvalid = (col[None, :] >= col_start[:, None]) & (col[None, :] < col_start[:, None] + NA_COLS)
    dc = jnp.clip(col[None, :] - col[:, None], -(NA_COLS - 1), NA_COLS - 1) + (NA_COLS - 1)
    tiles = jnp.where(cvalid[None, None], rpb.astype(F32)[:, :, dc], NEG_INF)
    tiles = jnp.concatenate([tiles, jnp.full_like(tiles[:, :1], NEG_INF)], axis=1)
    idx = []
    for j in (0, min(1, nj - 1), nj - 1):
        r = j * NA_QROWS + jnp.arange(NA_QROWS)
        krow = j * NA_QROWS - NA_KBLK // GRID_W + jnp.arange(NA_K // GRID_W)
        start = jnp.clip(r - kr // 2, 0, rows - kr)
        rvalid = (krow[None, :] >= start[:, None]) & (krow[None, :] < start[:, None] + kr)
        dr = krow[None, :] - r[:, None] + (NA_ROWS - 1)
        idx.append(jnp.where(rvalid, dr, NA_DR))
    return tiles, jnp.stack(idx).reshape(-1).astype(jnp.int32)


def _na_kernel(idx_ref, q_ref, k0, k1, k2, k3, v0, v1, v2, v3, kx_ref, vx_ref, t_ref, o_ref, bias_scr,
               *, nj):
    j = pl.program_id(2)
    nkr = NA_K // GRID_W

    @pl.when((j == 0) | (j == 1) | (j == nj - 1))
    def _():
        variant = jnp.where(j == 0, 0, jnp.where(j == nj - 1, 2, 1))
        for hh in range(LANES // HEAD_DIM):
            for qr in range(NA_QROWS):
                for kr in range(nkr):
                    d = idx_ref[(variant * NA_QROWS + qr) * nkr + kr]
                    bias_scr[hh, qr * GRID_W:(qr + 1) * GRID_W, kr * GRID_W:(kr + 1) * GRID_W] = (
                        t_ref[hh, d])

    q = q_ref[...]
    k_all = jnp.concatenate([k0[...], k1[...], k2[...], k3[...]], axis=0)
    v_all = jnp.concatenate([v0[...], v1[...], v2[...], v3[...]], axis=0)
    kx = kx_ref[...]
    vx = vx_ref[...]
    qh_rows = NA_Q // 2
    k_span = 3 * NA_KBLK

    def scores(hh, half):
        lo, hi = hh * HEAD_DIM, (hh + 1) * HEAD_DIM
        q0, k0_ = half * qh_rows, half * NA_KBLK
        qh = q[q0:q0 + qh_rows, lo:hi]
        s_loc = _dot_nt(qh, k_all[k0_:k0_ + k_span, lo:hi]) + bias_scr[hh, q0:q0 + qh_rows, k0_:k0_ + k_span]
        return s_loc, _dot_nt(qh, kx[:, lo:hi])

    units = [(hh, half) for half in range(2) for hh in range(LANES // HEAD_DIM)]
    pending = scores(*units[0])
    for n, (hh, half) in enumerate(units):
        s_loc, s_ctx = pending
        if n + 1 < len(units):
            pending = scores(*units[n + 1])
        lo, hi = hh * HEAD_DIM, (hh + 1) * HEAD_DIM
        k0_ = half * NA_KBLK
        o = _softmax_pv([(s_ctx, vx[:, lo:hi]), (s_loc, v_all[k0_:k0_ + k_span, lo:hi])], None)
        o_ref[half * qh_rows:(half + 1) * qh_rows, lo:hi] = o.astype(o_ref.dtype)


def _na_attn(bq, bk, bv, bkx, bvx, tiles, tile_idx, batch, seq, ctx_len):
    nj = seq // NA_Q
    nkb = seq // NA_KBLK
    hp = B_HEADS * HEAD_DIM // LANES
    per = LANES // HEAD_DIM

    def kb(t):
        return lambda p, b, j, idx: (b * nkb + jnp.clip(2 * j - 1 + t, 0, nkb - 1), p)

    kv_specs = [pl.BlockSpec((NA_KBLK, LANES), kb(t)) for t in range(4)]
    grid_spec = pltpu.PrefetchScalarGridSpec(
        num_scalar_prefetch=1,
        grid=(hp, batch, nj),
        in_specs=[pl.BlockSpec((NA_Q, LANES), lambda p, b, j, idx: (b * nj + j, p))]
        + kv_specs + kv_specs
        + [pl.BlockSpec((ctx_len, LANES), lambda p, b, j, idx: (b, p)),
           pl.BlockSpec((ctx_len, LANES), lambda p, b, j, idx: (b, p)),
           pl.BlockSpec((per, NA_DR + 1, GRID_W, GRID_W), lambda p, b, j, idx: (p, 0, 0, 0))],
        out_specs=pl.BlockSpec((NA_Q, LANES), lambda p, b, j, idx: (b * nj + j, p)),
        scratch_shapes=[pltpu.VMEM((per, NA_Q, NA_K), F32)],
    )
    return pl.pallas_call(
        functools.partial(_na_kernel, nj=nj),
        grid_spec=grid_spec,
        out_shape=jax.ShapeDtypeStruct((batch * seq, B_HEADS * HEAD_DIM), BF16),
        compiler_params=_params(("arbitrary", "arbitrary", "arbitrary")),
        name="na_attn",
    )(tile_idx, bq, bk, bk, bk, bk, bv, bv, bv, bv, bkx, bvx, tiles)


def _ctx_ab_kernel(sink_ref, aq_ref, ak_ref, av_ref, bq_ref, bk_ref, bv_ref, oa_ref, ob_ref):
    ctx_len = aq_ref.shape[0]
    g_sz = A_HEADS // A_KV_HEADS
    aq = aq_ref[...]
    ak = ak_ref[...]
    av = av_ref[...]
    outs = []
    for g in range(A_KV_HEADS):
        lo, hi = g * HEAD_DIM, (g + 1) * HEAD_DIM
        heads = [g * g_sz + i for i in range(g_sz)]
        qs = jnp.concatenate([aq[:, h * HEAD_DIM:(h + 1) * HEAD_DIM] for h in heads], axis=0)
        sink = jnp.concatenate(
            [jnp.full((ctx_len, 1), sink_ref[h], F32) for h in heads], axis=0)
        o = _softmax_pv([(_dot_nt(qs, ak[:, lo:hi]), av[:, lo:hi])], sink)
        outs.extend(o[i * ctx_len:(i + 1) * ctx_len] for i in range(g_sz))
    oa_ref[...] = jnp.concatenate(outs, axis=1).astype(oa_ref.dtype)
    bq = bq_ref[...]
    bk = bk_ref[...]
    bv = bv_ref[...]
    outs = []
    for h in range(B_HEADS):
        lo, hi = h * HEAD_DIM, (h + 1) * HEAD_DIM
        outs.append(_softmax_pv([(_dot_nt(bq[:, lo:hi], bk[:, lo:hi]), bv[:, lo:hi])], None))
    ob_ref[...] = jnp.concatenate(outs, axis=1).astype(ob_ref.dtype)


def _ctx_ab_attn(sink, aq, ak, av, bq, bk, bv, batch, ctx_len):
    def spec(w):
        return pl.BlockSpec((ctx_len, w), lambda b: (b, 0))

    return pl.pallas_call(
        _ctx_ab_kernel,
        grid=(batch,),
        in_specs=[pl.BlockSpec(memory_space=pltpu.SMEM)]
        + [spec(a.shape[1]) for a in (aq, ak, av, bq, bk, bv)],
        out_specs=[spec(aq.shape[1]), spec(bq.shape[1])],
        out_shape=[jax.ShapeDtypeStruct(aq.shape, BF16), jax.ShapeDtypeStruct(bq.shape, BF16)],
        compiler_params=_params(("arbitrary",)),
        name="ctx_ab_attn",
    )(sink, aq, ak, av, bq, bk, bv)


def _layer_norm(t, g, b):
    mu = jnp.mean(t, -1, keepdims=True)
    c = t - mu
    var = jnp.mean(c * c, -1, keepdims=True)
    return c * lax.rsqrt(var + 1e-5) * g + b


def _post_attn_kernel(*refs, alpha, lat_tiles, with_ctx, lat_transposed):
    n_src = 6 if with_ctx else 3
    srcs = refs[:n_src]
    (w_ref, mod_ref, lng_ref, lnb_ref, wr_hi_ref, wr_lo_ref, br_ref,
     x1_ref, h2_ref, lg_ref) = refs[n_src:]

    def rows_of(o_ref):
        if o_ref.ndim == 2:
            return o_ref[...]
        return o_ref[0].astype(F32).T.astype(BF16)

    def run(o1_ref, o2_ref, x_ref):
        half = w_ref.shape[0] // 2
        y = _dot(rows_of(o1_ref), w_ref[0:half, :]) + _dot(rows_of(o2_ref), w_ref[half:, :])
        g1 = mod_ref[0, 2:3, :]
        x1 = _layer_norm(alpha * x_ref[...] + g1 * y, lng_ref[...], lnb_ref[...])
        x1_ref[...] = x1
        h2 = x1 * (1.0 + mod_ref[0, 4:5, :]) + mod_ref[0, 3:4, :]
        h_hi, h_lo = _split_bf16(h2)
        h2_ref[...] = _pack_halves(h2)
        wr_hi = wr_hi_ref[...]
        lg_ref[...] = (_dot(h_hi, wr_hi) + _dot(h_lo, wr_hi) + _dot(h_hi, wr_lo_ref[...])
                       + br_ref[...])

    if not with_ctx:
        run(*srcs)
    else:
        i = pl.program_id(0)
        pl.when(i < lat_tiles)(lambda: run(*srcs[:3]))
        pl.when(i >= lat_tiles)(lambda: run(*srcs[3:]))


def _post_attn(lat, ctx_src, w_bf, mod, gid, lng, lnb, wr_hi, wr_lo, br, n_lat, alpha, tm):
    half = w_bf.shape[0] // 2
    d = lat[2].shape[1]
    lat_tiles = n_lat // tm
    n_rows = n_lat + (ctx_src[0].shape[0] if ctx_src is not None else 0)
    lat_transposed = lat[0].ndim == 3

    def lat_map(i):
        return (jnp.minimum(i, lat_tiles - 1), 0)

    def ctx_map(i):
        return (jnp.maximum(i - lat_tiles, 0), 0)

    def src_specs(index_map):
        return [pl.BlockSpec((tm, half), index_map), pl.BlockSpec((tm, half), index_map),
                pl.BlockSpec((tm, d), index_map)]

    in_specs = src_specs(lat_map)
    if lat_transposed:
        per_batch = lat[0].shape[2] // tm
        t_spec = pl.BlockSpec((1, half, tm), lambda i: (i // per_batch, 0, i % per_batch))
        in_specs[0] = in_specs[1] = t_spec
    args = list(lat)
    if ctx_src is not None:
        in_specs += src_specs(ctx_map)
        args += list(ctx_src)

    def const(shape):
        return pl.BlockSpec(shape, lambda i: (0,) * len(shape))

    in_specs += [const((2 * half, d)),
                 pl.BlockSpec((1, 6, d), lambda i: (gid(i), 0, 0)),
                 const((1, d)), const((1, d)),
                 const((d, ROUTER_PAD)), const((d, ROUTER_PAD)), const((1, ROUTER_PAD))]
    args += [w_bf, mod, lng, lnb, wr_hi, wr_lo, br]
    widths = (d, d // 2, ROUTER_PAD)
    dtypes = (F32, jnp.uint32, F32)
    return pl.pallas_call(
        functools.partial(_post_attn_kernel, alpha=alpha, lat_tiles=lat_tiles,
                          with_ctx=ctx_src is not None, lat_transposed=lat_transposed),
        grid=(n_rows // tm,),
        in_specs=in_specs,
        out_specs=[pl.BlockSpec((tm, w), lambda i: (i, 0)) for w in widths],
        out_shape=[jax.ShapeDtypeStruct((n_rows, w), t) for w, t in zip(widths, dtypes)],
        compiler_params=_params(("arbitrary",)),
        name="post_attn",
    )(*args)


def _moe_kernel(be_ref, xs_ref, wgu_ref, wdn_ref, ys_ref, wgu_bf, wdn_bf):
    i = pl.program_id(0)
    prev = be_ref[jnp.maximum(i - 1, 0)]
    changed = (i == 0) | (be_ref[i] != prev)

    @pl.when(changed)
    def _():
        wgu_bf[...] = wgu_ref[0].astype(BF16)
        wdn_bf[...] = wdn_ref[0].astype(BF16)

    de = wdn_bf.shape[0]
    subs = [slice(r, r + MOE_SUB) for r in range(0, MOE_BLOCK, MOE_SUB)]
    gus = [_dot(_unpack_halves(xs_ref[sl, :]).astype(BF16), wgu_bf[...]) for sl in subs]
    for sl, gu in zip(subs, gus):
        gate = gu[:, :de]
        up = gu[:, de:]
        act = gate / (1.0 + jnp.exp(-gate)) * up
        ys_ref[sl, :] = _pack_halves(_dot(act.astype(BF16), wdn_bf[...]))


def _moe_experts(block_e, xs, w_gu, w_dn):
    rows, dw = xs.shape
    d = 2 * dw
    nb = rows // MOE_BLOCK
    de = w_dn.shape[1]
    grid_spec = pltpu.PrefetchScalarGridSpec(
        num_scalar_prefetch=1,
        grid=(nb,),
        in_specs=[
            pl.BlockSpec((MOE_BLOCK, dw), lambda i, be: (i, 0)),
            pl.BlockSpec((1, d, 2 * de), lambda i, be: (be[i], 0, 0)),
            pl.BlockSpec((1, de, d), lambda i, be: (be[i], 0, 0)),
        ],
        out_specs=pl.BlockSpec((MOE_BLOCK, dw), lambda i, be: (i, 0)),
        scratch_shapes=[pltpu.VMEM((d, 2 * de), BF16), pltpu.VMEM((de, d), BF16)],
    )
    return pl.pallas_call(
        _moe_kernel,
        grid_spec=grid_spec,
        out_shape=jax.ShapeDtypeStruct((rows, dw), jnp.uint32),
        compiler_params=_params(("arbitrary",)),
        name="moe_experts",
    )(block_e, xs, w_gu, w_dn)


ROUTE_TILE = 512


def _route_kernel(lg_ref, rec_ref, cnt_ref, carry):
    i = pl.program_id(0)

    @pl.when(i == 0)
    def _():
        carry[...] = jnp.zeros(carry.shape, F32)

    lg = lg_ref[...]
    tm = lg.shape[0]
    lane = lax.broadcasted_iota(jnp.int32, lg.shape, 1)

    def first_lane(mask):
        return jnp.min(jnp.where(mask, lane, ROUTER_PAD), axis=1, keepdims=True)

    is_g = lane < N_GROUPS
    g_log = jnp.where(is_g, lg, NEG_INF)
    g_max = jnp.max(g_log, axis=1, keepdims=True)
    g_sum = jnp.sum(jnp.where(is_g, jnp.exp(lg - g_max), 0.0), axis=1, keepdims=True)
    g_val = 1.0 / g_sum
    g_idx = first_lane(g_log == g_max)
    lo = N_GROUPS + EXP_PER_GROUP * g_idx
    sel = (lane >= lo) & (lane < lo + EXP_PER_GROUP)
    e_log = jnp.where(sel, lg, NEG_INF)
    e_max = jnp.max(e_log, axis=1, keepdims=True)
    e_exp = jnp.where(sel, jnp.exp(lg - e_max), 0.0)
    e_prob = e_exp / jnp.sum(e_exp, axis=1, keepdims=True)
    p1 = jnp.where(sel, e_prob, -1.0)
    v1 = jnp.max(p1, axis=1, keepdims=True)
    i1 = first_lane(p1 == v1)
    p2 = jnp.where(lane == i1, -1.0, p1)
    v2 = jnp.max(p2, axis=1, keepdims=True)
    i2 = first_lane(p2 == v2)
    norm = g_val / (v1 + v2)

    hot1 = lane == i1
    hot2 = lane == i2
    hot = jnp.where(hot1 | hot2, 1.0, 0.0)
    rows = lax.broadcasted_iota(jnp.int32, (tm, tm), 0)
    cols = lax.broadcasted_iota(jnp.int32, (tm, tm), 1)
    before = jnp.where(cols < rows, 1.0, 0.0).astype(BF16)
    prefix = _dot(before, hot.astype(BF16)) + carry[0:1, :]
    r1 = jnp.sum(jnp.where(hot1, prefix, 0.0), axis=1, keepdims=True)
    r2 = jnp.sum(jnp.where(hot2, prefix, 0.0), axis=1, keepdims=True)
    carry[0:1, :] = carry[0:1, :] + jnp.sum(hot, axis=0, keepdims=True)

    fields = [(i1 - N_GROUPS).astype(F32), (i2 - N_GROUPS).astype(F32), r1, r2, v1 * norm, v2 * norm]
    rec = jnp.zeros(lg.shape, F32)
    for k, f in enumerate(fields):
        rec = jnp.where(lane == k, f, rec)
    rec_ref[...] = rec

    @pl.when(i == pl.num_programs(0) - 1)
    def _():
        cnt_ref[...] = carry[...]


def _route(logits, n_tok):
    rec, cnt = pl.pallas_call(
        _route_kernel,
        grid=(n_tok // ROUTE_TILE,),
        in_specs=[pl.BlockSpec((ROUTE_TILE, ROUTER_PAD), lambda i: (i, 0))],
        out_specs=[pl.BlockSpec((ROUTE_TILE, ROUTER_PAD), lambda i: (i, 0)),
                   pl.BlockSpec((8, ROUTER_PAD), lambda i: (0, 0))],
        out_shape=[jax.ShapeDtypeStruct((n_tok, ROUTER_PAD), F32),
                   jax.ShapeDtypeStruct((8, ROUTER_PAD), F32)],
        scratch_shapes=[pltpu.VMEM((8, ROUTER_PAD), F32)],
        compiler_params=_params(("arbitrary",)),
        name="route",
    )(logits)
    experts = rec[:, 0:2].astype(jnp.int32)
    rank = rec[:, 2:4].astype(jnp.int32)
    wts = rec[:, 4:6]
    counts = cnt[0, N_GROUPS:N_GROUPS + N_EXPERTS].astype(jnp.int32)

    n_asg = n_tok * TOP_K
    pcounts = ((counts + MOE_BLOCK - 1) // MOE_BLOCK) * MOE_BLOCK
    pends = jnp.cumsum(pcounts)
    pstarts = pends - pcounts
    dest = pstarts[experts] + rank
    nb = -(-n_asg // MOE_BLOCK) + N_EXPERTS
    block_start = jnp.arange(nb, dtype=jnp.int32) * MOE_BLOCK
    block_e = jnp.minimum((pends[None, :] <= block_start[:, None]).sum(1), N_EXPERTS - 1).astype(jnp.int32)
    tok = jnp.broadcast_to(jnp.arange(n_tok, dtype=jnp.int32)[:, None], (n_tok, TOP_K))
    filler = jnp.arange(nb * MOE_BLOCK, dtype=jnp.int32) % n_tok
    row_tok = filler.at[dest.reshape(-1)].set(tok.reshape(-1), unique_indices=True)
    return wts, dest, row_tok, block_e


def _post_moe_kernel(x_ref, ya_ref, yb_ref, w_ref, mod_ref, lng_ref, lnb_ref, o_ref, *, alpha):
    w = w_ref[...]
    y = w[:, 0:1] * _unpack_halves(ya_ref[...]) + w[:, 1:2] * _unpack_halves(yb_ref[...])
    g2 = mod_ref[0, 5:6, :]
    o_ref[...] = _layer_norm(alpha * x_ref[...] + g2 * y, lng_ref[...], lnb_ref[...])


def _post_moe(x1, ya, yb, wts, mod, gid, lng, lnb, alpha, tm):
    rows, d = x1.shape
    row = pl.BlockSpec((tm, d), lambda i: (i, 0))
    packed = pl.BlockSpec((tm, d // 2), lambda i: (i, 0))
    vec = pl.BlockSpec((1, d), lambda i: (0, 0))
    return pl.pallas_call(
        functools.partial(_post_moe_kernel, alpha=alpha),
        grid=(rows // tm,),
        in_specs=[row, packed, packed,
                  pl.BlockSpec((tm, TOP_K), lambda i: (i, 0)),
                  pl.BlockSpec((1, 6, d), lambda i: (gid(i), 0, 0)),
                  vec, vec],
        out_specs=row,
        out_shape=jax.ShapeDtypeStruct((rows, d), F32),
        compiler_params=_params(("arbitrary",)),
        name="post_moe",
    )(x1, ya, yb, wts, mod, lng, lnb)


CD_COLS = dict(cq=(0, 384), ckv=(384, 640), dq=(640, 1152), dk=(1152, 1664), dv=(1664, 2176),
               kr=(2176, 2304))
C_HEAD_PAD = LANES


def _rms(t, g, eps=1e-6):
    return t * lax.rsqrt(jnp.mean(t * t, -1, keepdims=True) + eps) * g


def _proj_cd_kernel(x_ref, mod_ref, w_ref, qn_ref, wuq_ref, kvn_ref, wukv_ref,
                    cos64_ref, sin64_ref, cos32_ref, sin32_ref,
                    qc_ref, kc_ref, vct_ref, dq_ref, dk_ref, dvt_ref):
    h = _modulate(x_ref, mod_ref, 0).astype(BF16)
    r = _dot(h, w_ref[...])

    def cols(name):
        lo, hi = CD_COLS[name]
        return r[:, lo:hi]

    q = _dot(_rms(cols('cq'), qn_ref[...]).astype(BF16), wuq_ref[...])
    kv = _dot(_rms(cols('ckv'), kvn_ref[...]).astype(BF16), wukv_ref[...])
    c32, s32 = cos32_ref[...], sin32_ref[...]
    c64, s64 = cos64_ref[...], sin64_ref[...]
    q = _rope_wide(q, c32, s32, C_ROPE // 4)
    kr = _rope_tile(cols('kr'), c32, s32, C_ROPE // 4)
    dq = _rope_wide(cols('dq'), c64, s64, D_QK // 4)
    dk = _rope_wide(cols('dk'), c64, s64, D_QK // 4)
    kw = C_HEADS * C_HEAD_PAD
    qc_ref[0] = (q * ((C_NOPE + C_ROPE) ** -0.5 * LOG2E)).astype(BF16)
    kc_ref[0] = (kv[:, :kw] + jnp.concatenate([kr] * C_HEADS, axis=1)).astype(BF16)
    dq_ref[0] = (dq * (D_QK ** -0.5 * LOG2E)).astype(BF16)
    dk_ref[0] = dk.astype(BF16)
    _store_vt(vct_ref, kv[:, kw:], C_HEADS, C_V)
    _store_vt(dvt_ref, cols('dv'), D_HEADS, D_V)


def _store_vt(vt_ref, v, heads, dv):
    tm = v.shape[0]
    v_t = v.T.astype(BF16)
    row = lax.broadcasted_iota(jnp.int32, (ONES_ROWS, tm), 0)
    extra = jnp.where(row == 0, 1.0, 0.0).astype(BF16)
    for h in range(heads):
        vt_ref[0, h, 0, 0:dv, :] = v_t[h * dv:(h + 1) * dv]
        vt_ref[0, h, 0, dv:dv + ONES_ROWS, :] = extra


def _proj_cd(xall, mod, w_bf, qn, wuq, kvn, wukv, tabs, batch, seq, ctx_len, tm, tk):
    d = xall.shape[1]
    lat_t, ctx_t = seq // tm, ctx_len // tm
    n_keys = seq + ctx_len
    per_chunk = tk // tm
    n_chunks = n_keys // tk
    widths = (C_HEADS * C_HEAD_PAD, C_HEADS * C_HEAD_PAD, 2 * D_HEADS * D_QK, 2 * D_HEADS * D_QK)

    def src(b, t):
        return (jnp.where(t < lat_t, b * lat_t + t, batch * lat_t + b * ctx_t + t - lat_t), 0)

    def full(a):
        return pl.BlockSpec(a.shape, lambda b, t: (0,) * a.ndim)

    def row_spec(w):
        return pl.BlockSpec((1, tm, w), lambda b, t: (b, t, 0))

    def row_shape(w):
        return jax.ShapeDtypeStruct((batch, n_keys, w), BF16)

    def vt_spec(heads, dv):
        return pl.BlockSpec((1, heads, 1, dv + ONES_ROWS, tm),
                            lambda b, t: (b, 0, t // per_chunk, 0, t % per_chunk))

    def vt_shape(heads, dv):
        return jax.ShapeDtypeStruct((batch, heads, n_chunks, dv + ONES_ROWS, tk), BF16)

    tab = pl.BlockSpec((tm, LANES), lambda b, t: (t, 0))
    return pl.pallas_call(
        _proj_cd_kernel,
        grid=(batch, lat_t + ctx_t),
        in_specs=[pl.BlockSpec((tm, d), src),
                  pl.BlockSpec((1, 6, d), lambda b, t: (jnp.where(t < lat_t, b, batch), 0, 0)),
                  full(w_bf), full(qn), full(wuq), full(kvn), full(wukv), tab, tab, tab, tab],
        out_specs=[row_spec(widths[0]), row_spec(widths[1]), vt_spec(C_HEADS, C_V),
                   row_spec(widths[2]), row_spec(widths[3]), vt_spec(D_HEADS, D_V)],
        out_shape=[row_shape(widths[0]), row_shape(widths[1]), vt_shape(C_HEADS, C_V),
                   row_shape(widths[2]), row_shape(widths[3]), vt_shape(D_HEADS, D_V)],
        compiler_params=_params(("arbitrary", "arbitrary")),
        name="proj_cd",
    )(xall, mod, w_bf, qn, wuq, kvn, wukv, *tabs)


DENSE_SHIFT = 80.0
DENSE_L_MIN = 2.0 ** -60
DENSE_L_MAX = 2.0 ** 120
DENSE_TQ = 1024
DENSE_TK_MAX = 1280
DENSE_UNROLL_MAX = 13
MXU_DEPTH = 256
PROJ_CD_TILE = 256
ONES_ROWS = 16


def _sq_norm_row(x):
    ones = jnp.ones((8, x.shape[1]), BF16)
    return _dot_nt(ones, (x * x).astype(BF16))[0:1] * 1.02


def _key_norm_max(kchunk, n_chunks, masks):
    def body(j, mx):
        kf = kchunk(j).astype(F32)
        ksq = kf * kf
        out = []
        for msk, cur in zip(masks, mx):
            part = ksq if msk is None else jnp.where(msk, ksq, 0.0)
            rn = jnp.sum(part, axis=1, keepdims=True)
            out.append(jnp.maximum(cur, jnp.max(rn, axis=0, keepdims=True)))
        return tuple(out)

    mx = lax.fori_loop(0, n_chunks, body, tuple(jnp.zeros((1, 1), F32) for _ in masks))
    return [jnp.sqrt(v) * 1.01 for v in mx]


def _safe_online(q, kchunk, vt_ref, n_chunks, m_scr, l_scr, acc_scr):
    m_scr[...] = jnp.full(m_scr.shape, NEG_INF, F32)
    l_scr[...] = jnp.zeros(l_scr.shape, F32)
    acc_scr[...] = jnp.zeros(acc_scr.shape, F32)

    def body(j, carry):
        s = _dot_nt(kchunk(j), q)
        m_prev = m_scr[...]
        m_new = jnp.maximum(m_prev, s.max(0, keepdims=True))
        alpha = jnp.exp2(m_prev - m_new)
        p = jnp.exp2(s - m_new)
        l_scr[...] = alpha * l_scr[...] + p.sum(0, keepdims=True)
        acc_scr[...] = alpha * acc_scr[...] + _dot(vt_ref[0, 0, j], p.astype(BF16))
        m_scr[...] = m_new
        return carry

    lax.fori_loop(0, n_chunks, body, 0)


def _denominators_ok(*ls):
    ok = None
    for l in ls:
        cur = (l > DENSE_L_MIN) & (l < DENSE_L_MAX)
        ok = cur if ok is None else ok & cur
    return jnp.max(jnp.where(ok, 0.0, 1.0)) == 0.0


def _mla_kernel(q_ref, k_ref, vt_ref, o_ref, kmax_scr, acc_scr, m_scr, l_scr,
                *, n_chunks, tk, unroll, depth):
    def kchunk(j):
        return k_ref[0, pl.ds(pl.multiple_of(j * tk, tk), tk), :]

    @pl.when(pl.program_id(2) == 0)
    def _():
        (kmax,) = _key_norm_max(kchunk, n_chunks, [None])
        kmax_scr[...] = jnp.broadcast_to(kmax, kmax_scr.shape)

    q = q_ref[0]
    m_row = jnp.sqrt(_sq_norm_row(q.astype(F32))) * kmax_scr[0:1, 0:1] - DENSE_SHIFT
    acc_scr[...] = jnp.zeros(acc_scr.shape, F32)

    def body(it, carry):
        j0 = it * unroll
        pend = [_dot_nt(kchunk(j0 + u), q) for u in range(min(depth, unroll))]
        acc = acc_scr[...]
        for u in range(unroll):
            s = pend.pop(0)
            if u + depth < unroll:
                pend.append(_dot_nt(kchunk(j0 + u + depth), q))
            acc = acc + _dot(vt_ref[0, 0, j0 + u], jnp.exp2(s - m_row).astype(BF16))
        acc_scr[...] = acc
        return carry

    lax.fori_loop(0, n_chunks // unroll, body, 0)
    l = acc_scr[C_V:C_V + 1, :]
    o_ref[0] = (acc_scr[0:C_V, :] / l).astype(o_ref.dtype)

    @pl.when(jnp.logical_not(_denominators_ok(l)))
    def _():
        _safe_online(q, kchunk, vt_ref, n_chunks, m_scr, l_scr, acc_scr)
        o_ref[0] = (acc_scr[0:C_V, :] / l_scr[...]).astype(o_ref.dtype)


def _mla_attn(qc, kc, vt, batch, seq, tq, tk, unroll, depth):
    n_keys = kc.shape[1]
    nq, n_chunks = seq // tq, n_keys // tk
    rows = C_V + ONES_ROWS
    return pl.pallas_call(
        functools.partial(_mla_kernel, n_chunks=n_chunks, tk=tk, unroll=unroll, depth=depth),
        grid=(batch, C_HEADS, nq),
        in_specs=[
            pl.BlockSpec((1, tq, C_HEAD_PAD), lambda b, h, i: (b, i, h)),
            pl.BlockSpec((1, n_keys, C_HEAD_PAD), lambda b, h, i: (b, 0, h)),
            pl.BlockSpec((1, 1, n_chunks, rows, tk), lambda b, h, i: (b, h, 0, 0, 0)),
        ],
        out_specs=pl.BlockSpec((1, C_V, tq), lambda b, h, i: (b, h, i)),
        out_shape=jax.ShapeDtypeStruct((batch, C_HEADS * C_V, seq), BF16),
        scratch_shapes=[pltpu.VMEM((8, LANES), F32), pltpu.VMEM((rows, tq), F32),
                        pltpu.VMEM((1, tq), F32), pltpu.VMEM((1, tq), F32)],
        compiler_params=_params(("arbitrary", "arbitrary", "arbitrary")),
        name="mla_attn",
    )(qc, kc, vt)


def _diff_kernel(lam_ref, q_ref, k_ref, vt_ref, subln_ref, o_ref, kmax_scr, acc1_scr, acc2_scr,
                 m_scr, l_scr, *, n_chunks, tk, unroll, depth, out_scale):
    def kchunk(j):
        return k_ref[0, pl.ds(pl.multiple_of(j * tk, tk), tk), :]

    @pl.when(pl.program_id(2) == 0)
    def _():
        klane = lax.broadcasted_iota(jnp.int32, (tk, 2 * D_QK), 1)
        k1, k2 = _key_norm_max(kchunk, n_chunks, [klane < D_QK, klane >= D_QK])
        kmax_scr[0:1, :] = jnp.broadcast_to(k1, (1, LANES))
        kmax_scr[1:2, :] = jnp.broadcast_to(k2, (1, LANES))

    q = q_ref[0]
    qlane = lax.broadcasted_iota(jnp.int32, q.shape, 1)
    zero = jnp.zeros_like(q)
    q1 = jnp.where(qlane < D_QK, q, zero)
    q2 = jnp.where(qlane >= D_QK, q, zero)
    m1 = jnp.sqrt(_sq_norm_row(q1.astype(F32))) * kmax_scr[0:1, 0:1] - DENSE_SHIFT
    m2 = jnp.sqrt(_sq_norm_row(q2.astype(F32))) * kmax_scr[1:2, 0:1] - DENSE_SHIFT
    acc1_scr[...] = jnp.zeros(acc1_scr.shape, F32)
    acc2_scr[...] = jnp.zeros(acc2_scr.shape, F32)

    def scores(j):
        k = kchunk(j)
        return _dot_nt(k, q1), _dot_nt(k, q2)

    def body(it, carry):
        j0 = it * unroll
        pend = [scores(j0 + u) for u in range(min(depth, unroll))]
        a1 = acc1_scr[...]
        a2 = acc2_scr[...]
        for u in range(unroll):
            s1, s2 = pend.pop(0)
            if u + depth < unroll:
                pend.append(scores(j0 + u + depth))
            vt = vt_ref[0, 0, j0 + u]
            a1 = a1 + _dot(vt, jnp.exp2(s1 - m1).astype(BF16))
            a2 = a2 + _dot(vt, jnp.exp2(s2 - m2).astype(BF16))
        acc1_scr[...] = a1
        acc2_scr[...] = a2
        return carry

    lax.fori_loop(0, n_chunks // unroll, body, 0)

    def finish(o1, o2):
        o = o1 - lam_ref[0] * o2
        o = o * lax.rsqrt(jnp.mean(o * o, 0, keepdims=True) + 1e-6) * subln_ref[...] * out_scale
        o_ref[0] = o.astype(o_ref.dtype)

    l1 = acc1_scr[D_V:D_V + 1, :]
    l2 = acc2_scr[D_V:D_V + 1, :]
    finish(acc1_scr[0:D_V, :] / l1, acc2_scr[0:D_V, :] / l2)

    @pl.when(jnp.logical_not(_denominators_ok(l1, l2)))
    def _():
        _safe_online(q1, kchunk, vt_ref, n_chunks, m_scr, l_scr, acc1_scr)
        o1 = acc1_scr[0:D_V, :] / l_scr[...]
        _safe_online(q2, kchunk, vt_ref, n_chunks, m_scr, l_scr, acc2_scr)
        finish(o1, acc2_scr[0:D_V, :] / l_scr[...])


def _diff_attn(lam, dq, dk, vt, subln_col, out_scale, batch, seq, tq, tk, unroll, depth):
    n_keys = dk.shape[1]
    nq, n_chunks = seq // tq, n_keys // tk
    rows = D_V + ONES_ROWS
    pair = 2 * D_QK
    return pl.pallas_call(
        functools.partial(_diff_kernel, n_chunks=n_chunks, tk=tk, unroll=unroll, depth=depth,
                          out_scale=out_scale),
        grid=(batch, D_HEADS, nq),
        in_specs=[
            pl.BlockSpec(memory_space=pltpu.SMEM),
            pl.BlockSpec((1, tq, pair), lambda b, h, i: (b, i, h)),
            pl.BlockSpec((1, n_keys, pair), lambda b, h, i: (b, 0, h)),
            pl.BlockSpec((1, 1, n_chunks, rows, tk), lambda b, h, i: (b, h, 0, 0, 0)),
            pl.BlockSpec((D_V, 1), lambda b, h, i: (0, 0)),
        ],
        out_specs=pl.BlockSpec((1, D_V, tq), lambda b, h, i: (b, h, i)),
        out_shape=jax.ShapeDtypeStruct((batch, D_HEADS * D_V, seq), BF16),
        scratch_shapes=[pltpu.VMEM((8, LANES), F32), pltpu.VMEM((rows, tq), F32),
                        pltpu.VMEM((rows, tq), F32), pltpu.VMEM((1, tq), F32), pltpu.VMEM((1, tq), F32)],
        compiler_params=_params(("arbitrary", "arbitrary", "arbitrary")),
        name="diff_attn",
    )(lam, dq, dk, vt, subln_col)


def _rope_tables(n_tok, dim, lane_lo):
    t = jnp.arange(n_tok)
    pos_r = (t // GRID_W).astype(F32)
    pos_c = (t % GRID_W).astype(F32)
    quarter = dim // 4
    inv = ROPE_BASE ** (-jnp.arange(quarter, dtype=F32) / quarter)
    ang_r = pos_r[:, None] * inv
    ang_c = pos_c[:, None] * inv
    ang = jnp.concatenate([ang_r, ang_r, ang_c, ang_c], -1)
    sign = jnp.tile(jnp.concatenate([-jnp.ones(quarter), jnp.ones(quarter)]), 2).astype(F32)
    cos, sin = jnp.cos(ang), jnp.sin(ang) * sign
    reps = (LANES - lane_lo) // dim
    cos = jnp.concatenate([jnp.ones((n_tok, lane_lo), F32)] + [cos] * reps, axis=1)
    sin = jnp.concatenate([jnp.zeros((n_tok, lane_lo), F32)] + [sin] * reps, axis=1)
    return cos, sin


def _cd_weights(w_in, w_uq, w_ukv):
    d = w_in.shape[0]
    s0 = C_Q_RANK
    s1 = s0 + C_KV_RANK
    s2 = s1 + C_ROPE
    s3 = s2 + 2 * D_HEADS * D_QK
    s4 = s3 + 2 * D_HEADS * D_QK
    kr = jnp.zeros((d, LANES), F32).at[:, C_NOPE:C_NOPE + C_ROPE].set(w_in[:, s1:s2])
    w_in_p = jnp.concatenate([w_in[:, :s1], w_in[:, s2:s3], w_in[:, s3:s4], w_in[:, s4:], kr], axis=1)
    dqk = C_NOPE + C_ROPE
    wq = w_uq.reshape(C_Q_RANK, C_HEADS, dqk)
    wq = jnp.pad(wq, ((0, 0), (0, 0), (0, C_HEAD_PAD - dqk))).reshape(C_Q_RANK, C_HEADS * C_HEAD_PAD)
    wkv = w_ukv.reshape(C_KV_RANK, C_HEADS, C_NOPE + C_V)
    wk = jnp.pad(wkv[:, :, :C_NOPE], ((0, 0), (0, 0), (0, C_HEAD_PAD - C_NOPE)))
    wk = wk.reshape(C_KV_RANK, C_HEADS * C_HEAD_PAD)
    wv = wkv[:, :, C_NOPE:].reshape(C_KV_RANK, C_HEADS * C_V)
    return w_in_p.astype(BF16), wq.astype(BF16), jnp.concatenate([wk, wv], axis=1).astype(BF16)


def _moe_layer(h2, logits, x1, w_gu, w_dn, mod, gid, lng, lnb, alpha, tm):
    n_tok = h2.shape[0]
    wts, dest, row_tok, block_e = _route(logits, n_tok)
    xs = jnp.take(h2, row_tok, axis=0, mode='clip')
    ys = _moe_experts(block_e, xs, w_gu, w_dn)
    ya = jnp.take(ys, dest[:, 0], axis=0, mode='clip')
    yb = jnp.take(ys, dest[:, 1], axis=0, mode='clip')
    return _post_moe(x1, ya, yb, wts.astype(F32), mod, gid, lng, lnb, alpha, tm)


def kernel(x, c, ctx, c_ctx, w_ada, b_ada, ln_g, ln_b, ab_w_in, a_sink, b_rpb, ab_w_out, cd_w_in, c_q_norm, c_w_uq, c_kv_norm, c_w_ukv, d_lambda, d_subln, cd_w_out, w_group, b_group, w_exp_router, b_exp_router, w_gate_up, w_down):
    batch, seq, d = x.shape
    ctx_len = ctx.shape[1]
    depth = w_ada.shape[0]
    alpha = (2 * depth) ** 0.25
    n_lat = batch * seq
    n_ctx = batch * ctx_len
    tm = 512
    tq = DENSE_TQ
    assert depth == 2 and seq % NA_Q == 0 and seq % tq == 0 and n_ctx % tm == 0
    assert seq % PROJ_CD_TILE == 0 and ctx_len % PROJ_CD_TILE == 0
    assert seq % (WIN_GROUP * A_BLOCK) == 0 and seq // NA_Q >= 2
    key_tiles = (seq + ctx_len) // MXU_DEPTH
    tk = MXU_DEPTH * max(u for u in range(1, DENSE_TK_MAX // MXU_DEPTH + 1) if key_tiles % u == 0)
    n_chunks = (seq + ctx_len) // tk

    def unroll_for(cap):
        return max(u for u in range(1, cap + 1) if n_chunks % u == 0)
    lat_tiles = seq // tm

    def gid_lat(i):
        return i // lat_tiles

    def gid_ctx(i):
        return batch

    def gid_all(i):
        return jnp.minimum(i // lat_tiles, batch)

    c_all = jnp.zeros((8, d), F32).at[:batch].set(c).at[batch].set(c_ctx)
    mod = _ada(c_all, w_ada, b_ada).reshape(depth, 8, 6, d)

    cos64, sin64 = _rope_tables(seq, HEAD_DIM, 0)
    cos32, sin32 = _rope_tables(seq, C_ROPE, C_NOPE)

    def router_weights(l):
        wr = jnp.zeros((d, ROUTER_PAD), F32)
        wr = wr.at[:, :N_GROUPS].set(w_group[l]).at[:, N_GROUPS:N_GROUPS + N_EXPERTS].set(w_exp_router[l])
        br = jnp.zeros((1, ROUTER_PAD), F32)
        br = br.at[0, :N_GROUPS].set(b_group[l]).at[0, N_GROUPS:N_GROUPS + N_EXPERTS].set(b_exp_router[l])
        hi = wr.astype(BF16)
        lo = (wr - hi.astype(F32)).astype(BF16)
        return hi, lo, br

    x2d = x.reshape(n_lat, d)
    ctx2d = ctx.reshape(n_ctx, d)

    l = 0
    w_in_bf = ab_w_in[0].astype(BF16)
    aq, ak, av, bq, bk, bv = _proj_ab(x2d, 0, n_lat, mod[l], gid_lat, w_in_bf, cos64, sin64,
                                      lat_tiles, True, tm)
    aqx, akx, avx, bqx, bkx, bvx = _proj_ab(ctx2d, 0, n_ctx, mod[l], gid_ctx, w_in_bf, cos64, sin64,
                                            1, False, tm)
    sink = a_sink[0].astype(F32)
    oa = _window_attn(sink, aq, ak, av, akx, avx, batch, seq, ctx_len)
    tiles, tile_idx = _na_bias_tiles(b_rpb[0], seq // GRID_W)
    ob = _na_attn(bq, bk, bv, bkx, bvx, tiles, tile_idx, batch, seq, ctx_len)
    oax, obx = _ctx_ab_attn(sink, aqx, akx, avx, bqx, bkx, bvx, batch, ctx_len)

    n_all = n_lat + n_ctx
    w_out_bf = ab_w_out[0].astype(BF16)
    lng, lnb = ln_g[l, 0][None], ln_b[l, 0][None]
    wr_hi, wr_lo, br = router_weights(l)
    x1, h2, logits = _post_attn((oa, ob, x2d), (oax, obx, ctx2d), w_out_bf, mod[l], gid_all,
                                lng, lnb, wr_hi, wr_lo, br, n_lat, alpha, tm)
    xall = _moe_layer(h2, logits, x1, w_gate_up[l], w_down[l], mod[l], gid_all,
                      ln_g[l, 1][None], ln_b[l, 1][None], alpha, tm)

    l = 1
    lam_init = 0.8 - 0.6 * math.exp(-0.3 * l)
    lp = d_lambda[0].astype(F32)
    lam = (jnp.exp(jnp.sum(lp[0] * lp[1])) - jnp.exp(jnp.sum(lp[2] * lp[3])) + lam_init).reshape(1)
    w_in_p, wuq_p, wukv_p = _cd_weights(cd_w_in[0], c_w_uq[0], c_w_ukv[0])
    qn, kvn = c_q_norm[0][None].astype(F32), c_kv_norm[0][None].astype(F32)
    def with_ctx_identity(cos, sin):
        return (jnp.concatenate([cos, jnp.ones((ctx_len, LANES), F32)]),
                jnp.concatenate([sin, jnp.zeros((ctx_len, LANES), F32)]))

    tabs = with_ctx_identity(cos64, sin64) + with_ctx_identity(cos32, sin32)
    qc, kc, vct, dq, dk, dvt = _proj_cd(xall, mod[l], w_in_p, qn, wuq_p, kvn, wukv_p, tabs,
                                        batch, seq, ctx_len, PROJ_CD_TILE, tk)
    unroll = unroll_for(DENSE_UNROLL_MAX)
    oc_t = _mla_attn(qc, kc, vct, batch, seq, tq, tk, unroll, 1)
    od_t = _diff_attn(lam, dq, dk, dvt, d_subln[0].astype(F32).reshape(D_V, 1), 1.0 - lam_init,
                      batch, seq, tq, tk, unroll, 1)
    wr_hi, wr_lo, br = router_weights(l)
    x1, h2, logits = _post_attn((oc_t, od_t, xall), None, cd_w_out[0].astype(BF16), mod[l], gid_lat,
                                ln_g[l, 0][None], ln_b[l, 0][None], wr_hi, wr_lo, br,
                                n_lat, alpha, tm)
    out = _moe_layer(h2, logits, x1, w_gate_up[l], w_down[l], mod[l], gid_lat,
                     ln_g[l, 1][None], ln_b[l, 1][None], alpha, tm)
    return out.reshape(batch, seq, d)
```

```python
import functools
import math

import jax
import jax.numpy as jnp
from jax import lax
from jax.experimental import pallas as pl
from jax.experimental.pallas import tpu as pltpu

F32 = jnp.float32
BF16 = jnp.bfloat16

GRID_W = 64
HEAD_DIM = 64
ROPE_BASE = 10000.0
NEG_INF = -1e30
LOG2E = 1.4426950408889634

A_HEADS = 8
A_KV_HEADS = 2
A_WINDOW = 128
A_BLOCK = 128
B_HEADS = 8
NA_ROWS = 8
NA_COLS = 16
C_HEADS = 8
C_Q_RANK = 384
C_KV_RANK = 256
C_NOPE = 64
C_ROPE = 32
C_V = 64
D_HEADS = 4
D_QK = 64
D_V = 128
N_GROUPS = 4
EXP_PER_GROUP = 8
N_EXPERTS = N_GROUPS * EXP_PER_GROUP
TOP_K = 2
MOE_BLOCK = 512
MOE_SUB = 256
ROUTER_PAD = 128

LANES = 128
VMEM_LIMIT = 56 * 1024 * 1024


def _params(sem):
    return pltpu.CompilerParams(dimension_semantics=sem, vmem_limit_bytes=VMEM_LIMIT)


def _dot(a, b):
    return jnp.dot(a, b, preferred_element_type=F32)


def _dot_nt(a, b):
    return lax.dot_general(a, b, (((1,), (1,)), ((), ())), preferred_element_type=F32)


def _split_bf16(x):
    hi = x.astype(BF16)
    lo = (x - hi.astype(F32)).astype(BF16)
    return hi, lo


HI_HALF = 0xFFFF0000


def _pack_halves(x):
    w = x.shape[1] // 2
    lo = lax.bitcast_convert_type(x[:, :w].astype(BF16).astype(F32), jnp.uint32)
    hi = lax.bitcast_convert_type(x[:, w:].astype(BF16).astype(F32), jnp.uint32)
    return (lo >> 16) | (hi & jnp.uint32(HI_HALF))


def _unpack_halves(u):
    lo = lax.bitcast_convert_type(u << 16, F32)
    hi = lax.bitcast_convert_type(u & jnp.uint32(HI_HALF), F32)
    return jnp.concatenate([lo, hi], axis=1)


def _ada_kernel(c_ref, w_ref, b_ref, o_ref):
    c = c_ref[...]
    a = c / (1.0 + jnp.exp(-c))
    a_hi, a_lo = _split_bf16(a)
    w_hi, w_lo = _split_bf16(w_ref[0])
    acc = _dot(a_hi, w_hi) + _dot(a_lo, w_hi) + _dot(a_hi, w_lo)
    o_ref[0] = acc + b_ref[0]


def _ada(c_all, w_ada, b_ada):
    depth, d, n = w_ada.shape
    tn = 1536
    return pl.pallas_call(
        _ada_kernel,
        grid=(depth, n // tn),
        in_specs=[
            pl.BlockSpec((8, d), lambda l, j: (0, 0)),
            pl.BlockSpec((1, d, tn), lambda l, j: (l, 0, j)),
            pl.BlockSpec((1, 1, tn), lambda l, j: (l, 0, j)),
        ],
        out_specs=pl.BlockSpec((1, 8, tn), lambda l, j: (l, 0, j)),
        out_shape=jax.ShapeDtypeStruct((depth, 8, n), F32),
        compiler_params=_params(("arbitrary", "arbitrary")),
        name="ada",
    )(c_all, w_ada, b_ada.reshape(depth, 1, n))


def _rope_tile(t, cos, sin_signed, q):
    lane = lax.broadcasted_iota(jnp.int32, t.shape, 1)
    first = (lane & (2 * q - 1)) < q
    up = pltpu.roll(t, LANES - q, 1)
    dn = pltpu.roll(t, q, 1)
    return t * cos + jnp.where(first, up, dn) * sin_signed


def _rope_wide(t, cos, sin_signed, q):
    n = t.shape[1] // LANES
    return jnp.concatenate(
        [_rope_tile(t[:, i * LANES:(i + 1) * LANES], cos, sin_signed, q) for i in range(n)], axis=1)


def _modulate(x_ref, mod_ref, which):
    sh = mod_ref[0, 3 * which:3 * which + 1, :]
    sc = mod_ref[0, 3 * which + 1:3 * which + 2, :]
    return x_ref[...] * (1.0 + sc) + sh


def _proj_ab_kernel(x_ref, mod_ref, w_ref, cos_ref, sin_ref,
                    aq_ref, ak_ref, av_ref, bq_ref, bk_ref, bv_ref, *, rope):
    h = _modulate(x_ref, mod_ref, 0).astype(BF16)
    r = _dot(h, w_ref[...])
    aq = r[:, 0:512]
    ak = r[:, 512:640]
    if rope:
        cos = cos_ref[...]
        sin = sin_ref[...]
        aq = _rope_wide(aq, cos, sin, 16)
        ak = _rope_wide(ak, cos, sin, 16)
    scale = HEAD_DIM ** -0.5
    aq_ref[...] = (aq * scale).astype(BF16)
    ak_ref[...] = ak.astype(BF16)
    av_ref[...] = r[:, 640:768].astype(BF16)
    bq_ref[...] = (r[:, 768:1280] * scale).astype(BF16)
    bk_ref[...] = r[:, 1280:1792].astype(BF16)
    bv_ref[...] = r[:, 1792:2304].astype(BF16)


def _proj_ab(x2d, row0, nrows, mod, gid, w_bf, cos, sin, pos_blocks, rope, tm):
    d = x2d.shape[1]
    n_in = w_bf.shape[1]
    b0 = row0 // tm
    widths = (512, 128, 128, 512, 512, 512)
    return pl.pallas_call(
        functools.partial(_proj_ab_kernel, rope=rope),
        grid=(nrows // tm,),
        in_specs=[
            pl.BlockSpec((tm, d), lambda i: (b0 + i, 0)),
            pl.BlockSpec((1, 6, d), lambda i: (gid(i), 0, 0)),
            pl.BlockSpec((d, n_in), lambda i: (0, 0)),
            pl.BlockSpec((tm, LANES), lambda i: (i % pos_blocks, 0)),
            pl.BlockSpec((tm, LANES), lambda i: (i % pos_blocks, 0)),
        ],
        out_specs=[pl.BlockSpec((tm, w), lambda i: (i, 0)) for w in widths],
        out_shape=[jax.ShapeDtypeStruct((nrows, w), BF16) for w in widths],
        compiler_params=_params(("arbitrary",)),
        name="proj_ab",
    )(x2d, mod, w_bf, cos, sin)


def _softmax_pv(parts, sink_col):
    m = parts[0][0].max(-1, keepdims=True)
    for s, _ in parts[1:]:
        m = jnp.maximum(m, s.max(-1, keepdims=True))
    if sink_col is not None:
        m = jnp.maximum(m, sink_col)
    denom = None if sink_col is None else jnp.exp(sink_col - m)
    o = None
    for s, v in parts:
        e = jnp.exp(s - m)
        d = e.sum(-1, keepdims=True)
        denom = d if denom is None else denom + d
        pv = _dot(e.astype(BF16), v)
        o = pv if o is None else o + pv
    return o / denom


def _win_kernel(sink_ref, q_ref, kp_ref, kc_ref, kn_ref, vp_ref, vc_ref, vn_ref,
                kx_ref, vx_ref, o_ref, *, seq):
    step = pl.program_id(1)
    blk = A_BLOCK
    g_sz = A_HEADS // A_KV_HEADS
    qi = lax.broadcasted_iota(jnp.int32, (blk, 3 * blk), 0)
    kj = lax.broadcasted_iota(jnp.int32, (blk, 3 * blk), 1)
    in_window = jnp.abs(kj - blk - qi) <= A_WINDOW
    k_ext = jnp.concatenate([kp_ref[...], kc_ref[...], kn_ref[...]], axis=0)
    v_ext = jnp.concatenate([vp_ref[...], vc_ref[...], vn_ref[...]], axis=0)
    kx = kx_ref[...]
    vx = vx_ref[...]

    def scores(t, g):
        lo, hi = g * HEAD_DIM, (g + 1) * HEAD_DIM
        kpos = (step * WIN_GROUP + t) * blk + kj - blk
        valid = in_window & (kpos >= 0) & (kpos < seq)
        valid = jnp.concatenate([valid] * g_sz, axis=0)
        q = q_ref[t * blk:(t + 1) * blk, :]
        heads = [g * g_sz + i for i in range(g_sz)]
        qs = jnp.concatenate([q[:, h * HEAD_DIM:(h + 1) * HEAD_DIM] for h in heads], axis=0)
        s_loc = jnp.where(valid, _dot_nt(qs, k_ext[t * blk:(t + 3) * blk, lo:hi]), NEG_INF)
        return s_loc, _dot_nt(qs, kx[:, lo:hi])

    units = [(t, g) for t in range(WIN_GROUP) for g in range(A_KV_HEADS)]
    pending = scores(*units[0])
    outs = []
    for n, (t, g) in enumerate(units):
        s_loc, s_ctx = pending
        if n + 1 < len(units):
            pending = scores(*units[n + 1])
        lo, hi = g * HEAD_DIM, (g + 1) * HEAD_DIM
        sink = jnp.concatenate(
            [jnp.full((blk, 1), sink_ref[g * g_sz + i], F32) for i in range(g_sz)], axis=0)
        o = _softmax_pv([(s_ctx, vx[:, lo:hi]), (s_loc, v_ext[t * blk:(t + 3) * blk, lo:hi])], sink)
        outs.extend(o[i * blk:(i + 1) * blk] for i in range(g_sz))
        if g == A_KV_HEADS - 1:
            o_ref[t * blk:(t + 1) * blk, :] = jnp.concatenate(outs, axis=1).astype(o_ref.dtype)
            outs = []


WIN_GROUP = 4


def _window_attn(sink, aq, ak, av, akx, avx, batch, seq, ctx_len):
    nblk = seq // A_BLOCK
    nstep = nblk // WIN_GROUP
    kvw = A_KV_HEADS * HEAD_DIM
    qw = A_HEADS * HEAD_DIM
    rows = WIN_GROUP * A_BLOCK

    def edge(delta):
        return lambda b, n: (b * nblk + jnp.clip(n * WIN_GROUP + delta, 0, nblk - 1), 0)

    def own(b, n):
        return (b * nstep + n, 0)

    kv_specs = [pl.BlockSpec((A_BLOCK, kvw), edge(-1)), pl.BlockSpec((rows, kvw), own),
                pl.BlockSpec((A_BLOCK, kvw), edge(WIN_GROUP))]
    return pl.pallas_call(
        functools.partial(_win_kernel, seq=seq),
        grid=(batch, nstep),
        in_specs=[pl.BlockSpec(memory_space=pltpu.SMEM), pl.BlockSpec((rows, qw), own)]
        + kv_specs + kv_specs
        + [pl.BlockSpec((ctx_len, kvw), lambda b, n: (b, 0)),
           pl.BlockSpec((ctx_len, kvw), lambda b, n: (b, 0))],
        out_specs=pl.BlockSpec((rows, qw), own),
        out_shape=jax.ShapeDtypeStruct((batch * seq, qw), BF16),
        compiler_params=_params(("arbitrary", "arbitrary")),
        name="window_attn",
    )(sink, aq, ak, ak, ak, av, av, av, akx, avx)


NA_QROWS = 8
NA_Q = NA_QROWS * GRID_W
NA_KBLK = 4 * GRID_W
NA_K = 4 * NA_KBLK


NA_DR = 2 * NA_ROWS - 1


def _na_bias_tiles(rpb, rows):
    nj = rows // NA_QROWS
    kr = min(NA_ROWS, rows)
    col = jnp.arange(GRID_W)
    col_start = jnp.clip(col - NA_COLS // 2, 0, GRID_W - NA_COLS)
    cvalid = (col[None, :] >= col_start[:, None]) & (col[None, :] < col_start[:, None] + NA_COLS)
    dc = jnp.clip(col[None, :] - col[:, None], -(NA_COLS - 1), NA_COLS - 1) + (NA_COLS - 1)
    tiles = jnp.where(cvalid[None, None], rpb.astype(F32)[:, :, dc], NEG_INF)
    tiles = jnp.concatenate([tiles, jnp.full_like(tiles[:, :1], NEG_INF)], axis=1)
    idx = []
    for j in (0, min(1, nj - 1), nj - 1):
        r = j * NA_QROWS + jnp.arange(NA_QROWS)
        krow = j * NA_QROWS - NA_KBLK // GRID_W + jnp.arange(NA_K // GRID_W)
        start = jnp.clip(r - kr // 2, 0, rows - kr)
        rvalid = (krow[None, :] >= start[:, None]) & (krow[None, :] < start[:, None] + kr)
        dr = krow[None, :] - r[:, None] + (NA_ROWS - 1)
        idx.append(jnp.where(rvalid, dr, NA_DR))
    return tiles, jnp.stack(idx).reshape(-1).astype(jnp.int32)


def _na_kernel(idx_ref, q_ref, k0, k1, k2, k3, v0, v1, v2, v3, kx_ref, vx_ref, t_ref, o_ref, bias_scr,
               *, nj):
    j = pl.program_id(2)
    nkr = NA_K // GRID_W

    @pl.when((j == 0) | (j == 1) | (j == nj - 1))
    def _():
        variant = jnp.where(j == 0, 0, jnp.where(j == nj - 1, 2, 1))
        for hh in range(LANES // HEAD_DIM):
            for qr in range(NA_QROWS):
                for kr in range(nkr):
                    d = idx_ref[(variant * NA_QROWS + qr) * nkr + kr]
                    bias_scr[hh, qr * GRID_W:(qr + 1) * GRID_W, kr * GRID_W:(kr + 1) * GRID_W] = (
                        t_ref[hh, d])

    q = q_ref[...]
    k_all = jnp.concatenate([k0[...], k1[...], k2[...], k3[...]], axis=0)
    v_all = jnp.concatenate([v0[...], v1[...], v2[...], v3[...]], axis=0)
    kx = kx_ref[...]
    vx = vx_ref[...]
    qh_rows = NA_Q // 2
    k_span = 3 * NA_KBLK

    def scores(hh, half):
        lo, hi = hh * HEAD_DIM, (hh + 1) * HEAD_DIM
        q0, k0_ = half * qh_rows, half * NA_KBLK
        qh = q[q0:q0 + qh_rows, lo:hi]
        s_loc = _dot_nt(qh, k_all[k0_:k0_ + k_span, lo:hi]) + bias_scr[hh, q0:q0 + qh_rows, k0_:k0_ + k_span]
        return s_loc, _dot_nt(qh, kx[:, lo:hi])

    units = [(hh, half) for half in range(2) for hh in range(LANES // HEAD_DIM)]
    pending = scores(*units[0])
    for n, (hh, half) in enumerate(units):
        s_loc, s_ctx = pending
        if n + 1 < len(units):
            pending = scores(*units[n + 1])
        lo, hi = hh * HEAD_DIM, (hh + 1) * HEAD_DIM
        k0_ = half * NA_KBLK
        o = _softmax_pv([(s_ctx, vx[:, lo:hi]), (s_loc, v_all[k0_:k0_ + k_span, lo:hi])], None)
        o_ref[half * qh_rows:(half + 1) * qh_rows, lo:hi] = o.astype(o_ref.dtype)


def _na_attn(bq, bk, bv, bkx, bvx, tiles, tile_idx, batch, seq, ctx_len):
    nj = seq // NA_Q
    nkb = seq // NA_KBLK
    hp = B_HEADS * HEAD_DIM // LANES
    per = LANES // HEAD_DIM

    def kb(t):
        return lambda p, b, j, idx: (b * nkb + jnp.clip(2 * j - 1 + t, 0, nkb - 1), p)

    kv_specs = [pl.BlockSpec((NA_KBLK, LANES), kb(t)) for t in range(4)]
    grid_spec = pltpu.PrefetchScalarGridSpec(
        num_scalar_prefetch=1,
        grid=(hp, batch, nj),
        in_specs=[pl.BlockSpec((NA_Q, LANES), lambda p, b, j, idx: (b * nj + j, p))]
        + kv_specs + kv_specs
        + [pl.BlockSpec((ctx_len, LANES), lambda p, b, j, idx: (b, p)),
           pl.BlockSpec((ctx_len, LANES), lambda p, b, j, idx: (b, p)),
           pl.BlockSpec((per, NA_DR + 1, GRID_W, GRID_W), lambda p, b, j, idx: (p, 0, 0, 0))],
        out_specs=pl.BlockSpec((NA_Q, LANES), lambda p, b, j, idx: (b * nj + j, p)),
        scratch_shapes=[pltpu.VMEM((per, NA_Q, NA_K), F32)],
    )
    return pl.pallas_call(
        functools.partial(_na_kernel, nj=nj),
        grid_spec=grid_spec,
        out_shape=jax.ShapeDtypeStruct((batch * seq, B_HEADS * HEAD_DIM), BF16),
        compiler_params=_params(("arbitrary", "arbitrary", "arbitrary")),
        name="na_attn",
    )(tile_idx, bq, bk, bk, bk, bk, bv, bv, bv, bv, bkx, bvx, tiles)


def _ctx_ab_kernel(sink_ref, aq_ref, ak_ref, av_ref, bq_ref, bk_ref, bv_ref, oa_ref, ob_ref):
    ctx_len = aq_ref.shape[0]
    g_sz = A_HEADS // A_KV_HEADS
    aq = aq_ref[...]
    ak = ak_ref[...]
    av = av_ref[...]
    outs = []
    for g in range(A_KV_HEADS):
        lo, hi = g * HEAD_DIM, (g + 1) * HEAD_DIM
        heads = [g * g_sz + i for i in range(g_sz)]
        qs = jnp.concatenate([aq[:, h * HEAD_DIM:(h + 1) * HEAD_DIM] for h in heads], axis=0)
        sink = jnp.concatenate(
            [jnp.full((ctx_len, 1), sink_ref[h], F32) for h in heads], axis=0)
        o = _softmax_pv([(_dot_nt(qs, ak[:, lo:hi]), av[:, lo:hi])], sink)
        outs.extend(o[i * ctx_len:(i + 1) * ctx_len] for i in range(g_sz))
    oa_ref[...] = jnp.concatenate(outs, axis=1).astype(oa_ref.dtype)
    bq = bq_ref[...]
    bk = bk_ref[...]
    bv = bv_ref[...]
    outs = []
    for h in range(B_HEADS):
        lo, hi = h * HEAD_DIM, (h + 1) * HEAD_DIM
        outs.append(_softmax_pv([(_dot_nt(bq[:, lo:hi], bk[:, lo:hi]), bv[:, lo:hi])], None))
    ob_ref[...] = jnp.concatenate(outs, axis=1).astype(ob_ref.dtype)


def _ctx_ab_attn(sink, aq, ak, av, bq, bk, bv, batch, ctx_len):
    def spec(w):
        return pl.BlockSpec((ctx_len, w), lambda b: (b, 0))

    return pl.pallas_call(
        _ctx_ab_kernel,
        grid=(batch,),
        in_specs=[pl.BlockSpec(memory_space=pltpu.SMEM)]
        + [spec(a.shape[1]) for a in (aq, ak, av, bq, bk, bv)],
        out_specs=[spec(aq.shape[1]), spec(bq.shape[1])],
        out_shape=[jax.ShapeDtypeStruct(aq.shape, BF16), jax.ShapeDtypeStruct(bq.shape, BF16)],
        compiler_params=_params(("arbitrary",)),
        name="ctx_ab_attn",
    )(sink, aq, ak, av, bq, bk, bv)


def _layer_norm(t, g, b):
    mu = jnp.mean(t, -1, keepdims=True)
    c = t - mu
    var = jnp.mean(c * c, -1, keepdims=True)
    return c * lax.rsqrt(var + 1e-5) * g + b


def _post_attn_kernel(*refs, alpha, lat_tiles, with_ctx, lat_transposed):
    n_src = 6 if with_ctx else 3
    srcs = refs[:n_src]
    (w_ref, mod_ref, lng_ref, lnb_ref, wr_hi_ref, wr_lo_ref, br_ref,
     x1_ref, h2_ref, lg_ref) = refs[n_src:]

    def rows_of(o_ref):
        if o_ref.ndim == 2:
            return o_ref[...]
        return o_ref[0].astype(F32).T.astype(BF16)

    def run(o1_ref, o2_ref, x_ref):
        half = w_ref.shape[0] // 2
        y = _dot(rows_of(o1_ref), w_ref[0:half, :]) + _dot(rows_of(o2_ref), w_ref[half:, :])
        g1 = mod_ref[0, 2:3, :]
        x1 = _layer_norm(alpha * x_ref[...] + g1 * y, lng_ref[...], lnb_ref[...])
        x1_ref[...] = x1
        h2 = x1 * (1.0 + mod_ref[0, 4:5, :]) + mod_ref[0, 3:4, :]
        h_hi, h_lo = _split_bf16(h2)
        h2_ref[...] = _pack_halves(h2)
        wr_hi = wr_hi_ref[...]
        lg_ref[...] = (_dot(h_hi, wr_hi) + _dot(h_lo, wr_hi) + _dot(h_hi, wr_lo_ref[...])
                       + br_ref[...])

    if not with_ctx:
        run(*srcs)
    else:
        i = pl.program_id(0)
        pl.when(i < lat_tiles)(lambda: run(*srcs[:3]))
        pl.when(i >= lat_tiles)(lambda: run(*srcs[3:]))


def _post_attn(lat, ctx_src, w_bf, mod, gid, lng, lnb, wr_hi, wr_lo, br, n_lat, alpha, tm):
    half = w_bf.shape[0] // 2
    d = lat[2].shape[1]
    lat_tiles = n_lat // tm
    n_rows = n_lat + (ctx_src[0].shape[0] if ctx_src is not None else 0)
    lat_transposed = lat[0].ndim == 3

    def lat_map(i):
        return (jnp.minimum(i, lat_tiles - 1), 0)

    def ctx_map(i):
        return (jnp.maximum(i - lat_tiles, 0), 0)

    def src_specs(index_map):
        return [pl.BlockSpec((tm, half), index_map), pl.BlockSpec((tm, half), index_map),
                pl.BlockSpec((tm, d), index_map)]

    in_specs = src_specs(lat_map)
    if lat_transposed:
        per_batch = lat[0].shape[2] // tm
        t_spec = pl.BlockSpec((1, half, tm), lambda i: (i // per_batch, 0, i % per_batch))
        in_specs[0] = in_specs[1] = t_spec
    args = list(lat)
    if ctx_src is not None:
        in_specs += src_specs(ctx_map)
        args += list(ctx_src)

    def const(shape):
        return pl.BlockSpec(shape, lambda i: (0,) * len(shape))

    in_specs += [const((2 * half, d)),
                 pl.BlockSpec((1, 6, d), lambda i: (gid(i), 0, 0)),
                 const((1, d)), const((1, d)),
                 const((d, ROUTER_PAD)), const((d, ROUTER_PAD)), const((1, ROUTER_PAD))]
    args += [w_bf, mod, lng, lnb, wr_hi, wr_lo, br]
    widths = (d, d // 2, ROUTER_PAD)
    dtypes = (F32, jnp.uint32, F32)
    return pl.pallas_call(
        functools.partial(_post_attn_kernel, alpha=alpha, lat_tiles=lat_tiles,
                          with_ctx=ctx_src is not None, lat_transposed=lat_transposed),
        grid=(n_rows // tm,),
        in_specs=in_specs,
        out_specs=[pl.BlockSpec((tm, w), lambda i: (i, 0)) for w in widths],
        out_shape=[jax.ShapeDtypeStruct((n_rows, w), t) for w, t in zip(widths, dtypes)],
        compiler_params=_params(("arbitrary",)),
        name="post_attn",
    )(*args)


def _moe_kernel(be_ref, xs_ref, wgu_ref, wdn_ref, ys_ref, wgu_bf, wdn_bf):
    i = pl.program_id(0)
    prev = be_ref[jnp.maximum(i - 1, 0)]
    changed = (i == 0) | (be_ref[i] != prev)

    @pl.when(changed)
    def _():
        wgu_bf[...] = wgu_ref[0].astype(BF16)
        wdn_bf[...] = wdn_ref[0].astype(BF16)

    de = wdn_bf.shape[0]
    subs = [slice(r, r + MOE_SUB) for r in range(0, MOE_BLOCK, MOE_SUB)]
    gus = [_dot(_unpack_halves(xs_ref[sl, :]).astype(BF16), wgu_bf[...]) for sl in subs]
    for sl, gu in zip(subs, gus):
        gate = gu[:, :de]
        up = gu[:, de:]
        act = gate / (1.0 + jnp.exp(-gate)) * up
        ys_ref[sl, :] = _pack_halves(_dot(act.astype(BF16), wdn_bf[...]))


def _moe_experts(block_e, xs, w_gu, w_dn):
    rows, dw = xs.shape
    d = 2 * dw
    nb = rows // MOE_BLOCK
    de = w_dn.shape[1]
    grid_spec = pltpu.PrefetchScalarGridSpec(
        num_scalar_prefetch=1,
        grid=(nb,),
        in_specs=[
            pl.BlockSpec((MOE_BLOCK, dw), lambda i, be: (i, 0)),
            pl.BlockSpec((1, d, 2 * de), lambda i, be: (be[i], 0, 0)),
            pl.BlockSpec((1, de, d), lambda i, be: (be[i], 0, 0)),
        ],
        out_specs=pl.BlockSpec((MOE_BLOCK, dw), lambda i, be: (i, 0)),
        scratch_shapes=[pltpu.VMEM((d, 2 * de), BF16), pltpu.VMEM((de, d), BF16)],
    )
    return pl.pallas_call(
        _moe_kernel,
        grid_spec=grid_spec,
        out_shape=jax.ShapeDtypeStruct((rows, dw), jnp.uint32),
        compiler_params=_params(("arbitrary",)),
        name="moe_experts",
    )(block_e, xs, w_gu, w_dn)


ROUTE_TILE = 512


def _route_kernel(lg_ref, rec_ref, cnt_ref, carry):
    i = pl.program_id(0)

    @pl.when(i == 0)
    def _():
        carry[...] = jnp.zeros(carry.shape, F32)

    lg = lg_ref[...]
    tm = lg.shape[0]
    lane = lax.broadcasted_iota(jnp.int32, lg.shape, 1)

    def first_lane(mask):
        return jnp.min(jnp.where(mask, lane, ROUTER_PAD), axis=1, keepdims=True)

    is_g = lane < N_GROUPS
    g_log = jnp.where(is_g, lg, NEG_INF)
    g_max = jnp.max(g_log, axis=1, keepdims=True)
    g_sum = jnp.sum(jnp.where(is_g, jnp.exp(lg - g_max), 0.0), axis=1, keepdims=True)
    g_val = 1.0 / g_sum
    g_idx = first_lane(g_log == g_max)
    lo = N_GROUPS + EXP_PER_GROUP * g_idx
    sel = (lane >= lo) & (lane < lo + EXP_PER_GROUP)
    e_log = jnp.where(sel, lg, NEG_INF)
    e_max = jnp.max(e_log, axis=1, keepdims=True)
    e_exp = jnp.where(sel, jnp.exp(lg - e_max), 0.0)
    e_prob = e_exp / jnp.sum(e_exp, axis=1, keepdims=True)
    p1 = jnp.where(sel, e_prob, -1.0)
    v1 = jnp.max(p1, axis=1, keepdims=True)
    i1 = first_lane(p1 == v1)
    p2 = jnp.where(lane == i1, -1.0, p1)
    v2 = jnp.max(p2, axis=1, keepdims=True)
    i2 = first_lane(p2 == v2)
    norm = g_val / (v1 + v2)

    hot1 = lane == i1
    hot2 = lane == i2
    hot = jnp.where(hot1 | hot2, 1.0, 0.0)
    rows = lax.broadcasted_iota(jnp.int32, (tm, tm), 0)
    cols = lax.broadcasted_iota(jnp.int32, (tm, tm), 1)
    before = jnp.where(cols < rows, 1.0, 0.0).astype(BF16)
    prefix = _dot(before, hot.astype(BF16)) + carry[0:1, :]
    r1 = jnp.sum(jnp.where(hot1, prefix, 0.0), axis=1, keepdims=True)
    r2 = jnp.sum(jnp.where(hot2, prefix, 0.0), axis=1, keepdims=True)
    carry[0:1, :] = carry[0:1, :] + jnp.sum(hot, axis=0, keepdims=True)

    fields = [(i1 - N_GROUPS).astype(F32), (i2 - N_GROUPS).astype(F32), r1, r2, v1 * norm, v2 * norm]
    rec = jnp.zeros(lg.shape, F32)
    for k, f in enumerate(fields):
        rec = jnp.where(lane == k, f, rec)
    rec_ref[...] = rec

    @pl.when(i == pl.num_programs(0) - 1)
    def _():
        cnt_ref[...] = carry[...]


def _route(logits, n_tok):
    rec, cnt = pl.pallas_call(
        _route_kernel,
        grid=(n_tok // ROUTE_TILE,),
        in_specs=[pl.BlockSpec((ROUTE_TILE, ROUTER_PAD), lambda i: (i, 0))],
        out_specs=[pl.BlockSpec((ROUTE_TILE, ROUTER_PAD), lambda i: (i, 0)),
                   pl.BlockSpec((8, ROUTER_PAD), lambda i: (0, 0))],
        out_shape=[jax.ShapeDtypeStruct((n_tok, ROUTER_PAD), F32),
                   jax.ShapeDtypeStruct((8, ROUTER_PAD), F32)],
        scratch_shapes=[pltpu.VMEM((8, ROUTER_PAD), F32)],
        compiler_params=_params(("arbitrary",)),
        name="route",
    )(logits)
    experts = rec[:, 0:2].astype(jnp.int32)
    rank = rec[:, 2:4].astype(jnp.int32)
    wts = rec[:, 4:6]
    counts = cnt[0, N_GROUPS:N_GROUPS + N_EXPERTS].astype(jnp.int32)

    n_asg = n_tok * TOP_K
    pcounts = ((counts + MOE_BLOCK - 1) // MOE_BLOCK) * MOE_BLOCK
    pends = jnp.cumsum(pcounts)
    pstarts = pends - pcounts
    dest = pstarts[experts] + rank
    nb = -(-n_asg // MOE_BLOCK) + N_EXPERTS
    block_start = jnp.arange(nb, dtype=jnp.int32) * MOE_BLOCK
    block_e = jnp.minimum((pends[None, :] <= block_start[:, None]).sum(1), N_EXPERTS - 1).astype(jnp.int32)
    tok = jnp.broadcast_to(jnp.arange(n_tok, dtype=jnp.int32)[:, None], (n_tok, TOP_K))
    filler = jnp.arange(nb * MOE_BLOCK, dtype=jnp.int32) % n_tok
    row_tok = filler.at[dest.reshape(-1)].set(tok.reshape(-1), unique_indices=True)
    return wts, dest, row_tok, block_e


def _post_moe_kernel(x_ref, ya_ref, yb_ref, w_ref, mod_ref, lng_ref, lnb_ref, o_ref, *, alpha):
    w = w_ref[...]
    y = w[:, 0:1] * _unpack_halves(ya_ref[...]) + w[:, 1:2] * _unpack_halves(yb_ref[...])
    g2 = mod_ref[0, 5:6, :]
    o_ref[...] = _layer_norm(alpha * x_ref[...] + g2 * y, lng_ref[...], lnb_ref[...])


def _post_moe(x1, ya, yb, wts, mod, gid, lng, lnb, alpha, tm):
    rows, d = x1.shape
    row = pl.BlockSpec((tm, d), lambda i: (i, 0))
    packed = pl.BlockSpec((tm, d // 2), lambda i: (i, 0))
    vec = pl.BlockSpec((1, d), lambda i: (0, 0))
    return pl.pallas_call(
        functools.partial(_post_moe_kernel, alpha=alpha),
        grid=(rows // tm,),
        in_specs=[row, packed, packed,
                  pl.BlockSpec((tm, TOP_K), lambda i: (i, 0)),
                  pl.BlockSpec((1, 6, d), lambda i: (gid(i), 0, 0)),
                  vec, vec],
        out_specs=row,
        out_shape=jax.ShapeDtypeStruct((rows, d), F32),
        compiler_params=_params(("arbitrary",)),
        name="post_moe",
    )(x1, ya, yb, wts, mod, lng, lnb)


CD_COLS = dict(cq=(0, 384), ckv=(384, 640), dq=(640, 1152), dk=(1152, 1664), dv=(1664, 2176),
               kr=(2176, 2304))
C_HEAD_PAD = LANES


def _rms(t, g, eps=1e-6):
    return t * lax.rsqrt(jnp.mean(t * t, -1, keepdims=True) + eps) * g


def _proj_cd_kernel(x_ref, mod_ref, w_ref, qn_ref, wuq_ref, kvn_ref, wukv_ref,
                    cos64_ref, sin64_ref, cos32_ref, sin32_ref,
                    qc_ref, kc_ref, vct_ref, dq_ref, dk_ref, dvt_ref):
    h = _modulate(x_ref, mod_ref, 0).astype(BF16)
    r = _dot(h, w_ref[...])

    def cols(name):
        lo, hi = CD_COLS[name]
        return r[:, lo:hi]

    q = _dot(_rms(cols('cq'), qn_ref[...]).astype(BF16), wuq_ref[...])
    kv = _dot(_rms(cols('ckv'), kvn_ref[...]).astype(BF16), wukv_ref[...])
    c32, s32 = cos32_ref[...], sin32_ref[...]
    c64, s64 = cos64_ref[...], sin64_ref[...]
    q = _rope_wide(q, c32, s32, C_ROPE // 4)
    kr = _rope_tile(cols('kr'), c32, s32, C_ROPE // 4)
    dq = _rope_wide(cols('dq'), c64, s64, D_QK // 4)
    dk = _rope_wide(cols('dk'), c64, s64, D_QK // 4)
    kw = C_HEADS * C_HEAD_PAD
    qc_ref[0] = (q * ((C_NOPE + C_ROPE) ** -0.5 * LOG2E)).astype(BF16)
    kc_ref[0] = (kv[:, :kw] + jnp.concatenate([kr] * C_HEADS, axis=1)).astype(BF16)
    dq_ref[0] = (dq * (D_QK ** -0.5 * LOG2E)).astype(BF16)
    dk_ref[0] = dk.astype(BF16)
    _store_vt(vct_ref, kv[:, kw:], C_HEADS, C_V)
    _store_vt(dvt_ref, cols('dv'), D_HEADS, D_V)


def _store_vt(vt_ref, v, heads, dv):
    tm = v.shape[0]
    v_t = v.T.astype(BF16)
    row = lax.broadcasted_iota(jnp.int32, (ONES_ROWS, tm), 0)
    extra = jnp.where(row == 0, 1.0, 0.0).astype(BF16)
    for h in range(heads):
        vt_ref[0, h, 0, 0:dv, :] = v_t[h * dv:(h + 1) * dv]
        vt_ref[0, h, 0, dv:dv + ONES_ROWS, :] = extra


def _proj_cd(xall, mod, w_bf, qn, wuq, kvn, wukv, tabs, batch, seq, ctx_len, tm, tk):
    d = xall.shape[1]
    lat_t, ctx_t = seq // tm, ctx_len // tm
    n_keys = seq + ctx_len
    per_chunk = tk // tm
    n_chunks = n_keys // tk
    widths = (C_HEADS * C_HEAD_PAD, C_HEADS * C_HEAD_PAD, 2 * D_HEADS * D_QK, 2 * D_HEADS * D_QK)

    def src(b, t):
        return (jnp.where(t < lat_t, b * lat_t + t, batch * lat_t + b * ctx_t + t - lat_t), 0)

    def full(a):
        return pl.BlockSpec(a.shape, lambda b, t: (0,) * a.ndim)

    def row_spec(w):
        return pl.BlockSpec((1, tm, w), lambda b, t: (b, t, 0))

    def row_shape(w):
        return jax.ShapeDtypeStruct((batch, n_keys, w), BF16)

    def vt_spec(heads, dv):
        return pl.BlockSpec((1, heads, 1, dv + ONES_ROWS, tm),
                            lambda b, t: (b, 0, t // per_chunk, 0, t % per_chunk))

    def vt_shape(heads, dv):
        return jax.ShapeDtypeStruct((batch, heads, n_chunks, dv + ONES_ROWS, tk), BF16)

    tab = pl.BlockSpec((tm, LANES), lambda b, t: (t, 0))
    return pl.pallas_call(
        _proj_cd_kernel,
        grid=(batch, lat_t + ctx_t),
        in_specs=[pl.BlockSpec((tm, d), src),
                  pl.BlockSpec((1, 6, d), lambda b, t: (jnp.where(t < lat_t, b, batch), 0, 0)),
                  full(w_bf), full(qn), full(wuq), full(kvn), full(wukv), tab, tab, tab, tab],
        out_specs=[row_spec(widths[0]), row_spec(widths[1]), vt_spec(C_HEADS, C_V),
                   row_spec(widths[2]), row_spec(widths[3]), vt_spec(D_HEADS, D_V)],
        out_shape=[row_shape(widths[0]), row_shape(widths[1]), vt_shape(C_HEADS, C_V),
                   row_shape(widths[2]), row_shape(widths[3]), vt_shape(D_HEADS, D_V)],
        compiler_params=_params(("arbitrary", "arbitrary")),
        name="proj_cd",
    )(xall, mod, w_bf, qn, wuq, kvn, wukv, *tabs)


DENSE_SHIFT = 80.0
DENSE_L_MIN = 2.0 ** -60
DENSE_L_MAX = 2.0 ** 120
DENSE_TQ = 1024
DENSE_TK_MAX = 1280
DENSE_UNROLL_MAX = 13
MXU_DEPTH = 256
PROJ_CD_TILE = 256
ONES_ROWS = 16


def _sq_norm_row(x):
    ones = jnp.ones((8, x.shape[1]), BF16)
    return _dot_nt(ones, (x * x).astype(BF16))[0:1] * 1.02


def _key_norm_max(kchunk, n_chunks, masks):
    def body(j, mx):
        kf = kchunk(j).astype(F32)
        ksq = kf * kf
        out = []
        for msk, cur in zip(masks, mx):
            part = ksq if msk is None else jnp.where(msk, ksq, 0.0)
            rn = jnp.sum(part, axis=1, keepdims=True)
            out.append(jnp.maximum(cur, jnp.max(rn, axis=0, keepdims=True)))
        return tuple(out)

    mx = lax.fori_loop(0, n_chunks, body, tuple(jnp.zeros((1, 1), F32) for _ in masks))
    return [jnp.sqrt(v) * 1.01 for v in mx]


def _safe_online(q, kchunk, vt_chunk, n_chunks, m_scr, l_scr, acc):
    m_scr[...] = jnp.full(m_scr.shape, NEG_INF, F32)
    l_scr[...] = jnp.zeros(l_scr.shape, F32)
    acc[...] = jnp.zeros(acc.shape, F32)

    def body(j, carry):
        s = _dot_nt(kchunk(j), q)
        m_prev = m_scr[...]
        m_new = jnp.maximum(m_prev, s.max(0, keepdims=True))
        alpha = jnp.exp2(m_prev - m_new)
        p = jnp.exp2(s - m_new)
        l_scr[...] = alpha * l_scr[...] + p.sum(0, keepdims=True)
        acc[...] = alpha * acc[...] + _dot(vt_chunk(j), p.astype(BF16))
        m_scr[...] = m_new
        return carry

    lax.fori_loop(0, n_chunks, body, 0)


def _denominators_ok(*ls):
    ok = None
    for l in ls:
        cur = (l > DENSE_L_MIN) & (l < DENSE_L_MAX)
        ok = cur if ok is None else ok & cur
    return jnp.max(jnp.where(ok, 0.0, 1.0)) == 0.0


MLA_HEADS_PER_STEP = 2


def _mla_kernel(q_ref, k_ref, vt_ref, o_ref, kmax_scr, acc_scr, m_scr, l_scr,
                *, n_chunks, tk, unroll, depth):
    hp = acc_scr.shape[0]

    def kchunk(j, h):
        return k_ref[0, pl.ds(pl.multiple_of(j * tk, tk), tk), h * C_HEAD_PAD:(h + 1) * C_HEAD_PAD]

    @pl.when(pl.program_id(2) == 0)
    def _():
        for h in range(hp):
            (kmax,) = _key_norm_max(lambda j: kchunk(j, h), n_chunks, [None])
            kmax_scr[h:h + 1, :] = jnp.broadcast_to(kmax, (1, LANES))

    qs = [q_ref[0, :, h * C_HEAD_PAD:(h + 1) * C_HEAD_PAD] for h in range(hp)]
    ms = [jnp.sqrt(_sq_norm_row(qs[h].astype(F32))) * kmax_scr[h:h + 1, 0:1] - DENSE_SHIFT
          for h in range(hp)]
    acc_scr[...] = jnp.zeros(acc_scr.shape, F32)

    def scores(j):
        return [_dot_nt(kchunk(j, h), qs[h]) for h in range(hp)]

    def body(it, carry):
        j0 = it * unroll
        pend = [scores(j0 + u) for u in range(min(depth, unroll))]
        accs = [acc_scr[h] for h in range(hp)]
        for u in range(unroll):
            ss = pend.pop(0)
            if u + depth < unroll:
                pend.append(scores(j0 + u + depth))
            for h in range(hp):
                accs[h] = accs[h] + _dot(vt_ref[0, h, j0 + u], jnp.exp2(ss[h] - ms[h]).astype(BF16))
        for h in range(hp):
            acc_scr[h] = accs[h]
        return carry

    lax.fori_loop(0, n_chunks // unroll, body, 0)
    ls = [acc_scr[h, C_V:C_V + 1, :] for h in range(hp)]
    for h in range(hp):
        o_ref[0, h * C_V:(h + 1) * C_V, :] = (acc_scr[h, 0:C_V, :] / ls[h]).astype(o_ref.dtype)

    @pl.when(jnp.logical_not(_denominators_ok(*ls)))
    def _():
        for h in range(hp):
            _safe_online(qs[h], lambda j: kchunk(j, h), lambda j: vt_ref[0, h, j], n_chunks,
                         m_scr, l_scr, acc_scr.at[h])
            o_ref[0, h * C_V:(h + 1) * C_V, :] = (acc_scr[h, 0:C_V, :] / l_scr[...]).astype(o_ref.dtype)


def _mla_attn(qc, kc, vt, batch, seq, tq, tk, unroll, depth):
    n_keys = kc.shape[1]
    nq, n_chunks = seq // tq, n_keys // tk
    rows = C_V + ONES_ROWS
    hp = MLA_HEADS_PER_STEP
    return pl.pallas_call(
        functools.partial(_mla_kernel, n_chunks=n_chunks, tk=tk, unroll=unroll, depth=depth),
        grid=(batch, C_HEADS // hp, nq),
        in_specs=[
            pl.BlockSpec((1, tq, hp * C_HEAD_PAD), lambda b, h, i: (b, i, h)),
            pl.BlockSpec((1, n_keys, hp * C_HEAD_PAD), lambda b, h, i: (b, 0, h)),
            pl.BlockSpec((1, hp, n_chunks, rows, tk), lambda b, h, i: (b, h, 0, 0, 0)),
        ],
        out_specs=pl.BlockSpec((1, hp * C_V, tq), lambda b, h, i: (b, h, i)),
        out_shape=jax.ShapeDtypeStruct((batch, C_HEADS * C_V, seq), BF16),
        scratch_shapes=[pltpu.VMEM((8, LANES), F32), pltpu.VMEM((hp, rows, tq), F32),
                        pltpu.VMEM((1, tq), F32), pltpu.VMEM((1, tq), F32)],
        compiler_params=_params(("arbitrary", "arbitrary", "arbitrary")),
        name="mla_attn",
    )(qc, kc, vt)


def _diff_kernel(lam_ref, q_ref, k_ref, vt_ref, subln_ref, o_ref, kmax_scr, acc1_scr, acc2_scr,
                 m_scr, l_scr, *, n_chunks, tk, unroll, depth, out_scale):
    def kchunk(j):
        return k_ref[0, pl.ds(pl.multiple_of(j * tk, tk), tk), :]

    @pl.when(pl.program_id(2) == 0)
    def _():
        klane = lax.broadcasted_iota(jnp.int32, (tk, 2 * D_QK), 1)
        k1, k2 = _key_norm_max(kchunk, n_chunks, [klane < D_QK, klane >= D_QK])
        kmax_scr[0:1, :] = jnp.broadcast_to(k1, (1, LANES))
        kmax_scr[1:2, :] = jnp.broadcast_to(k2, (1, LANES))

    q = q_ref[0]
    qlane = lax.broadcasted_iota(jnp.int32, q.shape, 1)
    zero = jnp.zeros_like(q)
    q1 = jnp.where(qlane < D_QK, q, zero)
    q2 = jnp.where(qlane >= D_QK, q, zero)
    m1 = jnp.sqrt(_sq_norm_row(q1.astype(F32))) * kmax_scr[0:1, 0:1] - DENSE_SHIFT
    m2 = jnp.sqrt(_sq_norm_row(q2.astype(F32))) * kmax_scr[1:2, 0:1] - DENSE_SHIFT
    acc1_scr[...] = jnp.zeros(acc1_scr.shape, F32)
    acc2_scr[...] = jnp.zeros(acc2_scr.shape, F32)

    def scores(j):
        k = kchunk(j)
        return _dot_nt(k, q1), _dot_nt(k, q2)

    def body(it, carry):
        j0 = it * unroll
        pend = [scores(j0 + u) for u in range(min(depth, unroll))]
        a1 = acc1_scr[...]
        a2 = acc2_scr[...]
        for u in range(unroll):
            s1, s2 = pend.pop(0)
            if u + depth < unroll:
                pend.append(scores(j0 + u + depth))
            vt = vt_ref[0, 0, j0 + u]
            a1 = a1 + _dot(vt, jnp.exp2(s1 - m1).astype(BF16))
            a2 = a2 + _dot(vt, jnp.exp2(s2 - m2).astype(BF16))
        acc1_scr[...] = a1
        acc2_scr[...] = a2
        return carry

    lax.fori_loop(0, n_chunks // unroll, body, 0)

    def finish(o1, o2):
        o = o1 - lam_ref[0] * o2
        o = o * lax.rsqrt(jnp.mean(o * o, 0, keepdims=True) + 1e-6) * subln_ref[...] * out_scale
        o_ref[0] = o.astype(o_ref.dtype)

    l1 = acc1_scr[D_V:D_V + 1, :]
    l2 = acc2_scr[D_V:D_V + 1, :]
    finish(acc1_scr[0:D_V, :] / l1, acc2_scr[0:D_V, :] / l2)

    @pl.when(jnp.logical_not(_denominators_ok(l1, l2)))
    def _():
        def vt_chunk(j):
            return vt_ref[0, 0, j]

        _safe_online(q1, kchunk, vt_chunk, n_chunks, m_scr, l_scr, acc1_scr)
        o1 = acc1_scr[0:D_V, :] / l_scr[...]
        _safe_online(q2, kchunk, vt_chunk, n_chunks, m_scr, l_scr, acc2_scr)
        finish(o1, acc2_scr[0:D_V, :] / l_scr[...])


def _diff_attn(lam, dq, dk, vt, subln_col, out_scale, batch, seq, tq, tk, unroll, depth):
    n_keys = dk.shape[1]
    nq, n_chunks = seq // tq, n_keys // tk
    rows = D_V + ONES_ROWS
    pair = 2 * D_QK
    return pl.pallas_call(
        functools.partial(_diff_kernel, n_chunks=n_chunks, tk=tk, unroll=unroll, depth=depth,
                          out_scale=out_scale),
        grid=(batch, D_HEADS, nq),
        in_specs=[
            pl.BlockSpec(memory_space=pltpu.SMEM),
            pl.BlockSpec((1, tq, pair), lambda b, h, i: (b, i, h)),
            pl.BlockSpec((1, n_keys, pair), lambda b, h, i: (b, 0, h)),
            pl.BlockSpec((1, 1, n_chunks, rows, tk), lambda b, h, i: (b, h, 0, 0, 0)),
            pl.BlockSpec((D_V, 1), lambda b, h, i: (0, 0)),
        ],
        out_specs=pl.BlockSpec((1, D_V, tq), lambda b, h, i: (b, h, i)),
        out_shape=jax.ShapeDtypeStruct((batch, D_HEADS * D_V, seq), BF16),
        scratch_shapes=[pltpu.VMEM((8, LANES), F32), pltpu.VMEM((rows, tq), F32),
                        pltpu.VMEM((rows, tq), F32), pltpu.VMEM((1, tq), F32), pltpu.VMEM((1, tq), F32)],
        compiler_params=_params(("arbitrary", "arbitrary", "arbitrary")),
        name="diff_attn",
    )(lam, dq, dk, vt, subln_col)


def _rope_tables(n_tok, dim, lane_lo):
    t = jnp.arange(n_tok)
    pos_r = (t // GRID_W).astype(F32)
    pos_c = (t % GRID_W).astype(F32)
    quarter = dim // 4
    inv = ROPE_BASE ** (-jnp.arange(quarter, dtype=F32) / quarter)
    ang_r = pos_r[:, None] * inv
    ang_c = pos_c[:, None] * inv
    ang = jnp.concatenate([ang_r, ang_r, ang_c, ang_c], -1)
    sign = jnp.tile(jnp.concatenate([-jnp.ones(quarter), jnp.ones(quarter)]), 2).astype(F32)
    cos, sin = jnp.cos(ang), jnp.sin(ang) * sign
    reps = (LANES - lane_lo) // dim
    cos = jnp.concatenate([jnp.ones((n_tok, lane_lo), F32)] + [cos] * reps, axis=1)
    sin = jnp.concatenate([jnp.zeros((n_tok, lane_lo), F32)] + [sin] * reps, axis=1)
    return cos, sin


def _cd_weights(w_in, w_uq, w_ukv):
    d = w_in.shape[0]
    s0 = C_Q_RANK
    s1 = s0 + C_KV_RANK
    s2 = s1 + C_ROPE
    s3 = s2 + 2 * D_HEADS * D_QK
    s4 = s3 + 2 * D_HEADS * D_QK
    kr = jnp.zeros((d, LANES), F32).at[:, C_NOPE:C_NOPE + C_ROPE].set(w_in[:, s1:s2])
    w_in_p = jnp.concatenate([w_in[:, :s1], w_in[:, s2:s3], w_in[:, s3:s4], w_in[:, s4:], kr], axis=1)
    dqk = C_NOPE + C_ROPE
    wq = w_uq.reshape(C_Q_RANK, C_HEADS, dqk)
    wq = jnp.pad(wq, ((0, 0), (0, 0), (0, C_HEAD_PAD - dqk))).reshape(C_Q_RANK, C_HEADS * C_HEAD_PAD)
    wkv = w_ukv.reshape(C_KV_RANK, C_HEADS, C_NOPE + C_V)
    wk = jnp.pad(wkv[:, :, :C_NOPE], ((0, 0), (0, 0), (0, C_HEAD_PAD - C_NOPE)))
    wk = wk.reshape(C_KV_RANK, C_HEADS * C_HEAD_PAD)
    wv = wkv[:, :, C_NOPE:].reshape(C_KV_RANK, C_HEADS * C_V)
    return w_in_p.astype(BF16), wq.astype(BF16), jnp.concatenate([wk, wv], axis=1).astype(BF16)


def _moe_layer(h2, logits, x1, w_gu, w_dn, mod, gid, lng, lnb, alpha, tm):
    n_tok = h2.shape[0]
    wts, dest, row_tok, block_e = _route(logits, n_tok)
    xs = jnp.take(h2, row_tok, axis=0, mode='clip')
    ys = _moe_experts(block_e, xs, w_gu, w_dn)
    ya = jnp.take(ys, dest[:, 0], axis=0, mode='clip')
    yb = jnp.take(ys, dest[:, 1], axis=0, mode='clip')
    return _post_moe(x1, ya, yb, wts.astype(F32), mod, gid, lng, lnb, alpha, tm)


def kernel(x, c, ctx, c_ctx, w_ada, b_ada, ln_g, ln_b, ab_w_in, a_sink, b_rpb, ab_w_out, cd_w_in, c_q_norm, c_w_uq, c_kv_norm, c_w_ukv, d_lambda, d_subln, cd_w_out, w_group, b_group, w_exp_router, b_exp_router, w_gate_up, w_down):
    batch, seq, d = x.shape
    ctx_len = ctx.shape[1]
    depth = w_ada.shape[0]
    alpha = (2 * depth) ** 0.25
    n_lat = batch * seq
    n_ctx = batch * ctx_len
    tm = 512
    tq = DENSE_TQ
    assert depth == 2 and seq % NA_Q == 0 and seq % tq == 0 and n_ctx % tm == 0
    assert seq % PROJ_CD_TILE == 0 and ctx_len % PROJ_CD_TILE == 0
    assert seq % (WIN_GROUP * A_BLOCK) == 0 and seq // NA_Q >= 2
    key_tiles = (seq + ctx_len) // MXU_DEPTH
    tk = MXU_DEPTH * max(u for u in range(1, DENSE_TK_MAX // MXU_DEPTH + 1) if key_tiles % u == 0)
    n_chunks = (seq + ctx_len) // tk

    def unroll_for(cap):
        return max(u for u in range(1, cap + 1) if n_chunks % u == 0)
    lat_tiles = seq // tm

    def gid_lat(i):
        return i // lat_tiles

    def gid_ctx(i):
        return batch

    def gid_all(i):
        return jnp.minimum(i // lat_tiles, batch)

    c_all = jnp.zeros((8, d), F32).at[:batch].set(c).at[batch].set(c_ctx)
    mod = _ada(c_all, w_ada, b_ada).reshape(depth, 8, 6, d)

    cos64, sin64 = _rope_tables(seq, HEAD_DIM, 0)
    cos32, sin32 = _rope_tables(seq, C_ROPE, C_NOPE)

    def router_weights(l):
        wr = jnp.zeros((d, ROUTER_PAD), F32)
        wr = wr.at[:, :N_GROUPS].set(w_group[l]).at[:, N_GROUPS:N_GROUPS + N_EXPERTS].set(w_exp_router[l])
        br = jnp.zeros((1, ROUTER_PAD), F32)
        br = br.at[0, :N_GROUPS].set(b_group[l]).at[0, N_GROUPS:N_GROUPS + N_EXPERTS].set(b_exp_router[l])
        hi = wr.astype(BF16)
        lo = (wr - hi.astype(F32)).astype(BF16)
        return hi, lo, br

    x2d = x.reshape(n_lat, d)
    ctx2d = ctx.reshape(n_ctx, d)

    l = 0
    w_in_bf = ab_w_in[0].astype(BF16)
    aq, ak, av, bq, bk, bv = _proj_ab(x2d, 0, n_lat, mod[l], gid_lat, w_in_bf, cos64, sin64,
                                      lat_tiles, True, tm)
    aqx, akx, avx, bqx, bkx, bvx = _proj_ab(ctx2d, 0, n_ctx, mod[l], gid_ctx, w_in_bf, cos64, sin64,
                                            1, False, tm)
    sink = a_sink[0].astype(F32)
    oa = _window_attn(sink, aq, ak, av, akx, avx, batch, seq, ctx_len)
    tiles, tile_idx = _na_bias_tiles(b_rpb[0], seq // GRID_W)
    ob = _na_attn(bq, bk, bv, bkx, bvx, tiles, tile_idx, batch, seq, ctx_len)
    oax, obx = _ctx_ab_attn(sink, aqx, akx, avx, bqx, bkx, bvx, batch, ctx_len)

    n_all = n_lat + n_ctx
    w_out_bf = ab_w_out[0].astype(BF16)
    lng, lnb = ln_g[l, 0][None], ln_b[l, 0][None]
    wr_hi, wr_lo, br = router_weights(l)
    x1, h2, logits = _post_attn((oa, ob, x2d), (oax, obx, ctx2d), w_out_bf, mod[l], gid_all,
                                lng, lnb, wr_hi, wr_lo, br, n_lat, alpha, tm)
    xall = _moe_layer(h2, logits, x1, w_gate_up[l], w_down[l], mod[l], gid_all,
                      ln_g[l, 1][None], ln_b[l, 1][None], alpha, tm)

    l = 1
    lam_init = 0.8 - 0.6 * math.exp(-0.3 * l)
    lp = d_lambda[0].astype(F32)
    lam = (jnp.exp(jnp.sum(lp[0] * lp[1])) - jnp.exp(jnp.sum(lp[2] * lp[3])) + lam_init).reshape(1)
    w_in_p, wuq_p, wukv_p = _cd_weights(cd_w_in[0], c_w_uq[0], c_w_ukv[0])
    qn, kvn = c_q_norm[0][None].astype(F32), c_kv_norm[0][None].astype(F32)
    def with_ctx_identity(cos, sin):
        return (jnp.concatenate([cos, jnp.ones((ctx_len, LANES), F32)]),
                jnp.concatenate([sin, jnp.zeros((ctx_len, LANES), F32)]))

    tabs = with_ctx_identity(cos64, sin64) + with_ctx_identity(cos32, sin32)
    qc, kc, vct, dq, dk, dvt = _proj_cd(xall, mod[l], w_in_p, qn, wuq_p, kvn, wukv_p, tabs,
                                        batch, seq, ctx_len, PROJ_CD_TILE, tk)
    unroll = unroll_for(DENSE_UNROLL_MAX)
    oc_t = _mla_attn(qc, kc, vct, batch, seq, tq, tk, unroll, 1)
    od_t = _diff_attn(lam, dq, dk, dvt, d_subln[0].astype(F32).reshape(D_V, 1), 1.0 - lam_init,
                      batch, seq, tq, tk, unroll, 1)
    wr_hi, wr_lo, br = router_weights(l)
    x1, h2, logits = _post_attn((oc_t, od_t, xall), None, cd_w_out[0].astype(BF16), mod[l], gid_lat,
                                ln_g[l, 0][None], ln_b[l, 0][None], wr_hi, wr_lo, br,
                                n_lat, alpha, tm)
    out = _moe_layer(h2, logits, x1, w_gate_up[l], w_down[l], mod[l], gid_lat,
                     ln_g[l, 1][None], ln_b[l, 1][None], alpha, tm)
    return out.reshape(batch, seq, d)
```

```python
import functools
import math

import jax
import jax.numpy as jnp
from jax import lax
from jax.experimental import pallas as pl
from jax.experimental.pallas import tpu as pltpu

F32 = jnp.float32
BF16 = jnp.bfloat16

GRID_W = 64
HEAD_DIM = 64
ROPE_BASE = 10000.0
NEG_INF = -1e30
LOG2E = 1.4426950408889634

A_HEADS = 8
A_KV_HEADS = 2
A_WINDOW = 128
A_BLOCK = 128
B_HEADS = 8
NA_ROWS = 8
NA_COLS = 16
C_HEADS = 8
C_Q_RANK = 384
C_KV_RANK = 256
C_NOPE = 64
C_ROPE = 32
C_V = 64
D_HEADS = 4
D_QK = 64
D_V = 128
N_GROUPS = 4
EXP_PER_GROUP = 8
N_EXPERTS = N_GROUPS * EXP_PER_GROUP
TOP_K = 2
MOE_BLOCK = 512
MOE_SUB = 256
ROUTER_PAD = 128

LANES = 128
VMEM_LIMIT = 56 * 1024 * 1024


def _params(sem):
    return pltpu.CompilerParams(dimension_semantics=sem, vmem_limit_bytes=VMEM_LIMIT)


def _dot(a, b):
    return jnp.dot(a, b, preferred_element_type=F32)


def _dot_nt(a, b):
    return lax.dot_general(a, b, (((1,), (1,)), ((), ())), preferred_element_type=F32)


def _split_bf16(x):
    hi = x.astype(BF16)
    lo = (x - hi.astype(F32)).astype(BF16)
    return hi, lo


HI_HALF = 0xFFFF0000


def _pack_halves(x):
    w = x.shape[1] // 2
    lo = lax.bitcast_convert_type(x[:, :w].astype(BF16).astype(F32), jnp.uint32)
    hi = lax.bitcast_convert_type(x[:, w:].astype(BF16).astype(F32), jnp.uint32)
    return (lo >> 16) | (hi & jnp.uint32(HI_HALF))


def _unpack_halves(u):
    lo = lax.bitcast_convert_type(u << 16, F32)
    hi = lax.bitcast_convert_type(u & jnp.uint32(HI_HALF), F32)
    return jnp.concatenate([lo, hi], axis=1)


def _ada_kernel(c_ref, w_ref, b_ref, o_ref):
    c = c_ref[...]
    a = c / (1.0 + jnp.exp(-c))
    a_hi, a_lo = _split_bf16(a)
    w_hi, w_lo = _split_bf16(w_ref[0])
    acc = _dot(a_hi, w_hi) + _dot(a_lo, w_hi) + _dot(a_hi, w_lo)
    o_ref[0] = acc + b_ref[0]


def _ada(c_all, w_ada, b_ada):
    depth, d, n = w_ada.shape
    tn = 1536
    return pl.pallas_call(
        _ada_kernel,
        grid=(depth, n // tn),
        in_specs=[
            pl.BlockSpec((8, d), lambda l, j: (0, 0)),
            pl.BlockSpec((1, d, tn), lambda l, j: (l, 0, j)),
            pl.BlockSpec((1, 1, tn), lambda l, j: (l, 0, j)),
        ],
        out_specs=pl.BlockSpec((1, 8, tn), lambda l, j: (l, 0, j)),
        out_shape=jax.ShapeDtypeStruct((depth, 8, n), F32),
        compiler_params=_params(("arbitrary", "arbitrary")),
        name="ada",
    )(c_all, w_ada, b_ada.reshape(depth, 1, n))


def _rope_tile(t, cos, sin_signed, q):
    lane = lax.broadcasted_iota(jnp.int32, t.shape, 1)
    first = (lane & (2 * q - 1)) < q
    up = pltpu.roll(t, LANES - q, 1)
    dn = pltpu.roll(t, q, 1)
    return t * cos + jnp.where(first, up, dn) * sin_signed


def _rope_wide(t, cos, sin_signed, q):
    n = t.shape[1] // LANES
    return jnp.concatenate(
        [_rope_tile(t[:, i * LANES:(i + 1) * LANES], cos, sin_signed, q) for i in range(n)], axis=1)


def _modulate(x_ref, mod_ref, which):
    sh = mod_ref[0, 3 * which:3 * which + 1, :]
    sc = mod_ref[0, 3 * which + 1:3 * which + 2, :]
    return x_ref[...] * (1.0 + sc) + sh


def _proj_ab_kernel(x_ref, mod_ref, w_ref, cos_ref, sin_ref,
                    aq_ref, ak_ref, av_ref, bq_ref, bk_ref, bv_ref, *, rope):
    h = _modulate(x_ref, mod_ref, 0).astype(BF16)
    r = _dot(h, w_ref[...])
    aq = r[:, 0:512]
    ak = r[:, 512:640]
    if rope:
        cos = cos_ref[...]
        sin = sin_ref[...]
        aq = _rope_wide(aq, cos, sin, 16)
        ak = _rope_wide(ak, cos, sin, 16)
    scale = HEAD_DIM ** -0.5
    aq_ref[...] = (aq * scale).astype(BF16)
    ak_ref[...] = ak.astype(BF16)
    av_ref[...] = r[:, 640:768].astype(BF16)
    bq_ref[...] = (r[:, 768:1280] * scale).astype(BF16)
    bk_ref[...] = r[:, 1280:1792].astype(BF16)
    bv_ref[...] = r[:, 1792:2304].astype(BF16)


def _proj_ab(x2d, row0, nrows, mod, gid, w_bf, cos, sin, pos_blocks, rope, tm):
    d = x2d.shape[1]
    n_in = w_bf.shape[1]
    b0 = row0 // tm
    widths = (512, 128, 128, 512, 512, 512)
    return pl.pallas_call(
        functools.partial(_proj_ab_kernel, rope=rope),
        grid=(nrows // tm,),
        in_specs=[
            pl.BlockSpec((tm, d), lambda i: (b0 + i, 0)),
            pl.BlockSpec((1, 6, d), lambda i: (gid(i), 0, 0)),
            pl.BlockSpec((d, n_in), lambda i: (0, 0)),
            pl.BlockSpec((tm, LANES), lambda i: (i % pos_blocks, 0)),
            pl.BlockSpec((tm, LANES), lambda i: (i % pos_blocks, 0)),
        ],
        out_specs=[pl.BlockSpec((tm, w), lambda i: (i, 0)) for w in widths],
        out_shape=[jax.ShapeDtypeStruct((nrows, w), BF16) for w in widths],
        compiler_params=_params(("arbitrary",)),
        name="proj_ab",
    )(x2d, mod, w_bf, cos, sin)


def _softmax_pv(parts, sink_col):
    m = parts[0][0].max(-1, keepdims=True)
    for s, _ in parts[1:]:
        m = jnp.maximum(m, s.max(-1, keepdims=True))
    if sink_col is not None:
        m = jnp.maximum(m, sink_col)
    denom = None if sink_col is None else jnp.exp(sink_col - m)
    o = None
    for s, v in parts:
        e = jnp.exp(s - m)
        d = e.sum(-1, keepdims=True)
        denom = d if denom is None else denom + d
        pv = _dot(e.astype(BF16), v)
        o = pv if o is None else o + pv
    return o / denom


def _win_kernel(sink_ref, q_ref, kp_ref, kc_ref, kn_ref, vp_ref, vc_ref, vn_ref,
                kx_ref, vx_ref, o_ref, *, seq):
    step = pl.program_id(1)
    blk = A_BLOCK
    g_sz = A_HEADS // A_KV_HEADS
    qi = lax.broadcasted_iota(jnp.int32, (blk, 3 * blk), 0)
    kj = lax.broadcasted_iota(jnp.int32, (blk, 3 * blk), 1)
    in_window = jnp.abs(kj - blk - qi) <= A_WINDOW
    k_ext = jnp.concatenate([kp_ref[...], kc_ref[...], kn_ref[...]], axis=0)
    v_ext = jnp.concatenate([vp_ref[...], vc_ref[...], vn_ref[...]], axis=0)
    kx = kx_ref[...]
    vx = vx_ref[...]

    def scores(t, g):
        lo, hi = g * HEAD_DIM, (g + 1) * HEAD_DIM
        kpos = (step * WIN_GROUP + t) * blk + kj - blk
        valid = in_window & (kpos >= 0) & (kpos < seq)
        valid = jnp.concatenate([valid] * g_sz, axis=0)
        q = q_ref[t * blk:(t + 1) * blk, :]
        heads = [g * g_sz + i for i in range(g_sz)]
        qs = jnp.concatenate([q[:, h * HEAD_DIM:(h + 1) * HEAD_DIM] for h in heads], axis=0)
        s_loc = jnp.where(valid, _dot_nt(qs, k_ext[t * blk:(t + 3) * blk, lo:hi]), NEG_INF)
        return s_loc, _dot_nt(qs, kx[:, lo:hi])

    units = [(t, g) for t in range(WIN_GROUP) for g in range(A_KV_HEADS)]
    pending = scores(*units[0])
    outs = []
    for n, (t, g) in enumerate(units):
        s_loc, s_ctx = pending
        if n + 1 < len(units):
            pending = scores(*units[n + 1])
        lo, hi = g * HEAD_DIM, (g + 1) * HEAD_DIM
        sink = jnp.concatenate(
            [jnp.full((blk, 1), sink_ref[g * g_sz + i], F32) for i in range(g_sz)], axis=0)
        o = _softmax_pv([(s_ctx, vx[:, lo:hi]), (s_loc, v_ext[t * blk:(t + 3) * blk, lo:hi])], sink)
        outs.extend(o[i * blk:(i + 1) * blk] for i in range(g_sz))
        if g == A_KV_HEADS - 1:
            o_ref[t * blk:(t + 1) * blk, :] = jnp.concatenate(outs, axis=1).astype(o_ref.dtype)
            outs = []


WIN_GROUP = 8


def _window_attn(sink, aq, ak, av, akx, avx, batch, seq, ctx_len):
    nblk = seq // A_BLOCK
    nstep = nblk // WIN_GROUP
    kvw = A_KV_HEADS * HEAD_DIM
    qw = A_HEADS * HEAD_DIM
    rows = WIN_GROUP * A_BLOCK

    def edge(delta):
        return lambda b, n: (b * nblk + jnp.clip(n * WIN_GROUP + delta, 0, nblk - 1), 0)

    def own(b, n):
        return (b * nstep + n, 0)

    kv_specs = [pl.BlockSpec((A_BLOCK, kvw), edge(-1)), pl.BlockSpec((rows, kvw), own),
                pl.BlockSpec((A_BLOCK, kvw), edge(WIN_GROUP))]
    return pl.pallas_call(
        functools.partial(_win_kernel, seq=seq),
        grid=(batch, nstep),
        in_specs=[pl.BlockSpec(memory_space=pltpu.SMEM), pl.BlockSpec((rows, qw), own)]
        + kv_specs + kv_specs
        + [pl.BlockSpec((ctx_len, kvw), lambda b, n: (b, 0)),
           pl.BlockSpec((ctx_len, kvw), lambda b, n: (b, 0))],
        out_specs=pl.BlockSpec((rows, qw), own),
        out_shape=jax.ShapeDtypeStruct((batch * seq, qw), BF16),
        compiler_params=_params(("arbitrary", "arbitrary")),
        name="window_attn",
    )(sink, aq, ak, ak, ak, av, av, av, akx, avx)


NA_QROWS = 8
NA_Q = NA_QROWS * GRID_W
NA_KBLK = 4 * GRID_W
NA_K = 4 * NA_KBLK


NA_DR = 2 * NA_ROWS - 1


def _na_bias_tiles(rpb, rows):
    nj = rows // NA_QROWS
    kr = min(NA_ROWS, rows)
    col = jnp.arange(GRID_W)
    col_start = jnp.clip(col - NA_COLS // 2, 0, GRID_W - NA_COLS)
    cvalid = (col[None, :] >= col_start[:, None]) & (col[None, :] < col_start[:, None] + NA_COLS)
    dc = jnp.clip(col[None, :] - col[:, None], -(NA_COLS - 1), NA_COLS - 1) + (NA_COLS - 1)
    tiles = jnp.where(cvalid[None, None], rpb.astype(F32)[:, :, dc], NEG_INF)
    tiles = jnp.concatenate([tiles, jnp.full_like(tiles[:, :1], NEG_INF)], axis=1)
    idx = []
    for j in (0, min(1, nj - 1), nj - 1):
        r = j * NA_QROWS + jnp.arange(NA_QROWS)
        krow = j * NA_QROWS - NA_KBLK // GRID_W + jnp.arange(NA_K // GRID_W)
        start = jnp.clip(r - kr // 2, 0, rows - kr)
        rvalid = (krow[None, :] >= start[:, None]) & (krow[None, :] < start[:, None] + kr)
        dr = krow[None, :] - r[:, None] + (NA_ROWS - 1)
        idx.append(jnp.where(rvalid, dr, NA_DR))
    return tiles, jnp.stack(idx).reshape(-1).astype(jnp.int32)


def _na_kernel(idx_ref, q_ref, k0, k1, k2, k3, v0, v1, v2, v3, kx_ref, vx_ref, t_ref, o_ref, bias_scr,
               *, nj):
    j = pl.program_id(2)
    nkr = NA_K // GRID_W

    @pl.when((j == 0) | (j == 1) | (j == nj - 1))
    def _():
        variant = jnp.where(j == 0, 0, jnp.where(j == nj - 1, 2, 1))
        for hh in range(LANES // HEAD_DIM):
            for qr in range(NA_QROWS):
                for kr in range(nkr):
                    d = idx_ref[(variant * NA_QROWS + qr) * nkr + kr]
                    bias_scr[hh, qr * GRID_W:(qr + 1) * GRID_W, kr * GRID_W:(kr + 1) * GRID_W] = (
                        t_ref[hh, d])

    q = q_ref[...]
    k_all = jnp.concatenate([k0[...], k1[...], k2[...], k3[...]], axis=0)
    v_all = jnp.concatenate([v0[...], v1[...], v2[...], v3[...]], axis=0)
    kx = kx_ref[...]
    vx = vx_ref[...]
    qh_rows = NA_Q // 2
    k_span = 3 * NA_KBLK

    def scores(hh, half):
        lo, hi = hh * HEAD_DIM, (hh + 1) * HEAD_DIM
        q0, k0_ = half * qh_rows, half * NA_KBLK
        qh = q[q0:q0 + qh_rows, lo:hi]
        s_loc = _dot_nt(qh, k_all[k0_:k0_ + k_span, lo:hi]) + bias_scr[hh, q0:q0 + qh_rows, k0_:k0_ + k_span]
        return s_loc, _dot_nt(qh, kx[:, lo:hi])

    units = [(hh, half) for half in range(2) for hh in range(LANES // HEAD_DIM)]
    pending = scores(*units[0])
    for n, (hh, half) in enumerate(units):
        s_loc, s_ctx = pending
        if n + 1 < len(units):
            pending = scores(*units[n + 1])
        lo, hi = hh * HEAD_DIM, (hh + 1) * HEAD_DIM
        k0_ = half * NA_KBLK
        o = _softmax_pv([(s_ctx, vx[:, lo:hi]), (s_loc, v_all[k0_:k0_ + k_span, lo:hi])], None)
        o_ref[half * qh_rows:(half + 1) * qh_rows, lo:hi] = o.astype(o_ref.dtype)


def _na_attn(bq, bk, bv, bkx, bvx, tiles, tile_idx, batch, seq, ctx_len):
    nj = seq // NA_Q
    nkb = seq // NA_KBLK
    hp = B_HEADS * HEAD_DIM // LANES
    per = LANES // HEAD_DIM

    def kb(t):
        return lambda p, b, j, idx: (b * nkb + jnp.clip(2 * j - 1 + t, 0, nkb - 1), p)

    kv_specs = [pl.BlockSpec((NA_KBLK, LANES), kb(t)) for t in range(4)]
    grid_spec = pltpu.PrefetchScalarGridSpec(
        num_scalar_prefetch=1,
        grid=(hp, batch, nj),
        in_specs=[pl.BlockSpec((NA_Q, LANES), lambda p, b, j, idx: (b * nj + j, p))]
        + kv_specs + kv_specs
        + [pl.BlockSpec((ctx_len, LANES), lambda p, b, j, idx: (b, p)),
           pl.BlockSpec((ctx_len, LANES), lambda p, b, j, idx: (b, p)),
           pl.BlockSpec((per, NA_DR + 1, GRID_W, GRID_W), lambda p, b, j, idx: (p, 0, 0, 0))],
        out_specs=pl.BlockSpec((NA_Q, LANES), lambda p, b, j, idx: (b * nj + j, p)),
        scratch_shapes=[pltpu.VMEM((per, NA_Q, NA_K), F32)],
    )
    return pl.pallas_call(
        functools.partial(_na_kernel, nj=nj),
        grid_spec=grid_spec,
        out_shape=jax.ShapeDtypeStruct((batch * seq, B_HEADS * HEAD_DIM), BF16),
        compiler_params=_params(("arbitrary", "arbitrary", "arbitrary")),
        name="na_attn",
    )(tile_idx, bq, bk, bk, bk, bk, bv, bv, bv, bv, bkx, bvx, tiles)


def _ctx_ab_kernel(sink_ref, aq_ref, ak_ref, av_ref, bq_ref, bk_ref, bv_ref, oa_ref, ob_ref):
    ctx_len = aq_ref.shape[0]
    g_sz = A_HEADS // A_KV_HEADS
    aq = aq_ref[...]
    ak = ak_ref[...]
    av = av_ref[...]
    outs = []
    for g in range(A_KV_HEADS):
        lo, hi = g * HEAD_DIM, (g + 1) * HEAD_DIM
        heads = [g * g_sz + i for i in range(g_sz)]
        qs = jnp.concatenate([aq[:, h * HEAD_DIM:(h + 1) * HEAD_DIM] for h in heads], axis=0)
        sink = jnp.concatenate(
            [jnp.full((ctx_len, 1), sink_ref[h], F32) for h in heads], axis=0)
        o = _softmax_pv([(_dot_nt(qs, ak[:, lo:hi]), av[:, lo:hi])], sink)
        outs.extend(o[i * ctx_len:(i + 1) * ctx_len] for i in range(g_sz))
    oa_ref[...] = jnp.concatenate(outs, axis=1).astype(oa_ref.dtype)
    bq = bq_ref[...]
    bk = bk_ref[...]
    bv = bv_ref[...]
    outs = []
    for h in range(B_HEADS):
        lo, hi = h * HEAD_DIM, (h + 1) * HEAD_DIM
        outs.append(_softmax_pv([(_dot_nt(bq[:, lo:hi], bk[:, lo:hi]), bv[:, lo:hi])], None))
    ob_ref[...] = jnp.concatenate(outs, axis=1).astype(ob_ref.dtype)


def _ctx_ab_attn(sink, aq, ak, av, bq, bk, bv, batch, ctx_len):
    def spec(w):
        return pl.BlockSpec((ctx_len, w), lambda b: (b, 0))

    return pl.pallas_call(
        _ctx_ab_kernel,
        grid=(batch,),
        in_specs=[pl.BlockSpec(memory_space=pltpu.SMEM)]
        + [spec(a.shape[1]) for a in (aq, ak, av, bq, bk, bv)],
        out_specs=[spec(aq.shape[1]), spec(bq.shape[1])],
        out_shape=[jax.ShapeDtypeStruct(aq.shape, BF16), jax.ShapeDtypeStruct(bq.shape, BF16)],
        compiler_params=_params(("arbitrary",)),
        name="ctx_ab_attn",
    )(sink, aq, ak, av, bq, bk, bv)


def _layer_norm(t, g, b):
    mu = jnp.mean(t, -1, keepdims=True)
    c = t - mu
    var = jnp.mean(c * c, -1, keepdims=True)
    return c * lax.rsqrt(var + 1e-5) * g + b


def _post_attn_kernel(*refs, alpha, lat_tiles, with_ctx, lat_transposed):
    n_src = 6 if with_ctx else 3
    srcs = refs[:n_src]
    (w_ref, mod_ref, lng_ref, lnb_ref, wr_hi_ref, wr_lo_ref, br_ref,
     x1_ref, h2_ref, lg_ref) = refs[n_src:]

    def rows_of(o_ref):
        if o_ref.ndim == 2:
            return o_ref[...]
        return o_ref[0].astype(F32).T.astype(BF16)

    def run(o1_ref, o2_ref, x_ref):
        half = w_ref.shape[0] // 2
        y = _dot(rows_of(o1_ref), w_ref[0:half, :]) + _dot(rows_of(o2_ref), w_ref[half:, :])
        g1 = mod_ref[0, 2:3, :]
        x1 = _layer_norm(alpha * x_ref[...] + g1 * y, lng_ref[...], lnb_ref[...])
        x1_ref[...] = x1
        h2 = x1 * (1.0 + mod_ref[0, 4:5, :]) + mod_ref[0, 3:4, :]
        h_hi, h_lo = _split_bf16(h2)
        h2_ref[...] = _pack_halves(h2)
        wr_hi = wr_hi_ref[...]
        lg_ref[...] = (_dot(h_hi, wr_hi) + _dot(h_lo, wr_hi) + _dot(h_hi, wr_lo_ref[...])
                       + br_ref[...])

    if not with_ctx:
        run(*srcs)
    else:
        i = pl.program_id(0)
        pl.when(i < lat_tiles)(lambda: run(*srcs[:3]))
        pl.when(i >= lat_tiles)(lambda: run(*srcs[3:]))


def _post_attn(lat, ctx_src, w_bf, mod, gid, lng, lnb, wr_hi, wr_lo, br, n_lat, alpha, tm):
    half = w_bf.shape[0] // 2
    d = lat[2].shape[1]
    lat_tiles = n_lat // tm
    n_rows = n_lat + (ctx_src[0].shape[0] if ctx_src is not None else 0)
    lat_transposed = lat[0].ndim == 3

    def lat_map(i):
        return (jnp.minimum(i, lat_tiles - 1), 0)

    def ctx_map(i):
        return (jnp.maximum(i - lat_tiles, 0), 0)

    def src_specs(index_map):
        return [pl.BlockSpec((tm, half), index_map), pl.BlockSpec((tm, half), index_map),
                pl.BlockSpec((tm, d), index_map)]

    in_specs = src_specs(lat_map)
    if lat_transposed:
        per_batch = lat[0].shape[2] // tm
        t_spec = pl.BlockSpec((1, half, tm), lambda i: (i // per_batch, 0, i % per_batch))
        in_specs[0] = in_specs[1] = t_spec
    args = list(lat)
    if ctx_src is not None:
        in_specs += src_specs(ctx_map)
        args += list(ctx_src)

    def const(shape):
        return pl.BlockSpec(shape, lambda i: (0,) * len(shape))

    in_specs += [const((2 * half, d)),
                 pl.BlockSpec((1, 6, d), lambda i: (gid(i), 0, 0)),
                 const((1, d)), const((1, d)),
                 const((d, ROUTER_PAD)), const((d, ROUTER_PAD)), const((1, ROUTER_PAD))]
    args += [w_bf, mod, lng, lnb, wr_hi, wr_lo, br]
    widths = (d, d // 2, ROUTER_PAD)
    dtypes = (F32, jnp.uint32, F32)
    return pl.pallas_call(
        functools.partial(_post_attn_kernel, alpha=alpha, lat_tiles=lat_tiles,
                          with_ctx=ctx_src is not None, lat_transposed=lat_transposed),
        grid=(n_rows // tm,),
        in_specs=in_specs,
        out_specs=[pl.BlockSpec((tm, w), lambda i: (i, 0)) for w in widths],
        out_shape=[jax.ShapeDtypeStruct((n_rows, w), t) for w, t in zip(widths, dtypes)],
        compiler_params=_params(("arbitrary",)),
        name="post_attn",
    )(*args)


def _moe_kernel(be_ref, xs_ref, wgu_ref, wdn_ref, ys_ref, wgu_bf, wdn_bf):
    i = pl.program_id(0)
    used = i < be_ref[pl.num_programs(0)]
    prev = be_ref[jnp.maximum(i - 1, 0)]
    changed = (i == 0) | (be_ref[i] != prev)

    @pl.when(changed & used)
    def _():
        wgu_bf[...] = wgu_ref[0].astype(BF16)
        wdn_bf[...] = wdn_ref[0].astype(BF16)

    @pl.when(used)
    def _():
        de = wdn_bf.shape[0]
        subs = [slice(r, r + MOE_SUB) for r in range(0, MOE_BLOCK, MOE_SUB)]
        gus = [_dot(_unpack_halves(xs_ref[sl, :]).astype(BF16), wgu_bf[...]) for sl in subs]
        for sl, gu in zip(subs, gus):
            gate = gu[:, :de]
            up = gu[:, de:]
            act = gate / (1.0 + jnp.exp(-gate)) * up
            ys_ref[sl, :] = _pack_halves(_dot(act.astype(BF16), wdn_bf[...]))

    @pl.when(jnp.logical_not(used))
    def _():
        ys_ref[...] = jnp.zeros(ys_ref.shape, ys_ref.dtype)


def _moe_experts(block_e, xs, w_gu, w_dn):
    rows, dw = xs.shape
    d = 2 * dw
    nb = rows // MOE_BLOCK
    de = w_dn.shape[1]
    grid_spec = pltpu.PrefetchScalarGridSpec(
        num_scalar_prefetch=1,
        grid=(nb,),
        in_specs=[
            pl.BlockSpec((MOE_BLOCK, dw), lambda i, be: (i, 0)),
            pl.BlockSpec((1, d, 2 * de), lambda i, be: (be[i], 0, 0)),
            pl.BlockSpec((1, de, d), lambda i, be: (be[i], 0, 0)),
        ],
        out_specs=pl.BlockSpec((MOE_BLOCK, dw), lambda i, be: (i, 0)),
        scratch_shapes=[pltpu.VMEM((d, 2 * de), BF16), pltpu.VMEM((de, d), BF16)],
    )
    return pl.pallas_call(
        _moe_kernel,
        grid_spec=grid_spec,
        out_shape=jax.ShapeDtypeStruct((rows, dw), jnp.uint32),
        compiler_params=_params(("arbitrary",)),
        name="moe_experts",
    )(block_e, xs, w_gu, w_dn)


ROUTE_TILE = 512


def _route_kernel(lg_ref, rec_ref, cnt_ref, carry):
    i = pl.program_id(0)

    @pl.when(i == 0)
    def _():
        carry[...] = jnp.zeros(carry.shape, F32)

    lg = lg_ref[...]
    tm = lg.shape[0]
    lane = lax.broadcasted_iota(jnp.int32, lg.shape, 1)

    def first_lane(mask):
        return jnp.min(jnp.where(mask, lane, ROUTER_PAD), axis=1, keepdims=True)

    is_g = lane < N_GROUPS
    g_log = jnp.where(is_g, lg, NEG_INF)
    g_max = jnp.max(g_log, axis=1, keepdims=True)
    g_sum = jnp.sum(jnp.where(is_g, jnp.exp(lg - g_max), 0.0), axis=1, keepdims=True)
    g_val = 1.0 / g_sum
    g_idx = first_lane(g_log == g_max)
    lo = N_GROUPS + EXP_PER_GROUP * g_idx
    sel = (lane >= lo) & (lane < lo + EXP_PER_GROUP)
    e_log = jnp.where(sel, lg, NEG_INF)
    e_max = jnp.max(e_log, axis=1, keepdims=True)
    e_exp = jnp.where(sel, jnp.exp(lg - e_max), 0.0)
    e_prob = e_exp / jnp.sum(e_exp, axis=1, keepdims=True)
    p1 = jnp.where(sel, e_prob, -1.0)
    v1 = jnp.max(p1, axis=1, keepdims=True)
    i1 = first_lane(p1 == v1)
    p2 = jnp.where(lane == i1, -1.0, p1)
    v2 = jnp.max(p2, axis=1, keepdims=True)
    i2 = first_lane(p2 == v2)
    norm = g_val / (v1 + v2)

    hot1 = lane == i1
    hot2 = lane == i2
    hot = jnp.where(hot1 | hot2, 1.0, 0.0)
    rows = lax.broadcasted_iota(jnp.int32, (tm, tm), 0)
    cols = lax.broadcasted_iota(jnp.int32, (tm, tm), 1)
    before = jnp.where(cols < rows, 1.0, 0.0).astype(BF16)
    prefix = _dot(before, hot.astype(BF16)) + carry[0:1, :]
    r1 = jnp.sum(jnp.where(hot1, prefix, 0.0), axis=1, keepdims=True)
    r2 = jnp.sum(jnp.where(hot2, prefix, 0.0), axis=1, keepdims=True)
    carry[0:1, :] = carry[0:1, :] + jnp.sum(hot, axis=0, keepdims=True)

    fields = [(i1 - N_GROUPS).astype(F32), (i2 - N_GROUPS).astype(F32), r1, r2, v1 * norm, v2 * norm]
    rec = jnp.zeros(lg.shape, F32)
    for k, f in enumerate(fields):
        rec = jnp.where(lane == k, f, rec)
    rec_ref[...] = rec

    @pl.when(i == pl.num_programs(0) - 1)
    def _():
        cnt_ref[...] = carry[...]


def _route(logits, n_tok):
    rec, cnt = pl.pallas_call(
        _route_kernel,
        grid=(n_tok // ROUTE_TILE,),
        in_specs=[pl.BlockSpec((ROUTE_TILE, ROUTER_PAD), lambda i: (i, 0))],
        out_specs=[pl.BlockSpec((ROUTE_TILE, ROUTER_PAD), lambda i: (i, 0)),
                   pl.BlockSpec((8, ROUTER_PAD), lambda i: (0, 0))],
        out_shape=[jax.ShapeDtypeStruct((n_tok, ROUTER_PAD), F32),
                   jax.ShapeDtypeStruct((8, ROUTER_PAD), F32)],
        scratch_shapes=[pltpu.VMEM((8, ROUTER_PAD), F32)],
        compiler_params=_params(("arbitrary",)),
        name="route",
    )(logits)
    experts = rec[:, 0:2].astype(jnp.int32)
    rank = rec[:, 2:4].astype(jnp.int32)
    wts = rec[:, 4:6]
    counts = cnt[0, N_GROUPS:N_GROUPS + N_EXPERTS].astype(jnp.int32)

    n_asg = n_tok * TOP_K
    pcounts = ((counts + MOE_BLOCK - 1) // MOE_BLOCK) * MOE_BLOCK
    pends = jnp.cumsum(pcounts)
    pstarts = pends - pcounts
    dest = pstarts[experts] + rank
    nb = -(-n_asg // MOE_BLOCK) + N_EXPERTS
    block_start = jnp.arange(nb, dtype=jnp.int32) * MOE_BLOCK
    block_e = jnp.minimum((pends[None, :] <= block_start[:, None]).sum(1), N_EXPERTS - 1).astype(jnp.int32)
    block_e = jnp.concatenate([block_e, (pends[-1:] // MOE_BLOCK).astype(jnp.int32)])
    tok = jnp.broadcast_to(jnp.arange(n_tok, dtype=jnp.int32)[:, None], (n_tok, TOP_K))
    filler = jnp.arange(nb * MOE_BLOCK, dtype=jnp.int32) % n_tok
    row_tok = filler.at[dest.reshape(-1)].set(tok.reshape(-1), unique_indices=True)
    return wts, dest, row_tok, block_e


def _post_moe_kernel(x_ref, ya_ref, yb_ref, w_ref, mod_ref, lng_ref, lnb_ref, o_ref, *, alpha):
    w = w_ref[...]
    y = w[:, 0:1] * _unpack_halves(ya_ref[...]) + w[:, 1:2] * _unpack_halves(yb_ref[...])
    g2 = mod_ref[0, 5:6, :]
    o_ref[...] = _layer_norm(alpha * x_ref[...] + g2 * y, lng_ref[...], lnb_ref[...])


def _post_moe(x1, ya, yb, wts, mod, gid, lng, lnb, alpha, tm):
    rows, d = x1.shape
    row = pl.BlockSpec((tm, d), lambda i: (i, 0))
    packed = pl.BlockSpec((tm, d // 2), lambda i: (i, 0))
    vec = pl.BlockSpec((1, d), lambda i: (0, 0))
    return pl.pallas_call(
        functools.partial(_post_moe_kernel, alpha=alpha),
        grid=(rows // tm,),
        in_specs=[row, packed, packed,
                  pl.BlockSpec((tm, TOP_K), lambda i: (i, 0)),
                  pl.BlockSpec((1, 6, d), lambda i: (gid(i), 0, 0)),
                  vec, vec],
        out_specs=row,
        out_shape=jax.ShapeDtypeStruct((rows, d), F32),
        compiler_params=_params(("arbitrary",)),
        name="post_moe",
    )(x1, ya, yb, wts, mod, lng, lnb)


CD_COLS = dict(cq=(0, 384), ckv=(384, 640), dq=(640, 1152), dk=(1152, 1664), dv=(1664, 2176),
               kr=(2176, 2304))
C_HEAD_PAD = LANES


def _rms(t, g, eps=1e-6):
    return t * lax.rsqrt(jnp.mean(t * t, -1, keepdims=True) + eps) * g


def _proj_cd_kernel(x_ref, mod_ref, w_ref, qn_ref, wuq_ref, kvn_ref, wukv_ref,
                    cos64_ref, sin64_ref, cos32_ref, sin32_ref,
                    qc_ref, kc_ref, vct_ref, dq_ref, dk_ref, dvt_ref):
    h = _modulate(x_ref, mod_ref, 0).astype(BF16)
    r = _dot(h, w_ref[...])

    def cols(name):
        lo, hi = CD_COLS[name]
        return r[:, lo:hi]

    q = _dot(_rms(cols('cq'), qn_ref[...]).astype(BF16), wuq_ref[...])
    kv = _dot(_rms(cols('ckv'), kvn_ref[...]).astype(BF16), wukv_ref[...])
    c32, s32 = cos32_ref[...], sin32_ref[...]
    c64, s64 = cos64_ref[...], sin64_ref[...]
    q = _rope_wide(q, c32, s32, C_ROPE // 4)
    kr = _rope_tile(cols('kr'), c32, s32, C_ROPE // 4)
    dq = _rope_wide(cols('dq'), c64, s64, D_QK // 4)
    dk = _rope_wide(cols('dk'), c64, s64, D_QK // 4)
    kw = C_HEADS * C_HEAD_PAD
    qc_ref[0] = (q * ((C_NOPE + C_ROPE) ** -0.5 * LOG2E)).astype(BF16)
    kc_ref[0] = (kv[:, :kw] + jnp.concatenate([kr] * C_HEADS, axis=1)).astype(BF16)
    dq_ref[0] = (dq * (D_QK ** -0.5 * LOG2E)).astype(BF16)
    dk_ref[0] = dk.astype(BF16)
    _store_vt(vct_ref, kv[:, kw:], C_HEADS, C_V)
    _store_vt(dvt_ref, cols('dv'), D_HEADS, D_V)


def _store_vt(vt_ref, v, heads, dv):
    tm = v.shape[0]
    v_t = v.T.astype(BF16)
    row = lax.broadcasted_iota(jnp.int32, (ONES_ROWS, tm), 0)
    extra = jnp.where(row == 0, 1.0, 0.0).astype(BF16)
    for h in range(heads):
        vt_ref[0, h, 0, 0:dv, :] = v_t[h * dv:(h + 1) * dv]
        vt_ref[0, h, 0, dv:dv + ONES_ROWS, :] = extra


def _proj_cd(xall, mod, w_bf, qn, wuq, kvn, wukv, tabs, batch, seq, ctx_len, tm, tk):
    d = xall.shape[1]
    lat_t, ctx_t = seq // tm, ctx_len // tm
    n_keys = seq + ctx_len
    per_chunk = tk // tm
    n_chunks = n_keys // tk
    widths = (C_HEADS * C_HEAD_PAD, C_HEADS * C_HEAD_PAD, 2 * D_HEADS * D_QK, 2 * D_HEADS * D_QK)

    def src(b, t):
        return (jnp.where(t < lat_t, b * lat_t + t, batch * lat_t + b * ctx_t + t - lat_t), 0)

    def full(a):
        return pl.BlockSpec(a.shape, lambda b, t: (0,) * a.ndim)

    def row_spec(w):
        return pl.BlockSpec((1, tm, w), lambda b, t: (b, t, 0))

    def row_shape(w):
        return jax.ShapeDtypeStruct((batch, n_keys, w), BF16)

    def vt_spec(heads, dv):
        return pl.BlockSpec((1, heads, 1, dv + ONES_ROWS, tm),
                            lambda b, t: (b, 0, t // per_chunk, 0, t % per_chunk))

    def vt_shape(heads, dv):
        return jax.ShapeDtypeStruct((batch, heads, n_chunks, dv + ONES_ROWS, tk), BF16)

    tab = pl.BlockSpec((tm, LANES), lambda b, t: (t, 0))
    return pl.pallas_call(
        _proj_cd_kernel,
        grid=(batch, lat_t + ctx_t),
        in_specs=[pl.BlockSpec((tm, d), src),
                  pl.BlockSpec((1, 6, d), lambda b, t: (jnp.where(t < lat_t, b, batch), 0, 0)),
                  full(w_bf), full(qn), full(wuq), full(kvn), full(wukv), tab, tab, tab, tab],
        out_specs=[row_spec(widths[0]), row_spec(widths[1]), vt_spec(C_HEADS, C_V),
                   row_spec(widths[2]), row_spec(widths[3]), vt_spec(D_HEADS, D_V)],
        out_shape=[row_shape(widths[0]), row_shape(widths[1]), vt_shape(C_HEADS, C_V),
                   row_shape(widths[2]), row_shape(widths[3]), vt_shape(D_HEADS, D_V)],
        compiler_params=_params(("arbitrary", "arbitrary")),
        name="proj_cd",
    )(xall, mod, w_bf, qn, wuq, kvn, wukv, *tabs)


DENSE_SHIFT = 80.0
DENSE_L_MIN = 2.0 ** -60
DENSE_L_MAX = 2.0 ** 120
DENSE_TQ = 1024
DENSE_TK_MAX = 1280
DENSE_UNROLL_MAX = 13
MXU_DEPTH = 256
PROJ_CD_TILE = 256
ONES_ROWS = 16


def _sq_norm_row(x):
    ones = jnp.ones((8, x.shape[1]), BF16)
    return _dot_nt(ones, (x * x).astype(BF16))[0:1] * 1.02


def _key_norm_max(kchunk, n_chunks, masks):
    def body(j, mx):
        kf = kchunk(j).astype(F32)
        ksq = kf * kf
        out = []
        for msk, cur in zip(masks, mx):
            part = ksq if msk is None else jnp.where(msk, ksq, 0.0)
            rn = jnp.sum(part, axis=1, keepdims=True)
            out.append(jnp.maximum(cur, jnp.max(rn, axis=0, keepdims=True)))
        return tuple(out)

    mx = lax.fori_loop(0, n_chunks, body, tuple(jnp.zeros((1, 1), F32) for _ in masks))
    return [jnp.sqrt(v) * 1.01 for v in mx]


def _safe_online(q, kchunk, vt_chunk, n_chunks, m_scr, l_scr, acc):
    m_scr[...] = jnp.full(m_scr.shape, NEG_INF, F32)
    l_scr[...] = jnp.zeros(l_scr.shape, F32)
    acc[...] = jnp.zeros(acc.shape, F32)

    def body(j, carry):
        s = _dot_nt(kchunk(j), q)
        m_prev = m_scr[...]
        m_new = jnp.maximum(m_prev, s.max(0, keepdims=True))
        alpha = jnp.exp2(m_prev - m_new)
        p = jnp.exp2(s - m_new)
        l_scr[...] = alpha * l_scr[...] + p.sum(0, keepdims=True)
        acc[...] = alpha * acc[...] + _dot(vt_chunk(j), p.astype(BF16))
        m_scr[...] = m_new
        return carry

    lax.fori_loop(0, n_chunks, body, 0)


def _denominators_ok(*ls):
    ok = None
    for l in ls:
        cur = (l > DENSE_L_MIN) & (l < DENSE_L_MAX)
        ok = cur if ok is None else ok & cur
    return jnp.max(jnp.where(ok, 0.0, 1.0)) == 0.0


MLA_HEADS_PER_STEP = 2


def _mla_kernel(q_ref, k_ref, vt_ref, o_ref, kmax_scr, acc_scr, m_scr, l_scr,
                *, n_chunks, tk, unroll, depth):
    hp = acc_scr.shape[0]

    def kchunk(j, h):
        return k_ref[0, pl.ds(pl.multiple_of(j * tk, tk), tk), h * C_HEAD_PAD:(h + 1) * C_HEAD_PAD]

    @pl.when(pl.program_id(2) == 0)
    def _():
        for h in range(hp):
            (kmax,) = _key_norm_max(lambda j: kchunk(j, h), n_chunks, [None])
            kmax_scr[h:h + 1, :] = jnp.broadcast_to(kmax, (1, LANES))

    qs = [q_ref[0, :, h * C_HEAD_PAD:(h + 1) * C_HEAD_PAD] for h in range(hp)]
    ms = [jnp.sqrt(_sq_norm_row(qs[h].astype(F32))) * kmax_scr[h:h + 1, 0:1] - DENSE_SHIFT
          for h in range(hp)]
    acc_scr[...] = jnp.zeros(acc_scr.shape, F32)

    def scores(j):
        return [_dot_nt(kchunk(j, h), qs[h]) for h in range(hp)]

    def body(it, carry):
        j0 = it * unroll
        pend = [scores(j0 + u) for u in range(min(depth, unroll))]
        accs = [acc_scr[h] for h in range(hp)]
        for u in range(unroll):
            ss = pend.pop(0)
            if u + depth < unroll:
                pend.append(scores(j0 + u + depth))
            for h in range(hp):
                accs[h] = accs[h] + _dot(vt_ref[0, h, j0 + u], jnp.exp2(ss[h] - ms[h]).astype(BF16))
        for h in range(hp):
            acc_scr[h] = accs[h]
        return carry

    lax.fori_loop(0, n_chunks // unroll, body, 0)
    ls = [acc_scr[h, C_V:C_V + 1, :] for h in range(hp)]
    for h in range(hp):
        o_ref[0, h * C_V:(h + 1) * C_V, :] = (acc_scr[h, 0:C_V, :] / ls[h]).astype(o_ref.dtype)

    @pl.when(jnp.logical_not(_denominators_ok(*ls)))
    def _():
        for h in range(hp):
            _safe_online(qs[h], lambda j: kchunk(j, h), lambda j: vt_ref[0, h, j], n_chunks,
                         m_scr, l_scr, acc_scr.at[h])
            o_ref[0, h * C_V:(h + 1) * C_V, :] = (acc_scr[h, 0:C_V, :] / l_scr[...]).astype(o_ref.dtype)


def _mla_attn(qc, kc, vt, batch, seq, tq, tk, unroll, depth):
    n_keys = kc.shape[1]
    nq, n_chunks = seq // tq, n_keys // tk
    rows = C_V + ONES_ROWS
    hp = MLA_HEADS_PER_STEP
    return pl.pallas_call(
        functools.partial(_mla_kernel, n_chunks=n_chunks, tk=tk, unroll=unroll, depth=depth),
        grid=(batch, C_HEADS // hp, nq),
        in_specs=[
            pl.BlockSpec((1, tq, hp * C_HEAD_PAD), lambda b, h, i: (b, i, h)),
            pl.BlockSpec((1, n_keys, hp * C_HEAD_PAD), lambda b, h, i: (b, 0, h)),
            pl.BlockSpec((1, hp, n_chunks, rows, tk), lambda b, h, i: (b, h, 0, 0, 0)),
        ],
        out_specs=pl.BlockSpec((1, hp * C_V, tq), lambda b, h, i: (b, h, i)),
        out_shape=jax.ShapeDtypeStruct((batch, C_HEADS * C_V, seq), BF16),
        scratch_shapes=[pltpu.VMEM((8, LANES), F32), pltpu.VMEM((hp, rows, tq), F32),
                        pltpu.VMEM((1, tq), F32), pltpu.VMEM((1, tq), F32)],
        compiler_params=_params(("arbitrary", "arbitrary", "arbitrary")),
        name="mla_attn",
    )(qc, kc, vt)


def _diff_kernel(lam_ref, q_ref, k_ref, vt_ref, subln_ref, o_ref, kmax_scr, acc1_scr, acc2_scr,
                 m_scr, l_scr, *, n_chunks, tk, unroll, depth, out_scale):
    def kchunk(j):
        return k_ref[0, pl.ds(pl.multiple_of(j * tk, tk), tk), :]

    @pl.when(pl.program_id(2) == 0)
    def _():
        klane = lax.broadcasted_iota(jnp.int32, (tk, 2 * D_QK), 1)
        k1, k2 = _key_norm_max(kchunk, n_chunks, [klane < D_QK, klane >= D_QK])
        kmax_scr[0:1, :] = jnp.broadcast_to(k1, (1, LANES))
        kmax_scr[1:2, :] = jnp.broadcast_to(k2, (1, LANES))

    q = q_ref[0]
    qlane = lax.broadcasted_iota(jnp.int32, q.shape, 1)
    zero = jnp.zeros_like(q)
    q1 = jnp.where(qlane < D_QK, q, zero)
    q2 = jnp.where(qlane >= D_QK, q, zero)
    m1 = jnp.sqrt(_sq_norm_row(q1.astype(F32))) * kmax_scr[0:1, 0:1] - DENSE_SHIFT
    m2 = jnp.sqrt(_sq_norm_row(q2.astype(F32))) * kmax_scr[1:2, 0:1] - DENSE_SHIFT
    acc1_scr[...] = jnp.zeros(acc1_scr.shape, F32)
    acc2_scr[...] = jnp.zeros(acc2_scr.shape, F32)

    def scores(j):
        k = kchunk(j)
        return _dot_nt(k, q1), _dot_nt(k, q2)

    def body(it, carry):
        j0 = it * unroll
        pend = [scores(j0 + u) for u in range(min(depth, unroll))]
        a1 = acc1_scr[...]
        a2 = acc2_scr[...]
        for u in range(unroll):
            s1, s2 = pend.pop(0)
            if u + depth < unroll:
                pend.append(scores(j0 + u + depth))
            vt = vt_ref[0, 0, j0 + u]
            a1 = a1 + _dot(vt, jnp.exp2(s1 - m1).astype(BF16))
            a2 = a2 + _dot(vt, jnp.exp2(s2 - m2).astype(BF16))
        acc1_scr[...] = a1
        acc2_scr[...] = a2
        return carry

    lax.fori_loop(0, n_chunks // unroll, body, 0)

    def finish(o1, o2):
        o = o1 - lam_ref[0] * o2
        o = o * lax.rsqrt(jnp.mean(o * o, 0, keepdims=True) + 1e-6) * subln_ref[...] * out_scale
        o_ref[0] = o.astype(o_ref.dtype)

    l1 = acc1_scr[D_V:D_V + 1, :]
    l2 = acc2_scr[D_V:D_V + 1, :]
    finish(acc1_scr[0:D_V, :] / l1, acc2_scr[0:D_V, :] / l2)

    @pl.when(jnp.logical_not(_denominators_ok(l1, l2)))
    def _():
        def vt_chunk(j):
            return vt_ref[0, 0, j]

        _safe_online(q1, kchunk, vt_chunk, n_chunks, m_scr, l_scr, acc1_scr)
        o1 = acc1_scr[0:D_V, :] / l_scr[...]
        _safe_online(q2, kchunk, vt_chunk, n_chunks, m_scr, l_scr, acc2_scr)
        finish(o1, acc2_scr[0:D_V, :] / l_scr[...])


def _diff_attn(lam, dq, dk, vt, subln_col, out_scale, batch, seq, tq, tk, unroll, depth):
    n_keys = dk.shape[1]
    nq, n_chunks = seq // tq, n_keys // tk
    rows = D_V + ONES_ROWS
    pair = 2 * D_QK
    return pl.pallas_call(
        functools.partial(_diff_kernel, n_chunks=n_chunks, tk=tk, unroll=unroll, depth=depth,
                          out_scale=out_scale),
        grid=(batch, D_HEADS, nq),
        in_specs=[
            pl.BlockSpec(memory_space=pltpu.SMEM),
            pl.BlockSpec((1, tq, pair), lambda b, h, i: (b, i, h)),
            pl.BlockSpec((1, n_keys, pair), lambda b, h, i: (b, 0, h)),
            pl.BlockSpec((1, 1, n_chunks, rows, tk), lambda b, h, i: (b, h, 0, 0, 0)),
            pl.BlockSpec((D_V, 1), lambda b, h, i: (0, 0)),
        ],
        out_specs=pl.BlockSpec((1, D_V, tq), lambda b, h, i: (b, h, i)),
        out_shape=jax.ShapeDtypeStruct((batch, D_HEADS * D_V, seq), BF16),
        scratch_shapes=[pltpu.VMEM((8, LANES), F32), pltpu.VMEM((rows, tq), F32),
                        pltpu.VMEM((rows, tq), F32), pltpu.VMEM((1, tq), F32), pltpu.VMEM((1, tq), F32)],
        compiler_params=_params(("arbitrary", "arbitrary", "arbitrary")),
        name="diff_attn",
    )(lam, dq, dk, vt, subln_col)


def _rope_tables(n_tok, dim, lane_lo):
    t = jnp.arange(n_tok)
    pos_r = (t // GRID_W).astype(F32)
    pos_c = (t % GRID_W).astype(F32)
    quarter = dim // 4
    inv = ROPE_BASE ** (-jnp.arange(quarter, dtype=F32) / quarter)
    ang_r = pos_r[:, None] * inv
    ang_c = pos_c[:, None] * inv
    ang = jnp.concatenate([ang_r, ang_r, ang_c, ang_c], -1)
    sign = jnp.tile(jnp.concatenate([-jnp.ones(quarter), jnp.ones(quarter)]), 2).astype(F32)
    cos, sin = jnp.cos(ang), jnp.sin(ang) * sign
    reps = (LANES - lane_lo) // dim
    cos = jnp.concatenate([jnp.ones((n_tok, lane_lo), F32)] + [cos] * reps, axis=1)
    sin = jnp.concatenate([jnp.zeros((n_tok, lane_lo), F32)] + [sin] * reps, axis=1)
    return cos, sin


def _cd_weights(w_in, w_uq, w_ukv):
    d = w_in.shape[0]
    s0 = C_Q_RANK
    s1 = s0 + C_KV_RANK
    s2 = s1 + C_ROPE
    s3 = s2 + 2 * D_HEADS * D_QK
    s4 = s3 + 2 * D_HEADS * D_QK
    kr = jnp.zeros((d, LANES), F32).at[:, C_NOPE:C_NOPE + C_ROPE].set(w_in[:, s1:s2])
    w_in_p = jnp.concatenate([w_in[:, :s1], w_in[:, s2:s3], w_in[:, s3:s4], w_in[:, s4:], kr], axis=1)
    dqk = C_NOPE + C_ROPE
    wq = w_uq.reshape(C_Q_RANK, C_HEADS, dqk)
    wq = jnp.pad(wq, ((0, 0), (0, 0), (0, C_HEAD_PAD - dqk))).reshape(C_Q_RANK, C_HEADS * C_HEAD_PAD)
    wkv = w_ukv.reshape(C_KV_RANK, C_HEADS, C_NOPE + C_V)
    wk = jnp.pad(wkv[:, :, :C_NOPE], ((0, 0), (0, 0), (0, C_HEAD_PAD - C_NOPE)))
    wk = wk.reshape(C_KV_RANK, C_HEADS * C_HEAD_PAD)
    wv = wkv[:, :, C_NOPE:].reshape(C_KV_RANK, C_HEADS * C_V)
    return w_in_p.astype(BF16), wq.astype(BF16), jnp.concatenate([wk, wv], axis=1).astype(BF16)


def _moe_layer(h2, logits, x1, w_gu, w_dn, mod, gid, lng, lnb, alpha, tm):
    n_tok = h2.shape[0]
    wts, dest, row_tok, block_e = _route(logits, n_tok)
    xs = jnp.take(h2, row_tok, axis=0, mode='clip')
    ys = _moe_experts(block_e, xs, w_gu, w_dn)
    ya = jnp.take(ys, dest[:, 0], axis=0, mode='clip')
    yb = jnp.take(ys, dest[:, 1], axis=0, mode='clip')
    return _post_moe(x1, ya, yb, wts.astype(F32), mod, gid, lng, lnb, alpha, tm)


def kernel(x, c, ctx, c_ctx, w_ada, b_ada, ln_g, ln_b, ab_w_in, a_sink, b_rpb, ab_w_out, cd_w_in, c_q_norm, c_w_uq, c_kv_norm, c_w_ukv, d_lambda, d_subln, cd_w_out, w_group, b_group, w_exp_router, b_exp_router, w_gate_up, w_down):
    batch, seq, d = x.shape
    ctx_len = ctx.shape[1]
    depth = w_ada.shape[0]
    alpha = (2 * depth) ** 0.25
    n_lat = batch * seq
    n_ctx = batch * ctx_len
    tm = 512
    tq = DENSE_TQ
    assert depth == 2 and seq % NA_Q == 0 and seq % tq == 0 and n_ctx % tm == 0
    assert seq % PROJ_CD_TILE == 0 and ctx_len % PROJ_CD_TILE == 0
    assert seq % (WIN_GROUP * A_BLOCK) == 0 and seq // NA_Q >= 2
    key_tiles = (seq + ctx_len) // MXU_DEPTH
    tk = MXU_DEPTH * max(u for u in range(1, DENSE_TK_MAX // MXU_DEPTH + 1) if key_tiles % u == 0)
    n_chunks = (seq + ctx_len) // tk

    def unroll_for(cap):
        return max(u for u in range(1, cap + 1) if n_chunks % u == 0)
    lat_tiles = seq // tm

    def gid_lat(i):
        return i // lat_tiles

    def gid_ctx(i):
        return batch

    def gid_all(i):
        return jnp.minimum(i // lat_tiles, batch)

    c_all = jnp.zeros((8, d), F32).at[:batch].set(c).at[batch].set(c_ctx)
    mod = _ada(c_all, w_ada, b_ada).reshape(depth, 8, 6, d)

    cos64, sin64 = _rope_tables(seq, HEAD_DIM, 0)
    cos32, sin32 = _rope_tables(seq, C_ROPE, C_NOPE)

    def router_weights(l):
        wr = jnp.zeros((d, ROUTER_PAD), F32)
        wr = wr.at[:, :N_GROUPS].set(w_group[l]).at[:, N_GROUPS:N_GROUPS + N_EXPERTS].set(w_exp_router[l])
        br = jnp.zeros((1, ROUTER_PAD), F32)
        br = br.at[0, :N_GROUPS].set(b_group[l]).at[0, N_GROUPS:N_GROUPS + N_EXPERTS].set(b_exp_router[l])
        hi = wr.astype(BF16)
        lo = (wr - hi.astype(F32)).astype(BF16)
        return hi, lo, br

    x2d = x.reshape(n_lat, d)
    ctx2d = ctx.reshape(n_ctx, d)

    l = 0
    w_in_bf = ab_w_in[0].astype(BF16)
    aq, ak, av, bq, bk, bv = _proj_ab(x2d, 0, n_lat, mod[l], gid_lat, w_in_bf, cos64, sin64,
                                      lat_tiles, True, tm)
    aqx, akx, avx, bqx, bkx, bvx = _proj_ab(ctx2d, 0, n_ctx, mod[l], gid_ctx, w_in_bf, cos64, sin64,
                                            1, False, tm)
    sink = a_sink[0].astype(F32)
    oa = _window_attn(sink, aq, ak, av, akx, avx, batch, seq, ctx_len)
    tiles, tile_idx = _na_bias_tiles(b_rpb[0], seq // GRID_W)
    ob = _na_attn(bq, bk, bv, bkx, bvx, tiles, tile_idx, batch, seq, ctx_len)
    oax, obx = _ctx_ab_attn(sink, aqx, akx, avx, bqx, bkx, bvx, batch, ctx_len)

    n_all = n_lat + n_ctx
    w_out_bf = ab_w_out[0].astype(BF16)
    lng, lnb = ln_g[l, 0][None], ln_b[l, 0][None]
    wr_hi, wr_lo, br = router_weights(l)
    x1, h2, logits = _post_attn((oa, ob, x2d), (oax, obx, ctx2d), w_out_bf, mod[l], gid_all,
                                lng, lnb, wr_hi, wr_lo, br, n_lat, alpha, tm)
    xall = _moe_layer(h2, logits, x1, w_gate_up[l], w_down[l], mod[l], gid_all,
                      ln_g[l, 1][None], ln_b[l, 1][None], alpha, tm)

    l = 1
    lam_init = 0.8 - 0.6 * math.exp(-0.3 * l)
    lp = d_lambda[0].astype(F32)
    lam = (jnp.exp(jnp.sum(lp[0] * lp[1])) - jnp.exp(jnp.sum(lp[2] * lp[3])) + lam_init).reshape(1)
    w_in_p, wuq_p, wukv_p = _cd_weights(cd_w_in[0], c_w_uq[0], c_w_ukv[0])
    qn, kvn = c_q_norm[0][None].astype(F32), c_kv_norm[0][None].astype(F32)
    def with_ctx_identity(cos, sin):
        return (jnp.concatenate([cos, jnp.ones((ctx_len, LANES), F32)]),
                jnp.concatenate([sin, jnp.zeros((ctx_len, LANES), F32)]))

    tabs = with_ctx_identity(cos64, sin64) + with_ctx_identity(cos32, sin32)
    qc, kc, vct, dq, dk, dvt = _proj_cd(xall, mod[l], w_in_p, qn, wuq_p, kvn, wukv_p, tabs,
                                        batch, seq, ctx_len, PROJ_CD_TILE, tk)
    unroll = unroll_for(DENSE_UNROLL_MAX)
    oc_t = _mla_attn(qc, kc, vct, batch, seq, tq, tk, unroll, 1)
    od_t = _diff_attn(lam, dq, dk, dvt, d_subln[0].astype(F32).reshape(D_V, 1), 1.0 - lam_init,
                      batch, seq, tq, tk, unroll, 1)
    wr_hi, wr_lo, br = router_weights(l)
    x1, h2, logits = _post_attn((oc_t, od_t, xall), None, cd_w_out[0].astype(BF16), mod[l], gid_lat,
                                ln_g[l, 0][None], ln_b[l, 0][None], wr_hi, wr_lo, br,
                                n_lat, alpha, tm)
    out = _moe_layer(h2, logits, x1, w_gate_up[l], w_down[l], mod[l], gid_lat,
                     ln_g[l, 1][None], ln_b[l, 1][None], alpha, tm)
    return out.reshape(batch, seq, d)
```

```python
import functools
import math

import jax
import jax.numpy as jnp
from jax import lax
from jax.experimental import pallas as pl
from jax.experimental.pallas import tpu as pltpu

F32 = jnp.float32
BF16 = jnp.bfloat16

GRID_W = 64
HEAD_DIM = 64
ROPE_BASE = 10000.0
NEG_INF = -1e30
LOG2E = 1.4426950408889634

A_HEADS = 8
A_KV_HEADS = 2
A_WINDOW = 128
A_BLOCK = 128
B_HEADS = 8
NA_ROWS = 8
NA_COLS = 16
C_HEADS = 8
C_Q_RANK = 384
C_KV_RANK = 256
C_NOPE = 64
C_ROPE = 32
C_V = 64
D_HEADS = 4
D_QK = 64
D_V = 128
N_GROUPS = 4
EXP_PER_GROUP = 8
N_EXPERTS = N_GROUPS * EXP_PER_GROUP
TOP_K = 2
MOE_BLOCK = 512
MOE_SUB = 256
ROUTER_PAD = 128

LANES = 128
VMEM_LIMIT = 56 * 1024 * 1024


def _params(sem):
    return pltpu.CompilerParams(dimension_semantics=sem, vmem_limit_bytes=VMEM_LIMIT)


def _dot(a, b):
    return jnp.dot(a, b, preferred_element_type=F32)


def _dot_nt(a, b):
    return lax.dot_general(a, b, (((1,), (1,)), ((), ())), preferred_element_type=F32)


def _split_bf16(x):
    hi = x.astype(BF16)
    lo = (x - hi.astype(F32)).astype(BF16)
    return hi, lo


HI_HALF = 0xFFFF0000


def _pack_halves(x):
    w = x.shape[1] // 2
    lo = lax.bitcast_convert_type(x[:, :w].astype(BF16).astype(F32), jnp.uint32)
    hi = lax.bitcast_convert_type(x[:, w:].astype(BF16).astype(F32), jnp.uint32)
    return (lo >> 16) | (hi & jnp.uint32(HI_HALF))


def _unpack_halves(u):
    lo = lax.bitcast_convert_type(u << 16, F32)
    hi = lax.bitcast_convert_type(u & jnp.uint32(HI_HALF), F32)
    return jnp.concatenate([lo, hi], axis=1)


def _ada_kernel(c_ref, w_ref, b_ref, o_ref):
    c = c_ref[...]
    a = c / (1.0 + jnp.exp(-c))
    a_hi, a_lo = _split_bf16(a)
    w_hi, w_lo = _split_bf16(w_ref[0])
    acc = _dot(a_hi, w_hi) + _dot(a_lo, w_hi) + _dot(a_hi, w_lo)
    o_ref[0] = acc + b_ref[0]


def _ada(c_all, w_ada, b_ada):
    depth, d, n = w_ada.shape
    tn = 1536
    return pl.pallas_call(
        _ada_kernel,
        grid=(depth, n // tn),
        in_specs=[
            pl.BlockSpec((8, d), lambda l, j: (0, 0)),
            pl.BlockSpec((1, d, tn), lambda l, j: (l, 0, j)),
            pl.BlockSpec((1, 1, tn), lambda l, j: (l, 0, j)),
        ],
        out_specs=pl.BlockSpec((1, 8, tn), lambda l, j: (l, 0, j)),
        out_shape=jax.ShapeDtypeStruct((depth, 8, n), F32),
        compiler_params=_params(("arbitrary", "arbitrary")),
        name="ada",
    )(c_all, w_ada, b_ada.reshape(depth, 1, n))


def _rope_tile(t, cos, sin_signed, q):
    lane = lax.broadcasted_iota(jnp.int32, t.shape, 1)
    first = (lane & (2 * q - 1)) < q
    up = pltpu.roll(t, LANES - q, 1)
    dn = pltpu.roll(t, q, 1)
    return t * cos + jnp.where(first, up, dn) * sin_signed


def _rope_wide(t, cos, sin_signed, q):
    n = t.shape[1] // LANES
    return jnp.concatenate(
        [_rope_tile(t[:, i * LANES:(i + 1) * LANES], cos, sin_signed, q) for i in range(n)], axis=1)


def _modulate(x_ref, mod_ref, which):
    sh = mod_ref[0, 3 * which:3 * which + 1, :]
    sc = mod_ref[0, 3 * which + 1:3 * which + 2, :]
    return x_ref[...] * (1.0 + sc) + sh


def _proj_ab_kernel(x_ref, mod_ref, w_ref, cos_ref, sin_ref,
                    aq_ref, ak_ref, av_ref, bq_ref, bk_ref, bv_ref, *, rope):
    h = _modulate(x_ref, mod_ref, 0).astype(BF16)
    r = _dot(h, w_ref[...])
    aq = r[:, 0:512]
    ak = r[:, 512:640]
    if rope:
        cos = cos_ref[...]
        sin = sin_ref[...]
        aq = _rope_wide(aq, cos, sin, 16)
        ak = _rope_wide(ak, cos, sin, 16)
    scale = HEAD_DIM ** -0.5
    aq_ref[...] = (aq * scale).astype(BF16)
    ak_ref[...] = ak.astype(BF16)
    av_ref[...] = r[:, 640:768].astype(BF16)
    bq_ref[...] = (r[:, 768:1280] * scale).astype(BF16)
    bk_ref[...] = r[:, 1280:1792].astype(BF16)
    bv_ref[...] = r[:, 1792:2304].astype(BF16)


def _proj_ab(x2d, row0, nrows, mod, gid, w_bf, cos, sin, pos_blocks, rope, tm):
    d = x2d.shape[1]
    n_in = w_bf.shape[1]
    b0 = row0 // tm
    widths = (512, 128, 128, 512, 512, 512)
    return pl.pallas_call(
        functools.partial(_proj_ab_kernel, rope=rope),
        grid=(nrows // tm,),
        in_specs=[
            pl.BlockSpec((tm, d), lambda i: (b0 + i, 0)),
            pl.BlockSpec((1, 6, d), lambda i: (gid(i), 0, 0)),
            pl.BlockSpec((d, n_in), lambda i: (0, 0)),
            pl.BlockSpec((tm, LANES), lambda i: (i % pos_blocks, 0)),
            pl.BlockSpec((tm, LANES), lambda i: (i % pos_blocks, 0)),
        ],
        out_specs=[pl.BlockSpec((tm, w), lambda i: (i, 0)) for w in widths],
        out_shape=[jax.ShapeDtypeStruct((nrows, w), BF16) for w in widths],
        compiler_params=_params(("arbitrary",)),
        name="proj_ab",
    )(x2d, mod, w_bf, cos, sin)


def _softmax_pv(parts, sink_col):
    m = parts[0][0].max(-1, keepdims=True)
    for s, _ in parts[1:]:
        m = jnp.maximum(m, s.max(-1, keepdims=True))
    if sink_col is not None:
        m = jnp.maximum(m, sink_col)
    denom = None if sink_col is None else jnp.exp(sink_col - m)
    o = None
    for s, v in parts:
        e = jnp.exp(s - m)
        d = e.sum(-1, keepdims=True)
        denom = d if denom is None else denom + d
        pv = _dot(e.astype(BF16), v)
        o = pv if o is None else o + pv
    return o / denom


def _win_kernel(sink_ref, q_ref, kp_ref, kc_ref, kn_ref, vp_ref, vc_ref, vn_ref,
                kx_ref, vx_ref, o_ref, *, seq):
    step = pl.program_id(1)
    blk = A_BLOCK
    g_sz = A_HEADS // A_KV_HEADS
    qi = lax.broadcasted_iota(jnp.int32, (blk, 3 * blk), 0)
    kj = lax.broadcasted_iota(jnp.int32, (blk, 3 * blk), 1)
    in_window = jnp.abs(kj - blk - qi) <= A_WINDOW
    k_ext = jnp.concatenate([kp_ref[...], kc_ref[...], kn_ref[...]], axis=0)
    v_ext = jnp.concatenate([vp_ref[...], vc_ref[...], vn_ref[...]], axis=0)
    kx = kx_ref[...]
    vx = vx_ref[...]

    def scores(t, g):
        lo, hi = g * HEAD_DIM, (g + 1) * HEAD_DIM
        kpos = (step * WIN_GROUP + t) * blk + kj - blk
        valid = in_window & (kpos >= 0) & (kpos < seq)
        valid = jnp.concatenate([valid] * g_sz, axis=0)
        q = q_ref[t * blk:(t + 1) * blk, :]
        heads = [g * g_sz + i for i in range(g_sz)]
        qs = jnp.concatenate([q[:, h * HEAD_DIM:(h + 1) * HEAD_DIM] for h in heads], axis=0)
        s_loc = jnp.where(valid, _dot_nt(qs, k_ext[t * blk:(t + 3) * blk, lo:hi]), NEG_INF)
        return s_loc, _dot_nt(qs, kx[:, lo:hi])

    units = [(t, g) for t in range(WIN_GROUP) for g in range(A_KV_HEADS)]
    pending = scores(*units[0])
    outs = []
    for n, (t, g) in enumerate(units):
        s_loc, s_ctx = pending
        if n + 1 < len(units):
            pending = scores(*units[n + 1])
        lo, hi = g * HEAD_DIM, (g + 1) * HEAD_DIM
        sink = jnp.concatenate(
            [jnp.full((blk, 1), sink_ref[g * g_sz + i], F32) for i in range(g_sz)], axis=0)
        o = _softmax_pv([(s_ctx, vx[:, lo:hi]), (s_loc, v_ext[t * blk:(t + 3) * blk, lo:hi])], sink)
        outs.extend(o[i * blk:(i + 1) * blk] for i in range(g_sz))
        if g == A_KV_HEADS - 1:
            o_ref[t * blk:(t + 1) * blk, :] = jnp.concatenate(outs, axis=1).astype(o_ref.dtype)
            outs = []


WIN_GROUP = 8


def _window_attn(sink, aq, ak, av, akx, avx, batch, seq, ctx_len):
    nblk = seq // A_BLOCK
    nstep = nblk // WIN_GROUP
    kvw = A_KV_HEADS * HEAD_DIM
    qw = A_HEADS * HEAD_DIM
    rows = WIN_GROUP * A_BLOCK

    def edge(delta):
        return lambda b, n: (b * nblk + jnp.clip(n * WIN_GROUP + delta, 0, nblk - 1), 0)

    def own(b, n):
        return (b * nstep + n, 0)

    kv_specs = [pl.BlockSpec((A_BLOCK, kvw), edge(-1)), pl.BlockSpec((rows, kvw), own),
                pl.BlockSpec((A_BLOCK, kvw), edge(WIN_GROUP))]
    return pl.pallas_call(
        functools.partial(_win_kernel, seq=seq),
        grid=(batch, nstep),
        in_specs=[pl.BlockSpec(memory_space=pltpu.SMEM), pl.BlockSpec((rows, qw), own)]
        + kv_specs + kv_specs
        + [pl.BlockSpec((ctx_len, kvw), lambda b, n: (b, 0)),
           pl.BlockSpec((ctx_len, kvw), lambda b, n: (b, 0))],
        out_specs=pl.BlockSpec((rows, qw), own),
        out_shape=jax.ShapeDtypeStruct((batch * seq, qw), BF16),
        compiler_params=_params(("arbitrary", "arbitrary")),
        name="window_attn",
    )(sink, aq, ak, ak, ak, av, av, av, akx, avx)


NA_QROWS = 8
NA_Q = NA_QROWS * GRID_W
NA_KBLK = 4 * GRID_W
NA_K = 4 * NA_KBLK


NA_DR = 2 * NA_ROWS - 1


def _na_bias_tiles(rpb, rows):
    nj = rows // NA_QROWS
    kr = min(NA_ROWS, rows)
    col = jnp.arange(GRID_W)
    col_start = jnp.clip(col - NA_COLS // 2, 0, GRID_W - NA_COLS)
    cvalid = (col[None, :] >= col_start[:, None]) & (col[None, :] < col_start[:, None] + NA_COLS)
    dc = jnp.clip(col[None, :] - col[:, None], -(NA_COLS - 1), NA_COLS - 1) + (NA_COLS - 1)
    tiles = jnp.where(cvalid[None, None], rpb.astype(F32)[:, :, dc], NEG_INF)
    tiles = jnp.concatenate([tiles, jnp.full_like(tiles[:, :1], NEG_INF)], axis=1)
    idx = []
    for j in (0, min(1, nj - 1), nj - 1):
        r = j * NA_QROWS + jnp.arange(NA_QROWS)
        krow = j * NA_QROWS - NA_KBLK // GRID_W + jnp.arange(NA_K // GRID_W)
        start = jnp.clip(r - kr // 2, 0, rows - kr)
        rvalid = (krow[None, :] >= start[:, None]) & (krow[None, :] < start[:, None] + kr)
        dr = krow[None, :] - r[:, None] + (NA_ROWS - 1)
        idx.append(jnp.where(rvalid, dr, NA_DR))
    return tiles, jnp.stack(idx).reshape(-1).astype(jnp.int32)


def _na_kernel(idx_ref, q_ref, k0, k1, k2, k3, v0, v1, v2, v3, kx_ref, vx_ref, t_ref, o_ref, bias_scr,
               *, nj):
    j = pl.program_id(2)
    nkr = NA_K // GRID_W
    per = LANES // HEAD_DIM

    @pl.when((j == 0) | (j == 1) | (j == nj - 1))
    def _():
        variant = jnp.where(j == 0, 0, jnp.where(j == nj - 1, 2, 1))
        for hh in range(per):
            for qr in range(NA_QROWS):
                for kr in range(nkr):
                    d = idx_ref[(variant * NA_QROWS + qr) * nkr + kr]
                    bias_scr[hh, kr * GRID_W:(kr + 1) * GRID_W, qr * GRID_W:(qr + 1) * GRID_W] = (
                        t_ref[hh, d])

    q = q_ref[...]
    lane = lax.broadcasted_iota(jnp.int32, q.shape, 1)
    zero = jnp.zeros_like(q)
    q_heads = [jnp.where((lane >= hh * HEAD_DIM) & (lane < (hh + 1) * HEAD_DIM), q, zero)
               for hh in range(per)]
    k_all = jnp.concatenate([k0[...], k1[...], k2[...], k3[...]], axis=0)
    v_t = jnp.concatenate([v0[...], v1[...], v2[...], v3[...]], axis=0).astype(F32).T.astype(BF16)
    kx = kx_ref[...]
    vx_t = vx_ref[...].astype(F32).T.astype(BF16)
    qh_rows = NA_Q // 2
    k_span = 3 * NA_KBLK

    def scores(hh, half):
        q0, k0_ = half * qh_rows, half * NA_KBLK
        qh = q_heads[hh][q0:q0 + qh_rows]
        s_loc = _dot_nt(k_all[k0_:k0_ + k_span], qh) + bias_scr[hh, k0_:k0_ + k_span, q0:q0 + qh_rows]
        return s_loc, _dot_nt(kx, qh)

    units = [(hh, half) for half in range(2) for hh in range(per)]
    pending = scores(*units[0])
    for n, (hh, half) in enumerate(units):
        s_loc, s_ctx = pending
        if n + 1 < len(units):
            pending = scores(*units[n + 1])
        lo, hi = hh * HEAD_DIM, (hh + 1) * HEAD_DIM
        q0, k0_ = half * qh_rows, half * NA_KBLK
        m = jnp.maximum(s_loc.max(0, keepdims=True), s_ctx.max(0, keepdims=True))
        e_loc = jnp.exp(s_loc - m)
        e_ctx = jnp.exp(s_ctx - m)
        denom = e_loc.sum(0, keepdims=True) + e_ctx.sum(0, keepdims=True)
        o_t = (_dot(v_t[lo:hi, k0_:k0_ + k_span], e_loc.astype(BF16))
               + _dot(vx_t[lo:hi, :], e_ctx.astype(BF16)))
        o_ref[0, lo:hi, q0:q0 + qh_rows] = (o_t / denom).astype(o_ref.dtype)


def _na_attn(bq, bk, bv, bkx, bvx, tiles, tile_idx, batch, seq, ctx_len):
    nj = seq // NA_Q
    nkb = seq // NA_KBLK
    hp = B_HEADS * HEAD_DIM // LANES
    per = LANES // HEAD_DIM

    def kb(t):
        return lambda p, b, j, idx: (b * nkb + jnp.clip(2 * j - 1 + t, 0, nkb - 1), p)

    kv_specs = [pl.BlockSpec((NA_KBLK, LANES), kb(t)) for t in range(4)]
    grid_spec = pltpu.PrefetchScalarGridSpec(
        num_scalar_prefetch=1,
        grid=(hp, batch, nj),
        in_specs=[pl.BlockSpec((NA_Q, LANES), lambda p, b, j, idx: (b * nj + j, p))]
        + kv_specs + kv_specs
        + [pl.BlockSpec((ctx_len, LANES), lambda p, b, j, idx: (b, p)),
           pl.BlockSpec((ctx_len, LANES), lambda p, b, j, idx: (b, p)),
           pl.BlockSpec((per, NA_DR + 1, GRID_W, GRID_W), lambda p, b, j, idx: (p, 0, 0, 0))],
        out_specs=pl.BlockSpec((1, LANES, NA_Q), lambda p, b, j, idx: (b, p, j)),
        scratch_shapes=[pltpu.VMEM((per, NA_K, NA_Q), F32)],
    )
    return pl.pallas_call(
        functools.partial(_na_kernel, nj=nj),
        grid_spec=grid_spec,
        out_shape=jax.ShapeDtypeStruct((batch, B_HEADS * HEAD_DIM, seq), BF16),
        compiler_params=_params(("arbitrary", "arbitrary", "arbitrary")),
        name="na_attn",
    )(tile_idx, bq, bk, bk, bk, bk, bv, bv, bv, bv, bkx, bvx, tiles.transpose(0, 1, 3, 2))


def _ctx_ab_kernel(sink_ref, aq_ref, ak_ref, av_ref, bq_ref, bk_ref, bv_ref, oa_ref, ob_ref):
    ctx_len = aq_ref.shape[0]
    g_sz = A_HEADS // A_KV_HEADS
    aq = aq_ref[...]
    ak = ak_ref[...]
    av = av_ref[...]
    outs = []
    for g in range(A_KV_HEADS):
        lo, hi = g * HEAD_DIM, (g + 1) * HEAD_DIM
        heads = [g * g_sz + i for i in range(g_sz)]
        qs = jnp.concatenate([aq[:, h * HEAD_DIM:(h + 1) * HEAD_DIM] for h in heads], axis=0)
        sink = jnp.concatenate(
            [jnp.full((ctx_len, 1), sink_ref[h], F32) for h in heads], axis=0)
        o = _softmax_pv([(_dot_nt(qs, ak[:, lo:hi]), av[:, lo:hi])], sink)
        outs.extend(o[i * ctx_len:(i + 1) * ctx_len] for i in range(g_sz))
    oa_ref[...] = jnp.concatenate(outs, axis=1).astype(oa_ref.dtype)
    bq = bq_ref[...]
    bk = bk_ref[...]
    bv = bv_ref[...]
    outs = []
    for h in range(B_HEADS):
        lo, hi = h * HEAD_DIM, (h + 1) * HEAD_DIM
        outs.append(_softmax_pv([(_dot_nt(bq[:, lo:hi], bk[:, lo:hi]), bv[:, lo:hi])], None))
    ob_ref[...] = jnp.concatenate(outs, axis=1).astype(ob_ref.dtype)


def _ctx_ab_attn(sink, aq, ak, av, bq, bk, bv, batch, ctx_len):
    def spec(w):
        return pl.BlockSpec((ctx_len, w), lambda b: (b, 0))

    return pl.pallas_call(
        _ctx_ab_kernel,
        grid=(batch,),
        in_specs=[pl.BlockSpec(memory_space=pltpu.SMEM)]
        + [spec(a.shape[1]) for a in (aq, ak, av, bq, bk, bv)],
        out_specs=[spec(aq.shape[1]), spec(bq.shape[1])],
        out_shape=[jax.ShapeDtypeStruct(aq.shape, BF16), jax.ShapeDtypeStruct(bq.shape, BF16)],
        compiler_params=_params(("arbitrary",)),
        name="ctx_ab_attn",
    )(sink, aq, ak, av, bq, bk, bv)


def _layer_norm(t, g, b):
    mu = jnp.mean(t, -1, keepdims=True)
    c = t - mu
    var = jnp.mean(c * c, -1, keepdims=True)
    return c * lax.rsqrt(var + 1e-5) * g + b


def _post_attn_kernel(*refs, alpha, lat_tiles, with_ctx):
    n_src = 6 if with_ctx else 3
    srcs = refs[:n_src]
    (w_ref, mod_ref, lng_ref, lnb_ref, wr_hi_ref, wr_lo_ref, br_ref,
     x1_ref, h2_ref, lg_ref) = refs[n_src:]

    def rows_of(o_ref):
        if o_ref.ndim == 2:
            return o_ref[...]
        return o_ref[0].astype(F32).T.astype(BF16)

    def run(o1_ref, o2_ref, x_ref):
        half = w_ref.shape[0] // 2
        y = _dot(rows_of(o1_ref), w_ref[0:half, :]) + _dot(rows_of(o2_ref), w_ref[half:, :])
        g1 = mod_ref[0, 2:3, :]
        x1 = _layer_norm(alpha * x_ref[...] + g1 * y, lng_ref[...], lnb_ref[...])
        x1_ref[...] = x1
        h2 = x1 * (1.0 + mod_ref[0, 4:5, :]) + mod_ref[0, 3:4, :]
        h_hi, h_lo = _split_bf16(h2)
        h2_ref[...] = _pack_halves(h2)
        wr_hi = wr_hi_ref[...]
        lg_ref[...] = (_dot(h_hi, wr_hi) + _dot(h_lo, wr_hi) + _dot(h_hi, wr_lo_ref[...])
                       + br_ref[...])

    if not with_ctx:
        run(*srcs)
    else:
        i = pl.program_id(0)
        pl.when(i < lat_tiles)(lambda: run(*srcs[:3]))
        pl.when(i >= lat_tiles)(lambda: run(*srcs[3:]))


def _post_attn(lat, ctx_src, w_bf, mod, gid, lng, lnb, wr_hi, wr_lo, br, n_lat, alpha, tm):
    half = w_bf.shape[0] // 2
    d = lat[2].shape[1]
    lat_tiles = n_lat // tm
    n_rows = n_lat + (ctx_src[0].shape[0] if ctx_src is not None else 0)

    def lat_map(i):
        return (jnp.minimum(i, lat_tiles - 1), 0)

    def ctx_map(i):
        return (jnp.maximum(i - lat_tiles, 0), 0)

    def src_specs(index_map):
        return [pl.BlockSpec((tm, half), index_map), pl.BlockSpec((tm, half), index_map),
                pl.BlockSpec((tm, d), index_map)]

    in_specs = src_specs(lat_map)
    for k in range(2):
        if lat[k].ndim == 3:
            per_batch = lat[k].shape[2] // tm

            def t_map(i, per_batch=per_batch):
                i = jnp.minimum(i, lat_tiles - 1)
                return (i // per_batch, 0, i % per_batch)

            in_specs[k] = pl.BlockSpec((1, half, tm), t_map)
    args = list(lat)
    if ctx_src is not None:
        in_specs += src_specs(ctx_map)
        args += list(ctx_src)

    def const(shape):
        return pl.BlockSpec(shape, lambda i: (0,) * len(shape))

    in_specs += [const((2 * half, d)),
                 pl.BlockSpec((1, 6, d), lambda i: (gid(i), 0, 0)),
                 const((1, d)), const((1, d)),
                 const((d, ROUTER_PAD)), const((d, ROUTER_PAD)), const((1, ROUTER_PAD))]
    args += [w_bf, mod, lng, lnb, wr_hi, wr_lo, br]
    widths = (d, d // 2, ROUTER_PAD)
    dtypes = (F32, jnp.uint32, F32)
    return pl.pallas_call(
        functools.partial(_post_attn_kernel, alpha=alpha, lat_tiles=lat_tiles,
                          with_ctx=ctx_src is not None),
        grid=(n_rows // tm,),
        in_specs=in_specs,
        out_specs=[pl.BlockSpec((tm, w), lambda i: (i, 0)) for w in widths],
        out_shape=[jax.ShapeDtypeStruct((n_rows, w), t) for w, t in zip(widths, dtypes)],
        compiler_params=_params(("arbitrary",)),
        name="post_attn",
    )(*args)


def _moe_kernel(be_ref, xs_ref, wgu_ref, wdn_ref, ys_ref, wgu_bf, wdn_bf):
    i = pl.program_id(0)
    used = i < be_ref[pl.num_programs(0)]
    prev = be_ref[jnp.maximum(i - 1, 0)]
    changed = (i == 0) | (be_ref[i] != prev)

    @pl.when(changed & used)
    def _():
        wgu_bf[...] = wgu_ref[0].astype(BF16)
        wdn_bf[...] = wdn_ref[0].astype(BF16)

    @pl.when(used)
    def _():
        de = wdn_bf.shape[0]
        subs = [slice(r, r + MOE_SUB) for r in range(0, MOE_BLOCK, MOE_SUB)]
        gus = [_dot(_unpack_halves(xs_ref[sl, :]).astype(BF16), wgu_bf[...]) for sl in subs]
        for sl, gu in zip(subs, gus):
            gate = gu[:, :de]
            up = gu[:, de:]
            act = gate / (1.0 + jnp.exp(-gate)) * up
            ys_ref[sl, :] = _pack_halves(_dot(act.astype(BF16), wdn_bf[...]))

    @pl.when(jnp.logical_not(used))
    def _():
        ys_ref[...] = jnp.zeros(ys_ref.shape, ys_ref.dtype)


def _moe_experts(block_e, xs, w_gu, w_dn):
    rows, dw = xs.shape
    d = 2 * dw
    nb = rows // MOE_BLOCK
    de = w_dn.shape[1]
    grid_spec = pltpu.PrefetchScalarGridSpec(
        num_scalar_prefetch=1,
        grid=(nb,),
        in_specs=[
            pl.BlockSpec((MOE_BLOCK, dw), lambda i, be: (i, 0)),
            pl.BlockSpec((1, d, 2 * de), lambda i, be: (be[i], 0, 0)),
            pl.BlockSpec((1, de, d), lambda i, be: (be[i], 0, 0)),
        ],
        out_specs=pl.BlockSpec((MOE_BLOCK, dw), lambda i, be: (i, 0)),
        scratch_shapes=[pltpu.VMEM((d, 2 * de), BF16), pltpu.VMEM((de, d), BF16)],
    )
    return pl.pallas_call(
        _moe_kernel,
        grid_spec=grid_spec,
        out_shape=jax.ShapeDtypeStruct((rows, dw), jnp.uint32),
        compiler_params=_params(("arbitrary",)),
        name="moe_experts",
    )(block_e, xs, w_gu, w_dn)


ROUTE_TILE = 512


def _route_kernel(lg_ref, rec_ref, cnt_ref, carry):
    i = pl.program_id(0)

    @pl.when(i == 0)
    def _():
        carry[...] = jnp.zeros(carry.shape, F32)

    lg = lg_ref[...]
    tm = lg.shape[0]
    lane = lax.broadcasted_iota(jnp.int32, lg.shape, 1)

    def first_lane(mask):
        return jnp.min(jnp.where(mask, lane, ROUTER_PAD), axis=1, keepdims=True)

    is_g = lane < N_GROUPS
    g_log = jnp.where(is_g, lg, NEG_INF)
    g_max = jnp.max(g_log, axis=1, keepdims=True)
    g_sum = jnp.sum(jnp.where(is_g, jnp.exp(lg - g_max), 0.0), axis=1, keepdims=True)
    g_val = 1.0 / g_sum
    g_idx = first_lane(g_log == g_max)
    lo = N_GROUPS + EXP_PER_GROUP * g_idx
    sel = (lane >= lo) & (lane < lo + EXP_PER_GROUP)
    e_log = jnp.where(sel, lg, NEG_INF)
    e_max = jnp.max(e_log, axis=1, keepdims=True)
    e_exp = jnp.where(sel, jnp.exp(lg - e_max), 0.0)
    e_prob = e_exp / jnp.sum(e_exp, axis=1, keepdims=True)
    p1 = jnp.where(sel, e_prob, -1.0)
    v1 = jnp.max(p1, axis=1, keepdims=True)
    i1 = first_lane(p1 == v1)
    p2 = jnp.where(lane == i1, -1.0, p1)
    v2 = jnp.max(p2, axis=1, keepdims=True)
    i2 = first_lane(p2 == v2)
    norm = g_val / (v1 + v2)

    hot1 = lane == i1
    hot2 = lane == i2
    hot = jnp.where(hot1 | hot2, 1.0, 0.0)
    rows = lax.broadcasted_iota(jnp.int32, (tm, tm), 0)
    cols = lax.broadcasted_iota(jnp.int32, (tm, tm), 1)
    before = jnp.where(cols < rows, 1.0, 0.0).astype(BF16)
    prefix = _dot(before, hot.astype(BF16)) + carry[0:1, :]
    r1 = jnp.sum(jnp.where(hot1, prefix, 0.0), axis=1, keepdims=True)
    r2 = jnp.sum(jnp.where(hot2, prefix, 0.0), axis=1, keepdims=True)
    carry[0:1, :] = carry[0:1, :] + jnp.sum(hot, axis=0, keepdims=True)

    fields = [(i1 - N_GROUPS).astype(F32), (i2 - N_GROUPS).astype(F32), r1, r2, v1 * norm, v2 * norm]
    rec = jnp.zeros(lg.shape, F32)
    for k, f in enumerate(fields):
        rec = jnp.where(lane == k, f, rec)
    rec_ref[...] = rec

    @pl.when(i == pl.num_programs(0) - 1)
    def _():
        cnt_ref[...] = carry[...]


def _route(logits, n_tok):
    rec, cnt = pl.pallas_call(
        _route_kernel,
        grid=(n_tok // ROUTE_TILE,),
        in_specs=[pl.BlockSpec((ROUTE_TILE, ROUTER_PAD), lambda i: (i, 0))],
        out_specs=[pl.BlockSpec((ROUTE_TILE, ROUTER_PAD), lambda i: (i, 0)),
                   pl.BlockSpec((8, ROUTER_PAD), lambda i: (0, 0))],
        out_shape=[jax.ShapeDtypeStruct((n_tok, ROUTER_PAD), F32),
                   jax.ShapeDtypeStruct((8, ROUTER_PAD), F32)],
        scratch_shapes=[pltpu.VMEM((8, ROUTER_PAD), F32)],
        compiler_params=_params(("arbitrary",)),
        name="route",
    )(logits)
    experts = rec[:, 0:2].astype(jnp.int32)
    rank = rec[:, 2:4].astype(jnp.int32)
    wts = rec[:, 4:6]
    counts = cnt[0, N_GROUPS:N_GROUPS + N_EXPERTS].astype(jnp.int32)

    n_asg = n_tok * TOP_K
    pcounts = ((counts + MOE_BLOCK - 1) // MOE_BLOCK) * MOE_BLOCK
    pends = jnp.cumsum(pcounts)
    pstarts = pends - pcounts
    dest = pstarts[experts] + rank
    nb = -(-n_asg // MOE_BLOCK) + N_EXPERTS
    block_start = jnp.arange(nb, dtype=jnp.int32) * MOE_BLOCK
    block_e = jnp.minimum((pends[None, :] <= block_start[:, None]).sum(1), N_EXPERTS - 1).astype(jnp.int32)
    block_e = jnp.concatenate([block_e, (pends[-1:] // MOE_BLOCK).astype(jnp.int32)])
    tok = jnp.broadcast_to(jnp.arange(n_tok, dtype=jnp.int32)[:, None], (n_tok, TOP_K))
    filler = jnp.arange(nb * MOE_BLOCK, dtype=jnp.int32) % n_tok
    row_tok = filler.at[dest.reshape(-1)].set(tok.reshape(-1), unique_indices=True)
    return wts, dest, row_tok, block_e


def _post_moe_kernel(x_ref, ya_ref, yb_ref, w_ref, mod_ref, lng_ref, lnb_ref, o_ref, *, alpha):
    w = w_ref[...]
    y = w[:, 0:1] * _unpack_halves(ya_ref[...]) + w[:, 1:2] * _unpack_halves(yb_ref[...])
    g2 = mod_ref[0, 5:6, :]
    o_ref[...] = _layer_norm(alpha * x_ref[...] + g2 * y, lng_ref[...], lnb_ref[...])


def _post_moe(x1, ya, yb, wts, mod, gid, lng, lnb, alpha, tm):
    rows, d = x1.shape
    row = pl.BlockSpec((tm, d), lambda i: (i, 0))
    packed = pl.BlockSpec((tm, d // 2), lambda i: (i, 0))
    vec = pl.BlockSpec((1, d), lambda i: (0, 0))
    return pl.pallas_call(
        functools.partial(_post_moe_kernel, alpha=alpha),
        grid=(rows // tm,),
        in_specs=[row, packed, packed,
                  pl.BlockSpec((tm, TOP_K), lambda i: (i, 0)),
                  pl.BlockSpec((1, 6, d), lambda i: (gid(i), 0, 0)),
                  vec, vec],
        out_specs=row,
        out_shape=jax.ShapeDtypeStruct((rows, d), F32),
        compiler_params=_params(("arbitrary",)),
        name="post_moe",
    )(x1, ya, yb, wts, mod, lng, lnb)


CD_COLS = dict(cq=(0, 384), ckv=(384, 640), dq=(640, 1152), dk=(1152, 1664), dv=(1664, 2176),
               kr=(2176, 2304))
C_HEAD_PAD = LANES


def _rms(t, g, eps=1e-6):
    return t * lax.rsqrt(jnp.mean(t * t, -1, keepdims=True) + eps) * g


def _proj_cd_kernel(x_ref, mod_ref, w_ref, qn_ref, wuq_ref, kvn_ref, wukv_ref,
                    cos64_ref, sin64_ref, cos32_ref, sin32_ref,
                    qc_ref, kc_ref, vct_ref, dq_ref, dk_ref, dvt_ref):
    h = _modulate(x_ref, mod_ref, 0).astype(BF16)
    r = _dot(h, w_ref[...])

    def cols(name):
        lo, hi = CD_COLS[name]
        return r[:, lo:hi]

    q = _dot(_rms(cols('cq'), qn_ref[...]).astype(BF16), wuq_ref[...])
    kv = _dot(_rms(cols('ckv'), kvn_ref[...]).astype(BF16), wukv_ref[...])
    c32, s32 = cos32_ref[...], sin32_ref[...]
    c64, s64 = cos64_ref[...], sin64_ref[...]
    q = _rope_wide(q, c32, s32, C_ROPE // 4)
    kr = _rope_tile(cols('kr'), c32, s32, C_ROPE // 4)
    dq = _rope_wide(cols('dq'), c64, s64, D_QK // 4)
    dk = _rope_wide(cols('dk'), c64, s64, D_QK // 4)
    kw = C_HEADS * C_HEAD_PAD
    qc_ref[0] = (q * ((C_NOPE + C_ROPE) ** -0.5 * LOG2E)).astype(BF16)
    kc_ref[0] = (kv[:, :kw] + jnp.concatenate([kr] * C_HEADS, axis=1)).astype(BF16)
    dq_ref[0] = (dq * (D_QK ** -0.5 * LOG2E)).astype(BF16)
    dk_ref[0] = dk.astype(BF16)
    _store_vt(vct_ref, kv[:, kw:], C_HEADS, C_V)
    _store_vt(dvt_ref, cols('dv'), D_HEADS, D_V)


def _store_vt(vt_ref, v, heads, dv):
    tm = v.shape[0]
    v_t = v.T.astype(BF16)
    row = lax.broadcasted_iota(jnp.int32, (ONES_ROWS, tm), 0)
    extra = jnp.where(row == 0, 1.0, 0.0).astype(BF16)
    for h in range(heads):
        vt_ref[0, h, 0, 0:dv, :] = v_t[h * dv:(h + 1) * dv]
        vt_ref[0, h, 0, dv:dv + ONES_ROWS, :] = extra


def _proj_cd(xall, mod, w_bf, qn, wuq, kvn, wukv, tabs, batch, seq, ctx_len, tm, tk):
    d = xall.shape[1]
    lat_t, ctx_t = seq // tm, ctx_len // tm
    n_keys = seq + ctx_len
    per_chunk = tk // tm
    n_chunks = n_keys // tk
    widths = (C_HEADS * C_HEAD_PAD, C_HEADS * C_HEAD_PAD, 2 * D_HEADS * D_QK, 2 * D_HEADS * D_QK)

    def src(b, t):
        return (jnp.where(t < lat_t, b * lat_t + t, batch * lat_t + b * ctx_t + t - lat_t), 0)

    def full(a):
        return pl.BlockSpec(a.shape, lambda b, t: (0,) * a.ndim)

    def row_spec(w):
        return pl.BlockSpec((1, tm, w), lambda b, t: (b, t, 0))

    def row_shape(w):
        return jax.ShapeDtypeStruct((batch, n_keys, w), BF16)

    def vt_spec(heads, dv):
        return pl.BlockSpec((1, heads, 1, dv + ONES_ROWS, tm),
                            lambda b, t: (b, 0, t // per_chunk, 0, t % per_chunk))

    def vt_shape(heads, dv):
        return jax.ShapeDtypeStruct((batch, heads, n_chunks, dv + ONES_ROWS, tk), BF16)

    tab = pl.BlockSpec((tm, LANES), lambda b, t: (t, 0))
    return pl.pallas_call(
        _proj_cd_kernel,
        grid=(batch, lat_t + ctx_t),
        in_specs=[pl.BlockSpec((tm, d), src),
                  pl.BlockSpec((1, 6, d), lambda b, t: (jnp.where(t < lat_t, b, batch), 0, 0)),
                  full(w_bf), full(qn), full(wuq), full(kvn), full(wukv), tab, tab, tab, tab],
        out_specs=[row_spec(widths[0]), row_spec(widths[1]), vt_spec(C_HEADS, C_V),
                   row_spec(widths[2]), row_spec(widths[3]), vt_spec(D_HEADS, D_V)],
        out_shape=[row_shape(widths[0]), row_shape(widths[1]), vt_shape(C_HEADS, C_V),
                   row_shape(widths[2]), row_shape(widths[3]), vt_shape(D_HEADS, D_V)],
        compiler_params=_params(("arbitrary", "arbitrary")),
        name="proj_cd",
    )(xall, mod, w_bf, qn, wuq, kvn, wukv, *tabs)


DENSE_SHIFT = 80.0
DENSE_L_MIN = 2.0 ** -60
DENSE_L_MAX = 2.0 ** 120
DENSE_TQ = 1024
DENSE_TK_MAX = 1280
DENSE_UNROLL_MAX = 13
MXU_DEPTH = 256
PROJ_CD_TILE = 256
ONES_ROWS = 16


def _sq_norm_row(x):
    ones = jnp.ones((8, x.shape[1]), BF16)
    return _dot_nt(ones, (x * x).astype(BF16))[0:1] * 1.02


def _key_norm_max(kchunk, n_chunks, masks):
    def body(j, mx):
        kf = kchunk(j).astype(F32)
        ksq = kf * kf
        out = []
        for msk, cur in zip(masks, mx):
            part = ksq if msk is None else jnp.where(msk, ksq, 0.0)
            rn = jnp.sum(part, axis=1, keepdims=True)
            out.append(jnp.maximum(cur, jnp.max(rn, axis=0, keepdims=True)))
        return tuple(out)

    mx = lax.fori_loop(0, n_chunks, body, tuple(jnp.zeros((1, 1), F32) for _ in masks))
    return [jnp.sqrt(v) * 1.01 for v in mx]


def _safe_online(q, kchunk, vt_chunk, n_chunks, m_scr, l_scr, acc):
    m_scr[...] = jnp.full(m_scr.shape, NEG_INF, F32)
    l_scr[...] = jnp.zeros(l_scr.shape, F32)
    acc[...] = jnp.zeros(acc.shape, F32)

    def body(j, carry):
        s = _dot_nt(kchunk(j), q)
        m_prev = m_scr[...]
        m_new = jnp.maximum(m_prev, s.max(0, keepdims=True))
        alpha = jnp.exp2(m_prev - m_new)
        p = jnp.exp2(s - m_new)
        l_scr[...] = alpha * l_scr[...] + p.sum(0, keepdims=True)
        acc[...] = alpha * acc[...] + _dot(vt_chunk(j), p.astype(BF16))
        m_scr[...] = m_new
        return carry

    lax.fori_loop(0, n_chunks, body, 0)


def _denominators_ok(*ls):
    ok = None
    for l in ls:
        cur = (l > DENSE_L_MIN) & (l < DENSE_L_MAX)
        ok = cur if ok is None else ok & cur
    return jnp.max(jnp.where(ok, 0.0, 1.0)) == 0.0


MLA_HEADS_PER_STEP = 2


def _mla_kernel(q_ref, k_ref, vt_ref, o_ref, kmax_scr, acc_scr, m_scr, l_scr,
                *, n_chunks, tk, unroll, depth):
    hp = acc_scr.shape[0]

    def kchunk(j, h):
        return k_ref[0, pl.ds(pl.multiple_of(j * tk, tk), tk), h * C_HEAD_PAD:(h + 1) * C_HEAD_PAD]

    @pl.when(pl.program_id(2) == 0)
    def _():
        for h in range(hp):
            (kmax,) = _key_norm_max(lambda j: kchunk(j, h), n_chunks, [None])
            kmax_scr[h:h + 1, :] = jnp.broadcast_to(kmax, (1, LANES))

    qs = [q_ref[0, :, h * C_HEAD_PAD:(h + 1) * C_HEAD_PAD] for h in range(hp)]
    ms = [jnp.sqrt(_sq_norm_row(qs[h].astype(F32))) * kmax_scr[h:h + 1, 0:1] - DENSE_SHIFT
          for h in range(hp)]
    acc_scr[...] = jnp.zeros(acc_scr.shape, F32)

    def scores(j):
        return [_dot_nt(kchunk(j, h), qs[h]) for h in range(hp)]

    def body(it, carry):
        j0 = it * unroll
        pend = [scores(j0 + u) for u in range(min(depth, unroll))]
        accs = [acc_scr[h] for h in range(hp)]
        for u in range(unroll):
            ss = pend.pop(0)
            if u + depth < unroll:
                pend.append(scores(j0 + u + depth))
            for h in range(hp):
                accs[h] = accs[h] + _dot(vt_ref[0, h, j0 + u], jnp.exp2(ss[h] - ms[h]).astype(BF16))
        for h in range(hp):
            acc_scr[h] = accs[h]
        return carry

    lax.fori_loop(0, n_chunks // unroll, body, 0)
    ls = [acc_scr[h, C_V:C_V + 1, :] for h in range(hp)]
    for h in range(hp):
        o_ref[0, h * C_V:(h + 1) * C_V, :] = (acc_scr[h, 0:C_V, :] / ls[h]).astype(o_ref.dtype)

    @pl.when(jnp.logical_not(_denominators_ok(*ls)))
    def _():
        for h in range(hp):
            _safe_online(qs[h], lambda j: kchunk(j, h), lambda j: vt_ref[0, h, j], n_chunks,
                         m_scr, l_scr, acc_scr.at[h])
            o_ref[0, h * C_V:(h + 1) * C_V, :] = (acc_scr[h, 0:C_V, :] / l_scr[...]).astype(o_ref.dtype)


def _mla_attn(qc, kc, vt, batch, seq, tq, tk, unroll, depth):
    n_keys = kc.shape[1]
    nq, n_chunks = seq // tq, n_keys // tk
    rows = C_V + ONES_ROWS
    hp = MLA_HEADS_PER_STEP
    return pl.pallas_call(
        functools.partial(_mla_kernel, n_chunks=n_chunks, tk=tk, unroll=unroll, depth=depth),
        grid=(batch, C_HEADS // hp, nq),
        in_specs=[
            pl.BlockSpec((1, tq, hp * C_HEAD_PAD), lambda b, h, i: (b, i, h)),
            pl.BlockSpec((1, n_keys, hp * C_HEAD_PAD), lambda b, h, i: (b, 0, h)),
            pl.BlockSpec((1, hp, n_chunks, rows, tk), lambda b, h, i: (b, h, 0, 0, 0)),
        ],
        out_specs=pl.BlockSpec((1, hp * C_V, tq), lambda b, h, i: (b, h, i)),
        out_shape=jax.ShapeDtypeStruct((batch, C_HEADS * C_V, seq), BF16),
        scratch_shapes=[pltpu.VMEM((8, LANES), F32), pltpu.VMEM((hp, rows, tq), F32),
                        pltpu.VMEM((1, tq), F32), pltpu.VMEM((1, tq), F32)],
        compiler_params=_params(("arbitrary", "arbitrary", "arbitrary")),
        name="mla_attn",
    )(qc, kc, vt)


def _diff_kernel(lam_ref, q_ref, k_ref, vt_ref, subln_ref, o_ref, kmax_scr, acc1_scr, acc2_scr,
                 m_scr, l_scr, *, n_chunks, tk, unroll, depth, out_scale):
    def kchunk(j):
        return k_ref[0, pl.ds(pl.multiple_of(j * tk, tk), tk), :]

    @pl.when(pl.program_id(2) == 0)
    def _():
        klane = lax.broadcasted_iota(jnp.int32, (tk, 2 * D_QK), 1)
        k1, k2 = _key_norm_max(kchunk, n_chunks, [klane < D_QK, klane >= D_QK])
        kmax_scr[0:1, :] = jnp.broadcast_to(k1, (1, LANES))
        kmax_scr[1:2, :] = jnp.broadcast_to(k2, (1, LANES))

    q = q_ref[0]
    qlane = lax.broadcasted_iota(jnp.int32, q.shape, 1)
    zero = jnp.zeros_like(q)
    q1 = jnp.where(qlane < D_QK, q, zero)
    q2 = jnp.where(qlane >= D_QK, q, zero)
    m1 = jnp.sqrt(_sq_norm_row(q1.astype(F32))) * kmax_scr[0:1, 0:1] - DENSE_SHIFT
    m2 = jnp.sqrt(_sq_norm_row(q2.astype(F32))) * kmax_scr[1:2, 0:1] - DENSE_SHIFT
    acc1_scr[...] = jnp.zeros(acc1_scr.shape, F32)
    acc2_scr[...] = jnp.zeros(acc2_scr.shape, F32)

    def scores(j):
        k = kchunk(j)
        return _dot_nt(k, q1), _dot_nt(k, q2)

    def body(it, carry):
        j0 = it * unroll
        pend = [scores(j0 + u) for u in range(min(depth, unroll))]
        a1 = acc1_scr[...]
        a2 = acc2_scr[...]
        for u in range(unroll):
            s1, s2 = pend.pop(0)
            if u + depth < unroll:
                pend.append(scores(j0 + u + depth))
            vt = vt_ref[0, 0, j0 + u]
            a1 = a1 + _dot(vt, jnp.exp2(s1 - m1).astype(BF16))
            a2 = a2 + _dot(vt, jnp.exp2(s2 - m2).astype(BF16))
        acc1_scr[...] = a1
        acc2_scr[...] = a2
        return carry

    lax.fori_loop(0, n_chunks // unroll, body, 0)

    def finish(o1, o2):
        o = o1 - lam_ref[0] * o2
        o = o * lax.rsqrt(jnp.mean(o * o, 0, keepdims=True) + 1e-6) * subln_ref[...] * out_scale
        o_ref[0] = o.astype(o_ref.dtype)

    l1 = acc1_scr[D_V:D_V + 1, :]
    l2 = acc2_scr[D_V:D_V + 1, :]
    finish(acc1_scr[0:D_V, :] / l1, acc2_scr[0:D_V, :] / l2)

    @pl.when(jnp.logical_not(_denominators_ok(l1, l2)))
    def _():
        def vt_chunk(j):
            return vt_ref[0, 0, j]

        _safe_online(q1, kchunk, vt_chunk, n_chunks, m_scr, l_scr, acc1_scr)
        o1 = acc1_scr[0:D_V, :] / l_scr[...]
        _safe_online(q2, kchunk, vt_chunk, n_chunks, m_scr, l_scr, acc2_scr)
        finish(o1, acc2_scr[0:D_V, :] / l_scr[...])


def _diff_attn(lam, dq, dk, vt, subln_col, out_scale, batch, seq, tq, tk, unroll, depth):
    n_keys = dk.shape[1]
    nq, n_chunks = seq // tq, n_keys // tk
    rows = D_V + ONES_ROWS
    pair = 2 * D_QK
    return pl.pallas_call(
        functools.partial(_diff_kernel, n_chunks=n_chunks, tk=tk, unroll=unroll, depth=depth,
                          out_scale=out_scale),
        grid=(batch, D_HEADS, nq),
        in_specs=[
            pl.BlockSpec(memory_space=pltpu.SMEM),
            pl.BlockSpec((1, tq, pair), lambda b, h, i: (b, i, h)),
            pl.BlockSpec((1, n_keys, pair), lambda b, h, i: (b, 0, h)),
            pl.BlockSpec((1, 1, n_chunks, rows, tk), lambda b, h, i: (b, h, 0, 0, 0)),
            pl.BlockSpec((D_V, 1), lambda b, h, i: (0, 0)),
        ],
        out_specs=pl.BlockSpec((1, D_V, tq), lambda b, h, i: (b, h, i)),
        out_shape=jax.ShapeDtypeStruct((batch, D_HEADS * D_V, seq), BF16),
        scratch_shapes=[pltpu.VMEM((8, LANES), F32), pltpu.VMEM((rows, tq), F32),
                        pltpu.VMEM((rows, tq), F32), pltpu.VMEM((1, tq), F32), pltpu.VMEM((1, tq), F32)],
        compiler_params=_params(("arbitrary", "arbitrary", "arbitrary")),
        name="diff_attn",
    )(lam, dq, dk, vt, subln_col)


def _rope_tables(n_tok, dim, lane_lo):
    t = jnp.arange(n_tok)
    pos_r = (t // GRID_W).astype(F32)
    pos_c = (t % GRID_W).astype(F32)
    quarter = dim // 4
    inv = ROPE_BASE ** (-jnp.arange(quarter, dtype=F32) / quarter)
    ang_r = pos_r[:, None] * inv
    ang_c = pos_c[:, None] * inv
    ang = jnp.concatenate([ang_r, ang_r, ang_c, ang_c], -1)
    sign = jnp.tile(jnp.concatenate([-jnp.ones(quarter), jnp.ones(quarter)]), 2).astype(F32)
    cos, sin = jnp.cos(ang), jnp.sin(ang) * sign
    reps = (LANES - lane_lo) // dim
    cos = jnp.concatenate([jnp.ones((n_tok, lane_lo), F32)] + [cos] * reps, axis=1)
    sin = jnp.concatenate([jnp.zeros((n_tok, lane_lo), F32)] + [sin] * reps, axis=1)
    return cos, sin


def _cd_weights(w_in, w_uq, w_ukv):
    d = w_in.shape[0]
    s0 = C_Q_RANK
    s1 = s0 + C_KV_RANK
    s2 = s1 + C_ROPE
    s3 = s2 + 2 * D_HEADS * D_QK
    s4 = s3 + 2 * D_HEADS * D_QK
    kr = jnp.zeros((d, LANES), F32).at[:, C_NOPE:C_NOPE + C_ROPE].set(w_in[:, s1:s2])
    w_in_p = jnp.concatenate([w_in[:, :s1], w_in[:, s2:s3], w_in[:, s3:s4], w_in[:, s4:], kr], axis=1)
    dqk = C_NOPE + C_ROPE
    wq = w_uq.reshape(C_Q_RANK, C_HEADS, dqk)
    wq = jnp.pad(wq, ((0, 0), (0, 0), (0, C_HEAD_PAD - dqk))).reshape(C_Q_RANK, C_HEADS * C_HEAD_PAD)
    wkv = w_ukv.reshape(C_KV_RANK, C_HEADS, C_NOPE + C_V)
    wk = jnp.pad(wkv[:, :, :C_NOPE], ((0, 0), (0, 0), (0, C_HEAD_PAD - C_NOPE)))
    wk = wk.reshape(C_KV_RANK, C_HEADS * C_HEAD_PAD)
    wv = wkv[:, :, C_NOPE:].reshape(C_KV_RANK, C_HEADS * C_V)
    return w_in_p.astype(BF16), wq.astype(BF16), jnp.concatenate([wk, wv], axis=1).astype(BF16)


def _moe_layer(h2, logits, x1, w_gu, w_dn, mod, gid, lng, lnb, alpha, tm):
    n_tok = h2.shape[0]
    wts, dest, row_tok, block_e = _route(logits, n_tok)
    xs = jnp.take(h2, row_tok, axis=0, mode='clip')
    ys = _moe_experts(block_e, xs, w_gu, w_dn)
    ya = jnp.take(ys, dest[:, 0], axis=0, mode='clip')
    yb = jnp.take(ys, dest[:, 1], axis=0, mode='clip')
    return _post_moe(x1, ya, yb, wts.astype(F32), mod, gid, lng, lnb, alpha, tm)


def kernel(x, c, ctx, c_ctx, w_ada, b_ada, ln_g, ln_b, ab_w_in, a_sink, b_rpb, ab_w_out, cd_w_in, c_q_norm, c_w_uq, c_kv_norm, c_w_ukv, d_lambda, d_subln, cd_w_out, w_group, b_group, w_exp_router, b_exp_router, w_gate_up, w_down):
    batch, seq, d = x.shape
    ctx_len = ctx.shape[1]
    depth = w_ada.shape[0]
    alpha = (2 * depth) ** 0.25
    n_lat = batch * seq
    n_ctx = batch * ctx_len
    tm = 512
    tq = DENSE_TQ
    assert depth == 2 and seq % NA_Q == 0 and seq % tq == 0 and n_ctx % tm == 0
    assert seq % PROJ_CD_TILE == 0 and ctx_len % PROJ_CD_TILE == 0
    assert seq % (WIN_GROUP * A_BLOCK) == 0 and seq // NA_Q >= 2
    key_tiles = (seq + ctx_len) // MXU_DEPTH
    tk = MXU_DEPTH * max(u for u in range(1, DENSE_TK_MAX // MXU_DEPTH + 1) if key_tiles % u == 0)
    n_chunks = (seq + ctx_len) // tk

    def unroll_for(cap):
        return max(u for u in range(1, cap + 1) if n_chunks % u == 0)
    lat_tiles = seq // tm

    def gid_lat(i):
        return i // lat_tiles

    def gid_ctx(i):
        return batch

    def gid_all(i):
        return jnp.minimum(i // lat_tiles, batch)

    c_all = jnp.zeros((8, d), F32).at[:batch].set(c).at[batch].set(c_ctx)
    mod = _ada(c_all, w_ada, b_ada).reshape(depth, 8, 6, d)

    cos64, sin64 = _rope_tables(seq, HEAD_DIM, 0)
    cos32, sin32 = _rope_tables(seq, C_ROPE, C_NOPE)

    def router_weights(l):
        wr = jnp.zeros((d, ROUTER_PAD), F32)
        wr = wr.at[:, :N_GROUPS].set(w_group[l]).at[:, N_GROUPS:N_GROUPS + N_EXPERTS].set(w_exp_router[l])
        br = jnp.zeros((1, ROUTER_PAD), F32)
        br = br.at[0, :N_GROUPS].set(b_group[l]).at[0, N_GROUPS:N_GROUPS + N_EXPERTS].set(b_exp_router[l])
        hi = wr.astype(BF16)
        lo = (wr - hi.astype(F32)).astype(BF16)
        return hi, lo, br

    x2d = x.reshape(n_lat, d)
    ctx2d = ctx.reshape(n_ctx, d)

    l = 0
    w_in_bf = ab_w_in[0].astype(BF16)
    aq, ak, av, bq, bk, bv = _proj_ab(x2d, 0, n_lat, mod[l], gid_lat, w_in_bf, cos64, sin64,
                                      lat_tiles, True, tm)
    aqx, akx, avx, bqx, bkx, bvx = _proj_ab(ctx2d, 0, n_ctx, mod[l], gid_ctx, w_in_bf, cos64, sin64,
                                            1, False, tm)
    sink = a_sink[0].astype(F32)
    oa = _window_attn(sink, aq, ak, av, akx, avx, batch, seq, ctx_len)
    tiles, tile_idx = _na_bias_tiles(b_rpb[0], seq // GRID_W)
    ob = _na_attn(bq, bk, bv, bkx, bvx, tiles, tile_idx, batch, seq, ctx_len)
    oax, obx = _ctx_ab_attn(sink, aqx, akx, avx, bqx, bkx, bvx, batch, ctx_len)

    n_all = n_lat + n_ctx
    w_out_bf = ab_w_out[0].astype(BF16)
    lng, lnb = ln_g[l, 0][None], ln_b[l, 0][None]
    wr_hi, wr_lo, br = router_weights(l)
    x1, h2, logits = _post_attn((oa, ob, x2d), (oax, obx, ctx2d), w_out_bf, mod[l], gid_all,
                                lng, lnb, wr_hi, wr_lo, br, n_lat, alpha, tm)
    xall = _moe_layer(h2, logits, x1, w_gate_up[l], w_down[l], mod[l], gid_all,
                      ln_g[l, 1][None], ln_b[l, 1][None], alpha, tm)

    l = 1
    lam_init = 0.8 - 0.6 * math.exp(-0.3 * l)
    lp = d_lambda[0].astype(F32)
    lam = (jnp.exp(jnp.sum(lp[0] * lp[1])) - jnp.exp(jnp.sum(lp[2] * lp[3])) + lam_init).reshape(1)
    w_in_p, wuq_p, wukv_p = _cd_weights(cd_w_in[0], c_w_uq[0], c_w_ukv[0])
    qn, kvn = c_q_norm[0][None].astype(F32), c_kv_norm[0][None].astype(F32)
    def with_ctx_identity(cos, sin):
        return (jnp.concatenate([cos, jnp.ones((ctx_len, LANES), F32)]),
                jnp.concatenate([sin, jnp.zeros((ctx_len, LANES), F32)]))

    tabs = with_ctx_identity(cos64, sin64) + with_ctx_identity(cos32, sin32)
    qc, kc, vct, dq, dk, dvt = _proj_cd(xall, mod[l], w_in_p, qn, wuq_p, kvn, wukv_p, tabs,
                                        batch, seq, ctx_len, PROJ_CD_TILE, tk)
    unroll = unroll_for(DENSE_UNROLL_MAX)
    oc_t = _mla_attn(qc, kc, vct, batch, seq, tq, tk, unroll, 1)
    od_t = _diff_attn(lam, dq, dk, dvt, d_subln[0].astype(F32).reshape(D_V, 1), 1.0 - lam_init,
                      batch, seq, tq, tk, unroll, 1)
    wr_hi, wr_lo, br = router_weights(l)
    x1, h2, logits = _post_attn((oc_t, od_t, xall), None, cd_w_out[0].astype(BF16), mod[l], gid_lat,
                                ln_g[l, 0][None], ln_b[l, 0][None], wr_hi, wr_lo, br,
                                n_lat, alpha, tm)
    out = _moe_layer(h2, logits, x1, w_gate_up[l], w_down[l], mod[l], gid_lat,
                     ln_g[l, 1][None], ln_b[l, 1][None], alpha, tm)
    return out.reshape(batch, seq, d)
```

```python
import functools
import math

import jax
import jax.numpy as jnp
from jax import lax
from jax.experimental import pallas as pl
from jax.experimental.pallas import tpu as pltpu

F32 = jnp.float32
BF16 = jnp.bfloat16

GRID_W = 64
HEAD_DIM = 64
ROPE_BASE = 10000.0
NEG_INF = -1e30
LOG2E = 1.4426950408889634

A_HEADS = 8
A_KV_HEADS = 2
A_WINDOW = 128
A_BLOCK = 128
B_HEADS = 8
NA_ROWS = 8
NA_COLS = 16
C_HEADS = 8
C_Q_RANK = 384
C_KV_RANK = 256
C_NOPE = 64
C_ROPE = 32
C_V = 64
D_HEADS = 4
D_QK = 64
D_V = 128
N_GROUPS = 4
EXP_PER_GROUP = 8
N_EXPERTS = N_GROUPS * EXP_PER_GROUP
TOP_K = 2
MOE_BLOCK = 512
MOE_SUB = 256
ROUTER_PAD = 128

LANES = 128
VMEM_LIMIT = 56 * 1024 * 1024


def _params(sem):
    return pltpu.CompilerParams(dimension_semantics=sem, vmem_limit_bytes=VMEM_LIMIT)


def _dot(a, b):
    return jnp.dot(a, b, preferred_element_type=F32)


def _dot_nt(a, b):
    return lax.dot_general(a, b, (((1,), (1,)), ((), ())), preferred_element_type=F32)


def _split_bf16(x):
    hi = x.astype(BF16)
    lo = (x - hi.astype(F32)).astype(BF16)
    return hi, lo


HI_HALF = 0xFFFF0000


def _pack_halves(x):
    w = x.shape[1] // 2
    lo = lax.bitcast_convert_type(x[:, :w].astype(BF16).astype(F32), jnp.uint32)
    hi = lax.bitcast_convert_type(x[:, w:].astype(BF16).astype(F32), jnp.uint32)
    return (lo >> 16) | (hi & jnp.uint32(HI_HALF))


def _unpack_halves(u):
    lo = lax.bitcast_convert_type(u << 16, F32)
    hi = lax.bitcast_convert_type(u & jnp.uint32(HI_HALF), F32)
    return jnp.concatenate([lo, hi], axis=1)


def _ada_kernel(c_ref, w_ref, b_ref, o_ref):
    c = c_ref[...]
    a = c / (1.0 + jnp.exp(-c))
    a_hi, a_lo = _split_bf16(a)
    w_hi, w_lo = _split_bf16(w_ref[0])
    acc = _dot(a_hi, w_hi) + _dot(a_lo, w_hi) + _dot(a_hi, w_lo)
    o_ref[0] = acc + b_ref[0]


def _ada(c_all, w_ada, b_ada):
    depth, d, n = w_ada.shape
    tn = 1536
    return pl.pallas_call(
        _ada_kernel,
        grid=(depth, n // tn),
        in_specs=[
            pl.BlockSpec((8, d), lambda l, j: (0, 0)),
            pl.BlockSpec((1, d, tn), lambda l, j: (l, 0, j)),
            pl.BlockSpec((1, 1, tn), lambda l, j: (l, 0, j)),
        ],
        out_specs=pl.BlockSpec((1, 8, tn), lambda l, j: (l, 0, j)),
        out_shape=jax.ShapeDtypeStruct((depth, 8, n), F32),
        compiler_params=_params(("arbitrary", "arbitrary")),
        name="ada",
    )(c_all, w_ada, b_ada.reshape(depth, 1, n))


def _rope_tile(t, cos, sin_signed, q):
    lane = lax.broadcasted_iota(jnp.int32, t.shape, 1)
    first = (lane & (2 * q - 1)) < q
    up = pltpu.roll(t, LANES - q, 1)
    dn = pltpu.roll(t, q, 1)
    return t * cos + jnp.where(first, up, dn) * sin_signed


def _rope_wide(t, cos, sin_signed, q):
    n = t.shape[1] // LANES
    return jnp.concatenate(
        [_rope_tile(t[:, i * LANES:(i + 1) * LANES], cos, sin_signed, q) for i in range(n)], axis=1)


def _modulate(x_ref, mod_ref, which):
    sh = mod_ref[0, 3 * which:3 * which + 1, :]
    sc = mod_ref[0, 3 * which + 1:3 * which + 2, :]
    return x_ref[...] * (1.0 + sc) + sh


def _proj_ab_kernel(x_ref, mod_ref, w_ref, cos_ref, sin_ref,
                    aq_ref, ak_ref, av_ref, bq_ref, bk_ref, bv_ref, *, rope):
    h = _modulate(x_ref, mod_ref, 0).astype(BF16)
    r = _dot(h, w_ref[...])
    aq = r[:, 0:512]
    ak = r[:, 512:640]
    if rope:
        cos = cos_ref[...]
        sin = sin_ref[...]
        aq = _rope_wide(aq, cos, sin, 16)
        ak = _rope_wide(ak, cos, sin, 16)
    scale = HEAD_DIM ** -0.5
    aq_ref[...] = (aq * scale).astype(BF16)
    ak_ref[...] = ak.astype(BF16)
    av_ref[...] = r[:, 640:768].astype(BF16)
    bq_ref[...] = (r[:, 768:1280] * scale).astype(BF16)
    bk_ref[...] = r[:, 1280:1792].astype(BF16)
    bv_ref[...] = r[:, 1792:2304].astype(BF16)


def _proj_ab(x2d, row0, nrows, mod, gid, w_bf, cos, sin, pos_blocks, rope, tm):
    d = x2d.shape[1]
    n_in = w_bf.shape[1]
    b0 = row0 // tm
    widths = (512, 128, 128, 512, 512, 512)
    return pl.pallas_call(
        functools.partial(_proj_ab_kernel, rope=rope),
        grid=(nrows // tm,),
        in_specs=[
            pl.BlockSpec((tm, d), lambda i: (b0 + i, 0)),
            pl.BlockSpec((1, 6, d), lambda i: (gid(i), 0, 0)),
            pl.BlockSpec((d, n_in), lambda i: (0, 0)),
            pl.BlockSpec((tm, LANES), lambda i: (i % pos_blocks, 0)),
            pl.BlockSpec((tm, LANES), lambda i: (i % pos_blocks, 0)),
        ],
        out_specs=[pl.BlockSpec((tm, w), lambda i: (i, 0)) for w in widths],
        out_shape=[jax.ShapeDtypeStruct((nrows, w), BF16) for w in widths],
        compiler_params=_params(("arbitrary",)),
        name="proj_ab",
    )(x2d, mod, w_bf, cos, sin)


def _softmax_pv(parts, sink_col):
    m = parts[0][0].max(-1, keepdims=True)
    for s, _ in parts[1:]:
        m = jnp.maximum(m, s.max(-1, keepdims=True))
    if sink_col is not None:
        m = jnp.maximum(m, sink_col)
    denom = None if sink_col is None else jnp.exp(sink_col - m)
    o = None
    for s, v in parts:
        e = jnp.exp(s - m)
        d = e.sum(-1, keepdims=True)
        denom = d if denom is None else denom + d
        pv = _dot(e.astype(BF16), v)
        o = pv if o is None else o + pv
    return o / denom


def _win_kernel(sink_ref, q_ref, kp_ref, kc_ref, kn_ref, vp_ref, vc_ref, vn_ref,
                kx_ref, vx_ref, o_ref, *, seq):
    step = pl.program_id(1)
    blk = A_BLOCK
    g_sz = A_HEADS // A_KV_HEADS
    qi = lax.broadcasted_iota(jnp.int32, (blk, 3 * blk), 0)
    kj = lax.broadcasted_iota(jnp.int32, (blk, 3 * blk), 1)
    in_window = jnp.abs(kj - blk - qi) <= A_WINDOW
    k_ext = jnp.concatenate([kp_ref[...], kc_ref[...], kn_ref[...]], axis=0)
    v_ext = jnp.concatenate([vp_ref[...], vc_ref[...], vn_ref[...]], axis=0)
    kx = kx_ref[...]
    vx = vx_ref[...]

    def scores(t, g):
        lo, hi = g * HEAD_DIM, (g + 1) * HEAD_DIM
        kpos = (step * WIN_GROUP + t) * blk + kj - blk
        valid = in_window & (kpos >= 0) & (kpos < seq)
        valid = jnp.concatenate([valid] * g_sz, axis=0)
        q = q_ref[t * blk:(t + 1) * blk, :]
        heads = [g * g_sz + i for i in range(g_sz)]
        qs = jnp.concatenate([q[:, h * HEAD_DIM:(h + 1) * HEAD_DIM] for h in heads], axis=0)
        s_loc = jnp.where(valid, _dot_nt(qs, k_ext[t * blk:(t + 3) * blk, lo:hi]), NEG_INF)
        return s_loc, _dot_nt(qs, kx[:, lo:hi])

    units = [(t, g) for t in range(WIN_GROUP) for g in range(A_KV_HEADS)]
    pending = scores(*units[0])
    outs = []
    for n, (t, g) in enumerate(units):
        s_loc, s_ctx = pending
        if n + 1 < len(units):
            pending = scores(*units[n + 1])
        lo, hi = g * HEAD_DIM, (g + 1) * HEAD_DIM
        sink = jnp.concatenate(
            [jnp.full((blk, 1), sink_ref[g * g_sz + i], F32) for i in range(g_sz)], axis=0)
        o = _softmax_pv([(s_ctx, vx[:, lo:hi]), (s_loc, v_ext[t * blk:(t + 3) * blk, lo:hi])], sink)
        outs.extend(o[i * blk:(i + 1) * blk] for i in range(g_sz))
        if g == A_KV_HEADS - 1:
            o_ref[t * blk:(t + 1) * blk, :] = jnp.concatenate(outs, axis=1).astype(o_ref.dtype)
            outs = []


WIN_GROUP = 16


def _window_attn(sink, aq, ak, av, akx, avx, batch, seq, ctx_len):
    nblk = seq // A_BLOCK
    nstep = nblk // WIN_GROUP
    kvw = A_KV_HEADS * HEAD_DIM
    qw = A_HEADS * HEAD_DIM
    rows = WIN_GROUP * A_BLOCK

    def edge(delta):
        return lambda b, n: (b * nblk + jnp.clip(n * WIN_GROUP + delta, 0, nblk - 1), 0)

    def own(b, n):
        return (b * nstep + n, 0)

    kv_specs = [pl.BlockSpec((A_BLOCK, kvw), edge(-1)), pl.BlockSpec((rows, kvw), own),
                pl.BlockSpec((A_BLOCK, kvw), edge(WIN_GROUP))]
    return pl.pallas_call(
        functools.partial(_win_kernel, seq=seq),
        grid=(batch, nstep),
        in_specs=[pl.BlockSpec(memory_space=pltpu.SMEM), pl.BlockSpec((rows, qw), own)]
        + kv_specs + kv_specs
        + [pl.BlockSpec((ctx_len, kvw), lambda b, n: (b, 0)),
           pl.BlockSpec((ctx_len, kvw), lambda b, n: (b, 0))],
        out_specs=pl.BlockSpec((rows, qw), own),
        out_shape=jax.ShapeDtypeStruct((batch * seq, qw), BF16),
        compiler_params=_params(("arbitrary", "arbitrary")),
        name="window_attn",
    )(sink, aq, ak, ak, ak, av, av, av, akx, avx)


NA_QROWS = 8
NA_Q = NA_QROWS * GRID_W
NA_KBLK = 4 * GRID_W
NA_K = 4 * NA_KBLK


NA_DR = 2 * NA_ROWS - 1


def _na_bias_tiles(rpb, rows):
    nj = rows // NA_QROWS
    kr = min(NA_ROWS, rows)
    col = jnp.arange(GRID_W)
    col_start = jnp.clip(col - NA_COLS // 2, 0, GRID_W - NA_COLS)
    cvalid = (col[None, :] >= col_start[:, None]) & (col[None, :] < col_start[:, None] + NA_COLS)
    dc = jnp.clip(col[None, :] - col[:, None], -(NA_COLS - 1), NA_COLS - 1) + (NA_COLS - 1)
    tiles = jnp.where(cvalid[None, None], rpb.astype(F32)[:, :, dc], NEG_INF)
    tiles = jnp.concatenate([tiles, jnp.full_like(tiles[:, :1], NEG_INF)], axis=1)
    idx = []
    for j in (0, min(1, nj - 1), nj - 1):
        r = j * NA_QROWS + jnp.arange(NA_QROWS)
        krow = j * NA_QROWS - NA_KBLK // GRID_W + jnp.arange(NA_K // GRID_W)
        start = jnp.clip(r - kr // 2, 0, rows - kr)
        rvalid = (krow[None, :] >= start[:, None]) & (krow[None, :] < start[:, None] + kr)
        dr = krow[None, :] - r[:, None] + (NA_ROWS - 1)
        idx.append(jnp.where(rvalid, dr, NA_DR))
    return tiles, jnp.stack(idx).reshape(-1).astype(jnp.int32)


def _na_kernel(idx_ref, q_ref, k0, k1, k2, k3, v0, v1, v2, v3, kx_ref, vx_ref, t_ref, o_ref, bias_scr,
               *, nj):
    j = pl.program_id(2)
    nkr = NA_K // GRID_W

    @pl.when((j == 0) | (j == 1) | (j == nj - 1))
    def _():
        variant = jnp.where(j == 0, 0, jnp.where(j == nj - 1, 2, 1))
        for hh in range(LANES // HEAD_DIM):
            for qr in range(NA_QROWS):
                for kr in range(nkr):
                    d = idx_ref[(variant * NA_QROWS + qr) * nkr + kr]
                    bias_scr[hh, qr * GRID_W:(qr + 1) * GRID_W, kr * GRID_W:(kr + 1) * GRID_W] = (
                        t_ref[hh, d])

    q = q_ref[...]
    k_all = jnp.concatenate([k0[...], k1[...], k2[...], k3[...]], axis=0)
    v_all = jnp.concatenate([v0[...], v1[...], v2[...], v3[...]], axis=0)
    kx = kx_ref[...]
    vx = vx_ref[...]
    qh_rows = NA_Q // 2
    k_span = 3 * NA_KBLK

    def scores(hh, half):
        lo, hi = hh * HEAD_DIM, (hh + 1) * HEAD_DIM
        q0, k0_ = half * qh_rows, half * NA_KBLK
        qh = q[q0:q0 + qh_rows, lo:hi]
        s_loc = _dot_nt(qh, k_all[k0_:k0_ + k_span, lo:hi]) + bias_scr[hh, q0:q0 + qh_rows, k0_:k0_ + k_span]
        return s_loc, _dot_nt(qh, kx[:, lo:hi])

    units = [(hh, half) for half in range(2) for hh in range(LANES // HEAD_DIM)]
    pending = scores(*units[0])
    for n, (hh, half) in enumerate(units):
        s_loc, s_ctx = pending
        if n + 1 < len(units):
            pending = scores(*units[n + 1])
        lo, hi = hh * HEAD_DIM, (hh + 1) * HEAD_DIM
        k0_ = half * NA_KBLK
        o = _softmax_pv([(s_ctx, vx[:, lo:hi]), (s_loc, v_all[k0_:k0_ + k_span, lo:hi])], None)
        o_ref[half * qh_rows:(half + 1) * qh_rows, lo:hi] = o.astype(o_ref.dtype)


def _na_attn(bq, bk, bv, bkx, bvx, tiles, tile_idx, batch, seq, ctx_len):
    nj = seq // NA_Q
    nkb = seq // NA_KBLK
    hp = B_HEADS * HEAD_DIM // LANES
    per = LANES // HEAD_DIM

    def kb(t):
        return lambda p, b, j, idx: (b * nkb + jnp.clip(2 * j - 1 + t, 0, nkb - 1), p)

    kv_specs = [pl.BlockSpec((NA_KBLK, LANES), kb(t)) for t in range(4)]
    grid_spec = pltpu.PrefetchScalarGridSpec(
        num_scalar_prefetch=1,
        grid=(hp, batch, nj),
        in_specs=[pl.BlockSpec((NA_Q, LANES), lambda p, b, j, idx: (b * nj + j, p))]
        + kv_specs + kv_specs
        + [pl.BlockSpec((ctx_len, LANES), lambda p, b, j, idx: (b, p)),
           pl.BlockSpec((ctx_len, LANES), lambda p, b, j, idx: (b, p)),
           pl.BlockSpec((per, NA_DR + 1, GRID_W, GRID_W), lambda p, b, j, idx: (p, 0, 0, 0))],
        out_specs=pl.BlockSpec((NA_Q, LANES), lambda p, b, j, idx: (b * nj + j, p)),
        scratch_shapes=[pltpu.VMEM((per, NA_Q, NA_K), F32)],
    )
    return pl.pallas_call(
        functools.partial(_na_kernel, nj=nj),
        grid_spec=grid_spec,
        out_shape=jax.ShapeDtypeStruct((batch * seq, B_HEADS * HEAD_DIM), BF16),
        compiler_params=_params(("arbitrary", "arbitrary", "arbitrary")),
        name="na_attn",
    )(tile_idx, bq, bk, bk, bk, bk, bv, bv, bv, bv, bkx, bvx, tiles)


def _ctx_ab_kernel(sink_ref, aq_ref, ak_ref, av_ref, bq_ref, bk_ref, bv_ref, oa_ref, ob_ref):
    ctx_len = aq_ref.shape[0]
    g_sz = A_HEADS // A_KV_HEADS
    aq = aq_ref[...]
    ak = ak_ref[...]
    av = av_ref[...]
    outs = []
    for g in range(A_KV_HEADS):
        lo, hi = g * HEAD_DIM, (g + 1) * HEAD_DIM
        heads = [g * g_sz + i for i in range(g_sz)]
        qs = jnp.concatenate([aq[:, h * HEAD_DIM:(h + 1) * HEAD_DIM] for h in heads], axis=0)
        sink = jnp.concatenate(
            [jnp.full((ctx_len, 1), sink_ref[h], F32) for h in heads], axis=0)
        o = _softmax_pv([(_dot_nt(qs, ak[:, lo:hi]), av[:, lo:hi])], sink)
        outs.extend(o[i * ctx_len:(i + 1) * ctx_len] for i in range(g_sz))
    oa_ref[...] = jnp.concatenate(outs, axis=1).astype(oa_ref.dtype)
    bq = bq_ref[...]
    bk = bk_ref[...]
    bv = bv_ref[...]
    outs = []
    for h in range(B_HEADS):
        lo, hi = h * HEAD_DIM, (h + 1) * HEAD_DIM
        outs.append(_softmax_pv([(_dot_nt(bq[:, lo:hi], bk[:, lo:hi]), bv[:, lo:hi])], None))
    ob_ref[...] = jnp.concatenate(outs, axis=1).astype(ob_ref.dtype)


def _ctx_ab_attn(sink, aq, ak, av, bq, bk, bv, batch, ctx_len):
    def spec(w):
        return pl.BlockSpec((ctx_len, w), lambda b: (b, 0))

    return pl.pallas_call(
        _ctx_ab_kernel,
        grid=(batch,),
        in_specs=[pl.BlockSpec(memory_space=pltpu.SMEM)]
        + [spec(a.shape[1]) for a in (aq, ak, av, bq, bk, bv)],
        out_specs=[spec(aq.shape[1]), spec(bq.shape[1])],
        out_shape=[jax.ShapeDtypeStruct(aq.shape, BF16), jax.ShapeDtypeStruct(bq.shape, BF16)],
        compiler_params=_params(("arbitrary",)),
        name="ctx_ab_attn",
    )(sink, aq, ak, av, bq, bk, bv)


def _layer_norm(t, g, b):
    mu = jnp.mean(t, -1, keepdims=True)
    c = t - mu
    var = jnp.mean(c * c, -1, keepdims=True)
    return c * lax.rsqrt(var + 1e-5) * g + b


def _post_attn_kernel(*refs, alpha, lat_tiles, with_ctx, lat_transposed):
    n_src = 6 if with_ctx else 3
    srcs = refs[:n_src]
    (w_ref, mod_ref, lng_ref, lnb_ref, wr_hi_ref, wr_lo_ref, br_ref,
     x1_ref, h2_ref, lg_ref) = refs[n_src:]

    def rows_of(o_ref):
        if o_ref.ndim == 2:
            return o_ref[...]
        return o_ref[0].astype(F32).T.astype(BF16)

    def run(o1_ref, o2_ref, x_ref):
        half = w_ref.shape[0] // 2
        y = _dot(rows_of(o1_ref), w_ref[0:half, :]) + _dot(rows_of(o2_ref), w_ref[half:, :])
        g1 = mod_ref[0, 2:3, :]
        x1 = _layer_norm(alpha * x_ref[...] + g1 * y, lng_ref[...], lnb_ref[...])
        x1_ref[...] = x1
        h2 = x1 * (1.0 + mod_ref[0, 4:5, :]) + mod_ref[0, 3:4, :]
        h_hi, h_lo = _split_bf16(h2)
        h2_ref[...] = _pack_halves(h2)
        both = _dot(h_hi, wr_lo_ref[...])
        lg_ref[...] = (both[:, :ROUTER_PAD] + both[:, ROUTER_PAD:] + _dot(h_lo, wr_hi_ref[...])
                       + br_ref[...])

    if not with_ctx:
        run(*srcs)
    else:
        i = pl.program_id(0)
        pl.when(i < lat_tiles)(lambda: run(*srcs[:3]))
        pl.when(i >= lat_tiles)(lambda: run(*srcs[3:]))


def _post_attn(lat, ctx_src, w_bf, mod, gid, lng, lnb, wr_hi, wr_lo, br, n_lat, alpha, tm):
    half = w_bf.shape[0] // 2
    d = lat[2].shape[1]
    lat_tiles = n_lat // tm
    n_rows = n_lat + (ctx_src[0].shape[0] if ctx_src is not None else 0)
    lat_transposed = lat[0].ndim == 3

    def lat_map(i):
        return (jnp.minimum(i, lat_tiles - 1), 0)

    def ctx_map(i):
        return (jnp.maximum(i - lat_tiles, 0), 0)

    def src_specs(index_map):
        return [pl.BlockSpec((tm, half), index_map), pl.BlockSpec((tm, half), index_map),
                pl.BlockSpec((tm, d), index_map)]

    in_specs = src_specs(lat_map)
    if lat_transposed:
        per_batch = lat[0].shape[2] // tm
        t_spec = pl.BlockSpec((1, half, tm), lambda i: (i // per_batch, 0, i % per_batch))
        in_specs[0] = in_specs[1] = t_spec
    args = list(lat)
    if ctx_src is not None:
        in_specs += src_specs(ctx_map)
        args += list(ctx_src)

    def const(shape):
        return pl.BlockSpec(shape, lambda i: (0,) * len(shape))

    in_specs += [const((2 * half, d)),
                 pl.BlockSpec((1, 6, d), lambda i: (gid(i), 0, 0)),
                 const((1, d)), const((1, d)),
                 const((d, ROUTER_PAD)), const((d, 2 * ROUTER_PAD)), const((1, ROUTER_PAD))]
    args += [w_bf, mod, lng, lnb, wr_hi, wr_lo, br]
    widths = (d, d // 2, ROUTER_PAD)
    dtypes = (F32, jnp.uint32, F32)
    return pl.pallas_call(
        functools.partial(_post_attn_kernel, alpha=alpha, lat_tiles=lat_tiles,
                          with_ctx=ctx_src is not None, lat_transposed=lat_transposed),
        grid=(n_rows // tm,),
        in_specs=in_specs,
        out_specs=[pl.BlockSpec((tm, w), lambda i: (i, 0)) for w in widths],
        out_shape=[jax.ShapeDtypeStruct((n_rows, w), t) for w, t in zip(widths, dtypes)],
        compiler_params=_params(("arbitrary",)),
        name="post_attn",
    )(*args)


def _moe_kernel(be_ref, xs_ref, wgu_ref, wdn_ref, ys_ref, wgu_bf, wdn_bf):
    i = pl.program_id(0)
    used = i < be_ref[pl.num_programs(0)]
    prev = be_ref[jnp.maximum(i - 1, 0)]
    changed = (i == 0) | (be_ref[i] != prev)

    @pl.when(changed & used)
    def _():
        wgu_bf[...] = wgu_ref[0].astype(BF16)
        wdn_bf[...] = wdn_ref[0].astype(BF16)

    @pl.when(used)
    def _():
        de = wdn_bf.shape[0]
        subs = [slice(r, r + MOE_SUB) for r in range(0, MOE_BLOCK, MOE_SUB)]
        gus = [_dot(_unpack_halves(xs_ref[sl, :]).astype(BF16), wgu_bf[...]) for sl in subs]
        for sl, gu in zip(subs, gus):
            gate = gu[:, :de]
            up = gu[:, de:]
            act = gate / (1.0 + jnp.exp(-gate)) * up
            ys_ref[sl, :] = _pack_halves(_dot(act.astype(BF16), wdn_bf[...]))

    @pl.when(jnp.logical_not(used))
    def _():
        ys_ref[...] = jnp.zeros(ys_ref.shape, ys_ref.dtype)


def _moe_experts(block_e, xs, w_gu, w_dn):
    rows, dw = xs.shape
    d = 2 * dw
    nb = rows // MOE_BLOCK
    de = w_dn.shape[1]
    grid_spec = pltpu.PrefetchScalarGridSpec(
        num_scalar_prefetch=1,
        grid=(nb,),
        in_specs=[
            pl.BlockSpec((MOE_BLOCK, dw), lambda i, be: (i, 0)),
            pl.BlockSpec((1, d, 2 * de), lambda i, be: (be[i], 0, 0)),
            pl.BlockSpec((1, de, d), lambda i, be: (be[i], 0, 0)),
        ],
        out_specs=pl.BlockSpec((MOE_BLOCK, dw), lambda i, be: (i, 0)),
        scratch_shapes=[pltpu.VMEM((d, 2 * de), BF16), pltpu.VMEM((de, d), BF16)],
    )
    return pl.pallas_call(
        _moe_kernel,
        grid_spec=grid_spec,
        out_shape=jax.ShapeDtypeStruct((rows, dw), jnp.uint32),
        compiler_params=_params(("arbitrary",)),
        name="moe_experts",
    )(block_e, xs, w_gu, w_dn)


ROUTE_TILE = 512


def _route_kernel(lg_ref, rec_ref, cnt_ref, carry):
    i = pl.program_id(0)

    @pl.when(i == 0)
    def _():
        carry[...] = jnp.zeros(carry.shape, F32)

    lg = lg_ref[...]
    tm = lg.shape[0]
    lane = lax.broadcasted_iota(jnp.int32, lg.shape, 1)

    def first_lane(mask):
        return jnp.min(jnp.where(mask, lane, ROUTER_PAD), axis=1, keepdims=True)

    is_g = lane < N_GROUPS
    g_log = jnp.where(is_g, lg, NEG_INF)
    g_max = jnp.max(g_log, axis=1, keepdims=True)
    g_sum = jnp.sum(jnp.where(is_g, jnp.exp(lg - g_max), 0.0), axis=1, keepdims=True)
    g_val = 1.0 / g_sum
    g_idx = first_lane(g_log == g_max)
    lo = N_GROUPS + EXP_PER_GROUP * g_idx
    sel = (lane >= lo) & (lane < lo + EXP_PER_GROUP)
    e_log = jnp.where(sel, lg, NEG_INF)
    e_max = jnp.max(e_log, axis=1, keepdims=True)
    e_exp = jnp.where(sel, jnp.exp(lg - e_max), 0.0)
    e_prob = e_exp / jnp.sum(e_exp, axis=1, keepdims=True)
    p1 = jnp.where(sel, e_prob, -1.0)
    v1 = jnp.max(p1, axis=1, keepdims=True)
    i1 = first_lane(p1 == v1)
    p2 = jnp.where(lane == i1, -1.0, p1)
    v2 = jnp.max(p2, axis=1, keepdims=True)
    i2 = first_lane(p2 == v2)
    norm = g_val / (v1 + v2)

    hot1 = lane == i1
    hot2 = lane == i2
    hot = jnp.where(hot1 | hot2, 1.0, 0.0)
    rows = lax.broadcasted_iota(jnp.int32, (tm, tm), 0)
    cols = lax.broadcasted_iota(jnp.int32, (tm, tm), 1)
    before = jnp.where(cols < rows, 1.0, 0.0).astype(BF16)
    prefix = _dot(before, hot.astype(BF16)) + carry[0:1, :]
    r1 = jnp.sum(jnp.where(hot1, prefix, 0.0), axis=1, keepdims=True)
    r2 = jnp.sum(jnp.where(hot2, prefix, 0.0), axis=1, keepdims=True)
    carry[0:1, :] = carry[0:1, :] + jnp.sum(hot, axis=0, keepdims=True)

    fields = [(i1 - N_GROUPS).astype(F32), (i2 - N_GROUPS).astype(F32), r1, r2, v1 * norm, v2 * norm]
    rec = jnp.zeros(lg.shape, F32)
    for k, f in enumerate(fields):
        rec = jnp.where(lane == k, f, rec)
    rec_ref[...] = rec

    @pl.when(i == pl.num_programs(0) - 1)
    def _():
        cnt_ref[...] = carry[...]


def _route(logits, n_tok):
    rec, cnt = pl.pallas_call(
        _route_kernel,
        grid=(n_tok // ROUTE_TILE,),
        in_specs=[pl.BlockSpec((ROUTE_TILE, ROUTER_PAD), lambda i: (i, 0))],
        out_specs=[pl.BlockSpec((ROUTE_TILE, ROUTER_PAD), lambda i: (i, 0)),
                   pl.BlockSpec((8, ROUTER_PAD), lambda i: (0, 0))],
        out_shape=[jax.ShapeDtypeStruct((n_tok, ROUTER_PAD), F32),
                   jax.ShapeDtypeStruct((8, ROUTER_PAD), F32)],
        scratch_shapes=[pltpu.VMEM((8, ROUTER_PAD), F32)],
        compiler_params=_params(("arbitrary",)),
        name="route",
    )(logits)
    experts = rec[:, 0:2].astype(jnp.int32)
    rank = rec[:, 2:4].astype(jnp.int32)
    wts = rec[:, 4:6]
    counts = cnt[0, N_GROUPS:N_GROUPS + N_EXPERTS].astype(jnp.int32)

    n_asg = n_tok * TOP_K
    pcounts = ((counts + MOE_BLOCK - 1) // MOE_BLOCK) * MOE_BLOCK
    pends = jnp.cumsum(pcounts)
    pstarts = pends - pcounts
    dest = pstarts[experts] + rank
    nb = -(-n_asg // MOE_BLOCK) + N_EXPERTS
    block_start = jnp.arange(nb, dtype=jnp.int32) * MOE_BLOCK
    block_e = jnp.minimum((pends[None, :] <= block_start[:, None]).sum(1), N_EXPERTS - 1).astype(jnp.int32)
    block_e = jnp.concatenate([block_e, (pends[-1:] // MOE_BLOCK).astype(jnp.int32)])
    tok = jnp.broadcast_to(jnp.arange(n_tok, dtype=jnp.int32)[:, None], (n_tok, TOP_K))
    filler = jnp.arange(nb * MOE_BLOCK, dtype=jnp.int32) % n_tok
    row_tok = filler.at[dest.reshape(-1)].set(tok.reshape(-1), unique_indices=True)
    return wts, dest, row_tok, block_e


def _post_moe_kernel(x_ref, y_ref, w_ref, mod_ref, lng_ref, lnb_ref, o_ref, *, alpha):
    w = w_ref[...]
    half = y_ref.shape[1] // 2
    y = (w[:, 0:1] * _unpack_halves(y_ref[:, :half]) + w[:, 1:2] * _unpack_halves(y_ref[:, half:]))
    g2 = mod_ref[0, 5:6, :]
    o_ref[...] = _layer_norm(alpha * x_ref[...] + g2 * y, lng_ref[...], lnb_ref[...])


def _post_moe(x1, y_pair, wts, mod, gid, lng, lnb, alpha, tm):
    rows, d = x1.shape
    row = pl.BlockSpec((tm, d), lambda i: (i, 0))
    vec = pl.BlockSpec((1, d), lambda i: (0, 0))
    return pl.pallas_call(
        functools.partial(_post_moe_kernel, alpha=alpha),
        grid=(rows // tm,),
        in_specs=[row, row,
                  pl.BlockSpec((tm, TOP_K), lambda i: (i, 0)),
                  pl.BlockSpec((1, 6, d), lambda i: (gid(i), 0, 0)),
                  vec, vec],
        out_specs=row,
        out_shape=jax.ShapeDtypeStruct((rows, d), F32),
        compiler_params=_params(("arbitrary",)),
        name="post_moe",
    )(x1, y_pair, wts, mod, lng, lnb)


CD_COLS = dict(cq=(0, 384), ckv=(384, 640), dq=(640, 1152), dk=(1152, 1664), dv=(1664, 2176),
               kr=(2176, 2304))
C_HEAD_PAD = LANES


def _rms(t, g, eps=1e-6):
    return t * lax.rsqrt(jnp.mean(t * t, -1, keepdims=True) + eps) * g


def _proj_cd_kernel(x_ref, mod_ref, w_ref, qn_ref, wuq_ref, kvn_ref, wukv_ref,
                    cos64_ref, sin64_ref, cos32_ref, sin32_ref,
                    qc_ref, kc_ref, vct_ref, dq_ref, dk_ref, dvt_ref):
    h = _modulate(x_ref, mod_ref, 0).astype(BF16)
    r = _dot(h, w_ref[...])

    def cols(name):
        lo, hi = CD_COLS[name]
        return r[:, lo:hi]

    q = _dot(_rms(cols('cq'), qn_ref[...]).astype(BF16), wuq_ref[...])
    kv = _dot(_rms(cols('ckv'), kvn_ref[...]).astype(BF16), wukv_ref[...])
    c32, s32 = cos32_ref[...], sin32_ref[...]
    c64, s64 = cos64_ref[...], sin64_ref[...]
    q = _rope_wide(q, c32, s32, C_ROPE // 4)
    kr = _rope_tile(cols('kr'), c32, s32, C_ROPE // 4)
    dq = _rope_wide(cols('dq'), c64, s64, D_QK // 4)
    dk = _rope_wide(cols('dk'), c64, s64, D_QK // 4)
    kw = C_HEADS * C_HEAD_PAD
    qc_ref[0] = (q * ((C_NOPE + C_ROPE) ** -0.5 * LOG2E)).astype(BF16)
    kc_ref[0] = (kv[:, :kw] + jnp.concatenate([kr] * C_HEADS, axis=1)).astype(BF16)
    dq_ref[0] = (dq * (D_QK ** -0.5 * LOG2E)).astype(BF16)
    dk_ref[0] = dk.astype(BF16)
    _store_vt(vct_ref, kv[:, kw:], C_HEADS, C_V)
    _store_vt(dvt_ref, cols('dv'), D_HEADS, D_V)


def _store_vt(vt_ref, v, heads, dv):
    tm = v.shape[0]
    v_t = v.T.astype(BF16)
    row = lax.broadcasted_iota(jnp.int32, (ONES_ROWS, tm), 0)
    extra = jnp.where(row == 0, 1.0, 0.0).astype(BF16)
    for h in range(heads):
        vt_ref[0, h, 0, 0:dv, :] = v_t[h * dv:(h + 1) * dv]
        vt_ref[0, h, 0, dv:dv + ONES_ROWS, :] = extra


def _proj_cd(xall, mod, w_bf, qn, wuq, kvn, wukv, tabs, batch, seq, ctx_len, tm, tk):
    d = xall.shape[1]
    lat_t, ctx_t = seq // tm, ctx_len // tm
    n_keys = seq + ctx_len
    per_chunk = tk // tm
    n_chunks = n_keys // tk
    widths = (C_HEADS * C_HEAD_PAD, C_HEADS * C_HEAD_PAD, 2 * D_HEADS * D_QK, 2 * D_HEADS * D_QK)

    def src(b, t):
        return (jnp.where(t < lat_t, b * lat_t + t, batch * lat_t + b * ctx_t + t - lat_t), 0)

    def full(a):
        return pl.BlockSpec(a.shape, lambda b, t: (0,) * a.ndim)

    def row_spec(w):
        return pl.BlockSpec((1, tm, w), lambda b, t: (b, t, 0))

    def row_shape(w):
        return jax.ShapeDtypeStruct((batch, n_keys, w), BF16)

    def vt_spec(heads, dv):
        return pl.BlockSpec((1, heads, 1, dv + ONES_ROWS, tm),
                            lambda b, t: (b, 0, t // per_chunk, 0, t % per_chunk))

    def vt_shape(heads, dv):
        return jax.ShapeDtypeStruct((batch, heads, n_chunks, dv + ONES_ROWS, tk), BF16)

    tab = pl.BlockSpec((tm, LANES), lambda b, t: (t, 0))
    return pl.pallas_call(
        _proj_cd_kernel,
        grid=(batch, lat_t + ctx_t),
        in_specs=[pl.BlockSpec((tm, d), src),
                  pl.BlockSpec((1, 6, d), lambda b, t: (jnp.where(t < lat_t, b, batch), 0, 0)),
                  full(w_bf), full(qn), full(wuq), full(kvn), full(wukv), tab, tab, tab, tab],
        out_specs=[row_spec(widths[0]), row_spec(widths[1]), vt_spec(C_HEADS, C_V),
                   row_spec(widths[2]), row_spec(widths[3]), vt_spec(D_HEADS, D_V)],
        out_shape=[row_shape(widths[0]), row_shape(widths[1]), vt_shape(C_HEADS, C_V),
                   row_shape(widths[2]), row_shape(widths[3]), vt_shape(D_HEADS, D_V)],
        compiler_params=_params(("arbitrary", "arbitrary")),
        name="proj_cd",
    )(xall, mod, w_bf, qn, wuq, kvn, wukv, *tabs)


DENSE_SHIFT = 80.0
DENSE_L_MIN = 2.0 ** -60
DENSE_L_MAX = 2.0 ** 120
DENSE_TQ = 1024
DENSE_TK_MAX = 1280
DENSE_UNROLL_MAX = 13
MXU_DEPTH = 256
PROJ_CD_TILE = 256
ONES_ROWS = 16


def _sq_norm_row(x):
    ones = jnp.ones((8, x.shape[1]), BF16)
    return _dot_nt(ones, (x * x).astype(BF16))[0:1] * 1.02


def _key_norm_max(kchunk, n_chunks, masks):
    def body(j, mx):
        kf = kchunk(j).astype(F32)
        ksq = kf * kf
        out = []
        for msk, cur in zip(masks, mx):
            part = ksq if msk is None else jnp.where(msk, ksq, 0.0)
            rn = jnp.sum(part, axis=1, keepdims=True)
            out.append(jnp.maximum(cur, jnp.max(rn, axis=0, keepdims=True)))
        return tuple(out)

    mx = lax.fori_loop(0, n_chunks, body, tuple(jnp.zeros((1, 1), F32) for _ in masks))
    return [jnp.sqrt(v) * 1.01 for v in mx]


def _safe_online(q, kchunk, vt_chunk, n_chunks, m_scr, l_scr, acc):
    m_scr[...] = jnp.full(m_scr.shape, NEG_INF, F32)
    l_scr[...] = jnp.zeros(l_scr.shape, F32)
    acc[...] = jnp.zeros(acc.shape, F32)

    def body(j, carry):
        s = _dot_nt(kchunk(j), q)
        m_prev = m_scr[...]
        m_new = jnp.maximum(m_prev, s.max(0, keepdims=True))
        alpha = jnp.exp2(m_prev - m_new)
        p = jnp.exp2(s - m_new)
        l_scr[...] = alpha * l_scr[...] + p.sum(0, keepdims=True)
        acc[...] = alpha * acc[...] + _dot(vt_chunk(j), p.astype(BF16))
        m_scr[...] = m_new
        return carry

    lax.fori_loop(0, n_chunks, body, 0)


def _denominators_ok(*ls):
    ok = None
    for l in ls:
        cur = (l > DENSE_L_MIN) & (l < DENSE_L_MAX)
        ok = cur if ok is None else ok & cur
    return jnp.max(jnp.where(ok, 0.0, 1.0)) == 0.0


MLA_HEADS_PER_STEP = 2


def _mla_kernel(q_ref, k_ref, vt_ref, o_ref, kmax_scr, acc_scr, m_scr, l_scr,
                *, n_chunks, tk, unroll, depth):
    hp = acc_scr.shape[0]

    def kchunk(j, h):
        return k_ref[0, pl.ds(pl.multiple_of(j * tk, tk), tk), h * C_HEAD_PAD:(h + 1) * C_HEAD_PAD]

    @pl.when(pl.program_id(2) == 0)
    def _():
        for h in range(hp):
            (kmax,) = _key_norm_max(lambda j: kchunk(j, h), n_chunks, [None])
            kmax_scr[h:h + 1, :] = jnp.broadcast_to(kmax, (1, LANES))

    qs = [q_ref[0, :, h * C_HEAD_PAD:(h + 1) * C_HEAD_PAD] for h in range(hp)]
    ms = [jnp.sqrt(_sq_norm_row(qs[h].astype(F32))) * kmax_scr[h:h + 1, 0:1] - DENSE_SHIFT
          for h in range(hp)]
    acc_scr[...] = jnp.zeros(acc_scr.shape, F32)

    def scores(j):
        return [_dot_nt(kchunk(j, h), qs[h]) for h in range(hp)]

    def body(it, carry):
        j0 = it * unroll
        pend = [scores(j0 + u) for u in range(min(depth, unroll))]
        accs = [acc_scr[h] for h in range(hp)]
        for u in range(unroll):
            ss = pend.pop(0)
            if u + depth < unroll:
                pend.append(scores(j0 + u + depth))
            for h in range(hp):
                accs[h] = accs[h] + _dot(vt_ref[0, h, j0 + u], jnp.exp2(ss[h] - ms[h]).astype(BF16))
        for h in range(hp):
            acc_scr[h] = accs[h]
        return carry

    lax.fori_loop(0, n_chunks // unroll, body, 0)
    ls = [acc_scr[h, C_V:C_V + 1, :] for h in range(hp)]
    for h in range(hp):
        o_ref[0, h * C_V:(h + 1) * C_V, :] = (acc_scr[h, 0:C_V, :] / ls[h]).astype(o_ref.dtype)

    @pl.when(jnp.logical_not(_denominators_ok(*ls)))
    def _():
        for h in range(hp):
            _safe_online(qs[h], lambda j: kchunk(j, h), lambda j: vt_ref[0, h, j], n_chunks,
                         m_scr, l_scr, acc_scr.at[h])
            o_ref[0, h * C_V:(h + 1) * C_V, :] = (acc_scr[h, 0:C_V, :] / l_scr[...]).astype(o_ref.dtype)


def _mla_attn(qc, kc, vt, batch, seq, tq, tk, unroll, depth):
    n_keys = kc.shape[1]
    nq, n_chunks = seq // tq, n_keys // tk
    rows = C_V + ONES_ROWS
    hp = MLA_HEADS_PER_STEP
    return pl.pallas_call(
        functools.partial(_mla_kernel, n_chunks=n_chunks, tk=tk, unroll=unroll, depth=depth),
        grid=(batch, C_HEADS // hp, nq),
        in_specs=[
            pl.BlockSpec((1, tq, hp * C_HEAD_PAD), lambda b, h, i: (b, i, h)),
            pl.BlockSpec((1, n_keys, hp * C_HEAD_PAD), lambda b, h, i: (b, 0, h)),
            pl.BlockSpec((1, hp, n_chunks, rows, tk), lambda b, h, i: (b, h, 0, 0, 0)),
        ],
        out_specs=pl.BlockSpec((1, hp * C_V, tq), lambda b, h, i: (b, h, i)),
        out_shape=jax.ShapeDtypeStruct((batch, C_HEADS * C_V, seq), BF16),
        scratch_shapes=[pltpu.VMEM((8, LANES), F32), pltpu.VMEM((hp, rows, tq), F32),
                        pltpu.VMEM((1, tq), F32), pltpu.VMEM((1, tq), F32)],
        compiler_params=_params(("arbitrary", "arbitrary", "arbitrary")),
        name="mla_attn",
    )(qc, kc, vt)


def _diff_kernel(lam_ref, q_ref, k_ref, vt_ref, subln_ref, o_ref, kmax_scr, acc1_scr, acc2_scr,
                 m_scr, l_scr, *, n_chunks, tk, unroll, depth, out_scale):
    def kchunk(j):
        return k_ref[0, pl.ds(pl.multiple_of(j * tk, tk), tk), :]

    @pl.when(pl.program_id(2) == 0)
    def _():
        klane = lax.broadcasted_iota(jnp.int32, (tk, 2 * D_QK), 1)
        k1, k2 = _key_norm_max(kchunk, n_chunks, [klane < D_QK, klane >= D_QK])
        kmax_scr[0:1, :] = jnp.broadcast_to(k1, (1, LANES))
        kmax_scr[1:2, :] = jnp.broadcast_to(k2, (1, LANES))

    q = q_ref[0]
    qlane = lax.broadcasted_iota(jnp.int32, q.shape, 1)
    zero = jnp.zeros_like(q)
    q1 = jnp.where(qlane < D_QK, q, zero)
    q2 = jnp.where(qlane >= D_QK, q, zero)
    m1 = jnp.sqrt(_sq_norm_row(q1.astype(F32))) * kmax_scr[0:1, 0:1] - DENSE_SHIFT
    m2 = jnp.sqrt(_sq_norm_row(q2.astype(F32))) * kmax_scr[1:2, 0:1] - DENSE_SHIFT
    acc1_scr[...] = jnp.zeros(acc1_scr.shape, F32)
    acc2_scr[...] = jnp.zeros(acc2_scr.shape, F32)

    def scores(j):
        k = kchunk(j)
        return _dot_nt(k, q1), _dot_nt(k, q2)

    def body(it, carry):
        j0 = it * unroll
        pend = [scores(j0 + u) for u in range(min(depth, unroll))]
        a1 = acc1_scr[...]
        a2 = acc2_scr[...]
        for u in range(unroll):
            s1, s2 = pend.pop(0)
            if u + depth < unroll:
                pend.append(scores(j0 + u + depth))
            vt = vt_ref[0, 0, j0 + u]
            a1 = a1 + _dot(vt, jnp.exp2(s1 - m1).astype(BF16))
            a2 = a2 + _dot(vt, jnp.exp2(s2 - m2).astype(BF16))
        acc1_scr[...] = a1
        acc2_scr[...] = a2
        return carry

    lax.fori_loop(0, n_chunks // unroll, body, 0)

    def finish(o1, o2):
        o = o1 - lam_ref[0] * o2
        o = o * lax.rsqrt(jnp.mean(o * o, 0, keepdims=True) + 1e-6) * subln_ref[...] * out_scale
        o_ref[0] = o.astype(o_ref.dtype)

    l1 = acc1_scr[D_V:D_V + 1, :]
    l2 = acc2_scr[D_V:D_V + 1, :]
    finish(acc1_scr[0:D_V, :] / l1, acc2_scr[0:D_V, :] / l2)

    @pl.when(jnp.logical_not(_denominators_ok(l1, l2)))
    def _():
        def vt_chunk(j):
            return vt_ref[0, 0, j]

        _safe_online(q1, kchunk, vt_chunk, n_chunks, m_scr, l_scr, acc1_scr)
        o1 = acc1_scr[0:D_V, :] / l_scr[...]
        _safe_online(q2, kchunk, vt_chunk, n_chunks, m_scr, l_scr, acc2_scr)
        finish(o1, acc2_scr[0:D_V, :] / l_scr[...])


def _diff_attn(lam, dq, dk, vt, subln_col, out_scale, batch, seq, tq, tk, unroll, depth):
    n_keys = dk.shape[1]
    nq, n_chunks = seq // tq, n_keys // tk
    rows = D_V + ONES_ROWS
    pair = 2 * D_QK
    return pl.pallas_call(
        functools.partial(_diff_kernel, n_chunks=n_chunks, tk=tk, unroll=unroll, depth=depth,
                          out_scale=out_scale),
        grid=(batch, D_HEADS, nq),
        in_specs=[
            pl.BlockSpec(memory_space=pltpu.SMEM),
            pl.BlockSpec((1, tq, pair), lambda b, h, i: (b, i, h)),
            pl.BlockSpec((1, n_keys, pair), lambda b, h, i: (b, 0, h)),
            pl.BlockSpec((1, 1, n_chunks, rows, tk), lambda b, h, i: (b, h, 0, 0, 0)),
            pl.BlockSpec((D_V, 1), lambda b, h, i: (0, 0)),
        ],
        out_specs=pl.BlockSpec((1, D_V, tq), lambda b, h, i: (b, h, i)),
        out_shape=jax.ShapeDtypeStruct((batch, D_HEADS * D_V, seq), BF16),
        scratch_shapes=[pltpu.VMEM((8, LANES), F32), pltpu.VMEM((rows, tq), F32),
                        pltpu.VMEM((rows, tq), F32), pltpu.VMEM((1, tq), F32), pltpu.VMEM((1, tq), F32)],
        compiler_params=_params(("arbitrary", "arbitrary", "arbitrary")),
        name="diff_attn",
    )(lam, dq, dk, vt, subln_col)


def _rope_tables(n_tok, dim, lane_lo):
    t = jnp.arange(n_tok)
    pos_r = (t // GRID_W).astype(F32)
    pos_c = (t % GRID_W).astype(F32)
    quarter = dim // 4
    inv = ROPE_BASE ** (-jnp.arange(quarter, dtype=F32) / quarter)
    ang_r = pos_r[:, None] * inv
    ang_c = pos_c[:, None] * inv
    ang = jnp.concatenate([ang_r, ang_r, ang_c, ang_c], -1)
    sign = jnp.tile(jnp.concatenate([-jnp.ones(quarter), jnp.ones(quarter)]), 2).astype(F32)
    cos, sin = jnp.cos(ang), jnp.sin(ang) * sign
    reps = (LANES - lane_lo) // dim
    cos = jnp.concatenate([jnp.ones((n_tok, lane_lo), F32)] + [cos] * reps, axis=1)
    sin = jnp.concatenate([jnp.zeros((n_tok, lane_lo), F32)] + [sin] * reps, axis=1)
    return cos, sin


def _cd_weights(w_in, w_uq, w_ukv):
    d = w_in.shape[0]
    s0 = C_Q_RANK
    s1 = s0 + C_KV_RANK
    s2 = s1 + C_ROPE
    s3 = s2 + 2 * D_HEADS * D_QK
    s4 = s3 + 2 * D_HEADS * D_QK
    kr = jnp.zeros((d, LANES), F32).at[:, C_NOPE:C_NOPE + C_ROPE].set(w_in[:, s1:s2])
    w_in_p = jnp.concatenate([w_in[:, :s1], w_in[:, s2:s3], w_in[:, s3:s4], w_in[:, s4:], kr], axis=1)
    dqk = C_NOPE + C_ROPE
    wq = w_uq.reshape(C_Q_RANK, C_HEADS, dqk)
    wq = jnp.pad(wq, ((0, 0), (0, 0), (0, C_HEAD_PAD - dqk))).reshape(C_Q_RANK, C_HEADS * C_HEAD_PAD)
    wkv = w_ukv.reshape(C_KV_RANK, C_HEADS, C_NOPE + C_V)
    wk = jnp.pad(wkv[:, :, :C_NOPE], ((0, 0), (0, 0), (0, C_HEAD_PAD - C_NOPE)))
    wk = wk.reshape(C_KV_RANK, C_HEADS * C_HEAD_PAD)
    wv = wkv[:, :, C_NOPE:].reshape(C_KV_RANK, C_HEADS * C_V)
    return w_in_p.astype(BF16), wq.astype(BF16), jnp.concatenate([wk, wv], axis=1).astype(BF16)


def _moe_layer(h2, logits, x1, w_gu, w_dn, mod, gid, lng, lnb, alpha, tm):
    n_tok = h2.shape[0]
    wts, dest, row_tok, block_e = _route(logits, n_tok)
    xs = jnp.take(h2, row_tok, axis=0, mode='clip')
    ys = _moe_experts(block_e, xs, w_gu, w_dn)
    y_pair = jnp.take(ys, dest.reshape(-1), axis=0, mode='clip').reshape(n_tok, -1)
    return _post_moe(x1, y_pair, wts.astype(F32), mod, gid, lng, lnb, alpha, tm)


def kernel(x, c, ctx, c_ctx, w_ada, b_ada, ln_g, ln_b, ab_w_in, a_sink, b_rpb, ab_w_out, cd_w_in, c_q_norm, c_w_uq, c_kv_norm, c_w_ukv, d_lambda, d_subln, cd_w_out, w_group, b_group, w_exp_router, b_exp_router, w_gate_up, w_down):
    batch, seq, d = x.shape
    ctx_len = ctx.shape[1]
    depth = w_ada.shape[0]
    alpha = (2 * depth) ** 0.25
    n_lat = batch * seq
    n_ctx = batch * ctx_len
    tm = 512
    tq = DENSE_TQ
    assert depth == 2 and seq % NA_Q == 0 and seq % tq == 0 and n_ctx % tm == 0
    assert seq % PROJ_CD_TILE == 0 and ctx_len % PROJ_CD_TILE == 0
    assert seq % (WIN_GROUP * A_BLOCK) == 0 and seq // NA_Q >= 2
    key_tiles = (seq + ctx_len) // MXU_DEPTH
    tk = MXU_DEPTH * max(u for u in range(1, DENSE_TK_MAX // MXU_DEPTH + 1) if key_tiles % u == 0)
    n_chunks = (seq + ctx_len) // tk

    def unroll_for(cap):
        return max(u for u in range(1, cap + 1) if n_chunks % u == 0)
    lat_tiles = seq // tm

    def gid_lat(i):
        return i // lat_tiles

    def gid_ctx(i):
        return batch

    def gid_all(i):
        return jnp.minimum(i // lat_tiles, batch)

    c_all = jnp.zeros((8, d), F32).at[:batch].set(c).at[batch].set(c_ctx)
    mod = _ada(c_all, w_ada, b_ada).reshape(depth, 8, 6, d)

    cos64, sin64 = _rope_tables(seq, HEAD_DIM, 0)
    cos32, sin32 = _rope_tables(seq, C_ROPE, C_NOPE)

    def router_weights(l):
        wr = jnp.zeros((d, ROUTER_PAD), F32)
        wr = wr.at[:, :N_GROUPS].set(w_group[l]).at[:, N_GROUPS:N_GROUPS + N_EXPERTS].set(w_exp_router[l])
        br = jnp.zeros((1, ROUTER_PAD), F32)
        br = br.at[0, :N_GROUPS].set(b_group[l]).at[0, N_GROUPS:N_GROUPS + N_EXPERTS].set(b_exp_router[l])
        hi = wr.astype(BF16)
        lo = (wr - hi.astype(F32)).astype(BF16)
        return hi, jnp.concatenate([hi, lo], axis=1), br

    x2d = x.reshape(n_lat, d)
    ctx2d = ctx.reshape(n_ctx, d)

    l = 0
    w_in_bf = ab_w_in[0].astype(BF16)
    aq, ak, av, bq, bk, bv = _proj_ab(x2d, 0, n_lat, mod[l], gid_lat, w_in_bf, cos64, sin64,
                                      lat_tiles, True, tm)
    aqx, akx, avx, bqx, bkx, bvx = _proj_ab(ctx2d, 0, n_ctx, mod[l], gid_ctx, w_in_bf, cos64, sin64,
                                            1, False, tm)
    sink = a_sink[0].astype(F32)
    oa = _window_attn(sink, aq, ak, av, akx, avx, batch, seq, ctx_len)
    tiles, tile_idx = _na_bias_tiles(b_rpb[0], seq // GRID_W)
    ob = _na_attn(bq, bk, bv, bkx, bvx, tiles, tile_idx, batch, seq, ctx_len)
    oax, obx = _ctx_ab_attn(sink, aqx, akx, avx, bqx, bkx, bvx, batch, ctx_len)

    n_all = n_lat + n_ctx
    w_out_bf = ab_w_out[0].astype(BF16)
    lng, lnb = ln_g[l, 0][None], ln_b[l, 0][None]
    wr_hi, wr_lo, br = router_weights(l)
    x1, h2, logits = _post_attn((oa, ob, x2d), (oax, obx, ctx2d), w_out_bf, mod[l], gid_all,
                                lng, lnb, wr_hi, wr_lo, br, n_lat, alpha, tm)
    xall = _moe_layer(h2, logits, x1, w_gate_up[l], w_down[l], mod[l], gid_all,
                      ln_g[l, 1][None], ln_b[l, 1][None], alpha, tm)

    l = 1
    lam_init = 0.8 - 0.6 * math.exp(-0.3 * l)
    lp = d_lambda[0].astype(F32)
    lam = (jnp.exp(jnp.sum(lp[0] * lp[1])) - jnp.exp(jnp.sum(lp[2] * lp[3])) + lam_init).reshape(1)
    w_in_p, wuq_p, wukv_p = _cd_weights(cd_w_in[0], c_w_uq[0], c_w_ukv[0])
    qn, kvn = c_q_norm[0][None].astype(F32), c_kv_norm[0][None].astype(F32)
    def with_ctx_identity(cos, sin):
        return (jnp.concatenate([cos, jnp.ones((ctx_len, LANES), F32)]),
                jnp.concatenate([sin, jnp.zeros((ctx_len, LANES), F32)]))

    tabs = with_ctx_identity(cos64, sin64) + with_ctx_identity(cos32, sin32)
    qc, kc, vct, dq, dk, dvt = _proj_cd(xall, mod[l], w_in_p, qn, wuq_p, kvn, wukv_p, tabs,
                                        batch, seq, ctx_len, PROJ_CD_TILE, tk)
    unroll = unroll_for(DENSE_UNROLL_MAX)
    oc_t = _mla_attn(qc, kc, vct, batch, seq, tq, tk, unroll, 1)
    od_t = _diff_attn(lam, dq, dk, dvt, d_subln[0].astype(F32).reshape(D_V, 1), 1.0 - lam_init,
                      batch, seq, tq, tk, unroll, 1)
    wr_hi, wr_lo, br = router_weights(l)
    x1, h2, logits = _post_attn((oc_t, od_t, xall), None, cd_w_out[0].astype(BF16), mod[l], gid_lat,
                                ln_g[l, 0][None], ln_b[l, 0][None], wr_hi, wr_lo, br,
                                n_lat, alpha, tm)
    out = _moe_layer(h2, logits, x1, w_gate_up[l], w_down[l], mod[l], gid_lat,
                     ln_g[l, 1][None], ln_b[l, 1][None], alpha, tm)
    return out.reshape(batch, seq, d)
```

```python
import functools
import math

import jax
import jax.numpy as jnp
from jax import lax
from jax.experimental import pallas as pl
from jax.experimental.pallas import tpu as pltpu

F32 = jnp.float32
BF16 = jnp.bfloat16

GRID_W = 64
HEAD_DIM = 64
ROPE_BASE = 10000.0
NEG_INF = -1e30
LOG2E = 1.4426950408889634

A_HEADS = 8
A_KV_HEADS = 2
A_WINDOW = 128
A_BLOCK = 128
B_HEADS = 8
NA_ROWS = 8
NA_COLS = 16
C_HEADS = 8
C_Q_RANK = 384
C_KV_RANK = 256
C_NOPE = 64
C_ROPE = 32
C_V = 64
D_HEADS = 4
D_QK = 64
D_V = 128
N_GROUPS = 4
EXP_PER_GROUP = 8
N_EXPERTS = N_GROUPS * EXP_PER_GROUP
TOP_K = 2
MOE_BLOCK = 512
MOE_SUB = 256
ROUTER_PAD = 128

LANES = 128
VMEM_LIMIT = 56 * 1024 * 1024


def _params(sem):
    return pltpu.CompilerParams(dimension_semantics=sem, vmem_limit_bytes=VMEM_LIMIT)


def _dot(a, b):
    return jnp.dot(a, b, preferred_element_type=F32)


def _dot_nt(a, b):
    return lax.dot_general(a, b, (((1,), (1,)), ((), ())), preferred_element_type=F32)


def _split_bf16(x):
    hi = x.astype(BF16)
    lo = (x - hi.astype(F32)).astype(BF16)
    return hi, lo


HI_HALF = 0xFFFF0000


def _pack_halves(x):
    w = x.shape[1] // 2
    lo = lax.bitcast_convert_type(x[:, :w].astype(BF16).astype(F32), jnp.uint32)
    hi = lax.bitcast_convert_type(x[:, w:].astype(BF16).astype(F32), jnp.uint32)
    return (lo >> 16) | (hi & jnp.uint32(HI_HALF))


def _unpack_halves(u):
    lo = lax.bitcast_convert_type(u << 16, F32)
    hi = lax.bitcast_convert_type(u & jnp.uint32(HI_HALF), F32)
    return jnp.concatenate([lo, hi], axis=1)


def _ada_kernel(c_ref, w_ref, b_ref, o_ref):
    c = c_ref[...]
    a = c / (1.0 + jnp.exp(-c))
    a_hi, a_lo = _split_bf16(a)
    w_hi, w_lo = _split_bf16(w_ref[0])
    acc = _dot(a_hi, w_hi) + _dot(a_lo, w_hi) + _dot(a_hi, w_lo)
    o_ref[0] = acc + b_ref[0]


def _ada(c_all, w_ada, b_ada):
    depth, d, n = w_ada.shape
    tn = 1536
    return pl.pallas_call(
        _ada_kernel,
        grid=(depth, n // tn),
        in_specs=[
            pl.BlockSpec((8, d), lambda l, j: (0, 0)),
            pl.BlockSpec((1, d, tn), lambda l, j: (l, 0, j)),
            pl.BlockSpec((1, 1, tn), lambda l, j: (l, 0, j)),
        ],
        out_specs=pl.BlockSpec((1, 8, tn), lambda l, j: (l, 0, j)),
        out_shape=jax.ShapeDtypeStruct((depth, 8, n), F32),
        compiler_params=_params(("arbitrary", "arbitrary")),
        name="ada",
    )(c_all, w_ada, b_ada.reshape(depth, 1, n))


def _rope_tile(t, cos, sin_signed, q):
    lane = lax.broadcasted_iota(jnp.int32, t.shape, 1)
    first = (lane & (2 * q - 1)) < q
    up = pltpu.roll(t, LANES - q, 1)
    dn = pltpu.roll(t, q, 1)
    return t * cos + jnp.where(first, up, dn) * sin_signed


def _rope_wide(t, cos, sin_signed, q):
    n = t.shape[1] // LANES
    return jnp.concatenate(
        [_rope_tile(t[:, i * LANES:(i + 1) * LANES], cos, sin_signed, q) for i in range(n)], axis=1)


def _modulate(x_ref, mod_ref, which):
    sh = mod_ref[0, 3 * which:3 * which + 1, :]
    sc = mod_ref[0, 3 * which + 1:3 * which + 2, :]
    return x_ref[...] * (1.0 + sc) + sh


def _proj_ab_kernel(x_ref, mod_ref, w_ref, cos_ref, sin_ref,
                    aq_ref, ak_ref, av_ref, bq_ref, bk_ref, bv_ref, *, rope):
    h = _modulate(x_ref, mod_ref, 0).astype(BF16)
    r = _dot(h, w_ref[...])
    aq = r[:, 0:512]
    ak = r[:, 512:640]
    if rope:
        cos = cos_ref[...]
        sin = sin_ref[...]
        aq = _rope_wide(aq, cos, sin, 16)
        ak = _rope_wide(ak, cos, sin, 16)
    scale = HEAD_DIM ** -0.5
    aq_ref[...] = (aq * scale).astype(BF16)
    ak_ref[...] = ak.astype(BF16)
    av_ref[...] = r[:, 640:768].astype(BF16)
    bq_ref[...] = (r[:, 768:1280] * scale).astype(BF16)
    bk_ref[...] = r[:, 1280:1792].astype(BF16)
    bv_ref[...] = r[:, 1792:2304].astype(BF16)


def _proj_ab(x2d, row0, nrows, mod, gid, w_bf, cos, sin, pos_blocks, rope, tm):
    d = x2d.shape[1]
    n_in = w_bf.shape[1]
    b0 = row0 // tm
    widths = (512, 128, 128, 512, 512, 512)
    return pl.pallas_call(
        functools.partial(_proj_ab_kernel, rope=rope),
        grid=(nrows // tm,),
        in_specs=[
            pl.BlockSpec((tm, d), lambda i: (b0 + i, 0)),
            pl.BlockSpec((1, 6, d), lambda i: (gid(i), 0, 0)),
            pl.BlockSpec((d, n_in), lambda i: (0, 0)),
            pl.BlockSpec((tm, LANES), lambda i: (i % pos_blocks, 0)),
            pl.BlockSpec((tm, LANES), lambda i: (i % pos_blocks, 0)),
        ],
        out_specs=[pl.BlockSpec((tm, w), lambda i: (i, 0)) for w in widths],
        out_shape=[jax.ShapeDtypeStruct((nrows, w), BF16) for w in widths],
        compiler_params=_params(("arbitrary",)),
        name="proj_ab",
    )(x2d, mod, w_bf, cos, sin)


def _softmax_pv(parts, sink_col):
    m = parts[0][0].max(-1, keepdims=True)
    for s, _ in parts[1:]:
        m = jnp.maximum(m, s.max(-1, keepdims=True))
    if sink_col is not None:
        m = jnp.maximum(m, sink_col)
    denom = None if sink_col is None else jnp.exp(sink_col - m)
    o = None
    for s, v in parts:
        e = jnp.exp(s - m)
        d = e.sum(-1, keepdims=True)
        denom = d if denom is None else denom + d
        pv = _dot(e.astype(BF16), v)
        o = pv if o is None else o + pv
    return o / denom


def _win_kernel(sink_ref, q_ref, kp_ref, kc_ref, kn_ref, vp_ref, vc_ref, vn_ref,
                kx_ref, vx_ref, o_ref, *, seq):
    step = pl.program_id(1)
    blk = A_BLOCK
    g_sz = A_HEADS // A_KV_HEADS
    qi = lax.broadcasted_iota(jnp.int32, (blk, 3 * blk), 0)
    kj = lax.broadcasted_iota(jnp.int32, (blk, 3 * blk), 1)
    in_window = jnp.abs(kj - blk - qi) <= A_WINDOW
    k_ext = jnp.concatenate([kp_ref[...], kc_ref[...], kn_ref[...]], axis=0)
    v_ext = jnp.concatenate([vp_ref[...], vc_ref[...], vn_ref[...]], axis=0)
    kx = kx_ref[...]
    vx = vx_ref[...]

    def scores(t, g):
        lo, hi = g * HEAD_DIM, (g + 1) * HEAD_DIM
        kpos = (step * WIN_GROUP + t) * blk + kj - blk
        valid = in_window & (kpos >= 0) & (kpos < seq)
        valid = jnp.concatenate([valid] * g_sz, axis=0)
        q = q_ref[t * blk:(t + 1) * blk, :]
        heads = [g * g_sz + i for i in range(g_sz)]
        qs = jnp.concatenate([q[:, h * HEAD_DIM:(h + 1) * HEAD_DIM] for h in heads], axis=0)
        s_loc = jnp.where(valid, _dot_nt(qs, k_ext[t * blk:(t + 3) * blk, lo:hi]), NEG_INF)
        return s_loc, _dot_nt(qs, kx[:, lo:hi])

    units = [(t, g) for t in range(WIN_GROUP) for g in range(A_KV_HEADS)]
    pending = scores(*units[0])
    outs = []
    for n, (t, g) in enumerate(units):
        s_loc, s_ctx = pending
        if n + 1 < len(units):
            pending = scores(*units[n + 1])
        lo, hi = g * HEAD_DIM, (g + 1) * HEAD_DIM
        sink = jnp.concatenate(
            [jnp.full((blk, 1), sink_ref[g * g_sz + i], F32) for i in range(g_sz)], axis=0)
        o = _softmax_pv([(s_ctx, vx[:, lo:hi]), (s_loc, v_ext[t * blk:(t + 3) * blk, lo:hi])], sink)
        outs.extend(o[i * blk:(i + 1) * blk] for i in range(g_sz))
        if g == A_KV_HEADS - 1:
            o_ref[t * blk:(t + 1) * blk, :] = jnp.concatenate(outs, axis=1).astype(o_ref.dtype)
            outs = []


WIN_GROUP = 16


def _window_attn(sink, aq, ak, av, akx, avx, batch, seq, ctx_len):
    nblk = seq // A_BLOCK
    nstep = nblk // WIN_GROUP
    kvw = A_KV_HEADS * HEAD_DIM
    qw = A_HEADS * HEAD_DIM
    rows = WIN_GROUP * A_BLOCK

    def edge(delta):
        return lambda b, n: (b * nblk + jnp.clip(n * WIN_GROUP + delta, 0, nblk - 1), 0)

    def own(b, n):
        return (b * nstep + n, 0)

    kv_specs = [pl.BlockSpec((A_BLOCK, kvw), edge(-1)), pl.BlockSpec((rows, kvw), own),
                pl.BlockSpec((A_BLOCK, kvw), edge(WIN_GROUP))]
    return pl.pallas_call(
        functools.partial(_win_kernel, seq=seq),
        grid=(batch, nstep),
        in_specs=[pl.BlockSpec(memory_space=pltpu.SMEM), pl.BlockSpec((rows, qw), own)]
        + kv_specs + kv_specs
        + [pl.BlockSpec((ctx_len, kvw), lambda b, n: (b, 0)),
           pl.BlockSpec((ctx_len, kvw), lambda b, n: (b, 0))],
        out_specs=pl.BlockSpec((rows, qw), own),
        out_shape=jax.ShapeDtypeStruct((batch * seq, qw), BF16),
        compiler_params=_params(("arbitrary", "arbitrary")),
        name="window_attn",
    )(sink, aq, ak, ak, ak, av, av, av, akx, avx)


NA_QROWS = 8
NA_Q = NA_QROWS * GRID_W
NA_KBLK = 4 * GRID_W
NA_K = 4 * NA_KBLK


NA_DR = 2 * NA_ROWS - 1


def _na_bias_tiles(rpb, rows):
    nj = rows // NA_QROWS
    kr = min(NA_ROWS, rows)
    col = jnp.arange(GRID_W)
    col_start = jnp.clip(col - NA_COLS // 2, 0, GRID_W - NA_COLS)
    cvalid = (col[None, :] >= col_start[:, None]) & (col[None, :] < col_start[:, None] + NA_COLS)
    dc = jnp.clip(col[None, :] - col[:, None], -(NA_COLS - 1), NA_COLS - 1) + (NA_COLS - 1)
    tiles = jnp.where(cvalid[None, None], rpb.astype(F32)[:, :, dc], NEG_INF)
    tiles = jnp.concatenate([tiles, jnp.full_like(tiles[:, :1], NEG_INF)], axis=1)
    idx = []
    for j in (0, min(1, nj - 1), nj - 1):
        r = j * NA_QROWS + jnp.arange(NA_QROWS)
        krow = j * NA_QROWS - NA_KBLK // GRID_W + jnp.arange(NA_K // GRID_W)
        start = jnp.clip(r - kr // 2, 0, rows - kr)
        rvalid = (krow[None, :] >= start[:, None]) & (krow[None, :] < start[:, None] + kr)
        dr = krow[None, :] - r[:, None] + (NA_ROWS - 1)
        idx.append(jnp.where(rvalid, dr, NA_DR))
    return tiles, jnp.stack(idx).reshape(-1).astype(jnp.int32)


def _na_kernel(idx_ref, q_ref, k0, k1, k2, k3, v0, v1, v2, v3, kx_ref, vx_ref, t_ref, o_ref, bias_scr,
               *, nj):
    j = pl.program_id(2)
    nkr = NA_K // GRID_W

    @pl.when((j == 0) | (j == 1) | (j == nj - 1))
    def _():
        variant = jnp.where(j == 0, 0, jnp.where(j == nj - 1, 2, 1))
        for hh in range(LANES // HEAD_DIM):
            for qr in range(NA_QROWS):
                for kr in range(nkr):
                    d = idx_ref[(variant * NA_QROWS + qr) * nkr + kr]
                    bias_scr[hh, qr * GRID_W:(qr + 1) * GRID_W, kr * GRID_W:(kr + 1) * GRID_W] = (
                        t_ref[hh, d])

    q = q_ref[...]
    k_all = jnp.concatenate([k0[...], k1[...], k2[...], k3[...]], axis=0)
    v_all = jnp.concatenate([v0[...], v1[...], v2[...], v3[...]], axis=0)
    kx = kx_ref[...]
    vx = vx_ref[...]
    qh_rows = NA_Q // 2
    k_span = 3 * NA_KBLK

    def scores(hh, half):
        lo, hi = hh * HEAD_DIM, (hh + 1) * HEAD_DIM
        q0, k0_ = half * qh_rows, half * NA_KBLK
        qh = q[q0:q0 + qh_rows, lo:hi]
        s_loc = _dot_nt(qh, k_all[k0_:k0_ + k_span, lo:hi]) + bias_scr[hh, q0:q0 + qh_rows, k0_:k0_ + k_span]
        return s_loc, _dot_nt(qh, kx[:, lo:hi])

    units = [(hh, half) for half in range(2) for hh in range(LANES // HEAD_DIM)]
    pending = scores(*units[0])
    for n, (hh, half) in enumerate(units):
        s_loc, s_ctx = pending
        if n + 1 < len(units):
            pending = scores(*units[n + 1])
        lo, hi = hh * HEAD_DIM, (hh + 1) * HEAD_DIM
        k0_ = half * NA_KBLK
        o = _softmax_pv([(s_ctx, vx[:, lo:hi]), (s_loc, v_all[k0_:k0_ + k_span, lo:hi])], None)
        o_ref[half * qh_rows:(half + 1) * qh_rows, lo:hi] = o.astype(o_ref.dtype)


def _na_attn(bq, bk, bv, bkx, bvx, tiles, tile_idx, batch, seq, ctx_len):
    nj = seq // NA_Q
    nkb = seq // NA_KBLK
    hp = B_HEADS * HEAD_DIM // LANES
    per = LANES // HEAD_DIM

    def kb(t):
        return lambda p, b, j, idx: (b * nkb + jnp.clip(2 * j - 1 + t, 0, nkb - 1), p)

    kv_specs = [pl.BlockSpec((NA_KBLK, LANES), kb(t)) for t in range(4)]
    grid_spec = pltpu.PrefetchScalarGridSpec(
        num_scalar_prefetch=1,
        grid=(hp, batch, nj),
        in_specs=[pl.BlockSpec((NA_Q, LANES), lambda p, b, j, idx: (b * nj + j, p))]
        + kv_specs + kv_specs
        + [pl.BlockSpec((ctx_len, LANES), lambda p, b, j, idx: (b, p)),
           pl.BlockSpec((ctx_len, LANES), lambda p, b, j, idx: (b, p)),
           pl.BlockSpec((per, NA_DR + 1, GRID_W, GRID_W), lambda p, b, j, idx: (p, 0, 0, 0))],
        out_specs=pl.BlockSpec((NA_Q, LANES), lambda p, b, j, idx: (b * nj + j, p)),
        scratch_shapes=[pltpu.VMEM((per, NA_Q, NA_K), F32)],
    )
    return pl.pallas_call(
        functools.partial(_na_kernel, nj=nj),
        grid_spec=grid_spec,
        out_shape=jax.ShapeDtypeStruct((batch * seq, B_HEADS * HEAD_DIM), BF16),
        compiler_params=_params(("arbitrary", "arbitrary", "arbitrary")),
        name="na_attn",
    )(tile_idx, bq, bk, bk, bk, bk, bv, bv, bv, bv, bkx, bvx, tiles)


def _ctx_ab_kernel(sink_ref, aq_ref, ak_ref, av_ref, bq_ref, bk_ref, bv_ref, oa_ref, ob_ref):
    ctx_len = aq_ref.shape[0]
    g_sz = A_HEADS // A_KV_HEADS
    aq = aq_ref[...]
    ak = ak_ref[...]
    av = av_ref[...]
    outs = []
    for g in range(A_KV_HEADS):
        lo, hi = g * HEAD_DIM, (g + 1) * HEAD_DIM
        heads = [g * g_sz + i for i in range(g_sz)]
        qs = jnp.concatenate([aq[:, h * HEAD_DIM:(h + 1) * HEAD_DIM] for h in heads], axis=0)
        sink = jnp.concatenate(
            [jnp.full((ctx_len, 1), sink_ref[h], F32) for h in heads], axis=0)
        o = _softmax_pv([(_dot_nt(qs, ak[:, lo:hi]), av[:, lo:hi])], sink)
        outs.extend(o[i * ctx_len:(i + 1) * ctx_len] for i in range(g_sz))
    oa_ref[...] = jnp.concatenate(outs, axis=1).astype(oa_ref.dtype)
    bq = bq_ref[...]
    bk = bk_ref[...]
    bv = bv_ref[...]
    outs = []
    for h in range(B_HEADS):
        lo, hi = h * HEAD_DIM, (h + 1) * HEAD_DIM
        outs.append(_softmax_pv([(_dot_nt(bq[:, lo:hi], bk[:, lo:hi]), bv[:, lo:hi])], None))
    ob_ref[...] = jnp.concatenate(outs, axis=1).astype(ob_ref.dtype)


def _ctx_ab_attn(sink, aq, ak, av, bq, bk, bv, batch, ctx_len):
    def spec(w):
        return pl.BlockSpec((ctx_len, w), lambda b: (b, 0))

    return pl.pallas_call(
        _ctx_ab_kernel,
        grid=(batch,),
        in_specs=[pl.BlockSpec(memory_space=pltpu.SMEM)]
        + [spec(a.shape[1]) for a in (aq, ak, av, bq, bk, bv)],
        out_specs=[spec(aq.shape[1]), spec(bq.shape[1])],
        out_shape=[jax.ShapeDtypeStruct(aq.shape, BF16), jax.ShapeDtypeStruct(bq.shape, BF16)],
        compiler_params=_params(("arbitrary",)),
        name="ctx_ab_attn",
    )(sink, aq, ak, av, bq, bk, bv)


def _layer_norm(t, g, b):
    mu = jnp.mean(t, -1, keepdims=True)
    c = t - mu
    var = jnp.mean(c * c, -1, keepdims=True)
    return c * lax.rsqrt(var + 1e-5) * g + b


def _post_attn_kernel(*refs, alpha, lat_tiles, with_ctx, lat_transposed):
    n_src = 6 if with_ctx else 3
    srcs = refs[:n_src]
    (w_ref, mod_ref, lng_ref, lnb_ref, wr_hi_ref, wr_lo_ref, br_ref,
     x1_ref, h2_ref, lg_ref) = refs[n_src:]

    def rows_of(o_ref):
        if o_ref.ndim == 2:
            return o_ref[...]
        return o_ref[0].astype(F32).T.astype(BF16)

    def run(o1_ref, o2_ref, x_ref):
        half = w_ref.shape[0] // 2
        y = _dot(rows_of(o1_ref), w_ref[0:half, :]) + _dot(rows_of(o2_ref), w_ref[half:, :])
        g1 = mod_ref[0, 2:3, :]
        x1 = _layer_norm(alpha * x_ref[...] + g1 * y, lng_ref[...], lnb_ref[...])
        x1_ref[...] = x1
        h2 = x1 * (1.0 + mod_ref[0, 4:5, :]) + mod_ref[0, 3:4, :]
        h_hi, h_lo = _split_bf16(h2)
        h2_ref[...] = _pack_halves(h2)
        both = _dot(h_hi, wr_lo_ref[...])
        lg_ref[...] = (both[:, :ROUTER_PAD] + both[:, ROUTER_PAD:] + _dot(h_lo, wr_hi_ref[...])
                       + br_ref[...])

    if not with_ctx:
        run(*srcs)
    else:
        i = pl.program_id(0)
        pl.when(i < lat_tiles)(lambda: run(*srcs[:3]))
        pl.when(i >= lat_tiles)(lambda: run(*srcs[3:]))


def _post_attn(lat, ctx_src, w_bf, mod, gid, lng, lnb, wr_hi, wr_lo, br, n_lat, alpha, tm):
    half = w_bf.shape[0] // 2
    d = lat[2].shape[1]
    lat_tiles = n_lat // tm
    n_rows = n_lat + (ctx_src[0].shape[0] if ctx_src is not None else 0)
    lat_transposed = lat[0].ndim == 3

    def lat_map(i):
        return (jnp.minimum(i, lat_tiles - 1), 0)

    def ctx_map(i):
        return (jnp.maximum(i - lat_tiles, 0), 0)

    def src_specs(index_map):
        return [pl.BlockSpec((tm, half), index_map), pl.BlockSpec((tm, half), index_map),
                pl.BlockSpec((tm, d), index_map)]

    in_specs = src_specs(lat_map)
    if lat_transposed:
        per_batch = lat[0].shape[2] // tm
        t_spec = pl.BlockSpec((1, half, tm), lambda i: (i // per_batch, 0, i % per_batch))
        in_specs[0] = in_specs[1] = t_spec
    args = list(lat)
    if ctx_src is not None:
        in_specs += src_specs(ctx_map)
        args += list(ctx_src)

    def const(shape):
        return pl.BlockSpec(shape, lambda i: (0,) * len(shape))

    in_specs += [const((2 * half, d)),
                 pl.BlockSpec((1, 6, d), lambda i: (gid(i), 0, 0)),
                 const((1, d)), const((1, d)),
                 const((d, ROUTER_PAD)), const((d, 2 * ROUTER_PAD)), const((1, ROUTER_PAD))]
    args += [w_bf, mod, lng, lnb, wr_hi, wr_lo, br]
    widths = (d, d // 2, ROUTER_PAD)
    dtypes = (F32, jnp.uint32, F32)
    return pl.pallas_call(
        functools.partial(_post_attn_kernel, alpha=alpha, lat_tiles=lat_tiles,
                          with_ctx=ctx_src is not None, lat_transposed=lat_transposed),
        grid=(n_rows // tm,),
        in_specs=in_specs,
        out_specs=[pl.BlockSpec((tm, w), lambda i: (i, 0)) for w in widths],
        out_shape=[jax.ShapeDtypeStruct((n_rows, w), t) for w, t in zip(widths, dtypes)],
        compiler_params=_params(("arbitrary",)),
        name="post_attn",
    )(*args)


def _moe_kernel(be_ref, xs_ref, wgu_ref, wdn_ref, ys_ref, wgu_bf, wdn_bf):
    i = pl.program_id(0)
    used = i < be_ref[pl.num_programs(0)]
    prev = be_ref[jnp.maximum(i - 1, 0)]
    changed = (i == 0) | (be_ref[i] != prev)

    @pl.when(changed & used)
    def _():
        wgu_bf[...] = wgu_ref[0].astype(BF16)
        wdn_bf[...] = wdn_ref[0].astype(BF16)

    @pl.when(used)
    def _():
        de = wdn_bf.shape[0]
        subs = [slice(r, r + MOE_SUB) for r in range(0, MOE_BLOCK, MOE_SUB)]
        gus = [_dot(_unpack_halves(xs_ref[sl, :]).astype(BF16), wgu_bf[...]) for sl in subs]
        for sl, gu in zip(subs, gus):
            gate = gu[:, :de]
            up = gu[:, de:]
            act = gate / (1.0 + jnp.exp(-gate)) * up
            ys_ref[sl, :] = _pack_halves(_dot(act.astype(BF16), wdn_bf[...]))

    @pl.when(jnp.logical_not(used))
    def _():
        ys_ref[...] = jnp.zeros(ys_ref.shape, ys_ref.dtype)


def _moe_experts(block_e, xs, w_gu, w_dn):
    rows, dw = xs.shape
    d = 2 * dw
    nb = rows // MOE_BLOCK
    de = w_dn.shape[1]
    grid_spec = pltpu.PrefetchScalarGridSpec(
        num_scalar_prefetch=1,
        grid=(nb,),
        in_specs=[
            pl.BlockSpec((MOE_BLOCK, dw), lambda i, be: (i, 0)),
            pl.BlockSpec((1, d, 2 * de), lambda i, be: (be[i], 0, 0)),
            pl.BlockSpec((1, de, d), lambda i, be: (be[i], 0, 0)),
        ],
        out_specs=pl.BlockSpec((MOE_BLOCK, dw), lambda i, be: (i, 0)),
        scratch_shapes=[pltpu.VMEM((d, 2 * de), BF16), pltpu.VMEM((de, d), BF16)],
    )
    return pl.pallas_call(
        _moe_kernel,
        grid_spec=grid_spec,
        out_shape=jax.ShapeDtypeStruct((rows, dw), jnp.uint32),
        compiler_params=_params(("arbitrary",)),
        name="moe_experts",
    )(block_e, xs, w_gu, w_dn)


ROUTE_TILE = 512


def _route_kernel(lg_ref, rec_ref, cnt_ref, carry):
    i = pl.program_id(0)

    @pl.when(i == 0)
    def _():
        carry[...] = jnp.zeros(carry.shape, F32)

    lg = lg_ref[...]
    tm = lg.shape[0]
    lane = lax.broadcasted_iota(jnp.int32, lg.shape, 1)

    def first_lane(mask):
        return jnp.min(jnp.where(mask, lane, ROUTER_PAD), axis=1, keepdims=True)

    is_g = lane < N_GROUPS
    g_log = jnp.where(is_g, lg, NEG_INF)
    g_max = jnp.max(g_log, axis=1, keepdims=True)
    g_sum = jnp.sum(jnp.where(is_g, jnp.exp(lg - g_max), 0.0), axis=1, keepdims=True)
    g_val = 1.0 / g_sum
    g_idx = first_lane(g_log == g_max)
    lo = N_GROUPS + EXP_PER_GROUP * g_idx
    sel = (lane >= lo) & (lane < lo + EXP_PER_GROUP)
    e_log = jnp.where(sel, lg, NEG_INF)
    e_max = jnp.max(e_log, axis=1, keepdims=True)
    e_exp = jnp.where(sel, jnp.exp(lg - e_max), 0.0)
    e_prob = e_exp / jnp.sum(e_exp, axis=1, keepdims=True)
    p1 = jnp.where(sel, e_prob, -1.0)
    v1 = jnp.max(p1, axis=1, keepdims=True)
    i1 = first_lane(p1 == v1)
    p2 = jnp.where(lane == i1, -1.0, p1)
    v2 = jnp.max(p2, axis=1, keepdims=True)
    i2 = first_lane(p2 == v2)
    norm = g_val / (v1 + v2)

    hot1 = lane == i1
    hot2 = lane == i2
    hot = jnp.where(hot1 | hot2, 1.0, 0.0)
    rows = lax.broadcasted_iota(jnp.int32, (tm, tm), 0)
    cols = lax.broadcasted_iota(jnp.int32, (tm, tm), 1)
    before = jnp.where(cols < rows, 1.0, 0.0).astype(BF16)
    prefix = _dot(before, hot.astype(BF16)) + carry[0:1, :]
    r1 = jnp.sum(jnp.where(hot1, prefix, 0.0), axis=1, keepdims=True)
    r2 = jnp.sum(jnp.where(hot2, prefix, 0.0), axis=1, keepdims=True)
    carry[0:1, :] = carry[0:1, :] + jnp.sum(hot, axis=0, keepdims=True)

    fields = [(i1 - N_GROUPS).astype(F32), (i2 - N_GROUPS).astype(F32), r1, r2, v1 * norm, v2 * norm]
    rec = jnp.zeros(lg.shape, F32)
    for k, f in enumerate(fields):
        rec = jnp.where(lane == k, f, rec)
    rec_ref[...] = rec

    @pl.when(i == pl.num_programs(0) - 1)
    def _():
        cnt_ref[...] = carry[...]


def _route(logits, n_tok):
    rec, cnt = pl.pallas_call(
        _route_kernel,
        grid=(n_tok // ROUTE_TILE,),
        in_specs=[pl.BlockSpec((ROUTE_TILE, ROUTER_PAD), lambda i: (i, 0))],
        out_specs=[pl.BlockSpec((ROUTE_TILE, ROUTER_PAD), lambda i: (i, 0)),
                   pl.BlockSpec((8, ROUTER_PAD), lambda i: (0, 0))],
        out_shape=[jax.ShapeDtypeStruct((n_tok, ROUTER_PAD), F32),
                   jax.ShapeDtypeStruct((8, ROUTER_PAD), F32)],
        scratch_shapes=[pltpu.VMEM((8, ROUTER_PAD), F32)],
        compiler_params=_params(("arbitrary",)),
        name="route",
    )(logits)
    experts = rec[:, 0:2].astype(jnp.int32)
    rank = rec[:, 2:4].astype(jnp.int32)
    wts = rec[:, 4:6]
    counts = cnt[0, N_GROUPS:N_GROUPS + N_EXPERTS].astype(jnp.int32)

    n_asg = n_tok * TOP_K
    pcounts = ((counts + MOE_BLOCK - 1) // MOE_BLOCK) * MOE_BLOCK
    pends = jnp.cumsum(pcounts)
    pstarts = pends - pcounts
    dest = pstarts[experts] + rank
    nb = -(-n_asg // MOE_BLOCK) + N_EXPERTS
    block_start = jnp.arange(nb, dtype=jnp.int32) * MOE_BLOCK
    block_e = jnp.minimum((pends[None, :] <= block_start[:, None]).sum(1), N_EXPERTS - 1).astype(jnp.int32)
    block_e = jnp.concatenate([block_e, (pends[-1:] // MOE_BLOCK).astype(jnp.int32)])
    tok = jnp.broadcast_to(jnp.arange(n_tok, dtype=jnp.int32)[:, None], (n_tok, TOP_K))
    filler = jnp.arange(nb * MOE_BLOCK, dtype=jnp.int32) % n_tok
    row_tok = filler.at[dest.reshape(-1)].set(tok.reshape(-1), unique_indices=True)
    return wts, dest, row_tok, block_e


def _post_moe_kernel(x_ref, ya_ref, yb_ref, w_ref, mod_ref, lng_ref, lnb_ref, o_ref, *, alpha):
    w = w_ref[...]
    y = w[:, 0:1] * _unpack_halves(ya_ref[...]) + w[:, 1:2] * _unpack_halves(yb_ref[...])
    g2 = mod_ref[0, 5:6, :]
    o_ref[...] = _layer_norm(alpha * x_ref[...] + g2 * y, lng_ref[...], lnb_ref[...])


def _post_moe(x1, ya, yb, wts, mod, gid, lng, lnb, alpha, tm):
    rows, d = x1.shape
    row = pl.BlockSpec((tm, d), lambda i: (i, 0))
    packed = pl.BlockSpec((tm, d // 2), lambda i: (i, 0))
    vec = pl.BlockSpec((1, d), lambda i: (0, 0))
    return pl.pallas_call(
        functools.partial(_post_moe_kernel, alpha=alpha),
        grid=(rows // tm,),
        in_specs=[row, packed, packed,
                  pl.BlockSpec((tm, TOP_K), lambda i: (i, 0)),
                  pl.BlockSpec((1, 6, d), lambda i: (gid(i), 0, 0)),
                  vec, vec],
        out_specs=row,
        out_shape=jax.ShapeDtypeStruct((rows, d), F32),
        compiler_params=_params(("arbitrary",)),
        name="post_moe",
    )(x1, ya, yb, wts, mod, lng, lnb)


CD_COLS = dict(cq=(0, 384), ckv=(384, 640), dq=(640, 1152), dk=(1152, 1664), dv=(1664, 2176),
               kr=(2176, 2304))
C_HEAD_PAD = LANES


def _rms(t, g, eps=1e-6):
    return t * lax.rsqrt(jnp.mean(t * t, -1, keepdims=True) + eps) * g


def _proj_cd_kernel(x_ref, mod_ref, w_ref, qn_ref, wuq_ref, kvn_ref, wukv_ref,
                    cos64_ref, sin64_ref, cos32_ref, sin32_ref,
                    qc_ref, kc_ref, vct_ref, dq_ref, dk_ref, dvt_ref):
    h = _modulate(x_ref, mod_ref, 0).astype(BF16)
    r = _dot(h, w_ref[...])

    def cols(name):
        lo, hi = CD_COLS[name]
        return r[:, lo:hi]

    q = _dot(_rms(cols('cq'), qn_ref[...]).astype(BF16), wuq_ref[...])
    kv = _dot(_rms(cols('ckv'), kvn_ref[...]).astype(BF16), wukv_ref[...])
    c32, s32 = cos32_ref[...], sin32_ref[...]
    c64, s64 = cos64_ref[...], sin64_ref[...]
    q = _rope_wide(q, c32, s32, C_ROPE // 4)
    kr = _rope_tile(cols('kr'), c32, s32, C_ROPE // 4)
    dq = _rope_wide(cols('dq'), c64, s64, D_QK // 4)
    dk = _rope_wide(cols('dk'), c64, s64, D_QK // 4)
    kw = C_HEADS * C_HEAD_PAD
    qc_ref[0] = (q * ((C_NOPE + C_ROPE) ** -0.5 * LOG2E)).astype(BF16)
    kc_ref[0] = (kv[:, :kw] + jnp.concatenate([kr] * C_HEADS, axis=1)).astype(BF16)
    dq_ref[0] = (dq * (D_QK ** -0.5 * LOG2E)).astype(BF16)
    dk_ref[0] = dk.astype(BF16)
    _store_vt(vct_ref, kv[:, kw:], C_HEADS, C_V)
    _store_vt(dvt_ref, cols('dv'), D_HEADS, D_V)


def _store_vt(vt_ref, v, heads, dv):
    tm = v.shape[0]
    v_t = v.T.astype(BF16)
    row = lax.broadcasted_iota(jnp.int32, (ONES_ROWS, tm), 0)
    extra = jnp.where(row == 0, 1.0, 0.0).astype(BF16)
    for h in range(heads):
        vt_ref[0, h, 0, 0:dv, :] = v_t[h * dv:(h + 1) * dv]
        vt_ref[0, h, 0, dv:dv + ONES_ROWS, :] = extra


def _proj_cd(xall, mod, w_bf, qn, wuq, kvn, wukv, tabs, batch, seq, ctx_len, tm, tk):
    d = xall.shape[1]
    lat_t, ctx_t = seq // tm, ctx_len // tm
    n_keys = seq + ctx_len
    per_chunk = tk // tm
    n_chunks = n_keys // tk
    widths = (C_HEADS * C_HEAD_PAD, C_HEADS * C_HEAD_PAD, 2 * D_HEADS * D_QK, 2 * D_HEADS * D_QK)

    def src(b, t):
        return (jnp.where(t < lat_t, b * lat_t + t, batch * lat_t + b * ctx_t + t - lat_t), 0)

    def full(a):
        return pl.BlockSpec(a.shape, lambda b, t: (0,) * a.ndim)

    def row_spec(w):
        return pl.BlockSpec((1, tm, w), lambda b, t: (b, t, 0))

    def row_shape(w):
        return jax.ShapeDtypeStruct((batch, n_keys, w), BF16)

    def vt_spec(heads, dv):
        return pl.BlockSpec((1, heads, 1, dv + ONES_ROWS, tm),
                            lambda b, t: (b, 0, t // per_chunk, 0, t % per_chunk))

    def vt_shape(heads, dv):
        return jax.ShapeDtypeStruct((batch, heads, n_chunks, dv + ONES_ROWS, tk), BF16)

    tab = pl.BlockSpec((tm, LANES), lambda b, t: (t, 0))
    return pl.pallas_call(
        _proj_cd_kernel,
        grid=(batch, lat_t + ctx_t),
        in_specs=[pl.BlockSpec((tm, d), src),
                  pl.BlockSpec((1, 6, d), lambda b, t: (jnp.where(t < lat_t, b, batch), 0, 0)),
                  full(w_bf), full(qn), full(wuq), full(kvn), full(wukv), tab, tab, tab, tab],
        out_specs=[row_spec(widths[0]), row_spec(widths[1]), vt_spec(C_HEADS, C_V),
                   row_spec(widths[2]), row_spec(widths[3]), vt_spec(D_HEADS, D_V)],
        out_shape=[row_shape(widths[0]), row_shape(widths[1]), vt_shape(C_HEADS, C_V),
                   row_shape(widths[2]), row_shape(widths[3]), vt_shape(D_HEADS, D_V)],
        compiler_params=_params(("arbitrary", "arbitrary")),
        name="proj_cd",
    )(xall, mod, w_bf, qn, wuq, kvn, wukv, *tabs)


DENSE_SHIFT = 80.0
DENSE_L_MIN = 2.0 ** -60
DENSE_L_MAX = 2.0 ** 120
DENSE_TQ = 1024
DENSE_TK_MAX = 1280
DENSE_UNROLL_MAX = 13
MXU_DEPTH = 256
PROJ_CD_TILE = 256
ONES_ROWS = 16


def _sq_norm_row(x):
    ones = jnp.ones((8, x.shape[1]), BF16)
    return _dot_nt(ones, (x * x).astype(BF16))[0:1] * 1.02


def _key_norm_max(kchunk, n_chunks, masks):
    def body(j, mx):
        kf = kchunk(j).astype(F32)
        ksq = kf * kf
        out = []
        for msk, cur in zip(masks, mx):
            part = ksq if msk is None else jnp.where(msk, ksq, 0.0)
            rn = jnp.sum(part, axis=1, keepdims=True)
            out.append(jnp.maximum(cur, jnp.max(rn, axis=0, keepdims=True)))
        return tuple(out)

    mx = lax.fori_loop(0, n_chunks, body, tuple(jnp.zeros((1, 1), F32) for _ in masks))
    return [jnp.sqrt(v) * 1.01 for v in mx]


def _safe_online(q, kchunk, vt_chunk, n_chunks, m_scr, l_scr, acc):
    m_scr[...] = jnp.full(m_scr.shape, NEG_INF, F32)
    l_scr[...] = jnp.zeros(l_scr.shape, F32)
    acc[...] = jnp.zeros(acc.shape, F32)

    def body(j, carry):
        s = _dot_nt(kchunk(j), q)
        m_prev = m_scr[...]
        m_new = jnp.maximum(m_prev, s.max(0, keepdims=True))
        alpha = jnp.exp2(m_prev - m_new)
        p = jnp.exp2(s - m_new)
        l_scr[...] = alpha * l_scr[...] + p.sum(0, keepdims=True)
        acc[...] = alpha * acc[...] + _dot(vt_chunk(j), p.astype(BF16))
        m_scr[...] = m_new
        return carry

    lax.fori_loop(0, n_chunks, body, 0)


def _denominators_ok(*ls):
    ok = None
    for l in ls:
        cur = (l > DENSE_L_MIN) & (l < DENSE_L_MAX)
        ok = cur if ok is None else ok & cur
    return jnp.max(jnp.where(ok, 0.0, 1.0)) == 0.0


MLA_HEADS_PER_STEP = 2


def _mla_kernel(q_ref, k_ref, vt_ref, o_ref, kmax_scr, acc_scr, m_scr, l_scr,
                *, n_chunks, tk, unroll, depth):
    hp = acc_scr.shape[0]

    def kchunk(j, h):
        return k_ref[0, pl.ds(pl.multiple_of(j * tk, tk), tk), h * C_HEAD_PAD:(h + 1) * C_HEAD_PAD]

    @pl.when(pl.program_id(2) == 0)
    def _():
        for h in range(hp):
            (kmax,) = _key_norm_max(lambda j: kchunk(j, h), n_chunks, [None])
            kmax_scr[h:h + 1, :] = jnp.broadcast_to(kmax, (1, LANES))

    qs = [q_ref[0, :, h * C_HEAD_PAD:(h + 1) * C_HEAD_PAD] for h in range(hp)]
    ms = [jnp.sqrt(_sq_norm_row(qs[h].astype(F32))) * kmax_scr[h:h + 1, 0:1] - DENSE_SHIFT
          for h in range(hp)]
    acc_scr[...] = jnp.zeros(acc_scr.shape, F32)

    def scores(j):
        return [_dot_nt(kchunk(j, h), qs[h]) for h in range(hp)]

    def body(it, carry):
        j0 = it * unroll
        pend = [scores(j0 + u) for u in range(min(depth, unroll))]
        accs = [acc_scr[h] for h in range(hp)]
        for u in range(unroll):
            ss = pend.pop(0)
            if u + depth < unroll:
                pend.append(scores(j0 + u + depth))
            for h in range(hp):
                accs[h] = accs[h] + _dot(vt_ref[0, h, j0 + u], jnp.exp2(ss[h] - ms[h]).astype(BF16))
        for h in range(hp):
            acc_scr[h] = accs[h]
        return carry

    lax.fori_loop(0, n_chunks // unroll, body, 0)
    ls = [acc_scr[h, C_V:C_V + 1, :] for h in range(hp)]
    for h in range(hp):
        o_ref[0, h * C_V:(h + 1) * C_V, :] = (acc_scr[h, 0:C_V, :] / ls[h]).astype(o_ref.dtype)

    @pl.when(jnp.logical_not(_denominators_ok(*ls)))
    def _():
        for h in range(hp):
            _safe_online(qs[h], lambda j: kchunk(j, h), lambda j: vt_ref[0, h, j], n_chunks,
                         m_scr, l_scr, acc_scr.at[h])
            o_ref[0, h * C_V:(h + 1) * C_V, :] = (acc_scr[h, 0:C_V, :] / l_scr[...]).astype(o_ref.dtype)


def _mla_attn(qc, kc, vt, batch, seq, tq, tk, unroll, depth):
    n_keys = kc.shape[1]
    nq, n_chunks = seq // tq, n_keys // tk
    rows = C_V + ONES_ROWS
    hp = MLA_HEADS_PER_STEP
    return pl.pallas_call(
        functools.partial(_mla_kernel, n_chunks=n_chunks, tk=tk, unroll=unroll, depth=depth),
        grid=(batch, C_HEADS // hp, nq),
        in_specs=[
            pl.BlockSpec((1, tq, hp * C_HEAD_PAD), lambda b, h, i: (b, i, h)),
            pl.BlockSpec((1, n_keys, hp * C_HEAD_PAD), lambda b, h, i: (b, 0, h)),
            pl.BlockSpec((1, hp, n_chunks, rows, tk), lambda b, h, i: (b, h, 0, 0, 0)),
        ],
        out_specs=pl.BlockSpec((1, hp * C_V, tq), lambda b, h, i: (b, h, i)),
        out_shape=jax.ShapeDtypeStruct((batch, C_HEADS * C_V, seq), BF16),
        scratch_shapes=[pltpu.VMEM((8, LANES), F32), pltpu.VMEM((hp, rows, tq), F32),
                        pltpu.VMEM((1, tq), F32), pltpu.VMEM((1, tq), F32)],
        compiler_params=_params(("arbitrary", "arbitrary", "arbitrary")),
        name="mla_attn",
    )(qc, kc, vt)


def _diff_kernel(lam_ref, q_ref, k_ref, vt_ref, subln_ref, o_ref, kmax_scr, acc1_scr, acc2_scr,
                 m_scr, l_scr, *, n_chunks, tk, unroll, depth, out_scale):
    def kchunk(j):
        return k_ref[0, pl.ds(pl.multiple_of(j * tk, tk), tk), :]

    @pl.when(pl.program_id(2) == 0)
    def _():
        klane = lax.broadcasted_iota(jnp.int32, (tk, 2 * D_QK), 1)
        k1, k2 = _key_norm_max(kchunk, n_chunks, [klane < D_QK, klane >= D_QK])
        kmax_scr[0:1, :] = jnp.broadcast_to(k1, (1, LANES))
        kmax_scr[1:2, :] = jnp.broadcast_to(k2, (1, LANES))

    q = q_ref[0]
    qlane = lax.broadcasted_iota(jnp.int32, q.shape, 1)
    zero = jnp.zeros_like(q)
    q1 = jnp.where(qlane < D_QK, q, zero)
    q2 = jnp.where(qlane >= D_QK, q, zero)
    m1 = jnp.sqrt(_sq_norm_row(q1.astype(F32))) * kmax_scr[0:1, 0:1] - DENSE_SHIFT
    m2 = jnp.sqrt(_sq_norm_row(q2.astype(F32))) * kmax_scr[1:2, 0:1] - DENSE_SHIFT
    acc1_scr[...] = jnp.zeros(acc1_scr.shape, F32)
    acc2_scr[...] = jnp.zeros(acc2_scr.shape, F32)

    def scores(j):
        k = kchunk(j)
        return _dot_nt(k, q1), _dot_nt(k, q2)

    def body(it, carry):
        j0 = it * unroll
        pend = [scores(j0 + u) for u in range(min(depth, unroll))]
        a1 = acc1_scr[...]
        a2 = acc2_scr[...]
        for u in range(unroll):
            s1, s2 = pend.pop(0)
            if u + depth < unroll:
                pend.append(scores(j0 + u + depth))
            vt = vt_ref[0, 0, j0 + u]
            a1 = a1 + _dot(vt, jnp.exp2(s1 - m1).astype(BF16))
            a2 = a2 + _dot(vt, jnp.exp2(s2 - m2).astype(BF16))
        acc1_scr[...] = a1
        acc2_scr[...] = a2
        return carry

    lax.fori_loop(0, n_chunks // unroll, body, 0)

    def finish(o1, o2):
        o = o1 - lam_ref[0] * o2
        o = o * lax.rsqrt(jnp.mean(o * o, 0, keepdims=True) + 1e-6) * subln_ref[...] * out_scale
        o_ref[0] = o.astype(o_ref.dtype)

    l1 = acc1_scr[D_V:D_V + 1, :]
    l2 = acc2_scr[D_V:D_V + 1, :]
    finish(acc1_scr[0:D_V, :] / l1, acc2_scr[0:D_V, :] / l2)

    @pl.when(jnp.logical_not(_denominators_ok(l1, l2)))
    def _():
        def vt_chunk(j):
            return vt_ref[0, 0, j]

        _safe_online(q1, kchunk, vt_chunk, n_chunks, m_scr, l_scr, acc1_scr)
        o1 = acc1_scr[0:D_V, :] / l_scr[...]
        _safe_online(q2, kchunk, vt_chunk, n_chunks, m_scr, l_scr, acc2_scr)
        finish(o1, acc2_scr[0:D_V, :] / l_scr[...])


def _diff_attn(lam, dq, dk, vt, subln_col, out_scale, batch, seq, tq, tk, unroll, depth):
    n_keys = dk.shape[1]
    nq, n_chunks = seq // tq, n_keys // tk
    rows = D_V + ONES_ROWS
    pair = 2 * D_QK
    return pl.pallas_call(
        functools.partial(_diff_kernel, n_chunks=n_chunks, tk=tk, unroll=unroll, depth=depth,
                          out_scale=out_scale),
        grid=(batch, D_HEADS, nq),
        in_specs=[
            pl.BlockSpec(memory_space=pltpu.SMEM),
            pl.BlockSpec((1, tq, pair), lambda b, h, i: (b, i, h)),
            pl.BlockSpec((1, n_keys, pair), lambda b, h, i: (b, 0, h)),
            pl.BlockSpec((1, 1, n_chunks, rows, tk), lambda b, h, i: (b, h, 0, 0, 0)),
            pl.BlockSpec((D_V, 1), lambda b, h, i: (0, 0)),
        ],
        out_specs=pl.BlockSpec((1, D_V, tq), lambda b, h, i: (b, h, i)),
        out_shape=jax.ShapeDtypeStruct((batch, D_HEADS * D_V, seq), BF16),
        scratch_shapes=[pltpu.VMEM((8, LANES), F32), pltpu.VMEM((rows, tq), F32),
                        pltpu.VMEM((rows, tq), F32), pltpu.VMEM((1, tq), F32), pltpu.VMEM((1, tq), F32)],
        compiler_params=_params(("arbitrary", "arbitrary", "arbitrary")),
        name="diff_attn",
    )(lam, dq, dk, vt, subln_col)


def _rope_tables(n_tok, dim, lane_lo):
    t = jnp.arange(n_tok)
    pos_r = (t // GRID_W).astype(F32)
    pos_c = (t % GRID_W).astype(F32)
    quarter = dim // 4
    inv = ROPE_BASE ** (-jnp.arange(quarter, dtype=F32) / quarter)
    ang_r = pos_r[:, None] * inv
    ang_c = pos_c[:, None] * inv
    ang = jnp.concatenate([ang_r, ang_r, ang_c, ang_c], -1)
    sign = jnp.tile(jnp.concatenate([-jnp.ones(quarter), jnp.ones(quarter)]), 2).astype(F32)
    cos, sin = jnp.cos(ang), jnp.sin(ang) * sign
    reps = (LANES - lane_lo) // dim
    cos = jnp.concatenate([jnp.ones((n_tok, lane_lo), F32)] + [cos] * reps, axis=1)
    sin = jnp.concatenate([jnp.zeros((n_tok, lane_lo), F32)] + [sin] * reps, axis=1)
    return cos, sin


def _cd_weights(w_in, w_uq, w_ukv):
    d = w_in.shape[0]
    s0 = C_Q_RANK
    s1 = s0 + C_KV_RANK
    s2 = s1 + C_ROPE
    s3 = s2 + 2 * D_HEADS * D_QK
    s4 = s3 + 2 * D_HEADS * D_QK
    kr = jnp.zeros((d, LANES), F32).at[:, C_NOPE:C_NOPE + C_ROPE].set(w_in[:, s1:s2])
    w_in_p = jnp.concatenate([w_in[:, :s1], w_in[:, s2:s3], w_in[:, s3:s4], w_in[:, s4:], kr], axis=1)
    dqk = C_NOPE + C_ROPE
    wq = w_uq.reshape(C_Q_RANK, C_HEADS, dqk)
    wq = jnp.pad(wq, ((0, 0), (0, 0), (0, C_HEAD_PAD - dqk))).reshape(C_Q_RANK, C_HEADS * C_HEAD_PAD)
    wkv = w_ukv.reshape(C_KV_RANK, C_HEADS, C_NOPE + C_V)
    wk = jnp.pad(wkv[:, :, :C_NOPE], ((0, 0), (0, 0), (0, C_HEAD_PAD - C_NOPE)))
    wk = wk.reshape(C_KV_RANK, C_HEADS * C_HEAD_PAD)
    wv = wkv[:, :, C_NOPE:].reshape(C_KV_RANK, C_HEADS * C_V)
    return w_in_p.astype(BF16), wq.astype(BF16), jnp.concatenate([wk, wv], axis=1).astype(BF16)


def _moe_layer(h2, logits, x1, w_gu, w_dn, mod, gid, lng, lnb, alpha, tm):
    n_tok = h2.shape[0]
    wts, dest, row_tok, block_e = _route(logits, n_tok)
    xs = jnp.take(h2, row_tok, axis=0, mode='clip')
    ys = _moe_experts(block_e, xs, w_gu, w_dn)
    ya = jnp.take(ys, dest[:, 0], axis=0, mode='clip')
    yb = jnp.take(ys, dest[:, 1], axis=0, mode='clip')
    return _post_moe(x1, ya, yb, wts.astype(F32), mod, gid, lng, lnb, alpha, tm)


def kernel(x, c, ctx, c_ctx, w_ada, b_ada, ln_g, ln_b, ab_w_in, a_sink, b_rpb, ab_w_out, cd_w_in, c_q_norm, c_w_uq, c_kv_norm, c_w_ukv, d_lambda, d_subln, cd_w_out, w_group, b_group, w_exp_router, b_exp_router, w_gate_up, w_down):
    batch, seq, d = x.shape
    ctx_len = ctx.shape[1]
    depth = w_ada.shape[0]
    alpha = (2 * depth) ** 0.25
    n_lat = batch * seq
    n_ctx = batch * ctx_len
    tm = 512
    tq = DENSE_TQ
    assert depth == 2 and seq % NA_Q == 0 and seq % tq == 0 and n_ctx % tm == 0
    assert seq % PROJ_CD_TILE == 0 and ctx_len % PROJ_CD_TILE == 0
    assert seq % (WIN_GROUP * A_BLOCK) == 0 and seq // NA_Q >= 2
    key_tiles = (seq + ctx_len) // MXU_DEPTH
    tk = MXU_DEPTH * max(u for u in range(1, DENSE_TK_MAX // MXU_DEPTH + 1) if key_tiles % u == 0)
    n_chunks = (seq + ctx_len) // tk

    def unroll_for(cap):
        return max(u for u in range(1, cap + 1) if n_chunks % u == 0)
    lat_tiles = seq // tm

    def gid_lat(i):
        return i // lat_tiles

    def gid_ctx(i):
        return batch

    def gid_all(i):
        return jnp.minimum(i // lat_tiles, batch)

    c_all = jnp.zeros((8, d), F32).at[:batch].set(c).at[batch].set(c_ctx)
    mod = _ada(c_all, w_ada, b_ada).reshape(depth, 8, 6, d)

    cos64, sin64 = _rope_tables(seq, HEAD_DIM, 0)
    cos32, sin32 = _rope_tables(seq, C_ROPE, C_NOPE)

    def router_weights(l):
        wr = jnp.zeros((d, ROUTER_PAD), F32)
        wr = wr.at[:, :N_GROUPS].set(w_group[l]).at[:, N_GROUPS:N_GROUPS + N_EXPERTS].set(w_exp_router[l])
        br = jnp.zeros((1, ROUTER_PAD), F32)
        br = br.at[0, :N_GROUPS].set(b_group[l]).at[0, N_GROUPS:N_GROUPS + N_EXPERTS].set(b_exp_router[l])
        hi = wr.astype(BF16)
        lo = (wr - hi.astype(F32)).astype(BF16)
        return hi, jnp.concatenate([hi, lo], axis=1), br

    x2d = x.reshape(n_lat, d)
    ctx2d = ctx.reshape(n_ctx, d)

    l = 0
    w_in_bf = ab_w_in[0].astype(BF16)
    aq, ak, av, bq, bk, bv = _proj_ab(x2d, 0, n_lat, mod[l], gid_lat, w_in_bf, cos64, sin64,
                                      lat_tiles, True, tm)
    aqx, akx, avx, bqx, bkx, bvx = _proj_ab(ctx2d, 0, n_ctx, mod[l], gid_ctx, w_in_bf, cos64, sin64,
                                            1, False, tm)
    sink = a_sink[0].astype(F32)
    oa = _window_attn(sink, aq, ak, av, akx, avx, batch, seq, ctx_len)
    tiles, tile_idx = _na_bias_tiles(b_rpb[0], seq // GRID_W)
    ob = _na_attn(bq, bk, bv, bkx, bvx, tiles, tile_idx, batch, seq, ctx_len)
    oax, obx = _ctx_ab_attn(sink, aqx, akx, avx, bqx, bkx, bvx, batch, ctx_len)

    n_all = n_lat + n_ctx
    w_out_bf = ab_w_out[0].astype(BF16)
    lng, lnb = ln_g[l, 0][None], ln_b[l, 0][None]
    wr_hi, wr_lo, br = router_weights(l)
    x1, h2, logits = _post_attn((oa, ob, x2d), (oax, obx, ctx2d), w_out_bf, mod[l], gid_all,
                                lng, lnb, wr_hi, wr_lo, br, n_lat, alpha, tm)
    xall = _moe_layer(h2, logits, x1, w_gate_up[l], w_down[l], mod[l], gid_all,
                      ln_g[l, 1][None], ln_b[l, 1][None], alpha, tm)

    l = 1
    lam_init = 0.8 - 0.6 * math.exp(-0.3 * l)
    lp = d_lambda[0].astype(F32)
    lam = (jnp.exp(jnp.sum(lp[0] * lp[1])) - jnp.exp(jnp.sum(lp[2] * lp[3])) + lam_init).reshape(1)
    w_in_p, wuq_p, wukv_p = _cd_weights(cd_w_in[0], c_w_uq[0], c_w_ukv[0])
    qn, kvn = c_q_norm[0][None].astype(F32), c_kv_norm[0][None].astype(F32)
    def with_ctx_identity(cos, sin):
        return (jnp.concatenate([cos, jnp.ones((ctx_len, LANES), F32)]),
                jnp.concatenate([sin, jnp.zeros((ctx_len, LANES), F32)]))

    tabs = with_ctx_identity(cos64, sin64) + with_ctx_identity(cos32, sin32)
    qc, kc, vct, dq, dk, dvt = _proj_cd(xall, mod[l], w_in_p, qn, wuq_p, kvn, wukv_p, tabs,
                                        batch, seq, ctx_len, PROJ_CD_TILE, tk)
    unroll = unroll_for(DENSE_UNROLL_MAX)
    oc_t = _mla_attn(qc, kc, vct, batch, seq, tq, tk, unroll, 1)
    od_t = _diff_attn(lam, dq, dk, dvt, d_subln[0].astype(F32).reshape(D_V, 1), 1.0 - lam_init,
                      batch, seq, tq, tk, unroll, 1)
    wr_hi, wr_lo, br = router_weights(l)
    x1, h2, logits = _post_attn((oc_t, od_t, xall), None, cd_w_out[0].astype(BF16), mod[l], gid_lat,
                                ln_g[l, 0][None], ln_b[l, 0][None], wr_hi, wr_lo, br,
                                n_lat, alpha, tm)
    out = _moe_layer(h2, logits, x1, w_gate_up[l], w_down[l], mod[l], gid_lat,
                     ln_g[l, 1][None], ln_b[l, 1][None], alpha, tm)
    return out.reshape(batch, seq, d)
```

```python
import functools
import math

import jax
import jax.numpy as jnp
from jax import lax
from jax.experimental import pallas as pl
from jax.experimental.pallas import tpu as pltpu

F32 = jnp.float32
BF16 = jnp.bfloat16

GRID_W = 64
HEAD_DIM = 64
ROPE_BASE = 10000.0
NEG_INF = -1e30
LOG2E = 1.4426950408889634

A_HEADS = 8
A_KV_HEADS = 2
A_WINDOW = 128
A_BLOCK = 128
B_HEADS = 8
NA_ROWS = 8
NA_COLS = 16
C_HEADS = 8
C_Q_RANK = 384
C_KV_RANK = 256
C_NOPE = 64
C_ROPE = 32
C_V = 64
D_HEADS = 4
D_QK = 64
D_V = 128
N_GROUPS = 4
EXP_PER_GROUP = 8
N_EXPERTS = N_GROUPS * EXP_PER_GROUP
TOP_K = 2
MOE_BLOCK = 512
MOE_SUB = 256
ROUTER_PAD = 128

LANES = 128
VMEM_LIMIT = 56 * 1024 * 1024


def _params(sem):
    return pltpu.CompilerParams(dimension_semantics=sem, vmem_limit_bytes=VMEM_LIMIT)


def _dot(a, b):
    return jnp.dot(a, b, preferred_element_type=F32)


def _dot_nt(a, b):
    return lax.dot_general(a, b, (((1,), (1,)), ((), ())), preferred_element_type=F32)


def _split_bf16(x):
    hi = x.astype(BF16)
    lo = (x - hi.astype(F32)).astype(BF16)
    return hi, lo


HI_HALF = 0xFFFF0000


def _pack_halves(x):
    w = x.shape[1] // 2
    lo = lax.bitcast_convert_type(x[:, :w].astype(BF16).astype(F32), jnp.uint32)
    hi = lax.bitcast_convert_type(x[:, w:].astype(BF16).astype(F32), jnp.uint32)
    return (lo >> 16) | (hi & jnp.uint32(HI_HALF))


def _unpack_halves(u):
    lo = lax.bitcast_convert_type(u << 16, F32)
    hi = lax.bitcast_convert_type(u & jnp.uint32(HI_HALF), F32)
    return jnp.concatenate([lo, hi], axis=1)


def _ada_kernel(c_ref, w_ref, b_ref, o_ref):
    c = c_ref[...]
    a = c / (1.0 + jnp.exp(-c))
    a_hi, a_lo = _split_bf16(a)
    w_hi, w_lo = _split_bf16(w_ref[0])
    acc = _dot(a_hi, w_hi) + _dot(a_lo, w_hi) + _dot(a_hi, w_lo)
    o_ref[0] = acc + b_ref[0]


def _ada(c_all, w_ada, b_ada):
    depth, d, n = w_ada.shape
    tn = 1536
    return pl.pallas_call(
        _ada_kernel,
        grid=(depth, n // tn),
        in_specs=[
            pl.BlockSpec((8, d), lambda l, j: (0, 0)),
            pl.BlockSpec((1, d, tn), lambda l, j: (l, 0, j)),
            pl.BlockSpec((1, 1, tn), lambda l, j: (l, 0, j)),
        ],
        out_specs=pl.BlockSpec((1, 8, tn), lambda l, j: (l, 0, j)),
        out_shape=jax.ShapeDtypeStruct((depth, 8, n), F32),
        compiler_params=_params(("arbitrary", "arbitrary")),
        name="ada",
    )(c_all, w_ada, b_ada.reshape(depth, 1, n))


def _rope_tile(t, cos, sin_signed, q):
    lane = lax.broadcasted_iota(jnp.int32, t.shape, 1)
    first = (lane & (2 * q - 1)) < q
    up = pltpu.roll(t, LANES - q, 1)
    dn = pltpu.roll(t, q, 1)
    return t * cos + jnp.where(first, up, dn) * sin_signed


def _rope_wide(t, cos, sin_signed, q):
    n = t.shape[1] // LANES
    return jnp.concatenate(
        [_rope_tile(t[:, i * LANES:(i + 1) * LANES], cos, sin_signed, q) for i in range(n)], axis=1)


def _modulate(x_ref, mod_ref, which):
    sh = mod_ref[0, 3 * which:3 * which + 1, :]
    sc = mod_ref[0, 3 * which + 1:3 * which + 2, :]
    return x_ref[...] * (1.0 + sc) + sh


def _proj_ab_kernel(x_ref, mod_ref, w_ref, cos_ref, sin_ref,
                    aq_ref, ak_ref, av_ref, bq_ref, bk_ref, bv_ref, *, rope):
    h = _modulate(x_ref, mod_ref, 0).astype(BF16)
    r = _dot(h, w_ref[...])
    aq = r[:, 0:512]
    ak = r[:, 512:640]
    if rope:
        cos = cos_ref[...]
        sin = sin_ref[...]
        aq = _rope_wide(aq, cos, sin, 16)
        ak = _rope_wide(ak, cos, sin, 16)
    scale = HEAD_DIM ** -0.5
    aq_ref[...] = (aq * scale).astype(BF16)
    ak_ref[...] = ak.astype(BF16)
    av_ref[...] = r[:, 640:768].astype(BF16)
    bq_ref[...] = (r[:, 768:1280] * scale).astype(BF16)
    bk_ref[...] = r[:, 1280:1792].astype(BF16)
    bv_ref[...] = r[:, 1792:2304].astype(BF16)


def _proj_ab(x2d, row0, nrows, mod, gid, w_bf, cos, sin, pos_blocks, rope, tm):
    d = x2d.shape[1]
    n_in = w_bf.shape[1]
    b0 = row0 // tm
    widths = (512, 128, 128, 512, 512, 512)
    return pl.pallas_call(
        functools.partial(_proj_ab_kernel, rope=rope),
        grid=(nrows // tm,),
        in_specs=[
            pl.BlockSpec((tm, d), lambda i: (b0 + i, 0)),
            pl.BlockSpec((1, 6, d), lambda i: (gid(i), 0, 0)),
            pl.BlockSpec((d, n_in), lambda i: (0, 0)),
            pl.BlockSpec((tm, LANES), lambda i: (i % pos_blocks, 0)),
            pl.BlockSpec((tm, LANES), lambda i: (i % pos_blocks, 0)),
        ],
        out_specs=[pl.BlockSpec((tm, w), lambda i: (i, 0)) for w in widths],
        out_shape=[jax.ShapeDtypeStruct((nrows, w), BF16) for w in widths],
        compiler_params=_params(("arbitrary",)),
        name="proj_ab",
    )(x2d, mod, w_bf, cos, sin)


def _softmax_pv(parts, sink_col):
    m = parts[0][0].max(-1, keepdims=True)
    for s, _ in parts[1:]:
        m = jnp.maximum(m, s.max(-1, keepdims=True))
    if sink_col is not None:
        m = jnp.maximum(m, sink_col)
    denom = None if sink_col is None else jnp.exp(sink_col - m)
    o = None
    for s, v in parts:
        e = jnp.exp(s - m)
        d = e.sum(-1, keepdims=True)
        denom = d if denom is None else denom + d
        pv = _dot(e.astype(BF16), v)
        o = pv if o is None else o + pv
    return o / denom


def _win_kernel(sink_ref, q_ref, kp_ref, kc_ref, kn_ref, vp_ref, vc_ref, vn_ref,
                kx_ref, vx_ref, o_ref, *, seq):
    step = pl.program_id(1)
    blk = A_BLOCK
    g_sz = A_HEADS // A_KV_HEADS
    qi = lax.broadcasted_iota(jnp.int32, (blk, 3 * blk), 0)
    kj = lax.broadcasted_iota(jnp.int32, (blk, 3 * blk), 1)
    in_window = jnp.abs(kj - blk - qi) <= A_WINDOW
    k_ext = jnp.concatenate([kp_ref[...], kc_ref[...], kn_ref[...]], axis=0)
    v_ext = jnp.concatenate([vp_ref[...], vc_ref[...], vn_ref[...]], axis=0)
    kx = kx_ref[...]
    vx = vx_ref[...]

    def scores(t, g):
        lo, hi = g * HEAD_DIM, (g + 1) * HEAD_DIM
        kpos = (step * WIN_GROUP + t) * blk + kj - blk
        valid = in_window & (kpos >= 0) & (kpos < seq)
        valid = jnp.concatenate([valid] * g_sz, axis=0)
        q = q_ref[t * blk:(t + 1) * blk, :]
        heads = [g * g_sz + i for i in range(g_sz)]
        qs = jnp.concatenate([q[:, h * HEAD_DIM:(h + 1) * HEAD_DIM] for h in heads], axis=0)
        s_loc = jnp.where(valid, _dot_nt(qs, k_ext[t * blk:(t + 3) * blk, lo:hi]), NEG_INF)
        return s_loc, _dot_nt(qs, kx[:, lo:hi])

    units = [(t, g) for t in range(WIN_GROUP) for g in range(A_KV_HEADS)]
    pending = scores(*units[0])
    outs = []
    for n, (t, g) in enumerate(units):
        s_loc, s_ctx = pending
        if n + 1 < len(units):
            pending = scores(*units[n + 1])
        lo, hi = g * HEAD_DIM, (g + 1) * HEAD_DIM
        sink = jnp.concatenate(
            [jnp.full((blk, 1), sink_ref[g * g_sz + i], F32) for i in range(g_sz)], axis=0)
        o = _softmax_pv([(s_ctx, vx[:, lo:hi]), (s_loc, v_ext[t * blk:(t + 3) * blk, lo:hi])], sink)
        outs.extend(o[i * blk:(i + 1) * blk] for i in range(g_sz))
        if g == A_KV_HEADS - 1:
            o_ref[t * blk:(t + 1) * blk, :] = jnp.concatenate(outs, axis=1).astype(o_ref.dtype)
            outs = []


WIN_GROUP = 16


def _window_attn(sink, aq, ak, av, akx, avx, batch, seq, ctx_len):
    nblk = seq // A_BLOCK
    nstep = nblk // WIN_GROUP
    kvw = A_KV_HEADS * HEAD_DIM
    qw = A_HEADS * HEAD_DIM
    rows = WIN_GROUP * A_BLOCK

    def edge(delta):
        return lambda b, n: (b * nblk + jnp.clip(n * WIN_GROUP + delta, 0, nblk - 1), 0)

    def own(b, n):
        return (b * nstep + n, 0)

    kv_specs = [pl.BlockSpec((A_BLOCK, kvw), edge(-1)), pl.BlockSpec((rows, kvw), own),
                pl.BlockSpec((A_BLOCK, kvw), edge(WIN_GROUP))]
    return pl.pallas_call(
        functools.partial(_win_kernel, seq=seq),
        grid=(batch, nstep),
        in_specs=[pl.BlockSpec(memory_space=pltpu.SMEM), pl.BlockSpec((rows, qw), own)]
        + kv_specs + kv_specs
        + [pl.BlockSpec((ctx_len, kvw), lambda b, n: (b, 0)),
           pl.BlockSpec((ctx_len, kvw), lambda b, n: (b, 0))],
        out_specs=pl.BlockSpec((rows, qw), own),
        out_shape=jax.ShapeDtypeStruct((batch * seq, qw), BF16),
        compiler_params=_params(("arbitrary", "arbitrary")),
        name="window_attn",
    )(sink, aq, ak, ak, ak, av, av, av, akx, avx)


NA_QROWS = 8
NA_Q = NA_QROWS * GRID_W
NA_KBLK = 4 * GRID_W
NA_K = 4 * NA_KBLK


NA_DR = 2 * NA_ROWS - 1


def _na_bias_tiles(rpb, rows):
    nj = rows // NA_QROWS
    kr = min(NA_ROWS, rows)
    col = jnp.arange(GRID_W)
    col_start = jnp.clip(col - NA_COLS // 2, 0, GRID_W - NA_COLS)
    cvalid = (col[None, :] >= col_start[:, None]) & (col[None, :] < col_start[:, None] + NA_COLS)
    dc = jnp.clip(col[None, :] - col[:, None], -(NA_COLS - 1), NA_COLS - 1) + (NA_COLS - 1)
    tiles = jnp.where(cvalid[None, None], rpb.astype(F32)[:, :, dc], NEG_INF)
    tiles = jnp.concatenate([tiles, jnp.full_like(tiles[:, :1], NEG_INF)], axis=1)
    idx = []
    for j in (0, min(1, nj - 1), nj - 1):
        r = j * NA_QROWS + jnp.arange(NA_QROWS)
        krow = j * NA_QROWS - NA_KBLK // GRID_W + jnp.arange(NA_K // GRID_W)
        start = jnp.clip(r - kr // 2, 0, rows - kr)
        rvalid = (krow[None, :] >= start[:, None]) & (krow[None, :] < start[:, None] + kr)
        dr = krow[None, :] - r[:, None] + (NA_ROWS - 1)
        idx.append(jnp.where(rvalid, dr, NA_DR))
    return tiles, jnp.stack(idx).reshape(-1).astype(jnp.int32)


def _na_kernel(idx_ref, q_ref, k0, k1, k2, k3, v0, v1, v2, v3, kx_ref, vx_ref, t_ref, o_ref, bias_scr,
               *, nj):
    j = pl.program_id(2)
    nkr = NA_K // GRID_W

    @pl.when((j == 0) | (j == 1) | (j == nj - 1))
    def _():
        variant = jnp.where(j == 0, 0, jnp.where(j == nj - 1, 2, 1))
        for hh in range(LANES // HEAD_DIM):
            for qr in range(NA_QROWS):
                for kr in range(nkr):
                    d = idx_ref[(variant * NA_QROWS + qr) * nkr + kr]
                    bias_scr[hh, qr * GRID_W:(qr + 1) * GRID_W, kr * GRID_W:(kr + 1) * GRID_W] = (
                        t_ref[hh, d])

    q = q_ref[...]
    k_all = jnp.concatenate([k0[...], k1[...], k2[...], k3[...]], axis=0)
    v_all = jnp.concatenate([v0[...], v1[...], v2[...], v3[...]], axis=0)
    kx = kx_ref[...]
    vx = vx_ref[...]
    qh_rows = NA_Q // 2
    k_span = 3 * NA_KBLK

    def scores(hh, half):
        lo, hi = hh * HEAD_DIM, (hh + 1) * HEAD_DIM
        q0, k0_ = half * qh_rows, half * NA_KBLK
        qh = q[q0:q0 + qh_rows, lo:hi]
        s_loc = _dot_nt(qh, k_all[k0_:k0_ + k_span, lo:hi]) + bias_scr[hh, q0:q0 + qh_rows, k0_:k0_ + k_span]
        return s_loc, _dot_nt(qh, kx[:, lo:hi])

    units = [(hh, half) for half in range(2) for hh in range(LANES // HEAD_DIM)]
    pending = scores(*units[0])
    for n, (hh, half) in enumerate(units):
        s_loc, s_ctx = pending
        if n + 1 < len(units):
            pending = scores(*units[n + 1])
        lo, hi = hh * HEAD_DIM, (hh + 1) * HEAD_DIM
        k0_ = half * NA_KBLK
        o = _softmax_pv([(s_ctx, vx[:, lo:hi]), (s_loc, v_all[k0_:k0_ + k_span, lo:hi])], None)
        o_ref[half * qh_rows:(half + 1) * qh_rows, lo:hi] = o.astype(o_ref.dtype)


def _na_attn(bq, bk, bv, bkx, bvx, tiles, tile_idx, batch, seq, ctx_len):
    nj = seq // NA_Q
    nkb = seq // NA_KBLK
    hp = B_HEADS * HEAD_DIM // LANES
    per = LANES // HEAD_DIM

    def kb(t):
        return lambda p, b, j, idx: (b * nkb + jnp.clip(2 * j - 1 + t, 0, nkb - 1), p)

    kv_specs = [pl.BlockSpec((NA_KBLK, LANES), kb(t)) for t in range(4)]
    grid_spec = pltpu.PrefetchScalarGridSpec(
        num_scalar_prefetch=1,
        grid=(hp, batch, nj),
        in_specs=[pl.BlockSpec((NA_Q, LANES), lambda p, b, j, idx: (b * nj + j, p))]
        + kv_specs + kv_specs
        + [pl.BlockSpec((ctx_len, LANES), lambda p, b, j, idx: (b, p)),
           pl.BlockSpec((ctx_len, LANES), lambda p, b, j, idx: (b, p)),
           pl.BlockSpec((per, NA_DR + 1, GRID_W, GRID_W), lambda p, b, j, idx: (p, 0, 0, 0))],
        out_specs=pl.BlockSpec((NA_Q, LANES), lambda p, b, j, idx: (b * nj + j, p)),
        scratch_shapes=[pltpu.VMEM((per, NA_Q, NA_K), F32)],
    )
    return pl.pallas_call(
        functools.partial(_na_kernel, nj=nj),
        grid_spec=grid_spec,
        out_shape=jax.ShapeDtypeStruct((batch * seq, B_HEADS * HEAD_DIM), BF16),
        compiler_params=_params(("arbitrary", "arbitrary", "arbitrary")),
        name="na_attn",
    )(tile_idx, bq, bk, bk, bk, bk, bv, bv, bv, bv, bkx, bvx, tiles)


def _ctx_ab_kernel(sink_ref, aq_ref, ak_ref, av_ref, bq_ref, bk_ref, bv_ref, oa_ref, ob_ref):
    ctx_len = aq_ref.shape[0]
    g_sz = A_HEADS // A_KV_HEADS
    aq = aq_ref[...]
    ak = ak_ref[...]
    av = av_ref[...]
    outs = []
    for g in range(A_KV_HEADS):
        lo, hi = g * HEAD_DIM, (g + 1) * HEAD_DIM
        heads = [g * g_sz + i for i in range(g_sz)]
        qs = jnp.concatenate([aq[:, h * HEAD_DIM:(h + 1) * HEAD_DIM] for h in heads], axis=0)
        sink = jnp.concatenate(
            [jnp.full((ctx_len, 1), sink_ref[h], F32) for h in heads], axis=0)
        o = _softmax_pv([(_dot_nt(qs, ak[:, lo:hi]), av[:, lo:hi])], sink)
        outs.extend(o[i * ctx_len:(i + 1) * ctx_len] for i in range(g_sz))
    oa_ref[...] = jnp.concatenate(outs, axis=1).astype(oa_ref.dtype)
    bq = bq_ref[...]
    bk = bk_ref[...]
    bv = bv_ref[...]
    outs = []
    for h in range(B_HEADS):
        lo, hi = h * HEAD_DIM, (h + 1) * HEAD_DIM
        outs.append(_softmax_pv([(_dot_nt(bq[:, lo:hi], bk[:, lo:hi]), bv[:, lo:hi])], None))
    ob_ref[...] = jnp.concatenate(outs, axis=1).astype(ob_ref.dtype)


def _ctx_ab_attn(sink, aq, ak, av, bq, bk, bv, batch, ctx_len):
    def spec(w):
        return pl.BlockSpec((ctx_len, w), lambda b: (b, 0))

    return pl.pallas_call(
        _ctx_ab_kernel,
        grid=(batch,),
        in_specs=[pl.BlockSpec(memory_space=pltpu.SMEM)]
        + [spec(a.shape[1]) for a in (aq, ak, av, bq, bk, bv)],
        out_specs=[spec(aq.shape[1]), spec(bq.shape[1])],
        out_shape=[jax.ShapeDtypeStruct(aq.shape, BF16), jax.ShapeDtypeStruct(bq.shape, BF16)],
        compiler_params=_params(("arbitrary",)),
        name="ctx_ab_attn",
    )(sink, aq, ak, av, bq, bk, bv)


def _layer_norm(t, g, b):
    mu = jnp.mean(t, -1, keepdims=True)
    c = t - mu
    var = jnp.mean(c * c, -1, keepdims=True)
    return c * lax.rsqrt(var + 1e-5) * g + b


def _post_attn_kernel(*refs, alpha, lat_tiles, with_ctx, lat_transposed):
    n_src = 6 if with_ctx else 3
    srcs = refs[:n_src]
    (w_ref, mod_ref, lng_ref, lnb_ref, wr_hi_ref, wr_lo_ref, br_ref,
     x1_ref, h2_ref, rec_ref, cnt_ref, carry) = refs[n_src:]
    step = pl.program_id(0)

    @pl.when(step == 0)
    def _():
        carry[...] = jnp.zeros(carry.shape, F32)

    def rows_of(o_ref):
        if o_ref.ndim == 2:
            return o_ref[...]
        return o_ref[0].astype(F32).T.astype(BF16)

    def run(o1_ref, o2_ref, x_ref):
        half = w_ref.shape[0] // 2
        y = _dot(rows_of(o1_ref), w_ref[0:half, :]) + _dot(rows_of(o2_ref), w_ref[half:, :])
        g1 = mod_ref[0, 2:3, :]
        x1 = _layer_norm(alpha * x_ref[...] + g1 * y, lng_ref[...], lnb_ref[...])
        x1_ref[...] = x1
        h2 = x1 * (1.0 + mod_ref[0, 4:5, :]) + mod_ref[0, 3:4, :]
        h_hi, h_lo = _split_bf16(h2)
        h2_ref[...] = _pack_halves(h2)
        both = _dot(h_hi, wr_lo_ref[...])
        logits = (both[:, :ROUTER_PAD] + both[:, ROUTER_PAD:] + _dot(h_lo, wr_hi_ref[...])
                  + br_ref[...])
        rec_ref[...] = _route_tile(logits, carry)

    if not with_ctx:
        run(*srcs)
    else:
        pl.when(step < lat_tiles)(lambda: run(*srcs[:3]))
        pl.when(step >= lat_tiles)(lambda: run(*srcs[3:]))

    @pl.when(step == pl.num_programs(0) - 1)
    def _():
        cnt_ref[...] = carry[...]


def _post_attn(lat, ctx_src, w_bf, mod, gid, lng, lnb, wr_hi, wr_lo, br, n_lat, alpha, tm):
    half = w_bf.shape[0] // 2
    d = lat[2].shape[1]
    lat_tiles = n_lat // tm
    n_rows = n_lat + (ctx_src[0].shape[0] if ctx_src is not None else 0)
    lat_transposed = lat[0].ndim == 3

    def lat_map(i):
        return (jnp.minimum(i, lat_tiles - 1), 0)

    def ctx_map(i):
        return (jnp.maximum(i - lat_tiles, 0), 0)

    def src_specs(index_map):
        return [pl.BlockSpec((tm, half), index_map), pl.BlockSpec((tm, half), index_map),
                pl.BlockSpec((tm, d), index_map)]

    in_specs = src_specs(lat_map)
    if lat_transposed:
        per_batch = lat[0].shape[2] // tm
        t_spec = pl.BlockSpec((1, half, tm), lambda i: (i // per_batch, 0, i % per_batch))
        in_specs[0] = in_specs[1] = t_spec
    args = list(lat)
    if ctx_src is not None:
        in_specs += src_specs(ctx_map)
        args += list(ctx_src)

    def const(shape):
        return pl.BlockSpec(shape, lambda i: (0,) * len(shape))

    in_specs += [const((2 * half, d)),
                 pl.BlockSpec((1, 6, d), lambda i: (gid(i), 0, 0)),
                 const((1, d)), const((1, d)),
                 const((d, ROUTER_PAD)), const((d, 2 * ROUTER_PAD)), const((1, ROUTER_PAD))]
    args += [w_bf, mod, lng, lnb, wr_hi, wr_lo, br]
    widths = (d, d // 2, ROUTER_PAD)
    dtypes = (F32, jnp.uint32, F32)
    return pl.pallas_call(
        functools.partial(_post_attn_kernel, alpha=alpha, lat_tiles=lat_tiles,
                          with_ctx=ctx_src is not None, lat_transposed=lat_transposed),
        grid=(n_rows // tm,),
        in_specs=in_specs,
        out_specs=[pl.BlockSpec((tm, w), lambda i: (i, 0)) for w in widths] + [const((8, ROUTER_PAD))],
        out_shape=[jax.ShapeDtypeStruct((n_rows, w), t) for w, t in zip(widths, dtypes)]
        + [jax.ShapeDtypeStruct((8, ROUTER_PAD), F32)],
        scratch_shapes=[pltpu.VMEM((8, ROUTER_PAD), F32)],
        compiler_params=_params(("arbitrary",)),
        name="post_attn",
    )(*args)


def _moe_kernel(be_ref, xs_ref, wgu_ref, wdn_ref, ys_ref, wgu_bf, wdn_bf):
    i = pl.program_id(0)
    used = i < be_ref[pl.num_programs(0)]
    prev = be_ref[jnp.maximum(i - 1, 0)]
    changed = (i == 0) | (be_ref[i] != prev)

    @pl.when(changed & used)
    def _():
        wgu_bf[...] = wgu_ref[0].astype(BF16)
        wdn_bf[...] = wdn_ref[0].astype(BF16)

    @pl.when(used)
    def _():
        de = wdn_bf.shape[0]
        subs = [slice(r, r + MOE_SUB) for r in range(0, MOE_BLOCK, MOE_SUB)]
        gus = [_dot(_unpack_halves(xs_ref[sl, :]).astype(BF16), wgu_bf[...]) for sl in subs]
        for sl, gu in zip(subs, gus):
            gate = gu[:, :de]
            up = gu[:, de:]
            act = gate / (1.0 + jnp.exp(-gate)) * up
            ys_ref[sl, :] = _pack_halves(_dot(act.astype(BF16), wdn_bf[...]))

    @pl.when(jnp.logical_not(used))
    def _():
        ys_ref[...] = jnp.zeros(ys_ref.shape, ys_ref.dtype)


def _moe_experts(block_e, xs, w_gu, w_dn):
    rows, dw = xs.shape
    d = 2 * dw
    nb = rows // MOE_BLOCK
    de = w_dn.shape[1]
    grid_spec = pltpu.PrefetchScalarGridSpec(
        num_scalar_prefetch=1,
        grid=(nb,),
        in_specs=[
            pl.BlockSpec((MOE_BLOCK, dw), lambda i, be: (i, 0)),
            pl.BlockSpec((1, d, 2 * de), lambda i, be: (be[i], 0, 0)),
            pl.BlockSpec((1, de, d), lambda i, be: (be[i], 0, 0)),
        ],
        out_specs=pl.BlockSpec((MOE_BLOCK, dw), lambda i, be: (i, 0)),
        scratch_shapes=[pltpu.VMEM((d, 2 * de), BF16), pltpu.VMEM((de, d), BF16)],
    )
    return pl.pallas_call(
        _moe_kernel,
        grid_spec=grid_spec,
        out_shape=jax.ShapeDtypeStruct((rows, dw), jnp.uint32),
        compiler_params=_params(("arbitrary",)),
        name="moe_experts",
    )(block_e, xs, w_gu, w_dn)


def _route_tile(lg, carry):
    tm = lg.shape[0]
    lane = lax.broadcasted_iota(jnp.int32, lg.shape, 1)

    def first_lane(mask):
        return jnp.min(jnp.where(mask, lane, ROUTER_PAD), axis=1, keepdims=True)

    is_g = lane < N_GROUPS
    g_log = jnp.where(is_g, lg, NEG_INF)
    g_max = jnp.max(g_log, axis=1, keepdims=True)
    g_sum = jnp.sum(jnp.where(is_g, jnp.exp(lg - g_max), 0.0), axis=1, keepdims=True)
    g_val = 1.0 / g_sum
    g_idx = first_lane(g_log == g_max)
    lo = N_GROUPS + EXP_PER_GROUP * g_idx
    sel = (lane >= lo) & (lane < lo + EXP_PER_GROUP)
    e_log = jnp.where(sel, lg, NEG_INF)
    e_max = jnp.max(e_log, axis=1, keepdims=True)
    e_exp = jnp.where(sel, jnp.exp(lg - e_max), 0.0)
    e_prob = e_exp / jnp.sum(e_exp, axis=1, keepdims=True)
    p1 = jnp.where(sel, e_prob, -1.0)
    v1 = jnp.max(p1, axis=1, keepdims=True)
    i1 = first_lane(p1 == v1)
    p2 = jnp.where(lane == i1, -1.0, p1)
    v2 = jnp.max(p2, axis=1, keepdims=True)
    i2 = first_lane(p2 == v2)
    norm = g_val / (v1 + v2)

    hot1 = lane == i1
    hot2 = lane == i2
    hot = jnp.where(hot1 | hot2, 1.0, 0.0)
    rows = lax.broadcasted_iota(jnp.int32, (tm, tm), 0)
    cols = lax.broadcasted_iota(jnp.int32, (tm, tm), 1)
    before = jnp.where(cols < rows, 1.0, 0.0).astype(BF16)
    prefix = _dot(before, hot.astype(BF16)) + carry[0:1, :]
    r1 = jnp.sum(jnp.where(hot1, prefix, 0.0), axis=1, keepdims=True)
    r2 = jnp.sum(jnp.where(hot2, prefix, 0.0), axis=1, keepdims=True)
    carry[0:1, :] = carry[0:1, :] + jnp.sum(hot, axis=0, keepdims=True)

    fields = [(i1 - N_GROUPS).astype(F32), (i2 - N_GROUPS).astype(F32), r1, r2, v1 * norm, v2 * norm]
    rec = jnp.zeros(lg.shape, F32)
    for k, f in enumerate(fields):
        rec = jnp.where(lane == k, f, rec)
    return rec


def _route_layout(rec, cnt, n_tok):
    experts = rec[:, 0:2].astype(jnp.int32)
    rank = rec[:, 2:4].astype(jnp.int32)
    wts = rec[:, 4:6]
    counts = cnt[0, N_GROUPS:N_GROUPS + N_EXPERTS].astype(jnp.int32)

    n_asg = n_tok * TOP_K
    pcounts = ((counts + MOE_BLOCK - 1) // MOE_BLOCK) * MOE_BLOCK
    pends = jnp.cumsum(pcounts)
    pstarts = pends - pcounts
    dest = pstarts[experts] + rank
    nb = -(-n_asg // MOE_BLOCK) + N_EXPERTS
    block_start = jnp.arange(nb, dtype=jnp.int32) * MOE_BLOCK
    block_e = jnp.minimum((pends[None, :] <= block_start[:, None]).sum(1), N_EXPERTS - 1).astype(jnp.int32)
    block_e = jnp.concatenate([block_e, (pends[-1:] // MOE_BLOCK).astype(jnp.int32)])
    tok = jnp.broadcast_to(jnp.arange(n_tok, dtype=jnp.int32)[:, None], (n_tok, TOP_K))
    filler = jnp.arange(nb * MOE_BLOCK, dtype=jnp.int32) % n_tok
    row_tok = filler.at[dest.reshape(-1)].set(tok.reshape(-1), unique_indices=True)
    return wts, dest, row_tok, block_e


def _post_moe_kernel(x_ref, ya_ref, yb_ref, w_ref, mod_ref, lng_ref, lnb_ref, o_ref, *, alpha):
    w = w_ref[...]
    y = w[:, 0:1] * _unpack_halves(ya_ref[...]) + w[:, 1:2] * _unpack_halves(yb_ref[...])
    g2 = mod_ref[0, 5:6, :]
    o_ref[...] = _layer_norm(alpha * x_ref[...] + g2 * y, lng_ref[...], lnb_ref[...])


def _post_moe(x1, ya, yb, wts, mod, gid, lng, lnb, alpha, tm):
    rows, d = x1.shape
    row = pl.BlockSpec((tm, d), lambda i: (i, 0))
    packed = pl.BlockSpec((tm, d // 2), lambda i: (i, 0))
    vec = pl.BlockSpec((1, d), lambda i: (0, 0))
    return pl.pallas_call(
        functools.partial(_post_moe_kernel, alpha=alpha),
        grid=(rows // tm,),
        in_specs=[row, packed, packed,
                  pl.BlockSpec((tm, TOP_K), lambda i: (i, 0)),
                  pl.BlockSpec((1, 6, d), lambda i: (gid(i), 0, 0)),
                  vec, vec],
        out_specs=row,
        out_shape=jax.ShapeDtypeStruct((rows, d), F32),
        compiler_params=_params(("arbitrary",)),
        name="post_moe",
    )(x1, ya, yb, wts, mod, lng, lnb)


CD_COLS = dict(cq=(0, 384), ckv=(384, 640), dq=(640, 1152), dk=(1152, 1664), dv=(1664, 2176),
               kr=(2176, 2304))
C_HEAD_PAD = LANES


def _rms(t, g, eps=1e-6):
    return t * lax.rsqrt(jnp.mean(t * t, -1, keepdims=True) + eps) * g


def _proj_cd_kernel(x_ref, mod_ref, w_ref, qn_ref, wuq_ref, kvn_ref, wukv_ref,
                    cos64_ref, sin64_ref, cos32_ref, sin32_ref,
                    qc_ref, kc_ref, vct_ref, dq_ref, dk_ref, dvt_ref):
    h = _modulate(x_ref, mod_ref, 0).astype(BF16)
    r = _dot(h, w_ref[...])

    def cols(name):
        lo, hi = CD_COLS[name]
        return r[:, lo:hi]

    q = _dot(_rms(cols('cq'), qn_ref[...]).astype(BF16), wuq_ref[...])
    kv = _dot(_rms(cols('ckv'), kvn_ref[...]).astype(BF16), wukv_ref[...])
    c32, s32 = cos32_ref[...], sin32_ref[...]
    c64, s64 = cos64_ref[...], sin64_ref[...]
    q = _rope_wide(q, c32, s32, C_ROPE // 4)
    kr = _rope_tile(cols('kr'), c32, s32, C_ROPE // 4)
    dq = _rope_wide(cols('dq'), c64, s64, D_QK // 4)
    dk = _rope_wide(cols('dk'), c64, s64, D_QK // 4)
    kw = C_HEADS * C_HEAD_PAD
    qc_ref[0] = (q * ((C_NOPE + C_ROPE) ** -0.5 * LOG2E)).astype(BF16)
    kc_ref[0] = (kv[:, :kw] + jnp.concatenate([kr] * C_HEADS, axis=1)).astype(BF16)
    dq_ref[0] = (dq * (D_QK ** -0.5 * LOG2E)).astype(BF16)
    dk_ref[0] = dk.astype(BF16)
    _store_vt(vct_ref, kv[:, kw:], C_HEADS, C_V)
    _store_vt(dvt_ref, cols('dv'), D_HEADS, D_V)


def _store_vt(vt_ref, v, heads, dv):
    tm = v.shape[0]
    v_t = v.T.astype(BF16)
    row = lax.broadcasted_iota(jnp.int32, (ONES_ROWS, tm), 0)
    extra = jnp.where(row == 0, 1.0, 0.0).astype(BF16)
    for h in range(heads):
        vt_ref[0, h, 0, 0:dv, :] = v_t[h * dv:(h + 1) * dv]
        vt_ref[0, h, 0, dv:dv + ONES_ROWS, :] = extra


def _proj_cd(xall, mod, w_bf, qn, wuq, kvn, wukv, tabs, batch, seq, ctx_len, tm, tk):
    d = xall.shape[1]
    lat_t, ctx_t = seq // tm, ctx_len // tm
    n_keys = seq + ctx_len
    per_chunk = tk // tm
    n_chunks = n_keys // tk
    widths = (C_HEADS * C_HEAD_PAD, C_HEADS * C_HEAD_PAD, 2 * D_HEADS * D_QK, 2 * D_HEADS * D_QK)

    def src(b, t):
        return (jnp.where(t < lat_t, b * lat_t + t, batch * lat_t + b * ctx_t + t - lat_t), 0)

    def full(a):
        return pl.BlockSpec(a.shape, lambda b, t: (0,) * a.ndim)

    def row_spec(w):
        return pl.BlockSpec((1, tm, w), lambda b, t: (b, t, 0))

    def row_shape(w):
        return jax.ShapeDtypeStruct((batch, n_keys, w), BF16)

    def vt_spec(heads, dv):
        return pl.BlockSpec((1, heads, 1, dv + ONES_ROWS, tm),
                            lambda b, t: (b, 0, t // per_chunk, 0, t % per_chunk))

    def vt_shape(heads, dv):
        return jax.ShapeDtypeStruct((batch, heads, n_chunks, dv + ONES_ROWS, tk), BF16)

    tab = pl.BlockSpec((tm, LANES), lambda b, t: (t, 0))
    return pl.pallas_call(
        _proj_cd_kernel,
        grid=(batch, lat_t + ctx_t),
        in_specs=[pl.BlockSpec((tm, d), src),
                  pl.BlockSpec((1, 6, d), lambda b, t: (jnp.where(t < lat_t, b, batch), 0, 0)),
                  full(w_bf), full(qn), full(wuq), full(kvn), full(wukv), tab, tab, tab, tab],
        out_specs=[row_spec(widths[0]), row_spec(widths[1]), vt_spec(C_HEADS, C_V),
                   row_spec(widths[2]), row_spec(widths[3]), vt_spec(D_HEADS, D_V)],
        out_shape=[row_shape(widths[0]), row_shape(widths[1]), vt_shape(C_HEADS, C_V),
                   row_shape(widths[2]), row_shape(widths[3]), vt_shape(D_HEADS, D_V)],
        compiler_params=_params(("arbitrary", "arbitrary")),
        name="proj_cd",
    )(xall, mod, w_bf, qn, wuq, kvn, wukv, *tabs)


DENSE_SHIFT = 80.0
DENSE_L_MIN = 2.0 ** -60
DENSE_L_MAX = 2.0 ** 120
DENSE_TQ = 1024
DENSE_TK_MAX = 1280
DENSE_UNROLL_MAX = 13
MXU_DEPTH = 256
PROJ_CD_TILE = 256
ONES_ROWS = 16


def _sq_norm_row(x):
    ones = jnp.ones((8, x.shape[1]), BF16)
    return _dot_nt(ones, (x * x).astype(BF16))[0:1] * 1.02


def _key_norm_max(kchunk, n_chunks, masks):
    def body(j, mx):
        kf = kchunk(j).astype(F32)
        ksq = kf * kf
        out = []
        for msk, cur in zip(masks, mx):
            part = ksq if msk is None else jnp.where(msk, ksq, 0.0)
            rn = jnp.sum(part, axis=1, keepdims=True)
            out.append(jnp.maximum(cur, jnp.max(rn, axis=0, keepdims=True)))
        return tuple(out)

    mx = lax.fori_loop(0, n_chunks, body, tuple(jnp.zeros((1, 1), F32) for _ in masks))
    return [jnp.sqrt(v) * 1.01 for v in mx]


def _safe_online(q, kchunk, vt_chunk, n_chunks, m_scr, l_scr, acc):
    m_scr[...] = jnp.full(m_scr.shape, NEG_INF, F32)
    l_scr[...] = jnp.zeros(l_scr.shape, F32)
    acc[...] = jnp.zeros(acc.shape, F32)

    def body(j, carry):
        s = _dot_nt(kchunk(j), q)
        m_prev = m_scr[...]
        m_new = jnp.maximum(m_prev, s.max(0, keepdims=True))
        alpha = jnp.exp2(m_prev - m_new)
        p = jnp.exp2(s - m_new)
        l_scr[...] = alpha * l_scr[...] + p.sum(0, keepdims=True)
        acc[...] = alpha * acc[...] + _dot(vt_chunk(j), p.astype(BF16))
        m_scr[...] = m_new
        return carry

    lax.fori_loop(0, n_chunks, body, 0)


def _denominators_ok(*ls):
    ok = None
    for l in ls:
        cur = (l > DENSE_L_MIN) & (l < DENSE_L_MAX)
        ok = cur if ok is None else ok & cur
    return jnp.max(jnp.where(ok, 0.0, 1.0)) == 0.0


MLA_HEADS_PER_STEP = 2


def _mla_kernel(q_ref, k_ref, vt_ref, o_ref, kmax_scr, acc_scr, m_scr, l_scr,
                *, n_chunks, tk, unroll, depth):
    hp = acc_scr.shape[0]

    def kchunk(j, h):
        return k_ref[0, pl.ds(pl.multiple_of(j * tk, tk), tk), h * C_HEAD_PAD:(h + 1) * C_HEAD_PAD]

    @pl.when(pl.program_id(2) == 0)
    def _():
        for h in range(hp):
            (kmax,) = _key_norm_max(lambda j: kchunk(j, h), n_chunks, [None])
            kmax_scr[h:h + 1, :] = jnp.broadcast_to(kmax, (1, LANES))

    qs = [q_ref[0, :, h * C_HEAD_PAD:(h + 1) * C_HEAD_PAD] for h in range(hp)]
    ms = [jnp.sqrt(_sq_norm_row(qs[h].astype(F32))) * kmax_scr[h:h + 1, 0:1] - DENSE_SHIFT
          for h in range(hp)]
    acc_scr[...] = jnp.zeros(acc_scr.shape, F32)

    def scores(j):
        return [_dot_nt(kchunk(j, h), qs[h]) for h in range(hp)]

    def body(it, carry):
        j0 = it * unroll
        pend = [scores(j0 + u) for u in range(min(depth, unroll))]
        accs = [acc_scr[h] for h in range(hp)]
        for u in range(unroll):
            ss = pend.pop(0)
            if u + depth < unroll:
                pend.append(scores(j0 + u + depth))
            for h in range(hp):
                accs[h] = accs[h] + _dot(vt_ref[0, h, j0 + u], jnp.exp2(ss[h] - ms[h]).astype(BF16))
        for h in range(hp):
            acc_scr[h] = accs[h]
        return carry

    lax.fori_loop(0, n_chunks // unroll, body, 0)
    ls = [acc_scr[h, C_V:C_V + 1, :] for h in range(hp)]
    for h in range(hp):
        o_ref[0, h * C_V:(h + 1) * C_V, :] = (acc_scr[h, 0:C_V, :] / ls[h]).astype(o_ref.dtype)

    @pl.when(jnp.logical_not(_denominators_ok(*ls)))
    def _():
        for h in range(hp):
            _safe_online(qs[h], lambda j: kchunk(j, h), lambda j: vt_ref[0, h, j], n_chunks,
                         m_scr, l_scr, acc_scr.at[h])
            o_ref[0, h * C_V:(h + 1) * C_V, :] = (acc_scr[h, 0:C_V, :] / l_scr[...]).astype(o_ref.dtype)


def _mla_attn(qc, kc, vt, batch, seq, tq, tk, unroll, depth):
    n_keys = kc.shape[1]
    nq, n_chunks = seq // tq, n_keys // tk
    rows = C_V + ONES_ROWS
    hp = MLA_HEADS_PER_STEP
    return pl.pallas_call(
        functools.partial(_mla_kernel, n_chunks=n_chunks, tk=tk, unroll=unroll, depth=depth),
        grid=(batch, C_HEADS // hp, nq),
        in_specs=[
            pl.BlockSpec((1, tq, hp * C_HEAD_PAD), lambda b, h, i: (b, i, h)),
            pl.BlockSpec((1, n_keys, hp * C_HEAD_PAD), lambda b, h, i: (b, 0, h)),
            pl.BlockSpec((1, hp, n_chunks, rows, tk), lambda b, h, i: (b, h, 0, 0, 0)),
        ],
        out_specs=pl.BlockSpec((1, hp * C_V, tq), lambda b, h, i: (b, h, i)),
        out_shape=jax.ShapeDtypeStruct((batch, C_HEADS * C_V, seq), BF16),
        scratch_shapes=[pltpu.VMEM((8, LANES), F32), pltpu.VMEM((hp, rows, tq), F32),
                        pltpu.VMEM((1, tq), F32), pltpu.VMEM((1, tq), F32)],
        compiler_params=_params(("arbitrary", "arbitrary", "arbitrary")),
        name="mla_attn",
    )(qc, kc, vt)


def _diff_kernel(lam_ref, q_ref, k_ref, vt_ref, subln_ref, o_ref, kmax_scr, acc1_scr, acc2_scr,
                 m_scr, l_scr, *, n_chunks, tk, unroll, depth, out_scale):
    def kchunk(j):
        return k_ref[0, pl.ds(pl.multiple_of(j * tk, tk), tk), :]

    @pl.when(pl.program_id(2) == 0)
    def _():
        klane = lax.broadcasted_iota(jnp.int32, (tk, 2 * D_QK), 1)
        k1, k2 = _key_norm_max(kchunk, n_chunks, [klane < D_QK, klane >= D_QK])
        kmax_scr[0:1, :] = jnp.broadcast_to(k1, (1, LANES))
        kmax_scr[1:2, :] = jnp.broadcast_to(k2, (1, LANES))

    q = q_ref[0]
    qlane = lax.broadcasted_iota(jnp.int32, q.shape, 1)
    zero = jnp.zeros_like(q)
    q1 = jnp.where(qlane < D_QK, q, zero)
    q2 = jnp.where(qlane >= D_QK, q, zero)
    m1 = jnp.sqrt(_sq_norm_row(q1.astype(F32))) * kmax_scr[0:1, 0:1] - DENSE_SHIFT
    m2 = jnp.sqrt(_sq_norm_row(q2.astype(F32))) * kmax_scr[1:2, 0:1] - DENSE_SHIFT
    acc1_scr[...] = jnp.zeros(acc1_scr.shape, F32)
    acc2_scr[...] = jnp.zeros(acc2_scr.shape, F32)

    def scores(j):
        k = kchunk(j)
        return _dot_nt(k, q1), _dot_nt(k, q2)

    def body(it, carry):
        j0 = it * unroll
        pend = [scores(j0 + u) for u in range(min(depth, unroll))]
        a1 = acc1_scr[...]
        a2 = acc2_scr[...]
        for u in range(unroll):
            s1, s2 = pend.pop(0)
            if u + depth < unroll:
                pend.append(scores(j0 + u + depth))
            vt = vt_ref[0, 0, j0 + u]
            a1 = a1 + _dot(vt, jnp.exp2(s1 - m1).astype(BF16))
            a2 = a2 + _dot(vt, jnp.exp2(s2 - m2).astype(BF16))
        acc1_scr[...] = a1
        acc2_scr[...] = a2
        return carry

    lax.fori_loop(0, n_chunks // unroll, body, 0)

    def finish(o1, o2):
        o = o1 - lam_ref[0] * o2
        o = o * lax.rsqrt(jnp.mean(o * o, 0, keepdims=True) + 1e-6) * subln_ref[...] * out_scale
        o_ref[0] = o.astype(o_ref.dtype)

    l1 = acc1_scr[D_V:D_V + 1, :]
    l2 = acc2_scr[D_V:D_V + 1, :]
    finish(acc1_scr[0:D_V, :] / l1, acc2_scr[0:D_V, :] / l2)

    @pl.when(jnp.logical_not(_denominators_ok(l1, l2)))
    def _():
        def vt_chunk(j):
            return vt_ref[0, 0, j]

        _safe_online(q1, kchunk, vt_chunk, n_chunks, m_scr, l_scr, acc1_scr)
        o1 = acc1_scr[0:D_V, :] / l_scr[...]
        _safe_online(q2, kchunk, vt_chunk, n_chunks, m_scr, l_scr, acc2_scr)
        finish(o1, acc2_scr[0:D_V, :] / l_scr[...])


def _diff_attn(lam, dq, dk, vt, subln_col, out_scale, batch, seq, tq, tk, unroll, depth):
    n_keys = dk.shape[1]
    nq, n_chunks = seq // tq, n_keys // tk
    rows = D_V + ONES_ROWS
    pair = 2 * D_QK
    return pl.pallas_call(
        functools.partial(_diff_kernel, n_chunks=n_chunks, tk=tk, unroll=unroll, depth=depth,
                          out_scale=out_scale),
        grid=(batch, D_HEADS, nq),
        in_specs=[
            pl.BlockSpec(memory_space=pltpu.SMEM),
            pl.BlockSpec((1, tq, pair), lambda b, h, i: (b, i, h)),
            pl.BlockSpec((1, n_keys, pair), lambda b, h, i: (b, 0, h)),
            pl.BlockSpec((1, 1, n_chunks, rows, tk), lambda b, h, i: (b, h, 0, 0, 0)),
            pl.BlockSpec((D_V, 1), lambda b, h, i: (0, 0)),
        ],
        out_specs=pl.BlockSpec((1, D_V, tq), lambda b, h, i: (b, h, i)),
        out_shape=jax.ShapeDtypeStruct((batch, D_HEADS * D_V, seq), BF16),
        scratch_shapes=[pltpu.VMEM((8, LANES), F32), pltpu.VMEM((rows, tq), F32),
                        pltpu.VMEM((rows, tq), F32), pltpu.VMEM((1, tq), F32), pltpu.VMEM((1, tq), F32)],
        compiler_params=_params(("arbitrary", "arbitrary", "arbitrary")),
        name="diff_attn",
    )(lam, dq, dk, vt, subln_col)


def _rope_tables(n_tok, dim, lane_lo):
    t = jnp.arange(n_tok)
    pos_r = (t // GRID_W).astype(F32)
    pos_c = (t % GRID_W).astype(F32)
    quarter = dim // 4
    inv = ROPE_BASE ** (-jnp.arange(quarter, dtype=F32) / quarter)
    ang_r = pos_r[:, None] * inv
    ang_c = pos_c[:, None] * inv
    ang = jnp.concatenate([ang_r, ang_r, ang_c, ang_c], -1)
    sign = jnp.tile(jnp.concatenate([-jnp.ones(quarter), jnp.ones(quarter)]), 2).astype(F32)
    cos, sin = jnp.cos(ang), jnp.sin(ang) * sign
    reps = (LANES - lane_lo) // dim
    cos = jnp.concatenate([jnp.ones((n_tok, lane_lo), F32)] + [cos] * reps, axis=1)
    sin = jnp.concatenate([jnp.zeros((n_tok, lane_lo), F32)] + [sin] * reps, axis=1)
    return cos, sin


def _cd_weights(w_in, w_uq, w_ukv):
    d = w_in.shape[0]
    s0 = C_Q_RANK
    s1 = s0 + C_KV_RANK
    s2 = s1 + C_ROPE
    s3 = s2 + 2 * D_HEADS * D_QK
    s4 = s3 + 2 * D_HEADS * D_QK
    kr = jnp.zeros((d, LANES), F32).at[:, C_NOPE:C_NOPE + C_ROPE].set(w_in[:, s1:s2])
    w_in_p = jnp.concatenate([w_in[:, :s1], w_in[:, s2:s3], w_in[:, s3:s4], w_in[:, s4:], kr], axis=1)
    dqk = C_NOPE + C_ROPE
    wq = w_uq.reshape(C_Q_RANK, C_HEADS, dqk)
    wq = jnp.pad(wq, ((0, 0), (0, 0), (0, C_HEAD_PAD - dqk))).reshape(C_Q_RANK, C_HEADS * C_HEAD_PAD)
    wkv = w_ukv.reshape(C_KV_RANK, C_HEADS, C_NOPE + C_V)
    wk = jnp.pad(wkv[:, :, :C_NOPE], ((0, 0), (0, 0), (0, C_HEAD_PAD - C_NOPE)))
    wk = wk.reshape(C_KV_RANK, C_HEADS * C_HEAD_PAD)
    wv = wkv[:, :, C_NOPE:].reshape(C_KV_RANK, C_HEADS * C_V)
    return w_in_p.astype(BF16), wq.astype(BF16), jnp.concatenate([wk, wv], axis=1).astype(BF16)


def _moe_layer(h2, rec, cnt, x1, w_gu, w_dn, mod, gid, lng, lnb, alpha, tm):
    n_tok = h2.shape[0]
    wts, dest, row_tok, block_e = _route_layout(rec, cnt, n_tok)
    xs = jnp.take(h2, row_tok, axis=0, mode='clip')
    ys = _moe_experts(block_e, xs, w_gu, w_dn)
    ya = jnp.take(ys, dest[:, 0], axis=0, mode='clip')
    yb = jnp.take(ys, dest[:, 1], axis=0, mode='clip')
    return _post_moe(x1, ya, yb, wts.astype(F32), mod, gid, lng, lnb, alpha, tm)


def kernel(x, c, ctx, c_ctx, w_ada, b_ada, ln_g, ln_b, ab_w_in, a_sink, b_rpb, ab_w_out, cd_w_in, c_q_norm, c_w_uq, c_kv_norm, c_w_ukv, d_lambda, d_subln, cd_w_out, w_group, b_group, w_exp_router, b_exp_router, w_gate_up, w_down):
    batch, seq, d = x.shape
    ctx_len = ctx.shape[1]
    depth = w_ada.shape[0]
    alpha = (2 * depth) ** 0.25
    n_lat = batch * seq
    n_ctx = batch * ctx_len
    tm = 512
    tq = DENSE_TQ
    assert depth == 2 and seq % NA_Q == 0 and seq % tq == 0 and n_ctx % tm == 0
    assert seq % PROJ_CD_TILE == 0 and ctx_len % PROJ_CD_TILE == 0
    assert seq % (WIN_GROUP * A_BLOCK) == 0 and seq // NA_Q >= 2
    key_tiles = (seq + ctx_len) // MXU_DEPTH
    tk = MXU_DEPTH * max(u for u in range(1, DENSE_TK_MAX // MXU_DEPTH + 1) if key_tiles % u == 0)
    n_chunks = (seq + ctx_len) // tk

    def unroll_for(cap):
        return max(u for u in range(1, cap + 1) if n_chunks % u == 0)
    lat_tiles = seq // tm

    def gid_lat(i):
        return i // lat_tiles

    def gid_ctx(i):
        return batch

    def gid_all(i):
        return jnp.minimum(i // lat_tiles, batch)

    c_all = jnp.zeros((8, d), F32).at[:batch].set(c).at[batch].set(c_ctx)
    mod = _ada(c_all, w_ada, b_ada).reshape(depth, 8, 6, d)

    cos64, sin64 = _rope_tables(seq, HEAD_DIM, 0)
    cos32, sin32 = _rope_tables(seq, C_ROPE, C_NOPE)

    def router_weights(l):
        wr = jnp.zeros((d, ROUTER_PAD), F32)
        wr = wr.at[:, :N_GROUPS].set(w_group[l]).at[:, N_GROUPS:N_GROUPS + N_EXPERTS].set(w_exp_router[l])
        br = jnp.zeros((1, ROUTER_PAD), F32)
        br = br.at[0, :N_GROUPS].set(b_group[l]).at[0, N_GROUPS:N_GROUPS + N_EXPERTS].set(b_exp_router[l])
        hi = wr.astype(BF16)
        lo = (wr - hi.astype(F32)).astype(BF16)
        return hi, jnp.concatenate([hi, lo], axis=1), br

    x2d = x.reshape(n_lat, d)
    ctx2d = ctx.reshape(n_ctx, d)

    l = 0
    w_in_bf = ab_w_in[0].astype(BF16)
    aq, ak, av, bq, bk, bv = _proj_ab(x2d, 0, n_lat, mod[l], gid_lat, w_in_bf, cos64, sin64,
                                      lat_tiles, True, tm)
    aqx, akx, avx, bqx, bkx, bvx = _proj_ab(ctx2d, 0, n_ctx, mod[l], gid_ctx, w_in_bf, cos64, sin64,
                                            1, False, tm)
    sink = a_sink[0].astype(F32)
    oa = _window_attn(sink, aq, ak, av, akx, avx, batch, seq, ctx_len)
    tiles, tile_idx = _na_bias_tiles(b_rpb[0], seq // GRID_W)
    ob = _na_attn(bq, bk, bv, bkx, bvx, tiles, tile_idx, batch, seq, ctx_len)
    oax, obx = _ctx_ab_attn(sink, aqx, akx, avx, bqx, bkx, bvx, batch, ctx_len)

    n_all = n_lat + n_ctx
    w_out_bf = ab_w_out[0].astype(BF16)
    lng, lnb = ln_g[l, 0][None], ln_b[l, 0][None]
    wr_hi, wr_lo, br = router_weights(l)
    x1, h2, rec, cnt = _post_attn((oa, ob, x2d), (oax, obx, ctx2d), w_out_bf, mod[l], gid_all,
                                  lng, lnb, wr_hi, wr_lo, br, n_lat, alpha, tm)
    xall = _moe_layer(h2, rec, cnt, x1, w_gate_up[l], w_down[l], mod[l], gid_all,
                      ln_g[l, 1][None], ln_b[l, 1][None], alpha, tm)

    l = 1
    lam_init = 0.8 - 0.6 * math.exp(-0.3 * l)
    lp = d_lambda[0].astype(F32)
    lam = (jnp.exp(jnp.sum(lp[0] * lp[1])) - jnp.exp(jnp.sum(lp[2] * lp[3])) + lam_init).reshape(1)
    w_in_p, wuq_p, wukv_p = _cd_weights(cd_w_in[0], c_w_uq[0], c_w_ukv[0])
    qn, kvn = c_q_norm[0][None].astype(F32), c_kv_norm[0][None].astype(F32)
    def with_ctx_identity(cos, sin):
        return (jnp.concatenate([cos, jnp.ones((ctx_len, LANES), F32)]),
                jnp.concatenate([sin, jnp.zeros((ctx_len, LANES), F32)]))

    tabs = with_ctx_identity(cos64, sin64) + with_ctx_identity(cos32, sin32)
    qc, kc, vct, dq, dk, dvt = _proj_cd(xall, mod[l], w_in_p, qn, wuq_p, kvn, wukv_p, tabs,
                                        batch, seq, ctx_len, PROJ_CD_TILE, tk)
    unroll = unroll_for(DENSE_UNROLL_MAX)
    oc_t = _mla_attn(qc, kc, vct, batch, seq, tq, tk, unroll, 1)
    od_t = _diff_attn(lam, dq, dk, dvt, d_subln[0].astype(F32).reshape(D_V, 1), 1.0 - lam_init,
                      batch, seq, tq, tk, unroll, 1)
    wr_hi, wr_lo, br = router_weights(l)
    x1, h2, rec, cnt = _post_attn((oc_t, od_t, xall), None, cd_w_out[0].astype(BF16), mod[l], gid_lat,
                                  ln_g[l, 0][None], ln_b[l, 0][None], wr_hi, wr_lo, br,
                                  n_lat, alpha, tm)
    out = _moe_layer(h2, rec, cnt, x1, w_gate_up[l], w_down[l], mod[l], gid_lat,
                     ln_g[l, 1][None], ln_b[l, 1][None], alpha, tm)
    return out.reshape(batch, seq, d)
```

```python
import functools
import math

import jax
import jax.numpy as jnp
from jax import lax
from jax.experimental import pallas as pl
from jax.experimental.pallas import tpu as pltpu

F32 = jnp.float32
BF16 = jnp.bfloat16

GRID_W = 64
HEAD_DIM = 64
ROPE_BASE = 10000.0
NEG_INF = -1e30
LOG2E = 1.4426950408889634

A_HEADS = 8
A_KV_HEADS = 2
A_WINDOW = 128
A_BLOCK = 128
B_HEADS = 8
NA_ROWS = 8
NA_COLS = 16
C_HEADS = 8
C_Q_RANK = 384
C_KV_RANK = 256
C_NOPE = 64
C_ROPE = 32
C_V = 64
D_HEADS = 4
D_QK = 64
D_V = 128
N_GROUPS = 4
EXP_PER_GROUP = 8
N_EXPERTS = N_GROUPS * EXP_PER_GROUP
TOP_K = 2
MOE_BLOCK = 512
MOE_SUB = 256
ROUTER_PAD = 128

LANES = 128
VMEM_LIMIT = 56 * 1024 * 1024


def _params(sem):
    return pltpu.CompilerParams(dimension_semantics=sem, vmem_limit_bytes=VMEM_LIMIT)


def _dot(a, b):
    return jnp.dot(a, b, preferred_element_type=F32)


def _dot_nt(a, b):
    return lax.dot_general(a, b, (((1,), (1,)), ((), ())), preferred_element_type=F32)


def _split_bf16(x):
    hi = x.astype(BF16)
    lo = (x - hi.astype(F32)).astype(BF16)
    return hi, lo


HI_HALF = 0xFFFF0000


def _pack_halves(x):
    w = x.shape[1] // 2
    lo = lax.bitcast_convert_type(x[:, :w].astype(BF16).astype(F32), jnp.uint32)
    hi = lax.bitcast_convert_type(x[:, w:].astype(BF16).astype(F32), jnp.uint32)
    return (lo >> 16) | (hi & jnp.uint32(HI_HALF))


def _unpack_halves(u):
    lo = lax.bitcast_convert_type(u << 16, F32)
    hi = lax.bitcast_convert_type(u & jnp.uint32(HI_HALF), F32)
    return jnp.concatenate([lo, hi], axis=1)


def _ada_kernel(c_ref, w_ref, b_ref, o_ref):
    c = c_ref[...]
    a = c / (1.0 + jnp.exp(-c))
    a_hi, a_lo = _split_bf16(a)
    w_hi, w_lo = _split_bf16(w_ref[0])
    acc = _dot(a_hi, w_hi) + _dot(a_lo, w_hi) + _dot(a_hi, w_lo)
    o_ref[0] = acc + b_ref[0]


def _ada(c_all, w_ada, b_ada):
    depth, d, n = w_ada.shape
    tn = 1536
    return pl.pallas_call(
        _ada_kernel,
        grid=(depth, n // tn),
        in_specs=[
            pl.BlockSpec((8, d), lambda l, j: (0, 0)),
            pl.BlockSpec((1, d, tn), lambda l, j: (l, 0, j)),
            pl.BlockSpec((1, 1, tn), lambda l, j: (l, 0, j)),
        ],
        out_specs=pl.BlockSpec((1, 8, tn), lambda l, j: (l, 0, j)),
        out_shape=jax.ShapeDtypeStruct((depth, 8, n), F32),
        compiler_params=_params(("arbitrary", "arbitrary")),
        name="ada",
    )(c_all, w_ada, b_ada.reshape(depth, 1, n))


def _rope_tile(t, cos, sin_signed, q):
    lane = lax.broadcasted_iota(jnp.int32, t.shape, 1)
    first = (lane & (2 * q - 1)) < q
    up = pltpu.roll(t, LANES - q, 1)
    dn = pltpu.roll(t, q, 1)
    return t * cos + jnp.where(first, up, dn) * sin_signed


def _rope_wide(t, cos, sin_signed, q):
    n = t.shape[1] // LANES
    return jnp.concatenate(
        [_rope_tile(t[:, i * LANES:(i + 1) * LANES], cos, sin_signed, q) for i in range(n)], axis=1)


def _modulate(x_ref, mod_ref, which):
    sh = mod_ref[0, 3 * which:3 * which + 1, :]
    sc = mod_ref[0, 3 * which + 1:3 * which + 2, :]
    return x_ref[...] * (1.0 + sc) + sh


def _proj_ab_kernel(x_ref, mod_ref, w_ref, cos_ref, sin_ref,
                    aq_ref, ak_ref, av_ref, bq_ref, bk_ref, bv_ref, *, rope):
    h = _modulate(x_ref, mod_ref, 0).astype(BF16)
    r = _dot(h, w_ref[...])
    aq = r[:, 0:512]
    ak = r[:, 512:640]
    if rope:
        cos = cos_ref[...]
        sin = sin_ref[...]
        aq = _rope_wide(aq, cos, sin, 16)
        ak = _rope_wide(ak, cos, sin, 16)
    scale = HEAD_DIM ** -0.5
    aq_ref[...] = (aq * scale).astype(BF16)
    ak_ref[...] = ak.astype(BF16)
    av_ref[...] = r[:, 640:768].astype(BF16)
    bq_ref[...] = (r[:, 768:1280] * scale).astype(BF16)
    bk_ref[...] = r[:, 1280:1792].astype(BF16)
    bv_ref[...] = r[:, 1792:2304].astype(BF16)


def _proj_ab(x2d, row0, nrows, mod, gid, w_bf, cos, sin, pos_blocks, rope, tm):
    d = x2d.shape[1]
    n_in = w_bf.shape[1]
    b0 = row0 // tm
    widths = (512, 128, 128, 512, 512, 512)
    return pl.pallas_call(
        functools.partial(_proj_ab_kernel, rope=rope),
        grid=(nrows // tm,),
        in_specs=[
            pl.BlockSpec((tm, d), lambda i: (b0 + i, 0)),
            pl.BlockSpec((1, 6, d), lambda i: (gid(i), 0, 0)),
            pl.BlockSpec((d, n_in), lambda i: (0, 0)),
            pl.BlockSpec((tm, LANES), lambda i: (i % pos_blocks, 0)),
            pl.BlockSpec((tm, LANES), lambda i: (i % pos_blocks, 0)),
        ],
        out_specs=[pl.BlockSpec((tm, w), lambda i: (i, 0)) for w in widths],
        out_shape=[jax.ShapeDtypeStruct((nrows, w), BF16) for w in widths],
        compiler_params=_params(("arbitrary",)),
        name="proj_ab",
    )(x2d, mod, w_bf, cos, sin)


def _softmax_pv(parts, sink_col):
    m = parts[0][0].max(-1, keepdims=True)
    for s, _ in parts[1:]:
        m = jnp.maximum(m, s.max(-1, keepdims=True))
    if sink_col is not None:
        m = jnp.maximum(m, sink_col)
    denom = None if sink_col is None else jnp.exp(sink_col - m)
    o = None
    for s, v in parts:
        e = jnp.exp(s - m)
        d = e.sum(-1, keepdims=True)
        denom = d if denom is None else denom + d
        pv = _dot(e.astype(BF16), v)
        o = pv if o is None else o + pv
    return o / denom


def _win_kernel(sink_ref, q_ref, kp_ref, kc_ref, kn_ref, vp_ref, vc_ref, vn_ref,
                kx_ref, vx_ref, o_ref, *, seq):
    step = pl.program_id(1)
    blk = A_BLOCK
    g_sz = A_HEADS // A_KV_HEADS
    qi = lax.broadcasted_iota(jnp.int32, (blk, 3 * blk), 0)
    kj = lax.broadcasted_iota(jnp.int32, (blk, 3 * blk), 1)
    in_window = jnp.abs(kj - blk - qi) <= A_WINDOW
    k_ext = jnp.concatenate([kp_ref[...], kc_ref[...], kn_ref[...]], axis=0)
    v_ext = jnp.concatenate([vp_ref[...], vc_ref[...], vn_ref[...]], axis=0)
    kx = kx_ref[...]
    vx = vx_ref[...]

    def scores(t, g):
        lo, hi = g * HEAD_DIM, (g + 1) * HEAD_DIM
        kpos = (step * WIN_GROUP + t) * blk + kj - blk
        valid = in_window & (kpos >= 0) & (kpos < seq)
        valid = jnp.concatenate([valid] * g_sz, axis=0)
        q = q_ref[t * blk:(t + 1) * blk, :]
        heads = [g * g_sz + i for i in range(g_sz)]
        qs = jnp.concatenate([q[:, h * HEAD_DIM:(h + 1) * HEAD_DIM] for h in heads], axis=0)
        s_loc = jnp.where(valid, _dot_nt(qs, k_ext[t * blk:(t + 3) * blk, lo:hi]), NEG_INF)
        return s_loc, _dot_nt(qs, kx[:, lo:hi])

    units = [(t, g) for t in range(WIN_GROUP) for g in range(A_KV_HEADS)]
    pending = scores(*units[0])
    outs = []
    for n, (t, g) in enumerate(units):
        s_loc, s_ctx = pending
        if n + 1 < len(units):
            pending = scores(*units[n + 1])
        lo, hi = g * HEAD_DIM, (g + 1) * HEAD_DIM
        sink = jnp.concatenate(
            [jnp.full((blk, 1), sink_ref[g * g_sz + i], F32) for i in range(g_sz)], axis=0)
        o = _softmax_pv([(s_ctx, vx[:, lo:hi]), (s_loc, v_ext[t * blk:(t + 3) * blk, lo:hi])], sink)
        outs.extend(o[i * blk:(i + 1) * blk] for i in range(g_sz))
        if g == A_KV_HEADS - 1:
            o_ref[t * blk:(t + 1) * blk, :] = jnp.concatenate(outs, axis=1).astype(o_ref.dtype)
            outs = []


WIN_GROUP = 16


def _window_attn(sink, aq, ak, av, akx, avx, batch, seq, ctx_len):
    nblk = seq // A_BLOCK
    nstep = nblk // WIN_GROUP
    kvw = A_KV_HEADS * HEAD_DIM
    qw = A_HEADS * HEAD_DIM
    rows = WIN_GROUP * A_BLOCK

    def edge(delta):
        return lambda b, n: (b * nblk + jnp.clip(n * WIN_GROUP + delta, 0, nblk - 1), 0)

    def own(b, n):
        return (b * nstep + n, 0)

    kv_specs = [pl.BlockSpec((A_BLOCK, kvw), edge(-1)), pl.BlockSpec((rows, kvw), own),
                pl.BlockSpec((A_BLOCK, kvw), edge(WIN_GROUP))]
    return pl.pallas_call(
        functools.partial(_win_kernel, seq=seq),
        grid=(batch, nstep),
        in_specs=[pl.BlockSpec(memory_space=pltpu.SMEM), pl.BlockSpec((rows, qw), own)]
        + kv_specs + kv_specs
        + [pl.BlockSpec((ctx_len, kvw), lambda b, n: (b, 0)),
           pl.BlockSpec((ctx_len, kvw), lambda b, n: (b, 0))],
        out_specs=pl.BlockSpec((rows, qw), own),
        out_shape=jax.ShapeDtypeStruct((batch * seq, qw), BF16),
        compiler_params=_params(("arbitrary", "arbitrary")),
        name="window_attn",
    )(sink, aq, ak, ak, ak, av, av, av, akx, avx)


NA_QROWS = 8
NA_Q = NA_QROWS * GRID_W
NA_KBLK = 4 * GRID_W
NA_K = 4 * NA_KBLK


NA_DR = 2 * NA_ROWS - 1


def _na_bias_tiles(rpb, rows):
    nj = rows // NA_QROWS
    kr = min(NA_ROWS, rows)
    col = jnp.arange(GRID_W)
    col_start = jnp.clip(col - NA_COLS // 2, 0, GRID_W - NA_COLS)
    cvalid = (col[None, :] >= col_start[:, None]) & (col[None, :] < col_start[:, None] + NA_COLS)
    dc = jnp.clip(col[None, :] - col[:, None], -(NA_COLS - 1), NA_COLS - 1) + (NA_COLS - 1)
    tiles = jnp.where(cvalid[None, None], rpb.astype(F32)[:, :, dc], NEG_INF)
    tiles = jnp.concatenate([tiles, jnp.full_like(tiles[:, :1], NEG_INF)], axis=1)
    idx = []
    for j in (0, min(1, nj - 1), nj - 1):
        r = j * NA_QROWS + jnp.arange(NA_QROWS)
        krow = j * NA_QROWS - NA_KBLK // GRID_W + jnp.arange(NA_K // GRID_W)
        start = jnp.clip(r - kr // 2, 0, rows - kr)
        rvalid = (krow[None, :] >= start[:, None]) & (krow[None, :] < start[:, None] + kr)
        dr = krow[None, :] - r[:, None] + (NA_ROWS - 1)
        idx.append(jnp.where(rvalid, dr, NA_DR))
    return tiles, jnp.stack(idx).reshape(-1).astype(jnp.int32)


def _na_kernel(idx_ref, q_ref, k0, k1, k2, k3, v0, v1, v2, v3, kx_ref, vx_ref, t_ref, o_ref, bias_scr,
               *, nj):
    j = pl.program_id(2)
    nkr = NA_K // GRID_W

    @pl.when((j == 0) | (j == 1) | (j == nj - 1))
    def _():
        variant = jnp.where(j == 0, 0, jnp.where(j == nj - 1, 2, 1))
        for hh in range(LANES // HEAD_DIM):
            for qr in range(NA_QROWS):
                for kr in range(nkr):
                    d = idx_ref[(variant * NA_QROWS + qr) * nkr + kr]
                    bias_scr[hh, qr * GRID_W:(qr + 1) * GRID_W, kr * GRID_W:(kr + 1) * GRID_W] = (
                        t_ref[hh, d])

    q = q_ref[...]
    k_all = jnp.concatenate([k0[...], k1[...], k2[...], k3[...]], axis=0)
    v_all = jnp.concatenate([v0[...], v1[...], v2[...], v3[...]], axis=0)
    kx = kx_ref[...]
    vx = vx_ref[...]
    qh_rows = NA_Q // 2
    k_span = 3 * NA_KBLK

    def scores(hh, half):
        lo, hi = hh * HEAD_DIM, (hh + 1) * HEAD_DIM
        q0, k0_ = half * qh_rows, half * NA_KBLK
        qh = q[q0:q0 + qh_rows, lo:hi]
        s_loc = _dot_nt(qh, k_all[k0_:k0_ + k_span, lo:hi]) + bias_scr[hh, q0:q0 + qh_rows, k0_:k0_ + k_span]
        return s_loc, _dot_nt(qh, kx[:, lo:hi])

    units = [(hh, half) for half in range(2) for hh in range(LANES // HEAD_DIM)]
    pending = scores(*units[0])
    for n, (hh, half) in enumerate(units):
        s_loc, s_ctx = pending
        if n + 1 < len(units):
            pending = scores(*units[n + 1])
        lo, hi = hh * HEAD_DIM, (hh + 1) * HEAD_DIM
        k0_ = half * NA_KBLK
        o = _softmax_pv([(s_ctx, vx[:, lo:hi]), (s_loc, v_all[k0_:k0_ + k_span, lo:hi])], None)
        o_ref[half * qh_rows:(half + 1) * qh_rows, lo:hi] = o.astype(o_ref.dtype)


def _na_attn(bq, bk, bv, bkx, bvx, tiles, tile_idx, batch, seq, ctx_len):
    nj = seq // NA_Q
    nkb = seq // NA_KBLK
    hp = B_HEADS * HEAD_DIM // LANES
    per = LANES // HEAD_DIM

    def kb(t):
        return lambda p, b, j, idx: (b * nkb + jnp.clip(2 * j - 1 + t, 0, nkb - 1), p)

    kv_specs = [pl.BlockSpec((NA_KBLK, LANES), kb(t)) for t in range(4)]
    grid_spec = pltpu.PrefetchScalarGridSpec(
        num_scalar_prefetch=1,
        grid=(hp, batch, nj),
        in_specs=[pl.BlockSpec((NA_Q, LANES), lambda p, b, j, idx: (b * nj + j, p))]
        + kv_specs + kv_specs
        + [pl.BlockSpec((ctx_len, LANES), lambda p, b, j, idx: (b, p)),
           pl.BlockSpec((ctx_len, LANES), lambda p, b, j, idx: (b, p)),
           pl.BlockSpec((per, NA_DR + 1, GRID_W, GRID_W), lambda p, b, j, idx: (p, 0, 0, 0))],
        out_specs=pl.BlockSpec((NA_Q, LANES), lambda p, b, j, idx: (b * nj + j, p)),
        scratch_shapes=[pltpu.VMEM((per, NA_Q, NA_K), F32)],
    )
    return pl.pallas_call(
        functools.partial(_na_kernel, nj=nj),
        grid_spec=grid_spec,
        out_shape=jax.ShapeDtypeStruct((batch * seq, B_HEADS * HEAD_DIM), BF16),
        compiler_params=_params(("arbitrary", "arbitrary", "arbitrary")),
        name="na_attn",
    )(tile_idx, bq, bk, bk, bk, bk, bv, bv, bv, bv, bkx, bvx, tiles)


def _ctx_ab_kernel(sink_ref, aq_ref, ak_ref, av_ref, bq_ref, bk_ref, bv_ref, oa_ref, ob_ref):
    ctx_len = aq_ref.shape[0]
    g_sz = A_HEADS // A_KV_HEADS
    aq = aq_ref[...]
    ak = ak_ref[...]
    av = av_ref[...]
    outs = []
    for g in range(A_KV_HEADS):
        lo, hi = g * HEAD_DIM, (g + 1) * HEAD_DIM
        heads = [g * g_sz + i for i in range(g_sz)]
        qs = jnp.concatenate([aq[:, h * HEAD_DIM:(h + 1) * HEAD_DIM] for h in heads], axis=0)
        sink = jnp.concatenate(
            [jnp.full((ctx_len, 1), sink_ref[h], F32) for h in heads], axis=0)
        o = _softmax_pv([(_dot_nt(qs, ak[:, lo:hi]), av[:, lo:hi])], sink)
        outs.extend(o[i * ctx_len:(i + 1) * ctx_len] for i in range(g_sz))
    oa_ref[...] = jnp.concatenate(outs, axis=1).astype(oa_ref.dtype)
    bq = bq_ref[...]
    bk = bk_ref[...]
    bv = bv_ref[...]
    outs = []
    for h in range(B_HEADS):
        lo, hi = h * HEAD_DIM, (h + 1) * HEAD_DIM
        outs.append(_softmax_pv([(_dot_nt(bq[:, lo:hi], bk[:, lo:hi]), bv[:, lo:hi])], None))
    ob_ref[...] = jnp.concatenate(outs, axis=1).astype(ob_ref.dtype)


def _ctx_ab_attn(sink, aq, ak, av, bq, bk, bv, batch, ctx_len):
    def spec(w):
        return pl.BlockSpec((ctx_len, w), lambda b: (b, 0))

    return pl.pallas_call(
        _ctx_ab_kernel,
        grid=(batch,),
        in_specs=[pl.BlockSpec(memory_space=pltpu.SMEM)]
        + [spec(a.shape[1]) for a in (aq, ak, av, bq, bk, bv)],
        out_specs=[spec(aq.shape[1]), spec(bq.shape[1])],
        out_shape=[jax.ShapeDtypeStruct(aq.shape, BF16), jax.ShapeDtypeStruct(bq.shape, BF16)],
        compiler_params=_params(("arbitrary",)),
        name="ctx_ab_attn",
    )(sink, aq, ak, av, bq, bk, bv)


def _layer_norm(t, g, b):
    mu = jnp.mean(t, -1, keepdims=True)
    c = t - mu
    var = jnp.mean(c * c, -1, keepdims=True)
    return c * lax.rsqrt(var + 1e-5) * g + b


def _post_attn_kernel(*refs, alpha, lat_tiles, with_ctx, lat_transposed):
    n_src = 6 if with_ctx else 3
    srcs = refs[:n_src]
    (w_ref, mod_ref, lng_ref, lnb_ref, wr_hi_ref, wr_lo_ref, br_ref,
     x1_ref, h2_ref, lg_ref) = refs[n_src:]

    def rows_of(o_ref):
        if o_ref.ndim == 2:
            return o_ref[...]
        return o_ref[0].astype(F32).T.astype(BF16)

    def run(o1_ref, o2_ref, x_ref):
        half = w_ref.shape[0] // 2
        y = _dot(rows_of(o1_ref), w_ref[0:half, :]) + _dot(rows_of(o2_ref), w_ref[half:, :])
        g1 = mod_ref[0, 2:3, :]
        x1 = _layer_norm(alpha * x_ref[...] + g1 * y, lng_ref[...], lnb_ref[...])
        x1_ref[...] = x1
        h2 = x1 * (1.0 + mod_ref[0, 4:5, :]) + mod_ref[0, 3:4, :]
        h_hi, h_lo = _split_bf16(h2)
        h2_ref[...] = _pack_halves(h2)
        both = _dot(h_hi, wr_lo_ref[...])
        lg_ref[...] = (both[:, :ROUTER_PAD] + both[:, ROUTER_PAD:] + _dot(h_lo, wr_hi_ref[...])
                       + br_ref[...])

    if not with_ctx:
        run(*srcs)
    else:
        i = pl.program_id(0)
        pl.when(i < lat_tiles)(lambda: run(*srcs[:3]))
        pl.when(i >= lat_tiles)(lambda: run(*srcs[3:]))


def _post_attn(lat, ctx_src, w_bf, mod, gid, lng, lnb, wr_hi, wr_lo, br, n_lat, alpha, tm):
    half = w_bf.shape[0] // 2
    d = lat[2].shape[1]
    lat_tiles = n_lat // tm
    n_rows = n_lat + (ctx_src[0].shape[0] if ctx_src is not None else 0)
    lat_transposed = lat[0].ndim == 3

    def lat_map(i):
        return (jnp.minimum(i, lat_tiles - 1), 0)

    def ctx_map(i):
        return (jnp.maximum(i - lat_tiles, 0), 0)

    def src_specs(index_map):
        return [pl.BlockSpec((tm, half), index_map), pl.BlockSpec((tm, half), index_map),
                pl.BlockSpec((tm, d), index_map)]

    in_specs = src_specs(lat_map)
    if lat_transposed:
        per_batch = lat[0].shape[2] // tm
        t_spec = pl.BlockSpec((1, half, tm), lambda i: (i // per_batch, 0, i % per_batch))
        in_specs[0] = in_specs[1] = t_spec
    args = list(lat)
    if ctx_src is not None:
        in_specs += src_specs(ctx_map)
        args += list(ctx_src)

    def const(shape):
        return pl.BlockSpec(shape, lambda i: (0,) * len(shape))

    in_specs += [const((2 * half, d)),
                 pl.BlockSpec((1, 6, d), lambda i: (gid(i), 0, 0)),
                 const((1, d)), const((1, d)),
                 const((d, ROUTER_PAD)), const((d, 2 * ROUTER_PAD)), const((1, ROUTER_PAD))]
    args += [w_bf, mod, lng, lnb, wr_hi, wr_lo, br]
    widths = (d, d // 2, ROUTER_PAD)
    dtypes = (F32, jnp.uint32, F32)
    return pl.pallas_call(
        functools.partial(_post_attn_kernel, alpha=alpha, lat_tiles=lat_tiles,
                          with_ctx=ctx_src is not None, lat_transposed=lat_transposed),
        grid=(n_rows // tm,),
        in_specs=in_specs,
        out_specs=[pl.BlockSpec((tm, w), lambda i: (i, 0)) for w in widths],
        out_shape=[jax.ShapeDtypeStruct((n_rows, w), t) for w, t in zip(widths, dtypes)],
        compiler_params=_params(("arbitrary",)),
        name="post_attn",
    )(*args)


def _moe_kernel(be_ref, xs_ref, wgu_ref, wdn_ref, ys_ref, wgu_bf, wdn_bf):
    i = pl.program_id(0)
    used = i < be_ref[pl.num_programs(0)]
    prev = be_ref[jnp.maximum(i - 1, 0)]
    changed = (i == 0) | (be_ref[i] != prev)

    @pl.when(changed & used)
    def _():
        wgu_bf[...] = wgu_ref[0, 0].astype(BF16)
        wdn_bf[...] = wdn_ref[0, 0].astype(BF16)

    @pl.when(used)
    def _():
        de = wdn_bf.shape[0]
        subs = [slice(r, r + MOE_SUB) for r in range(0, MOE_BLOCK, MOE_SUB)]
        gus = [_dot(_unpack_halves(xs_ref[sl, :]).astype(BF16), wgu_bf[...]) for sl in subs]
        for sl, gu in zip(subs, gus):
            gate = gu[:, :de]
            up = gu[:, de:]
            act = gate / (1.0 + jnp.exp(-gate)) * up
            ys_ref[sl, :] = _pack_halves(_dot(act.astype(BF16), wdn_bf[...]))

    @pl.when(jnp.logical_not(used))
    def _():
        ys_ref[...] = jnp.zeros(ys_ref.shape, ys_ref.dtype)


def _moe_experts(block_e, xs, w_gu, w_dn, layer):
    rows, dw = xs.shape
    d = 2 * dw
    nb = rows // MOE_BLOCK
    de = w_dn.shape[2]
    grid_spec = pltpu.PrefetchScalarGridSpec(
        num_scalar_prefetch=1,
        grid=(nb,),
        in_specs=[
            pl.BlockSpec((MOE_BLOCK, dw), lambda i, be: (i, 0)),
            pl.BlockSpec((1, 1, d, 2 * de), lambda i, be: (layer, be[i], 0, 0)),
            pl.BlockSpec((1, 1, de, d), lambda i, be: (layer, be[i], 0, 0)),
        ],
        out_specs=pl.BlockSpec((MOE_BLOCK, dw), lambda i, be: (i, 0)),
        scratch_shapes=[pltpu.VMEM((d, 2 * de), BF16), pltpu.VMEM((de, d), BF16)],
    )
    return pl.pallas_call(
        _moe_kernel,
        grid_spec=grid_spec,
        out_shape=jax.ShapeDtypeStruct((rows, dw), jnp.uint32),
        compiler_params=_params(("arbitrary",)),
        name="moe_experts",
    )(block_e, xs, w_gu, w_dn)


ROUTE_TILE = 512


def _route_kernel(lg_ref, rec_ref, cnt_ref, carry):
    i = pl.program_id(0)

    @pl.when(i == 0)
    def _():
        carry[...] = jnp.zeros(carry.shape, F32)

    lg = lg_ref[...]
    tm = lg.shape[0]
    lane = lax.broadcasted_iota(jnp.int32, lg.shape, 1)

    def first_lane(mask):
        return jnp.min(jnp.where(mask, lane, ROUTER_PAD), axis=1, keepdims=True)

    is_g = lane < N_GROUPS
    g_log = jnp.where(is_g, lg, NEG_INF)
    g_max = jnp.max(g_log, axis=1, keepdims=True)
    g_sum = jnp.sum(jnp.where(is_g, jnp.exp(lg - g_max), 0.0), axis=1, keepdims=True)
    g_val = 1.0 / g_sum
    g_idx = first_lane(g_log == g_max)
    lo = N_GROUPS + EXP_PER_GROUP * g_idx
    sel = (lane >= lo) & (lane < lo + EXP_PER_GROUP)
    e_log = jnp.where(sel, lg, NEG_INF)
    e_max = jnp.max(e_log, axis=1, keepdims=True)
    e_exp = jnp.where(sel, jnp.exp(lg - e_max), 0.0)
    e_prob = e_exp / jnp.sum(e_exp, axis=1, keepdims=True)
    p1 = jnp.where(sel, e_prob, -1.0)
    v1 = jnp.max(p1, axis=1, keepdims=True)
    i1 = first_lane(p1 == v1)
    p2 = jnp.where(lane == i1, -1.0, p1)
    v2 = jnp.max(p2, axis=1, keepdims=True)
    i2 = first_lane(p2 == v2)
    norm = g_val / (v1 + v2)

    hot1 = lane == i1
    hot2 = lane == i2
    hot = jnp.where(hot1 | hot2, 1.0, 0.0)
    rows = lax.broadcasted_iota(jnp.int32, (tm, tm), 0)
    cols = lax.broadcasted_iota(jnp.int32, (tm, tm), 1)
    before = jnp.where(cols < rows, 1.0, 0.0).astype(BF16)
    prefix = _dot(before, hot.astype(BF16)) + carry[0:1, :]
    r1 = jnp.sum(jnp.where(hot1, prefix, 0.0), axis=1, keepdims=True)
    r2 = jnp.sum(jnp.where(hot2, prefix, 0.0), axis=1, keepdims=True)
    carry[0:1, :] = carry[0:1, :] + jnp.sum(hot, axis=0, keepdims=True)

    fields = [(i1 - N_GROUPS).astype(F32), (i2 - N_GROUPS).astype(F32), r1, r2, v1 * norm, v2 * norm]
    rec = jnp.zeros(lg.shape, F32)
    for k, f in enumerate(fields):
        rec = jnp.where(lane == k, f, rec)
    rec_ref[...] = rec

    @pl.when(i == pl.num_programs(0) - 1)
    def _():
        cnt_ref[...] = carry[...]


def _route(logits, n_tok):
    rec, cnt = pl.pallas_call(
        _route_kernel,
        grid=(n_tok // ROUTE_TILE,),
        in_specs=[pl.BlockSpec((ROUTE_TILE, ROUTER_PAD), lambda i: (i, 0))],
        out_specs=[pl.BlockSpec((ROUTE_TILE, ROUTER_PAD), lambda i: (i, 0)),
                   pl.BlockSpec((8, ROUTER_PAD), lambda i: (0, 0))],
        out_shape=[jax.ShapeDtypeStruct((n_tok, ROUTER_PAD), F32),
                   jax.ShapeDtypeStruct((8, ROUTER_PAD), F32)],
        scratch_shapes=[pltpu.VMEM((8, ROUTER_PAD), F32)],
        compiler_params=_params(("arbitrary",)),
        name="route",
    )(logits)
    experts = rec[:, 0:2].astype(jnp.int32)
    rank = rec[:, 2:4].astype(jnp.int32)
    wts = rec[:, 4:6]
    counts = cnt[0, N_GROUPS:N_GROUPS + N_EXPERTS].astype(jnp.int32)

    n_asg = n_tok * TOP_K
    pcounts = ((counts + MOE_BLOCK - 1) // MOE_BLOCK) * MOE_BLOCK
    pends = jnp.cumsum(pcounts)
    pstarts = pends - pcounts
    dest = pstarts[experts] + rank
    nb = -(-n_asg // MOE_BLOCK) + N_EXPERTS
    block_start = jnp.arange(nb, dtype=jnp.int32) * MOE_BLOCK
    block_e = jnp.minimum((pends[None, :] <= block_start[:, None]).sum(1), N_EXPERTS - 1).astype(jnp.int32)
    block_e = jnp.concatenate([block_e, (pends[-1:] // MOE_BLOCK).astype(jnp.int32)])
    tok = jnp.broadcast_to(jnp.arange(n_tok, dtype=jnp.int32)[:, None], (n_tok, TOP_K))
    filler = jnp.arange(nb * MOE_BLOCK, dtype=jnp.int32) % n_tok
    row_tok = filler.at[dest.reshape(-1)].set(tok.reshape(-1), unique_indices=True)
    return wts, dest, row_tok, block_e


def _post_moe_kernel(x_ref, ya_ref, yb_ref, w_ref, mod_ref, lng_ref, lnb_ref, o_ref, *, alpha):
    w = w_ref[...]
    y = w[:, 0:1] * _unpack_halves(ya_ref[...]) + w[:, 1:2] * _unpack_halves(yb_ref[...])
    g2 = mod_ref[0, 5:6, :]
    o_ref[...] = _layer_norm(alpha * x_ref[...] + g2 * y, lng_ref[...], lnb_ref[...])


def _post_moe(x1, ya, yb, wts, mod, gid, lng, lnb, alpha, tm):
    rows, d = x1.shape
    row = pl.BlockSpec((tm, d), lambda i: (i, 0))
    packed = pl.BlockSpec((tm, d // 2), lambda i: (i, 0))
    vec = pl.BlockSpec((1, d), lambda i: (0, 0))
    return pl.pallas_call(
        functools.partial(_post_moe_kernel, alpha=alpha),
        grid=(rows // tm,),
        in_specs=[row, packed, packed,
                  pl.BlockSpec((tm, TOP_K), lambda i: (i, 0)),
                  pl.BlockSpec((1, 6, d), lambda i: (gid(i), 0, 0)),
                  vec, vec],
        out_specs=row,
        out_shape=jax.ShapeDtypeStruct((rows, d), F32),
        compiler_params=_params(("arbitrary",)),
        name="post_moe",
    )(x1, ya, yb, wts, mod, lng, lnb)


CD_COLS = dict(cq=(0, 384), ckv=(384, 640), dq=(640, 1152), dk=(1152, 1664), dv=(1664, 2176),
               kr=(2176, 2304))
C_HEAD_PAD = LANES


def _rms(t, g, eps=1e-6):
    return t * lax.rsqrt(jnp.mean(t * t, -1, keepdims=True) + eps) * g


def _proj_cd_kernel(x_ref, mod_ref, w_ref, qn_ref, wuq_ref, kvn_ref, wukv_ref,
                    cos64_ref, sin64_ref, cos32_ref, sin32_ref,
                    qc_ref, kc_ref, vct_ref, dq_ref, dk_ref, dvt_ref):
    h = _modulate(x_ref, mod_ref, 0).astype(BF16)
    r = _dot(h, w_ref[...])

    def cols(name):
        lo, hi = CD_COLS[name]
        return r[:, lo:hi]

    q = _dot(_rms(cols('cq'), qn_ref[...]).astype(BF16), wuq_ref[...])
    kv = _dot(_rms(cols('ckv'), kvn_ref[...]).astype(BF16), wukv_ref[...])
    c32, s32 = cos32_ref[...], sin32_ref[...]
    c64, s64 = cos64_ref[...], sin64_ref[...]
    q = _rope_wide(q, c32, s32, C_ROPE // 4)
    kr = _rope_tile(cols('kr'), c32, s32, C_ROPE // 4)
    dq = _rope_wide(cols('dq'), c64, s64, D_QK // 4)
    dk = _rope_wide(cols('dk'), c64, s64, D_QK // 4)
    kw = C_HEADS * C_HEAD_PAD
    qc_ref[0] = (q * ((C_NOPE + C_ROPE) ** -0.5 * LOG2E)).astype(BF16)
    kc_ref[0] = (kv[:, :kw] + jnp.concatenate([kr] * C_HEADS, axis=1)).astype(BF16)
    dq_ref[0] = (dq * (D_QK ** -0.5 * LOG2E)).astype(BF16)
    dk_ref[0] = dk.astype(BF16)
    _store_vt(vct_ref, kv[:, kw:], C_HEADS, C_V)
    _store_vt(dvt_ref, cols('dv'), D_HEADS, D_V)


def _store_vt(vt_ref, v, heads, dv):
    tm = v.shape[0]
    v_t = v.T.astype(BF16)
    row = lax.broadcasted_iota(jnp.int32, (ONES_ROWS, tm), 0)
    extra = jnp.where(row == 0, 1.0, 0.0).astype(BF16)
    for h in range(heads):
        vt_ref[0, h, 0, 0:dv, :] = v_t[h * dv:(h + 1) * dv]
        vt_ref[0, h, 0, dv:dv + ONES_ROWS, :] = extra


def _proj_cd(xall, mod, w_bf, qn, wuq, kvn, wukv, tabs, batch, seq, ctx_len, tm, tk):
    d = xall.shape[1]
    lat_t, ctx_t = seq // tm, ctx_len // tm
    n_keys = seq + ctx_len
    per_chunk = tk // tm
    n_chunks = n_keys // tk
    widths = (C_HEADS * C_HEAD_PAD, C_HEADS * C_HEAD_PAD, 2 * D_HEADS * D_QK, 2 * D_HEADS * D_QK)

    def src(b, t):
        return (jnp.where(t < lat_t, b * lat_t + t, batch * lat_t + b * ctx_t + t - lat_t), 0)

    def full(a):
        return pl.BlockSpec(a.shape, lambda b, t: (0,) * a.ndim)

    def row_spec(w):
        return pl.BlockSpec((1, tm, w), lambda b, t: (b, t, 0))

    def row_shape(w):
        return jax.ShapeDtypeStruct((batch, n_keys, w), BF16)

    def vt_spec(heads, dv):
        return pl.BlockSpec((1, heads, 1, dv + ONES_ROWS, tm),
                            lambda b, t: (b, 0, t // per_chunk, 0, t % per_chunk))

    def vt_shape(heads, dv):
        return jax.ShapeDtypeStruct((batch, heads, n_chunks, dv + ONES_ROWS, tk), BF16)

    tab = pl.BlockSpec((tm, LANES), lambda b, t: (t, 0))
    return pl.pallas_call(
        _proj_cd_kernel,
        grid=(batch, lat_t + ctx_t),
        in_specs=[pl.BlockSpec((tm, d), src),
                  pl.BlockSpec((1, 6, d), lambda b, t: (jnp.where(t < lat_t, b, batch), 0, 0)),
                  full(w_bf), full(qn), full(wuq), full(kvn), full(wukv), tab, tab, tab, tab],
        out_specs=[row_spec(widths[0]), row_spec(widths[1]), vt_spec(C_HEADS, C_V),
                   row_spec(widths[2]), row_spec(widths[3]), vt_spec(D_HEADS, D_V)],
        out_shape=[row_shape(widths[0]), row_shape(widths[1]), vt_shape(C_HEADS, C_V),
                   row_shape(widths[2]), row_shape(widths[3]), vt_shape(D_HEADS, D_V)],
        compiler_params=_params(("arbitrary", "arbitrary")),
        name="proj_cd",
    )(xall, mod, w_bf, qn, wuq, kvn, wukv, *tabs)


DENSE_SHIFT = 80.0
DENSE_L_MIN = 2.0 ** -60
DENSE_L_MAX = 2.0 ** 120
DENSE_TQ = 1024
DENSE_TK_MAX = 1280
DENSE_UNROLL_MAX = 13
MXU_DEPTH = 256
PROJ_CD_TILE = 256
ONES_ROWS = 16


def _sq_norm_row(x):
    ones = jnp.ones((8, x.shape[1]), BF16)
    return _dot_nt(ones, (x * x).astype(BF16))[0:1] * 1.02


def _key_norm_max(kchunk, n_chunks, masks):
    def body(j, mx):
        kf = kchunk(j).astype(F32)
        ksq = kf * kf
        out = []
        for msk, cur in zip(masks, mx):
            part = ksq if msk is None else jnp.where(msk, ksq, 0.0)
            rn = jnp.sum(part, axis=1, keepdims=True)
            out.append(jnp.maximum(cur, jnp.max(rn, axis=0, keepdims=True)))
        return tuple(out)

    mx = lax.fori_loop(0, n_chunks, body, tuple(jnp.zeros((1, 1), F32) for _ in masks))
    return [jnp.sqrt(v) * 1.01 for v in mx]


def _safe_online(q, kchunk, vt_chunk, n_chunks, m_scr, l_scr, acc):
    m_scr[...] = jnp.full(m_scr.shape, NEG_INF, F32)
    l_scr[...] = jnp.zeros(l_scr.shape, F32)
    acc[...] = jnp.zeros(acc.shape, F32)

    def body(j, carry):
        s = _dot_nt(kchunk(j), q)
        m_prev = m_scr[...]
        m_new = jnp.maximum(m_prev, s.max(0, keepdims=True))
        alpha = jnp.exp2(m_prev - m_new)
        p = jnp.exp2(s - m_new)
        l_scr[...] = alpha * l_scr[...] + p.sum(0, keepdims=True)
        acc[...] = alpha * acc[...] + _dot(vt_chunk(j), p.astype(BF16))
        m_scr[...] = m_new
        return carry

    lax.fori_loop(0, n_chunks, body, 0)


def _denominators_ok(*ls):
    ok = None
    for l in ls:
        cur = (l > DENSE_L_MIN) & (l < DENSE_L_MAX)
        ok = cur if ok is None else ok & cur
    return jnp.max(jnp.where(ok, 0.0, 1.0)) == 0.0


MLA_HEADS_PER_STEP = 2


def _mla_kernel(q_ref, k_ref, vt_ref, o_ref, kmax_scr, acc_scr, m_scr, l_scr,
                *, n_chunks, tk, unroll, depth):
    hp = acc_scr.shape[0]

    def kchunk(j, h):
        return k_ref[0, pl.ds(pl.multiple_of(j * tk, tk), tk), h * C_HEAD_PAD:(h + 1) * C_HEAD_PAD]

    @pl.when(pl.program_id(2) == 0)
    def _():
        for h in range(hp):
            (kmax,) = _key_norm_max(lambda j: kchunk(j, h), n_chunks, [None])
            kmax_scr[h:h + 1, :] = jnp.broadcast_to(kmax, (1, LANES))

    qs = [q_ref[0, :, h * C_HEAD_PAD:(h + 1) * C_HEAD_PAD] for h in range(hp)]
    ms = [jnp.sqrt(_sq_norm_row(qs[h].astype(F32))) * kmax_scr[h:h + 1, 0:1] - DENSE_SHIFT
          for h in range(hp)]
    acc_scr[...] = jnp.zeros(acc_scr.shape, F32)

    def scores(j):
        return [_dot_nt(kchunk(j, h), qs[h]) for h in range(hp)]

    def body(it, carry):
        j0 = it * unroll
        pend = [scores(j0 + u) for u in range(min(depth, unroll))]
        accs = [acc_scr[h] for h in range(hp)]
        for u in range(unroll):
            ss = pend.pop(0)
            if u + depth < unroll:
                pend.append(scores(j0 + u + depth))
            for h in range(hp):
                accs[h] = accs[h] + _dot(vt_ref[0, h, j0 + u], jnp.exp2(ss[h] - ms[h]).astype(BF16))
        for h in range(hp):
            acc_scr[h] = accs[h]
        return carry

    lax.fori_loop(0, n_chunks // unroll, body, 0)
    ls = [acc_scr[h, C_V:C_V + 1, :] for h in range(hp)]
    for h in range(hp):
        o_ref[0, h * C_V:(h + 1) * C_V, :] = (acc_scr[h, 0:C_V, :] / ls[h]).astype(o_ref.dtype)

    @pl.when(jnp.logical_not(_denominators_ok(*ls)))
    def _():
        for h in range(hp):
            _safe_online(qs[h], lambda j: kchunk(j, h), lambda j: vt_ref[0, h, j], n_chunks,
                         m_scr, l_scr, acc_scr.at[h])
            o_ref[0, h * C_V:(h + 1) * C_V, :] = (acc_scr[h, 0:C_V, :] / l_scr[...]).astype(o_ref.dtype)


def _mla_attn(qc, kc, vt, batch, seq, tq, tk, unroll, depth):
    n_keys = kc.shape[1]
    nq, n_chunks = seq // tq, n_keys // tk
    rows = C_V + ONES_ROWS
    hp = MLA_HEADS_PER_STEP
    return pl.pallas_call(
        functools.partial(_mla_kernel, n_chunks=n_chunks, tk=tk, unroll=unroll, depth=depth),
        grid=(batch, C_HEADS // hp, nq),
        in_specs=[
            pl.BlockSpec((1, tq, hp * C_HEAD_PAD), lambda b, h, i: (b, i, h)),
            pl.BlockSpec((1, n_keys, hp * C_HEAD_PAD), lambda b, h, i: (b, 0, h)),
            pl.BlockSpec((1, hp, n_chunks, rows, tk), lambda b, h, i: (b, h, 0, 0, 0)),
        ],
        out_specs=pl.BlockSpec((1, hp * C_V, tq), lambda b, h, i: (b, h, i)),
        out_shape=jax.ShapeDtypeStruct((batch, C_HEADS * C_V, seq), BF16),
        scratch_shapes=[pltpu.VMEM((8, LANES), F32), pltpu.VMEM((hp, rows, tq), F32),
                        pltpu.VMEM((1, tq), F32), pltpu.VMEM((1, tq), F32)],
        compiler_params=_params(("arbitrary", "arbitrary", "arbitrary")),
        name="mla_attn",
    )(qc, kc, vt)


def _diff_kernel(lam_ref, q_ref, k_ref, vt_ref, subln_ref, o_ref, kmax_scr, acc1_scr, acc2_scr,
                 m_scr, l_scr, *, n_chunks, tk, unroll, depth, out_scale):
    def kchunk(j):
        return k_ref[0, pl.ds(pl.multiple_of(j * tk, tk), tk), :]

    @pl.when(pl.program_id(2) == 0)
    def _():
        klane = lax.broadcasted_iota(jnp.int32, (tk, 2 * D_QK), 1)
        k1, k2 = _key_norm_max(kchunk, n_chunks, [klane < D_QK, klane >= D_QK])
        kmax_scr[0:1, :] = jnp.broadcast_to(k1, (1, LANES))
        kmax_scr[1:2, :] = jnp.broadcast_to(k2, (1, LANES))

    q = q_ref[0]
    qlane = lax.broadcasted_iota(jnp.int32, q.shape, 1)
    zero = jnp.zeros_like(q)
    q1 = jnp.where(qlane < D_QK, q, zero)
    q2 = jnp.where(qlane >= D_QK, q, zero)
    m1 = jnp.sqrt(_sq_norm_row(q1.astype(F32))) * kmax_scr[0:1, 0:1] - DENSE_SHIFT
    m2 = jnp.sqrt(_sq_norm_row(q2.astype(F32))) * kmax_scr[1:2, 0:1] - DENSE_SHIFT
    acc1_scr[...] = jnp.zeros(acc1_scr.shape, F32)
    acc2_scr[...] = jnp.zeros(acc2_scr.shape, F32)

    def scores(j):
        k = kchunk(j)
        return _dot_nt(k, q1), _dot_nt(k, q2)

    def body(it, carry):
        j0 = it * unroll
        pend = [scores(j0 + u) for u in range(min(depth, unroll))]
        a1 = acc1_scr[...]
        a2 = acc2_scr[...]
        for u in range(unroll):
            s1, s2 = pend.pop(0)
            if u + depth < unroll:
                pend.append(scores(j0 + u + depth))
            vt = vt_ref[0, 0, j0 + u]
            a1 = a1 + _dot(vt, jnp.exp2(s1 - m1).astype(BF16))
            a2 = a2 + _dot(vt, jnp.exp2(s2 - m2).astype(BF16))
        acc1_scr[...] = a1
        acc2_scr[...] = a2
        return carry

    lax.fori_loop(0, n_chunks // unroll, body, 0)

    def finish(o1, o2):
        o = o1 - lam_ref[0] * o2
        o = o * lax.rsqrt(jnp.mean(o * o, 0, keepdims=True) + 1e-6) * subln_ref[...] * out_scale
        o_ref[0] = o.astype(o_ref.dtype)

    l1 = acc1_scr[D_V:D_V + 1, :]
    l2 = acc2_scr[D_V:D_V + 1, :]
    finish(acc1_scr[0:D_V, :] / l1, acc2_scr[0:D_V, :] / l2)

    @pl.when(jnp.logical_not(_denominators_ok(l1, l2)))
    def _():
        def vt_chunk(j):
            return vt_ref[0, 0, j]

        _safe_online(q1, kchunk, vt_chunk, n_chunks, m_scr, l_scr, acc1_scr)
        o1 = acc1_scr[0:D_V, :] / l_scr[...]
        _safe_online(q2, kchunk, vt_chunk, n_chunks, m_scr, l_scr, acc2_scr)
        finish(o1, acc2_scr[0:D_V, :] / l_scr[...])


def _diff_attn(lam, dq, dk, vt, subln_col, out_scale, batch, seq, tq, tk, unroll, depth):
    n_keys = dk.shape[1]
    nq, n_chunks = seq // tq, n_keys // tk
    rows = D_V + ONES_ROWS
    pair = 2 * D_QK
    return pl.pallas_call(
        functools.partial(_diff_kernel, n_chunks=n_chunks, tk=tk, unroll=unroll, depth=depth,
                          out_scale=out_scale),
        grid=(batch, D_HEADS, nq),
        in_specs=[
            pl.BlockSpec(memory_space=pltpu.SMEM),
            pl.BlockSpec((1, tq, pair), lambda b, h, i: (b, i, h)),
            pl.BlockSpec((1, n_keys, pair), lambda b, h, i: (b, 0, h)),
            pl.BlockSpec((1, 1, n_chunks, rows, tk), lambda b, h, i: (b, h, 0, 0, 0)),
            pl.BlockSpec((D_V, 1), lambda b, h, i: (0, 0)),
        ],
        out_specs=pl.BlockSpec((1, D_V, tq), lambda b, h, i: (b, h, i)),
        out_shape=jax.ShapeDtypeStruct((batch, D_HEADS * D_V, seq), BF16),
        scratch_shapes=[pltpu.VMEM((8, LANES), F32), pltpu.VMEM((rows, tq), F32),
                        pltpu.VMEM((rows, tq), F32), pltpu.VMEM((1, tq), F32), pltpu.VMEM((1, tq), F32)],
        compiler_params=_params(("arbitrary", "arbitrary", "arbitrary")),
        name="diff_attn",
    )(lam, dq, dk, vt, subln_col)


def _rope_tables(n_tok, dim, lane_lo):
    t = jnp.arange(n_tok)
    pos_r = (t // GRID_W).astype(F32)
    pos_c = (t % GRID_W).astype(F32)
    quarter = dim // 4
    inv = ROPE_BASE ** (-jnp.arange(quarter, dtype=F32) / quarter)
    ang_r = pos_r[:, None] * inv
    ang_c = pos_c[:, None] * inv
    ang = jnp.concatenate([ang_r, ang_r, ang_c, ang_c], -1)
    sign = jnp.tile(jnp.concatenate([-jnp.ones(quarter), jnp.ones(quarter)]), 2).astype(F32)
    cos, sin = jnp.cos(ang), jnp.sin(ang) * sign
    reps = (LANES - lane_lo) // dim
    cos = jnp.concatenate([jnp.ones((n_tok, lane_lo), F32)] + [cos] * reps, axis=1)
    sin = jnp.concatenate([jnp.zeros((n_tok, lane_lo), F32)] + [sin] * reps, axis=1)
    return cos, sin


def _cd_weights(w_in, w_uq, w_ukv):
    d = w_in.shape[0]
    s0 = C_Q_RANK
    s1 = s0 + C_KV_RANK
    s2 = s1 + C_ROPE
    s3 = s2 + 2 * D_HEADS * D_QK
    s4 = s3 + 2 * D_HEADS * D_QK
    kr = jnp.zeros((d, LANES), F32).at[:, C_NOPE:C_NOPE + C_ROPE].set(w_in[:, s1:s2])
    w_in_p = jnp.concatenate([w_in[:, :s1], w_in[:, s2:s3], w_in[:, s3:s4], w_in[:, s4:], kr], axis=1)
    dqk = C_NOPE + C_ROPE
    wq = w_uq.reshape(C_Q_RANK, C_HEADS, dqk)
    wq = jnp.pad(wq, ((0, 0), (0, 0), (0, C_HEAD_PAD - dqk))).reshape(C_Q_RANK, C_HEADS * C_HEAD_PAD)
    wkv = w_ukv.reshape(C_KV_RANK, C_HEADS, C_NOPE + C_V)
    wk = jnp.pad(wkv[:, :, :C_NOPE], ((0, 0), (0, 0), (0, C_HEAD_PAD - C_NOPE)))
    wk = wk.reshape(C_KV_RANK, C_HEADS * C_HEAD_PAD)
    wv = wkv[:, :, C_NOPE:].reshape(C_KV_RANK, C_HEADS * C_V)
    return w_in_p.astype(BF16), wq.astype(BF16), jnp.concatenate([wk, wv], axis=1).astype(BF16)


def _moe_layer(h2, logits, x1, w_gu, w_dn, layer, mod, gid, lng, lnb, alpha, tm):
    n_tok = h2.shape[0]
    wts, dest, row_tok, block_e = _route(logits, n_tok)
    xs = jnp.take(h2, row_tok, axis=0, mode='clip')
    ys = _moe_experts(block_e, xs, w_gu, w_dn, layer)
    ya = jnp.take(ys, dest[:, 0], axis=0, mode='clip')
    yb = jnp.take(ys, dest[:, 1], axis=0, mode='clip')
    return _post_moe(x1, ya, yb, wts.astype(F32), mod, gid, lng, lnb, alpha, tm)


def kernel(x, c, ctx, c_ctx, w_ada, b_ada, ln_g, ln_b, ab_w_in, a_sink, b_rpb, ab_w_out, cd_w_in, c_q_norm, c_w_uq, c_kv_norm, c_w_ukv, d_lambda, d_subln, cd_w_out, w_group, b_group, w_exp_router, b_exp_router, w_gate_up, w_down):
    batch, seq, d = x.shape
    ctx_len = ctx.shape[1]
    depth = w_ada.shape[0]
    alpha = (2 * depth) ** 0.25
    n_lat = batch * seq
    n_ctx = batch * ctx_len
    tm = 512
    tq = DENSE_TQ
    assert depth == 2 and seq % NA_Q == 0 and seq % tq == 0 and n_ctx % tm == 0
    assert seq % PROJ_CD_TILE == 0 and ctx_len % PROJ_CD_TILE == 0
    assert seq % (WIN_GROUP * A_BLOCK) == 0 and seq // NA_Q >= 2
    key_tiles = (seq + ctx_len) // MXU_DEPTH
    tk = MXU_DEPTH * max(u for u in range(1, DENSE_TK_MAX // MXU_DEPTH + 1) if key_tiles % u == 0)
    n_chunks = (seq + ctx_len) // tk

    def unroll_for(cap):
        return max(u for u in range(1, cap + 1) if n_chunks % u == 0)
    lat_tiles = seq // tm

    def gid_lat(i):
        return i // lat_tiles

    def gid_ctx(i):
        return batch

    def gid_all(i):
        return jnp.minimum(i // lat_tiles, batch)

    c_all = jnp.zeros((8, d), F32).at[:batch].set(c).at[batch].set(c_ctx)
    mod = _ada(c_all, w_ada, b_ada).reshape(depth, 8, 6, d)

    cos64, sin64 = _rope_tables(seq, HEAD_DIM, 0)
    cos32, sin32 = _rope_tables(seq, C_ROPE, C_NOPE)

    def router_weights(l):
        wr = jnp.zeros((d, ROUTER_PAD), F32)
        wr = wr.at[:, :N_GROUPS].set(w_group[l]).at[:, N_GROUPS:N_GROUPS + N_EXPERTS].set(w_exp_router[l])
        br = jnp.zeros((1, ROUTER_PAD), F32)
        br = br.at[0, :N_GROUPS].set(b_group[l]).at[0, N_GROUPS:N_GROUPS + N_EXPERTS].set(b_exp_router[l])
        hi = wr.astype(BF16)
        lo = (wr - hi.astype(F32)).astype(BF16)
        return hi, jnp.concatenate([hi, lo], axis=1), br

    x2d = x.reshape(n_lat, d)
    ctx2d = ctx.reshape(n_ctx, d)

    l = 0
    w_in_bf = ab_w_in[0].astype(BF16)
    aq, ak, av, bq, bk, bv = _proj_ab(x2d, 0, n_lat, mod[l], gid_lat, w_in_bf, cos64, sin64,
                                      lat_tiles, True, tm)
    aqx, akx, avx, bqx, bkx, bvx = _proj_ab(ctx2d, 0, n_ctx, mod[l], gid_ctx, w_in_bf, cos64, sin64,
                                            1, False, tm)
    sink = a_sink[0].astype(F32)
    oa = _window_attn(sink, aq, ak, av, akx, avx, batch, seq, ctx_len)
    tiles, tile_idx = _na_bias_tiles(b_rpb[0], seq // GRID_W)
    ob = _na_attn(bq, bk, bv, bkx, bvx, tiles, tile_idx, batch, seq, ctx_len)
    oax, obx = _ctx_ab_attn(sink, aqx, akx, avx, bqx, bkx, bvx, batch, ctx_len)

    n_all = n_lat + n_ctx
    w_out_bf = ab_w_out[0].astype(BF16)
    lng, lnb = ln_g[l, 0][None], ln_b[l, 0][None]
    wr_hi, wr_lo, br = router_weights(l)
    x1, h2, logits = _post_attn((oa, ob, x2d), (oax, obx, ctx2d), w_out_bf, mod[l], gid_all,
                                lng, lnb, wr_hi, wr_lo, br, n_lat, alpha, tm)
    xall = _moe_layer(h2, logits, x1, w_gate_up, w_down, l, mod[l], gid_all,
                      ln_g[l, 1][None], ln_b[l, 1][None], alpha, tm)

    l = 1
    lam_init = 0.8 - 0.6 * math.exp(-0.3 * l)
    lp = d_lambda[0].astype(F32)
    lam = (jnp.exp(jnp.sum(lp[0] * lp[1])) - jnp.exp(jnp.sum(lp[2] * lp[3])) + lam_init).reshape(1)
    w_in_p, wuq_p, wukv_p = _cd_weights(cd_w_in[0], c_w_uq[0], c_w_ukv[0])
    qn, kvn = c_q_norm[0][None].astype(F32), c_kv_norm[0][None].astype(F32)
    def with_ctx_identity(cos, sin):
        return (jnp.concatenate([cos, jnp.ones((ctx_len, LANES), F32)]),
                jnp.concatenate([sin, jnp.zeros((ctx_len, LANES), F32)]))

    tabs = with_ctx_identity(cos64, sin64) + with_ctx_identity(cos32, sin32)
    qc, kc, vct, dq, dk, dvt = _proj_cd(xall, mod[l], w_in_p, qn, wuq_p, kvn, wukv_p, tabs,
                                        batch, seq, ctx_len, PROJ_CD_TILE, tk)
    unroll = unroll_for(DENSE_UNROLL_MAX)
    oc_t = _mla_attn(qc, kc, vct, batch, seq, tq, tk, unroll, 1)
    od_t = _diff_attn(lam, dq, dk, dvt, d_subln[0].astype(F32).reshape(D_V, 1), 1.0 - lam_init,
                      batch, seq, tq, tk, unroll, 1)
    wr_hi, wr_lo, br = router_weights(l)
    x1, h2, logits = _post_attn((oc_t, od_t, xall), None, cd_w_out[0].astype(BF16), mod[l], gid_lat,
                                ln_g[l, 0][None], ln_b[l, 0][None], wr_hi, wr_lo, br,
                                n_lat, alpha, tm)
    out = _moe_layer(h2, logits, x1, w_gate_up, w_down, l, mod[l], gid_lat,
                     ln_g[l, 1][None], ln_b[l, 1][None], alpha, tm)
    return out.reshape(batch, seq, d)
```

```python
import functools
import math

import jax
import jax.numpy as jnp
from jax import lax
from jax.experimental import pallas as pl
from jax.experimental.pallas import tpu as pltpu

F32 = jnp.float32
BF16 = jnp.bfloat16

GRID_W = 64
HEAD_DIM = 64
ROPE_BASE = 10000.0
NEG_INF = -1e30
LOG2E = 1.4426950408889634

A_HEADS = 8
A_KV_HEADS = 2
A_WINDOW = 128
A_BLOCK = 128
B_HEADS = 8
NA_ROWS = 8
NA_COLS = 16
C_HEADS = 8
C_Q_RANK = 384
C_KV_RANK = 256
C_NOPE = 64
C_ROPE = 32
C_V = 64
D_HEADS = 4
D_QK = 64
D_V = 128
N_GROUPS = 4
EXP_PER_GROUP = 8
N_EXPERTS = N_GROUPS * EXP_PER_GROUP
TOP_K = 2
MOE_BLOCK = 512
MOE_SUB = 256
ROUTER_PAD = 128

LANES = 128
VMEM_LIMIT = 56 * 1024 * 1024


def _params(sem):
    return pltpu.CompilerParams(dimension_semantics=sem, vmem_limit_bytes=VMEM_LIMIT)


def _dot(a, b):
    return jnp.dot(a, b, preferred_element_type=F32)


def _dot_nt(a, b):
    return lax.dot_general(a, b, (((1,), (1,)), ((), ())), preferred_element_type=F32)


def _split_bf16(x):
    hi = x.astype(BF16)
    lo = (x - hi.astype(F32)).astype(BF16)
    return hi, lo


HI_HALF = 0xFFFF0000


def _pack_halves(x):
    w = x.shape[1] // 2
    lo = lax.bitcast_convert_type(x[:, :w].astype(BF16).astype(F32), jnp.uint32)
    hi = lax.bitcast_convert_type(x[:, w:].astype(BF16).astype(F32), jnp.uint32)
    return (lo >> 16) | (hi & jnp.uint32(HI_HALF))


def _unpack_halves(u):
    lo = lax.bitcast_convert_type(u << 16, F32)
    hi = lax.bitcast_convert_type(u & jnp.uint32(HI_HALF), F32)
    return jnp.concatenate([lo, hi], axis=1)


def _ada_kernel(c_ref, w_ref, b_ref, o_ref):
    c = c_ref[...]
    a = c / (1.0 + jnp.exp(-c))
    a_hi, a_lo = _split_bf16(a)
    w_hi, w_lo = _split_bf16(w_ref[0])
    acc = _dot(a_hi, w_hi) + _dot(a_lo, w_hi) + _dot(a_hi, w_lo)
    o_ref[0] = acc + b_ref[0]


def _ada(c_all, w_ada, b_ada):
    depth, d, n = w_ada.shape
    tn = 1536
    return pl.pallas_call(
        _ada_kernel,
        grid=(depth, n // tn),
        in_specs=[
            pl.BlockSpec((8, d), lambda l, j: (0, 0)),
            pl.BlockSpec((1, d, tn), lambda l, j: (l, 0, j)),
            pl.BlockSpec((1, 1, tn), lambda l, j: (l, 0, j)),
        ],
        out_specs=pl.BlockSpec((1, 8, tn), lambda l, j: (l, 0, j)),
        out_shape=jax.ShapeDtypeStruct((depth, 8, n), F32),
        compiler_params=_params(("arbitrary", "arbitrary")),
        name="ada",
    )(c_all, w_ada, b_ada.reshape(depth, 1, n))


def _rope_tile(t, cos, sin_signed, q):
    lane = lax.broadcasted_iota(jnp.int32, t.shape, 1)
    first = (lane & (2 * q - 1)) < q
    up = pltpu.roll(t, LANES - q, 1)
    dn = pltpu.roll(t, q, 1)
    return t * cos + jnp.where(first, up, dn) * sin_signed


def _rope_wide(t, cos, sin_signed, q):
    n = t.shape[1] // LANES
    return jnp.concatenate(
        [_rope_tile(t[:, i * LANES:(i + 1) * LANES], cos, sin_signed, q) for i in range(n)], axis=1)


def _modulate(x_ref, mod_ref, which):
    sh = mod_ref[0, 3 * which:3 * which + 1, :]
    sc = mod_ref[0, 3 * which + 1:3 * which + 2, :]
    return x_ref[...] * (1.0 + sc) + sh


def _proj_ab_kernel(x_ref, mod_ref, w_ref, cos_ref, sin_ref,
                    aq_ref, ak_ref, av_ref, bq_ref, bk_ref, bv_ref, *, rope):
    h = _modulate(x_ref, mod_ref, 0).astype(BF16)
    r = _dot(h, w_ref[...])
    aq = r[:, 0:512]
    ak = r[:, 512:640]
    if rope:
        cos = cos_ref[...]
        sin = sin_ref[...]
        aq = _rope_wide(aq, cos, sin, 16)
        ak = _rope_wide(ak, cos, sin, 16)
    scale = HEAD_DIM ** -0.5
    aq_ref[...] = (aq * scale).astype(BF16)
    ak_ref[...] = ak.astype(BF16)
    av_ref[...] = r[:, 640:768].astype(BF16)
    bq_ref[...] = (r[:, 768:1280] * scale).astype(BF16)
    bk_ref[...] = r[:, 1280:1792].astype(BF16)
    bv_ref[...] = r[:, 1792:2304].astype(BF16)


def _proj_ab(x2d, row0, nrows, mod, gid, w_bf, cos, sin, pos_blocks, rope, tm):
    d = x2d.shape[1]
    n_in = w_bf.shape[1]
    b0 = row0 // tm
    widths = (512, 128, 128, 512, 512, 512)
    return pl.pallas_call(
        functools.partial(_proj_ab_kernel, rope=rope),
        grid=(nrows // tm,),
        in_specs=[
            pl.BlockSpec((tm, d), lambda i: (b0 + i, 0)),
            pl.BlockSpec((1, 6, d), lambda i: (gid(i), 0, 0)),
            pl.BlockSpec((d, n_in), lambda i: (0, 0)),
            pl.BlockSpec((tm, LANES), lambda i: (i % pos_blocks, 0)),
            pl.BlockSpec((tm, LANES), lambda i: (i % pos_blocks, 0)),
        ],
        out_specs=[pl.BlockSpec((tm, w), lambda i: (i, 0)) for w in widths],
        out_shape=[jax.ShapeDtypeStruct((nrows, w), BF16) for w in widths],
        compiler_params=_params(("arbitrary",)),
        name="proj_ab",
    )(x2d, mod, w_bf, cos, sin)


def _softmax_pv(parts, sink_col):
    m = parts[0][0].max(-1, keepdims=True)
    for s, _ in parts[1:]:
        m = jnp.maximum(m, s.max(-1, keepdims=True))
    if sink_col is not None:
        m = jnp.maximum(m, sink_col)
    denom = None if sink_col is None else jnp.exp(sink_col - m)
    o = None
    for s, v in parts:
        e = jnp.exp(s - m)
        d = e.sum(-1, keepdims=True)
        denom = d if denom is None else denom + d
        pv = _dot(e.astype(BF16), v)
        o = pv if o is None else o + pv
    return o / denom


def _win_kernel(sink_ref, q_ref, kp_ref, kc_ref, kn_ref, vp_ref, vc_ref, vn_ref,
                kx_ref, vx_ref, o_ref, *, seq):
    step = pl.program_id(1)
    blk = A_BLOCK
    g_sz = A_HEADS // A_KV_HEADS
    qi = lax.broadcasted_iota(jnp.int32, (blk, 3 * blk), 0)
    kj = lax.broadcasted_iota(jnp.int32, (blk, 3 * blk), 1)
    in_window = jnp.abs(kj - blk - qi) <= A_WINDOW
    k_ext = jnp.concatenate([kp_ref[...], kc_ref[...], kn_ref[...]], axis=0)
    v_ext = jnp.concatenate([vp_ref[...], vc_ref[...], vn_ref[...]], axis=0)
    kx = kx_ref[...]
    vx = vx_ref[...]

    def scores(t, g):
        lo, hi = g * HEAD_DIM, (g + 1) * HEAD_DIM
        kpos = (step * WIN_GROUP + t) * blk + kj - blk
        valid = in_window & (kpos >= 0) & (kpos < seq)
        valid = jnp.concatenate([valid] * g_sz, axis=0)
        q = q_ref[t * blk:(t + 1) * blk, :]
        heads = [g * g_sz + i for i in range(g_sz)]
        qs = jnp.concatenate([q[:, h * HEAD_DIM:(h + 1) * HEAD_DIM] for h in heads], axis=0)
        s_loc = jnp.where(valid, _dot_nt(qs, k_ext[t * blk:(t + 3) * blk, lo:hi]), NEG_INF)
        return s_loc, _dot_nt(qs, kx[:, lo:hi])

    units = [(t, g) for t in range(WIN_GROUP) for g in range(A_KV_HEADS)]
    pending = scores(*units[0])
    outs = []
    for n, (t, g) in enumerate(units):
        s_loc, s_ctx = pending
        if n + 1 < len(units):
            pending = scores(*units[n + 1])
        lo, hi = g * HEAD_DIM, (g + 1) * HEAD_DIM
        sink = jnp.concatenate(
            [jnp.full((blk, 1), sink_ref[g * g_sz + i], F32) for i in range(g_sz)], axis=0)
        o = _softmax_pv([(s_ctx, vx[:, lo:hi]), (s_loc, v_ext[t * blk:(t + 3) * blk, lo:hi])], sink)
        outs.extend(o[i * blk:(i + 1) * blk] for i in range(g_sz))
        if g == A_KV_HEADS - 1:
            o_ref[t * blk:(t + 1) * blk, :] = jnp.concatenate(outs, axis=1).astype(o_ref.dtype)
            outs = []


WIN_GROUP = 32


def _window_attn(sink, aq, ak, av, akx, avx, batch, seq, ctx_len):
    nblk = seq // A_BLOCK
    nstep = nblk // WIN_GROUP
    kvw = A_KV_HEADS * HEAD_DIM
    qw = A_HEADS * HEAD_DIM
    rows = WIN_GROUP * A_BLOCK

    def edge(delta):
        return lambda b, n: (b * nblk + jnp.clip(n * WIN_GROUP + delta, 0, nblk - 1), 0)

    def own(b, n):
        return (b * nstep + n, 0)

    kv_specs = [pl.BlockSpec((A_BLOCK, kvw), edge(-1)), pl.BlockSpec((rows, kvw), own),
                pl.BlockSpec((A_BLOCK, kvw), edge(WIN_GROUP))]
    return pl.pallas_call(
        functools.partial(_win_kernel, seq=seq),
        grid=(batch, nstep),
        in_specs=[pl.BlockSpec(memory_space=pltpu.SMEM), pl.BlockSpec((rows, qw), own)]
        + kv_specs + kv_specs
        + [pl.BlockSpec((ctx_len, kvw), lambda b, n: (b, 0)),
           pl.BlockSpec((ctx_len, kvw), lambda b, n: (b, 0))],
        out_specs=pl.BlockSpec((rows, qw), own),
        out_shape=jax.ShapeDtypeStruct((batch * seq, qw), BF16),
        compiler_params=_params(("arbitrary", "arbitrary")),
        name="window_attn",
    )(sink, aq, ak, ak, ak, av, av, av, akx, avx)


NA_QROWS = 8
NA_Q = NA_QROWS * GRID_W
NA_KBLK = 4 * GRID_W
NA_K = 4 * NA_KBLK


NA_DR = 2 * NA_ROWS - 1


def _na_bias_tiles(rpb, rows):
    nj = rows // NA_QROWS
    kr = min(NA_ROWS, rows)
    col = jnp.arange(GRID_W)
    col_start = jnp.clip(col - NA_COLS // 2, 0, GRID_W - NA_COLS)
    cvalid = (col[None, :] >= col_start[:, None]) & (col[None, :] < col_start[:, None] + NA_COLS)
    dc = jnp.clip(col[None, :] - col[:, None], -(NA_COLS - 1), NA_COLS - 1) + (NA_COLS - 1)
    tiles = jnp.where(cvalid[None, None], rpb.astype(F32)[:, :, dc], NEG_INF)
    tiles = jnp.concatenate([tiles, jnp.full_like(tiles[:, :1], NEG_INF)], axis=1)
    idx = []
    for j in (0, min(1, nj - 1), nj - 1):
        r = j * NA_QROWS + jnp.arange(NA_QROWS)
        krow = j * NA_QROWS - NA_KBLK // GRID_W + jnp.arange(NA_K // GRID_W)
        start = jnp.clip(r - kr // 2, 0, rows - kr)
        rvalid = (krow[None, :] >= start[:, None]) & (krow[None, :] < start[:, None] + kr)
        dr = krow[None, :] - r[:, None] + (NA_ROWS - 1)
        idx.append(jnp.where(rvalid, dr, NA_DR))
    return tiles, jnp.stack(idx).reshape(-1).astype(jnp.int32)


def _na_kernel(idx_ref, q_ref, k0, k1, k2, k3, v0, v1, v2, v3, kx_ref, vx_ref, t_ref, o_ref, bias_scr,
               *, nj):
    j = pl.program_id(2)
    nkr = NA_K // GRID_W

    @pl.when((j == 0) | (j == 1) | (j == nj - 1))
    def _():
        variant = jnp.where(j == 0, 0, jnp.where(j == nj - 1, 2, 1))
        for hh in range(LANES // HEAD_DIM):
            for qr in range(NA_QROWS):
                for kr in range(nkr):
                    d = idx_ref[(variant * NA_QROWS + qr) * nkr + kr]
                    bias_scr[hh, qr * GRID_W:(qr + 1) * GRID_W, kr * GRID_W:(kr + 1) * GRID_W] = (
                        t_ref[hh, d])

    q = q_ref[...]
    k_all = jnp.concatenate([k0[...], k1[...], k2[...], k3[...]], axis=0)
    v_all = jnp.concatenate([v0[...], v1[...], v2[...], v3[...]], axis=0)
    kx = kx_ref[...]
    vx = vx_ref[...]
    qh_rows = NA_Q // 2
    k_span = 3 * NA_KBLK

    def scores(hh, half):
        lo, hi = hh * HEAD_DIM, (hh + 1) * HEAD_DIM
        q0, k0_ = half * qh_rows, half * NA_KBLK
        qh = q[q0:q0 + qh_rows, lo:hi]
        s_loc = _dot_nt(qh, k_all[k0_:k0_ + k_span, lo:hi]) + bias_scr[hh, q0:q0 + qh_rows, k0_:k0_ + k_span]
        return s_loc, _dot_nt(qh, kx[:, lo:hi])

    units = [(hh, half) for half in range(2) for hh in range(LANES // HEAD_DIM)]
    pending = scores(*units[0])
    for n, (hh, half) in enumerate(units):
        s_loc, s_ctx = pending
        if n + 1 < len(units):
            pending = scores(*units[n + 1])
        lo, hi = hh * HEAD_DIM, (hh + 1) * HEAD_DIM
        k0_ = half * NA_KBLK
        o = _softmax_pv([(s_ctx, vx[:, lo:hi]), (s_loc, v_all[k0_:k0_ + k_span, lo:hi])], None)
        o_ref[half * qh_rows:(half + 1) * qh_rows, lo:hi] = o.astype(o_ref.dtype)


def _na_attn(bq, bk, bv, bkx, bvx, tiles, tile_idx, batch, seq, ctx_len):
    nj = seq // NA_Q
    nkb = seq // NA_KBLK
    hp = B_HEADS * HEAD_DIM // LANES
    per = LANES // HEAD_DIM

    def kb(t):
        return lambda p, b, j, idx: (b * nkb + jnp.clip(2 * j - 1 + t, 0, nkb - 1), p)

    kv_specs = [pl.BlockSpec((NA_KBLK, LANES), kb(t)) for t in range(4)]
    grid_spec = pltpu.PrefetchScalarGridSpec(
        num_scalar_prefetch=1,
        grid=(hp, batch, nj),
        in_specs=[pl.BlockSpec((NA_Q, LANES), lambda p, b, j, idx: (b * nj + j, p))]
        + kv_specs + kv_specs
        + [pl.BlockSpec((ctx_len, LANES), lambda p, b, j, idx: (b, p)),
           pl.BlockSpec((ctx_len, LANES), lambda p, b, j, idx: (b, p)),
           pl.BlockSpec((per, NA_DR + 1, GRID_W, GRID_W), lambda p, b, j, idx: (p, 0, 0, 0))],
        out_specs=pl.BlockSpec((NA_Q, LANES), lambda p, b, j, idx: (b * nj + j, p)),
        scratch_shapes=[pltpu.VMEM((per, NA_Q, NA_K), F32)],
    )
    return pl.pallas_call(
        functools.partial(_na_kernel, nj=nj),
        grid_spec=grid_spec,
        out_shape=jax.ShapeDtypeStruct((batch * seq, B_HEADS * HEAD_DIM), BF16),
        compiler_params=_params(("arbitrary", "arbitrary", "arbitrary")),
        name="na_attn",
    )(tile_idx, bq, bk, bk, bk, bk, bv, bv, bv, bv, bkx, bvx, tiles)


def _ctx_ab_kernel(sink_ref, aq_ref, ak_ref, av_ref, bq_ref, bk_ref, bv_ref, oa_ref, ob_ref):
    ctx_len = aq_ref.shape[0]
    g_sz = A_HEADS // A_KV_HEADS
    aq = aq_ref[...]
    ak = ak_ref[...]
    av = av_ref[...]
    outs = []
    for g in range(A_KV_HEADS):
        lo, hi = g * HEAD_DIM, (g + 1) * HEAD_DIM
        heads = [g * g_sz + i for i in range(g_sz)]
        qs = jnp.concatenate([aq[:, h * HEAD_DIM:(h + 1) * HEAD_DIM] for h in heads], axis=0)
        sink = jnp.concatenate(
            [jnp.full((ctx_len, 1), sink_ref[h], F32) for h in heads], axis=0)
        o = _softmax_pv([(_dot_nt(qs, ak[:, lo:hi]), av[:, lo:hi])], sink)
        outs.extend(o[i * ctx_len:(i + 1) * ctx_len] for i in range(g_sz))
    oa_ref[...] = jnp.concatenate(outs, axis=1).astype(oa_ref.dtype)
    bq = bq_ref[...]
    bk = bk_ref[...]
    bv = bv_ref[...]
    outs = []
    for h in range(B_HEADS):
        lo, hi = h * HEAD_DIM, (h + 1) * HEAD_DIM
        outs.append(_softmax_pv([(_dot_nt(bq[:, lo:hi], bk[:, lo:hi]), bv[:, lo:hi])], None))
    ob_ref[...] = jnp.concatenate(outs, axis=1).astype(ob_ref.dtype)


def _ctx_ab_attn(sink, aq, ak, av, bq, bk, bv, batch, ctx_len):
    def spec(w):
        return pl.BlockSpec((ctx_len, w), lambda b: (b, 0))

    return pl.pallas_call(
        _ctx_ab_kernel,
        grid=(batch,),
        in_specs=[pl.BlockSpec(memory_space=pltpu.SMEM)]
        + [spec(a.shape[1]) for a in (aq, ak, av, bq, bk, bv)],
        out_specs=[spec(aq.shape[1]), spec(bq.shape[1])],
        out_shape=[jax.ShapeDtypeStruct(aq.shape, BF16), jax.ShapeDtypeStruct(bq.shape, BF16)],
        compiler_params=_params(("arbitrary",)),
        name="ctx_ab_attn",
    )(sink, aq, ak, av, bq, bk, bv)


def _layer_norm(t, g, b):
    mu = jnp.mean(t, -1, keepdims=True)
    c = t - mu
    var = jnp.mean(c * c, -1, keepdims=True)
    return c * lax.rsqrt(var + 1e-5) * g + b


def _post_attn_kernel(*refs, alpha, lat_tiles, with_ctx, lat_transposed):
    n_src = 6 if with_ctx else 3
    srcs = refs[:n_src]
    (w_ref, mod_ref, lng_ref, lnb_ref, wr_hi_ref, wr_lo_ref, br_ref,
     x1_ref, h2_ref, lg_ref) = refs[n_src:]

    def rows_of(o_ref):
        if o_ref.ndim == 2:
            return o_ref[...]
        return o_ref[0].astype(F32).T.astype(BF16)

    def run(o1_ref, o2_ref, x_ref):
        half = w_ref.shape[0] // 2
        y = _dot(rows_of(o1_ref), w_ref[0:half, :]) + _dot(rows_of(o2_ref), w_ref[half:, :])
        g1 = mod_ref[0, 2:3, :]
        x1 = _layer_norm(alpha * x_ref[...] + g1 * y, lng_ref[...], lnb_ref[...])
        x1_ref[...] = x1
        h2 = x1 * (1.0 + mod_ref[0, 4:5, :]) + mod_ref[0, 3:4, :]
        h_hi, h_lo = _split_bf16(h2)
        h2_ref[...] = _pack_halves(h2)
        both = _dot(h_hi, wr_lo_ref[...])
        lg_ref[...] = (both[:, :ROUTER_PAD] + both[:, ROUTER_PAD:] + _dot(h_lo, wr_hi_ref[...])
                       + br_ref[...])

    if not with_ctx:
        run(*srcs)
    else:
        i = pl.program_id(0)
        pl.when(i < lat_tiles)(lambda: run(*srcs[:3]))
        pl.when(i >= lat_tiles)(lambda: run(*srcs[3:]))


def _post_attn(lat, ctx_src, w_bf, mod, gid, lng, lnb, wr_hi, wr_lo, br, n_lat, alpha, tm):
    half = w_bf.shape[0] // 2
    d = lat[2].shape[1]
    lat_tiles = n_lat // tm
    n_rows = n_lat + (ctx_src[0].shape[0] if ctx_src is not None else 0)
    lat_transposed = lat[0].ndim == 3

    def lat_map(i):
        return (jnp.minimum(i, lat_tiles - 1), 0)

    def ctx_map(i):
        return (jnp.maximum(i - lat_tiles, 0), 0)

    def src_specs(index_map):
        return [pl.BlockSpec((tm, half), index_map), pl.BlockSpec((tm, half), index_map),
                pl.BlockSpec((tm, d), index_map)]

    in_specs = src_specs(lat_map)
    if lat_transposed:
        per_batch = lat[0].shape[2] // tm
        t_spec = pl.BlockSpec((1, half, tm), lambda i: (i // per_batch, 0, i % per_batch))
        in_specs[0] = in_specs[1] = t_spec
    args = list(lat)
    if ctx_src is not None:
        in_specs += src_specs(ctx_map)
        args += list(ctx_src)

    def const(shape):
        return pl.BlockSpec(shape, lambda i: (0,) * len(shape))

    in_specs += [const((2 * half, d)),
                 pl.BlockSpec((1, 6, d), lambda i: (gid(i), 0, 0)),
                 const((1, d)), const((1, d)),
                 const((d, ROUTER_PAD)), const((d, 2 * ROUTER_PAD)), const((1, ROUTER_PAD))]
    args += [w_bf, mod, lng, lnb, wr_hi, wr_lo, br]
    widths = (d, d // 2, ROUTER_PAD)
    dtypes = (F32, jnp.uint32, F32)
    return pl.pallas_call(
        functools.partial(_post_attn_kernel, alpha=alpha, lat_tiles=lat_tiles,
                          with_ctx=ctx_src is not None, lat_transposed=lat_transposed),
        grid=(n_rows // tm,),
        in_specs=in_specs,
        out_specs=[pl.BlockSpec((tm, w), lambda i: (i, 0)) for w in widths],
        out_shape=[jax.ShapeDtypeStruct((n_rows, w), t) for w, t in zip(widths, dtypes)],
        compiler_params=_params(("arbitrary",)),
        name="post_attn",
    )(*args)


def _moe_kernel(be_ref, xs_ref, wgu_ref, wdn_ref, ys_ref, wgu_bf, wdn_bf):
    i = pl.program_id(0)
    used = i < be_ref[pl.num_programs(0)]
    prev = be_ref[jnp.maximum(i - 1, 0)]
    changed = (i == 0) | (be_ref[i] != prev)

    @pl.when(changed & used)
    def _():
        wgu_bf[...] = wgu_ref[0, 0].astype(BF16)
        wdn_bf[...] = wdn_ref[0, 0].astype(BF16)

    @pl.when(used)
    def _():
        de = wdn_bf.shape[0]
        subs = [slice(r, r + MOE_SUB) for r in range(0, MOE_BLOCK, MOE_SUB)]
        gus = [_dot(_unpack_halves(xs_ref[sl, :]).astype(BF16), wgu_bf[...]) for sl in subs]
        for sl, gu in zip(subs, gus):
            gate = gu[:, :de]
            up = gu[:, de:]
            act = gate / (1.0 + jnp.exp(-gate)) * up
            ys_ref[sl, :] = _pack_halves(_dot(act.astype(BF16), wdn_bf[...]))

    @pl.when(jnp.logical_not(used))
    def _():
        ys_ref[...] = jnp.zeros(ys_ref.shape, ys_ref.dtype)


def _moe_experts(block_e, xs, w_gu, w_dn, layer):
    rows, dw = xs.shape
    d = 2 * dw
    nb = rows // MOE_BLOCK
    de = w_dn.shape[2]
    grid_spec = pltpu.PrefetchScalarGridSpec(
        num_scalar_prefetch=1,
        grid=(nb,),
        in_specs=[
            pl.BlockSpec((MOE_BLOCK, dw), lambda i, be: (i, 0)),
            pl.BlockSpec((1, 1, d, 2 * de), lambda i, be: (layer, be[i], 0, 0)),
            pl.BlockSpec((1, 1, de, d), lambda i, be: (layer, be[i], 0, 0)),
        ],
        out_specs=pl.BlockSpec((MOE_BLOCK, dw), lambda i, be: (i, 0)),
        scratch_shapes=[pltpu.VMEM((d, 2 * de), BF16), pltpu.VMEM((de, d), BF16)],
    )
    return pl.pallas_call(
        _moe_kernel,
        grid_spec=grid_spec,
        out_shape=jax.ShapeDtypeStruct((rows, dw), jnp.uint32),
        compiler_params=_params(("arbitrary",)),
        name="moe_experts",
    )(block_e, xs, w_gu, w_dn)


ROUTE_TILE = 512


def _route_kernel(lg_ref, rec_ref, cnt_ref, carry):
    i = pl.program_id(0)

    @pl.when(i == 0)
    def _():
        carry[...] = jnp.zeros(carry.shape, F32)

    lg = lg_ref[...]
    tm = lg.shape[0]
    lane = lax.broadcasted_iota(jnp.int32, lg.shape, 1)

    def first_lane(mask):
        return jnp.min(jnp.where(mask, lane, ROUTER_PAD), axis=1, keepdims=True)

    is_g = lane < N_GROUPS
    g_log = jnp.where(is_g, lg, NEG_INF)
    g_max = jnp.max(g_log, axis=1, keepdims=True)
    g_sum = jnp.sum(jnp.where(is_g, jnp.exp(lg - g_max), 0.0), axis=1, keepdims=True)
    g_val = 1.0 / g_sum
    g_idx = first_lane(g_log == g_max)
    lo = N_GROUPS + EXP_PER_GROUP * g_idx
    sel = (lane >= lo) & (lane < lo + EXP_PER_GROUP)
    e_log = jnp.where(sel, lg, NEG_INF)
    e_max = jnp.max(e_log, axis=1, keepdims=True)
    e_exp = jnp.where(sel, jnp.exp(lg - e_max), 0.0)
    e_prob = e_exp / jnp.sum(e_exp, axis=1, keepdims=True)
    p1 = jnp.where(sel, e_prob, -1.0)
    v1 = jnp.max(p1, axis=1, keepdims=True)
    i1 = first_lane(p1 == v1)
    p2 = jnp.where(lane == i1, -1.0, p1)
    v2 = jnp.max(p2, axis=1, keepdims=True)
    i2 = first_lane(p2 == v2)
    norm = g_val / (v1 + v2)

    hot1 = lane == i1
    hot2 = lane == i2
    hot = jnp.where(hot1 | hot2, 1.0, 0.0)
    rows = lax.broadcasted_iota(jnp.int32, (tm, tm), 0)
    cols = lax.broadcasted_iota(jnp.int32, (tm, tm), 1)
    before = jnp.where(cols < rows, 1.0, 0.0).astype(BF16)
    prefix = _dot(before, hot.astype(BF16)) + carry[0:1, :]
    r1 = jnp.sum(jnp.where(hot1, prefix, 0.0), axis=1, keepdims=True)
    r2 = jnp.sum(jnp.where(hot2, prefix, 0.0), axis=1, keepdims=True)
    carry[0:1, :] = carry[0:1, :] + jnp.sum(hot, axis=0, keepdims=True)

    fields = [(i1 - N_GROUPS).astype(F32), (i2 - N_GROUPS).astype(F32), r1, r2, v1 * norm, v2 * norm]
    rec = jnp.zeros(lg.shape, F32)
    for k, f in enumerate(fields):
        rec = jnp.where(lane == k, f, rec)
    rec_ref[...] = rec

    @pl.when(i == pl.num_programs(0) - 1)
    def _():
        cnt_ref[...] = carry[...]


def _route(logits, n_tok):
    rec, cnt = pl.pallas_call(
        _route_kernel,
        grid=(n_tok // ROUTE_TILE,),
        in_specs=[pl.BlockSpec((ROUTE_TILE, ROUTER_PAD), lambda i: (i, 0))],
        out_specs=[pl.BlockSpec((ROUTE_TILE, ROUTER_PAD), lambda i: (i, 0)),
                   pl.BlockSpec((8, ROUTER_PAD), lambda i: (0, 0))],
        out_shape=[jax.ShapeDtypeStruct((n_tok, ROUTER_PAD), F32),
                   jax.ShapeDtypeStruct((8, ROUTER_PAD), F32)],
        scratch_shapes=[pltpu.VMEM((8, ROUTER_PAD), F32)],
        compiler_params=_params(("arbitrary",)),
        name="route",
    )(logits)
    experts = rec[:, 0:2].astype(jnp.int32)
    rank = rec[:, 2:4].astype(jnp.int32)
    wts = rec[:, 4:6]
    counts = cnt[0, N_GROUPS:N_GROUPS + N_EXPERTS].astype(jnp.int32)

    n_asg = n_tok * TOP_K
    pcounts = ((counts + MOE_BLOCK - 1) // MOE_BLOCK) * MOE_BLOCK
    pends = jnp.cumsum(pcounts)
    pstarts = pends - pcounts
    dest = pstarts[experts] + rank
    nb = -(-n_asg // MOE_BLOCK) + N_EXPERTS
    block_start = jnp.arange(nb, dtype=jnp.int32) * MOE_BLOCK
    block_e = jnp.minimum((pends[None, :] <= block_start[:, None]).sum(1), N_EXPERTS - 1).astype(jnp.int32)
    block_e = jnp.concatenate([block_e, (pends[-1:] // MOE_BLOCK).astype(jnp.int32)])
    tok = jnp.broadcast_to(jnp.arange(n_tok, dtype=jnp.int32)[:, None], (n_tok, TOP_K))
    filler = jnp.arange(nb * MOE_BLOCK, dtype=jnp.int32) % n_tok
    row_tok = filler.at[dest.reshape(-1)].set(tok.reshape(-1), unique_indices=True)
    return wts, dest, row_tok, block_e


def _post_moe_kernel(x_ref, ya_ref, yb_ref, w_ref, mod_ref, lng_ref, lnb_ref, o_ref, *, alpha):
    w = w_ref[...]
    y = w[:, 0:1] * _unpack_halves(ya_ref[...]) + w[:, 1:2] * _unpack_halves(yb_ref[...])
    g2 = mod_ref[0, 5:6, :]
    o_ref[...] = _layer_norm(alpha * x_ref[...] + g2 * y, lng_ref[...], lnb_ref[...])


def _post_moe(x1, ya, yb, wts, mod, gid, lng, lnb, alpha, tm):
    rows, d = x1.shape
    row = pl.BlockSpec((tm, d), lambda i: (i, 0))
    packed = pl.BlockSpec((tm, d // 2), lambda i: (i, 0))
    vec = pl.BlockSpec((1, d), lambda i: (0, 0))
    return pl.pallas_call(
        functools.partial(_post_moe_kernel, alpha=alpha),
        grid=(rows // tm,),
        in_specs=[row, packed, packed,
                  pl.BlockSpec((tm, TOP_K), lambda i: (i, 0)),
                  pl.BlockSpec((1, 6, d), lambda i: (gid(i), 0, 0)),
                  vec, vec],
        out_specs=row,
        out_shape=jax.ShapeDtypeStruct((rows, d), F32),
        compiler_params=_params(("arbitrary",)),
        name="post_moe",
    )(x1, ya, yb, wts, mod, lng, lnb)


CD_COLS = dict(cq=(0, 384), ckv=(384, 640), dq=(640, 1152), dk=(1152, 1664), dv=(1664, 2176),
               kr=(2176, 2304))
C_HEAD_PAD = LANES


def _rms(t, g, eps=1e-6):
    return t * lax.rsqrt(jnp.mean(t * t, -1, keepdims=True) + eps) * g


def _proj_cd_kernel(x_ref, mod_ref, w_ref, qn_ref, wuq_ref, kvn_ref, wukv_ref,
                    cos64_ref, sin64_ref, cos32_ref, sin32_ref,
                    qc_ref, kc_ref, vct_ref, dq_ref, dk_ref, dvt_ref):
    h = _modulate(x_ref, mod_ref, 0).astype(BF16)
    r = _dot(h, w_ref[...])

    def cols(name):
        lo, hi = CD_COLS[name]
        return r[:, lo:hi]

    q = _dot(_rms(cols('cq'), qn_ref[...]).astype(BF16), wuq_ref[...])
    kv = _dot(_rms(cols('ckv'), kvn_ref[...]).astype(BF16), wukv_ref[...])
    c32, s32 = cos32_ref[...], sin32_ref[...]
    c64, s64 = cos64_ref[...], sin64_ref[...]
    q = _rope_wide(q, c32, s32, C_ROPE // 4)
    kr = _rope_tile(cols('kr'), c32, s32, C_ROPE // 4)
    dq = _rope_wide(cols('dq'), c64, s64, D_QK // 4)
    dk = _rope_wide(cols('dk'), c64, s64, D_QK // 4)
    kw = C_HEADS * C_HEAD_PAD
    qc_ref[0] = (q * ((C_NOPE + C_ROPE) ** -0.5 * LOG2E)).astype(BF16)
    kc_ref[0] = (kv[:, :kw] + jnp.concatenate([kr] * C_HEADS, axis=1)).astype(BF16)
    dq_ref[0] = (dq * (D_QK ** -0.5 * LOG2E)).astype(BF16)
    dk_ref[0] = dk.astype(BF16)
    _store_vt(vct_ref, kv[:, kw:], C_HEADS, C_V)
    _store_vt(dvt_ref, cols('dv'), D_HEADS, D_V)


def _store_vt(vt_ref, v, heads, dv):
    tm = v.shape[0]
    v_t = v.T.astype(BF16)
    row = lax.broadcasted_iota(jnp.int32, (ONES_ROWS, tm), 0)
    extra = jnp.where(row == 0, 1.0, 0.0).astype(BF16)
    for h in range(heads):
        vt_ref[0, h, 0, 0:dv, :] = v_t[h * dv:(h + 1) * dv]
        vt_ref[0, h, 0, dv:dv + ONES_ROWS, :] = extra


def _proj_cd(xall, mod, w_bf, qn, wuq, kvn, wukv, tabs, batch, seq, ctx_len, tm, tk):
    d = xall.shape[1]
    lat_t, ctx_t = seq // tm, ctx_len // tm
    n_keys = seq + ctx_len
    per_chunk = tk // tm
    n_chunks = n_keys // tk
    widths = (C_HEADS * C_HEAD_PAD, C_HEADS * C_HEAD_PAD, 2 * D_HEADS * D_QK, 2 * D_HEADS * D_QK)

    def src(b, t):
        return (jnp.where(t < lat_t, b * lat_t + t, batch * lat_t + b * ctx_t + t - lat_t), 0)

    def full(a):
        return pl.BlockSpec(a.shape, lambda b, t: (0,) * a.ndim)

    def row_spec(w):
        return pl.BlockSpec((1, tm, w), lambda b, t: (b, t, 0))

    def row_shape(w):
        return jax.ShapeDtypeStruct((batch, n_keys, w), BF16)

    def vt_spec(heads, dv):
        return pl.BlockSpec((1, heads, 1, dv + ONES_ROWS, tm),
                            lambda b, t: (b, 0, t // per_chunk, 0, t % per_chunk))

    def vt_shape(heads, dv):
        return jax.ShapeDtypeStruct((batch, heads, n_chunks, dv + ONES_ROWS, tk), BF16)

    tab = pl.BlockSpec((tm, LANES), lambda b, t: (t, 0))
    return pl.pallas_call(
        _proj_cd_kernel,
        grid=(batch, lat_t + ctx_t),
        in_specs=[pl.BlockSpec((tm, d), src),
                  pl.BlockSpec((1, 6, d), lambda b, t: (jnp.where(t < lat_t, b, batch), 0, 0)),
                  full(w_bf), full(qn), full(wuq), full(kvn), full(wukv), tab, tab, tab, tab],
        out_specs=[row_spec(widths[0]), row_spec(widths[1]), vt_spec(C_HEADS, C_V),
                   row_spec(widths[2]), row_spec(widths[3]), vt_spec(D_HEADS, D_V)],
        out_shape=[row_shape(widths[0]), row_shape(widths[1]), vt_shape(C_HEADS, C_V),
                   row_shape(widths[2]), row_shape(widths[3]), vt_shape(D_HEADS, D_V)],
        compiler_params=_params(("arbitrary", "arbitrary")),
        name="proj_cd",
    )(xall, mod, w_bf, qn, wuq, kvn, wukv, *tabs)


DENSE_SHIFT = 80.0
DENSE_L_MIN = 2.0 ** -60
DENSE_L_MAX = 2.0 ** 120
DENSE_TQ = 1024
DENSE_TK_MAX = 1280
DENSE_UNROLL_MAX = 13
MXU_DEPTH = 256
PROJ_CD_TILE = 256
ONES_ROWS = 16


def _sq_norm_row(x):
    ones = jnp.ones((8, x.shape[1]), BF16)
    return _dot_nt(ones, (x * x).astype(BF16))[0:1] * 1.02


def _key_norm_max(kchunk, n_chunks, masks):
    def body(j, mx):
        kf = kchunk(j).astype(F32)
        ksq = kf * kf
        out = []
        for msk, cur in zip(masks, mx):
            part = ksq if msk is None else jnp.where(msk, ksq, 0.0)
            rn = jnp.sum(part, axis=1, keepdims=True)
            out.append(jnp.maximum(cur, jnp.max(rn, axis=0, keepdims=True)))
        return tuple(out)

    mx = lax.fori_loop(0, n_chunks, body, tuple(jnp.zeros((1, 1), F32) for _ in masks))
    return [jnp.sqrt(v) * 1.01 for v in mx]


def _safe_online(q, kchunk, vt_chunk, n_chunks, m_scr, l_scr, acc):
    m_scr[...] = jnp.full(m_scr.shape, NEG_INF, F32)
    l_scr[...] = jnp.zeros(l_scr.shape, F32)
    acc[...] = jnp.zeros(acc.shape, F32)

    def body(j, carry):
        s = _dot_nt(kchunk(j), q)
        m_prev = m_scr[...]
        m_new = jnp.maximum(m_prev, s.max(0, keepdims=True))
        alpha = jnp.exp2(m_prev - m_new)
        p = jnp.exp2(s - m_new)
        l_scr[...] = alpha * l_scr[...] + p.sum(0, keepdims=True)
        acc[...] = alpha * acc[...] + _dot(vt_chunk(j), p.astype(BF16))
        m_scr[...] = m_new
        return carry

    lax.fori_loop(0, n_chunks, body, 0)


def _denominators_ok(*ls):
    ok = None
    for l in ls:
        cur = (l > DENSE_L_MIN) & (l < DENSE_L_MAX)
        ok = cur if ok is None else ok & cur
    return jnp.max(jnp.where(ok, 0.0, 1.0)) == 0.0


MLA_HEADS_PER_STEP = 2


def _mla_kernel(q_ref, k_ref, vt_ref, o_ref, kmax_scr, acc_scr, m_scr, l_scr,
                *, n_chunks, tk, unroll, depth):
    hp = acc_scr.shape[0]

    def kchunk(j, h):
        return k_ref[0, pl.ds(pl.multiple_of(j * tk, tk), tk), h * C_HEAD_PAD:(h + 1) * C_HEAD_PAD]

    @pl.when(pl.program_id(2) == 0)
    def _():
        for h in range(hp):
            (kmax,) = _key_norm_max(lambda j: kchunk(j, h), n_chunks, [None])
            kmax_scr[h:h + 1, :] = jnp.broadcast_to(kmax, (1, LANES))

    qs = [q_ref[0, :, h * C_HEAD_PAD:(h + 1) * C_HEAD_PAD] for h in range(hp)]
    ms = [jnp.sqrt(_sq_norm_row(qs[h].astype(F32))) * kmax_scr[h:h + 1, 0:1] - DENSE_SHIFT
          for h in range(hp)]
    acc_scr[...] = jnp.zeros(acc_scr.shape, F32)

    def scores(j):
        return [_dot_nt(kchunk(j, h), qs[h]) for h in range(hp)]

    def body(it, carry):
        j0 = it * unroll
        pend = [scores(j0 + u) for u in range(min(depth, unroll))]
        accs = [acc_scr[h] for h in range(hp)]
        for u in range(unroll):
            ss = pend.pop(0)
            if u + depth < unroll:
                pend.append(scores(j0 + u + depth))
            for h in range(hp):
                accs[h] = accs[h] + _dot(vt_ref[0, h, j0 + u], jnp.exp2(ss[h] - ms[h]).astype(BF16))
        for h in range(hp):
            acc_scr[h] = accs[h]
        return carry

    lax.fori_loop(0, n_chunks // unroll, body, 0)
    ls = [acc_scr[h, C_V:C_V + 1, :] for h in range(hp)]
    for h in range(hp):
        o_ref[0, h * C_V:(h + 1) * C_V, :] = (acc_scr[h, 0:C_V, :] / ls[h]).astype(o_ref.dtype)

    @pl.when(jnp.logical_not(_denominators_ok(*ls)))
    def _():
        for h in range(hp):
            _safe_online(qs[h], lambda j: kchunk(j, h), lambda j: vt_ref[0, h, j], n_chunks,
                         m_scr, l_scr, acc_scr.at[h])
            o_ref[0, h * C_V:(h + 1) * C_V, :] = (acc_scr[h, 0:C_V, :] / l_scr[...]).astype(o_ref.dtype)


def _mla_attn(qc, kc, vt, batch, seq, tq, tk, unroll, depth):
    n_keys = kc.shape[1]
    nq, n_chunks = seq // tq, n_keys // tk
    rows = C_V + ONES_ROWS
    hp = MLA_HEADS_PER_STEP
    return pl.pallas_call(
        functools.partial(_mla_kernel, n_chunks=n_chunks, tk=tk, unroll=unroll, depth=depth),
        grid=(batch, C_HEADS // hp, nq),
        in_specs=[
            pl.BlockSpec((1, tq, hp * C_HEAD_PAD), lambda b, h, i: (b, i, h)),
            pl.BlockSpec((1, n_keys, hp * C_HEAD_PAD), lambda b, h, i: (b, 0, h)),
            pl.BlockSpec((1, hp, n_chunks, rows, tk), lambda b, h, i: (b, h, 0, 0, 0)),
        ],
        out_specs=pl.BlockSpec((1, hp * C_V, tq), lambda b, h, i: (b, h, i)),
        out_shape=jax.ShapeDtypeStruct((batch, C_HEADS * C_V, seq), BF16),
        scratch_shapes=[pltpu.VMEM((8, LANES), F32), pltpu.VMEM((hp, rows, tq), F32),
                        pltpu.VMEM((1, tq), F32), pltpu.VMEM((1, tq), F32)],
        compiler_params=_params(("arbitrary", "arbitrary", "arbitrary")),
        name="mla_attn",
    )(qc, kc, vt)


def _diff_kernel(lam_ref, q_ref, k_ref, vt_ref, subln_ref, o_ref, kmax_scr, acc1_scr, acc2_scr,
                 m_scr, l_scr, *, n_chunks, tk, unroll, depth, out_scale):
    def kchunk(j):
        return k_ref[0, pl.ds(pl.multiple_of(j * tk, tk), tk), :]

    @pl.when(pl.program_id(2) == 0)
    def _():
        klane = lax.broadcasted_iota(jnp.int32, (tk, 2 * D_QK), 1)
        k1, k2 = _key_norm_max(kchunk, n_chunks, [klane < D_QK, klane >= D_QK])
        kmax_scr[0:1, :] = jnp.broadcast_to(k1, (1, LANES))
        kmax_scr[1:2, :] = jnp.broadcast_to(k2, (1, LANES))

    q = q_ref[0]
    qlane = lax.broadcasted_iota(jnp.int32, q.shape, 1)
    zero = jnp.zeros_like(q)
    q1 = jnp.where(qlane < D_QK, q, zero)
    q2 = jnp.where(qlane >= D_QK, q, zero)
    m1 = jnp.sqrt(_sq_norm_row(q1.astype(F32))) * kmax_scr[0:1, 0:1] - DENSE_SHIFT
    m2 = jnp.sqrt(_sq_norm_row(q2.astype(F32))) * kmax_scr[1:2, 0:1] - DENSE_SHIFT
    acc1_scr[...] = jnp.zeros(acc1_scr.shape, F32)
    acc2_scr[...] = jnp.zeros(acc2_scr.shape, F32)

    def scores(j):
        k = kchunk(j)
        return _dot_nt(k, q1), _dot_nt(k, q2)

    def body(it, carry):
        j0 = it * unroll
        pend = [scores(j0 + u) for u in range(min(depth, unroll))]
        a1 = acc1_scr[...]
        a2 = acc2_scr[...]
        for u in range(unroll):
            s1, s2 = pend.pop(0)
            if u + depth < unroll:
                pend.append(scores(j0 + u + depth))
            vt = vt_ref[0, 0, j0 + u]
            a1 = a1 + _dot(vt, jnp.exp2(s1 - m1).astype(BF16))
            a2 = a2 + _dot(vt, jnp.exp2(s2 - m2).astype(BF16))
        acc1_scr[...] = a1
        acc2_scr[...] = a2
        return carry

    lax.fori_loop(0, n_chunks // unroll, body, 0)

    def finish(o1, o2):
        o = o1 - lam_ref[0] * o2
        o = o * lax.rsqrt(jnp.mean(o * o, 0, keepdims=True) + 1e-6) * subln_ref[...] * out_scale
        o_ref[0] = o.astype(o_ref.dtype)

    l1 = acc1_scr[D_V:D_V + 1, :]
    l2 = acc2_scr[D_V:D_V + 1, :]
    finish(acc1_scr[0:D_V, :] / l1, acc2_scr[0:D_V, :] / l2)

    @pl.when(jnp.logical_not(_denominators_ok(l1, l2)))
    def _():
        def vt_chunk(j):
            return vt_ref[0, 0, j]

        _safe_online(q1, kchunk, vt_chunk, n_chunks, m_scr, l_scr, acc1_scr)
        o1 = acc1_scr[0:D_V, :] / l_scr[...]
        _safe_online(q2, kchunk, vt_chunk, n_chunks, m_scr, l_scr, acc2_scr)
        finish(o1, acc2_scr[0:D_V, :] / l_scr[...])


def _diff_attn(lam, dq, dk, vt, subln_col, out_scale, batch, seq, tq, tk, unroll, depth):
    n_keys = dk.shape[1]
    nq, n_chunks = seq // tq, n_keys // tk
    rows = D_V + ONES_ROWS
    pair = 2 * D_QK
    return pl.pallas_call(
        functools.partial(_diff_kernel, n_chunks=n_chunks, tk=tk, unroll=unroll, depth=depth,
                          out_scale=out_scale),
        grid=(batch, D_HEADS, nq),
        in_specs=[
            pl.BlockSpec(memory_space=pltpu.SMEM),
            pl.BlockSpec((1, tq, pair), lambda b, h, i: (b, i, h)),
            pl.BlockSpec((1, n_keys, pair), lambda b, h, i: (b, 0, h)),
            pl.BlockSpec((1, 1, n_chunks, rows, tk), lambda b, h, i: (b, h, 0, 0, 0)),
            pl.BlockSpec((D_V, 1), lambda b, h, i: (0, 0)),
        ],
        out_specs=pl.BlockSpec((1, D_V, tq), lambda b, h, i: (b, h, i)),
        out_shape=jax.ShapeDtypeStruct((batch, D_HEADS * D_V, seq), BF16),
        scratch_shapes=[pltpu.VMEM((8, LANES), F32), pltpu.VMEM((rows, tq), F32),
                        pltpu.VMEM((rows, tq), F32), pltpu.VMEM((1, tq), F32), pltpu.VMEM((1, tq), F32)],
        compiler_params=_params(("arbitrary", "arbitrary", "arbitrary")),
        name="diff_attn",
    )(lam, dq, dk, vt, subln_col)


def _rope_tables(n_tok, dim, lane_lo):
    t = jnp.arange(n_tok)
    pos_r = (t // GRID_W).astype(F32)
    pos_c = (t % GRID_W).astype(F32)
    quarter = dim // 4
    inv = ROPE_BASE ** (-jnp.arange(quarter, dtype=F32) / quarter)
    ang_r = pos_r[:, None] * inv
    ang_c = pos_c[:, None] * inv
    ang = jnp.concatenate([ang_r, ang_r, ang_c, ang_c], -1)
    sign = jnp.tile(jnp.concatenate([-jnp.ones(quarter), jnp.ones(quarter)]), 2).astype(F32)
    cos, sin = jnp.cos(ang), jnp.sin(ang) * sign
    reps = (LANES - lane_lo) // dim
    cos = jnp.concatenate([jnp.ones((n_tok, lane_lo), F32)] + [cos] * reps, axis=1)
    sin = jnp.concatenate([jnp.zeros((n_tok, lane_lo), F32)] + [sin] * reps, axis=1)
    return cos, sin


def _cd_weights(w_in, w_uq, w_ukv):
    d = w_in.shape[0]
    s0 = C_Q_RANK
    s1 = s0 + C_KV_RANK
    s2 = s1 + C_ROPE
    s3 = s2 + 2 * D_HEADS * D_QK
    s4 = s3 + 2 * D_HEADS * D_QK
    kr = jnp.zeros((d, LANES), F32).at[:, C_NOPE:C_NOPE + C_ROPE].set(w_in[:, s1:s2])
    w_in_p = jnp.concatenate([w_in[:, :s1], w_in[:, s2:s3], w_in[:, s3:s4], w_in[:, s4:], kr], axis=1)
    dqk = C_NOPE + C_ROPE
    wq = w_uq.reshape(C_Q_RANK, C_HEADS, dqk)
    wq = jnp.pad(wq, ((0, 0), (0, 0), (0, C_HEAD_PAD - dqk))).reshape(C_Q_RANK, C_HEADS * C_HEAD_PAD)
    wkv = w_ukv.reshape(C_KV_RANK, C_HEADS, C_NOPE + C_V)
    wk = jnp.pad(wkv[:, :, :C_NOPE], ((0, 0), (0, 0), (0, C_HEAD_PAD - C_NOPE)))
    wk = wk.reshape(C_KV_RANK, C_HEADS * C_HEAD_PAD)
    wv = wkv[:, :, C_NOPE:].reshape(C_KV_RANK, C_HEADS * C_V)
    return w_in_p.astype(BF16), wq.astype(BF16), jnp.concatenate([wk, wv], axis=1).astype(BF16)


def _moe_layer(h2, logits, x1, w_gu, w_dn, layer, mod, gid, lng, lnb, alpha, tm):
    n_tok = h2.shape[0]
    wts, dest, row_tok, block_e = _route(logits, n_tok)
    xs = jnp.take(h2, row_tok, axis=0, mode='clip')
    ys = _moe_experts(block_e, xs, w_gu, w_dn, layer)
    ya = jnp.take(ys, dest[:, 0], axis=0, mode='clip')
    yb = jnp.take(ys, dest[:, 1], axis=0, mode='clip')
    return _post_moe(x1, ya, yb, wts.astype(F32), mod, gid, lng, lnb, alpha, tm)


def kernel(x, c, ctx, c_ctx, w_ada, b_ada, ln_g, ln_b, ab_w_in, a_sink, b_rpb, ab_w_out, cd_w_in, c_q_norm, c_w_uq, c_kv_norm, c_w_ukv, d_lambda, d_subln, cd_w_out, w_group, b_group, w_exp_router, b_exp_router, w_gate_up, w_down):
    batch, seq, d = x.shape
    ctx_len = ctx.shape[1]
    depth = w_ada.shape[0]
    alpha = (2 * depth) ** 0.25
    n_lat = batch * seq
    n_ctx = batch * ctx_len
    tm = 512
    tq = DENSE_TQ
    assert depth == 2 and seq % NA_Q == 0 and seq % tq == 0 and n_ctx % tm == 0
    assert seq % PROJ_CD_TILE == 0 and ctx_len % PROJ_CD_TILE == 0
    assert seq % (WIN_GROUP * A_BLOCK) == 0 and seq // NA_Q >= 2
    key_tiles = (seq + ctx_len) // MXU_DEPTH
    tk = MXU_DEPTH * max(u for u in range(1, DENSE_TK_MAX // MXU_DEPTH + 1) if key_tiles % u == 0)
    n_chunks = (seq + ctx_len) // tk

    def unroll_for(cap):
        return max(u for u in range(1, cap + 1) if n_chunks % u == 0)
    lat_tiles = seq // tm

    def gid_lat(i):
        return i // lat_tiles

    def gid_ctx(i):
        return batch

    def gid_all(i):
        return jnp.minimum(i // lat_tiles, batch)

    c_all = jnp.zeros((8, d), F32).at[:batch].set(c).at[batch].set(c_ctx)
    mod = _ada(c_all, w_ada, b_ada).reshape(depth, 8, 6, d)

    cos64, sin64 = _rope_tables(seq, HEAD_DIM, 0)
    cos32, sin32 = _rope_tables(seq, C_ROPE, C_NOPE)

    def router_weights(l):
        wr = jnp.zeros((d, ROUTER_PAD), F32)
        wr = wr.at[:, :N_GROUPS].set(w_group[l]).at[:, N_GROUPS:N_GROUPS + N_EXPERTS].set(w_exp_router[l])
        br = jnp.zeros((1, ROUTER_PAD), F32)
        br = br.at[0, :N_GROUPS].set(b_group[l]).at[0, N_GROUPS:N_GROUPS + N_EXPERTS].set(b_exp_router[l])
        hi = wr.astype(BF16)
        lo = (wr - hi.astype(F32)).astype(BF16)
        return hi, jnp.concatenate([hi, lo], axis=1), br

    x2d = x.reshape(n_lat, d)
    ctx2d = ctx.reshape(n_ctx, d)

    l = 0
    w_in_bf = ab_w_in[0].astype(BF16)
    aq, ak, av, bq, bk, bv = _proj_ab(x2d, 0, n_lat, mod[l], gid_lat, w_in_bf, cos64, sin64,
                                      lat_tiles, True, tm)
    aqx, akx, avx, bqx, bkx, bvx = _proj_ab(ctx2d, 0, n_ctx, mod[l], gid_ctx, w_in_bf, cos64, sin64,
                                            1, False, tm)
    sink = a_sink[0].astype(F32)
    oa = _window_attn(sink, aq, ak, av, akx, avx, batch, seq, ctx_len)
    tiles, tile_idx = _na_bias_tiles(b_rpb[0], seq // GRID_W)
    ob = _na_attn(bq, bk, bv, bkx, bvx, tiles, tile_idx, batch, seq, ctx_len)
    oax, obx = _ctx_ab_attn(sink, aqx, akx, avx, bqx, bkx, bvx, batch, ctx_len)

    n_all = n_lat + n_ctx
    w_out_bf = ab_w_out[0].astype(BF16)
    lng, lnb = ln_g[l, 0][None], ln_b[l, 0][None]
    wr_hi, wr_lo, br = router_weights(l)
    x1, h2, logits = _post_attn((oa, ob, x2d), (oax, obx, ctx2d), w_out_bf, mod[l], gid_all,
                                lng, lnb, wr_hi, wr_lo, br, n_lat, alpha, tm)
    xall = _moe_layer(h2, logits, x1, w_gate_up, w_down, l, mod[l], gid_all,
                      ln_g[l, 1][None], ln_b[l, 1][None], alpha, tm)

    l = 1
    lam_init = 0.8 - 0.6 * math.exp(-0.3 * l)
    lp = d_lambda[0].astype(F32)
    lam = (jnp.exp(jnp.sum(lp[0] * lp[1])) - jnp.exp(jnp.sum(lp[2] * lp[3])) + lam_init).reshape(1)
    w_in_p, wuq_p, wukv_p = _cd_weights(cd_w_in[0], c_w_uq[0], c_w_ukv[0])
    qn, kvn = c_q_norm[0][None].astype(F32), c_kv_norm[0][None].astype(F32)
    def with_ctx_identity(cos, sin):
        return (jnp.concatenate([cos, jnp.ones((ctx_len, LANES), F32)]),
                jnp.concatenate([sin, jnp.zeros((ctx_len, LANES), F32)]))

    tabs = with_ctx_identity(cos64, sin64) + with_ctx_identity(cos32, sin32)
    qc, kc, vct, dq, dk, dvt = _proj_cd(xall, mod[l], w_in_p, qn, wuq_p, kvn, wukv_p, tabs,
                                        batch, seq, ctx_len, PROJ_CD_TILE, tk)
    unroll = unroll_for(DENSE_UNROLL_MAX)
    oc_t = _mla_attn(qc, kc, vct, batch, seq, tq, tk, unroll, 1)
    od_t = _diff_attn(lam, dq, dk, dvt, d_subln[0].astype(F32).reshape(D_V, 1), 1.0 - lam_init,
                      batch, seq, tq, tk, unroll, 1)
    wr_hi, wr_lo, br = router_weights(l)
    x1, h2, logits = _post_attn((oc_t, od_t, xall), None, cd_w_out[0].astype(BF16), mod[l], gid_lat,
                                ln_g[l, 0][None], ln_b[l, 0][None], wr_hi, wr_lo, br,
                                n_lat, alpha, tm)
    out = _moe_layer(h2, logits, x1, w_gate_up, w_down, l, mod[l], gid_lat,
                     ln_g[l, 1][None], ln_b[l, 1][None], alpha, tm)
    return out.reshape(batch, seq, d)
```
